```python
import jax, jax.numpy as jnp
from jax import lax
import numpy as np

D_MODEL = 1024
BATCH = 2
SEQ = 16384
DEPTH = 2

CHUNK = 64
Q_BLOCK = 128
ROPE_BASE = 10000.0
EPS = 1e-6

MLA_HEADS = 8
MLA_NOPE = 64
MLA_ROPE = 32
MLA_V = 64
MLA_Q_LORA = 256
MLA_KV_LORA = 128
FOX_HEADS = 8
FOX_DIM = 64
RET_HEADS = 4
RET_QK = 256
RET_V = 512
N_GROUPS = 4
EXPERTS_PER_GROUP = 8
N_EXPERTS = N_GROUPS * EXPERTS_PER_GROUP
TOP_K = 2
D_EXPERT = 512
MOE_BLOCK = 128

N_EVEN = (DEPTH + 1) // 2
N_ODD = DEPTH // 2
EVEN_IN = MLA_Q_LORA + MLA_KV_LORA + MLA_ROPE + 3 * FOX_HEADS * FOX_DIM + FOX_HEADS
EVEN_MIX = MLA_HEADS * MLA_V + FOX_HEADS * FOX_DIM
ODD_IN = 2 * RET_HEADS * RET_QK + 2 * RET_HEADS * RET_V
ODD_MIX = RET_HEADS * RET_V

kernel_name = 'hybrid_mla_fox_retention_hmoe'


def _rms(x):
    xf = x.astype(jnp.float32)
    return xf * lax.rsqrt(jnp.mean(jnp.square(xf), axis=-1, keepdims=True) + EPS)


def rmsnorm(x, g):
    return (_rms(x) * g.astype(jnp.float32)).astype(x.dtype)


def rope(x):
    s, d = x.shape[1], x.shape[-1]
    half = d // 2
    inv = ROPE_BASE ** (-jnp.arange(half, dtype=jnp.float32) / half)
    ang = jnp.arange(s, dtype=jnp.float32)[:, None] * inv[None, :]
    cos = jnp.cos(ang)[None, :, None, :]
    sin = jnp.sin(ang)[None, :, None, :]
    xf = x.astype(jnp.float32)
    x1, x2 = xf[..., :half], xf[..., half:]
    return jnp.concatenate([x1 * cos - x2 * sin, x1 * sin + x2 * cos], axis=-1).astype(x.dtype)


def block_attention(q_a, k_a, v_a, q_b, k_b, v_b, cum):
    s = q_a.shape[2]
    n_blk = s // Q_BLOCK
    kpos = jnp.arange(s)
    scale_a = (MLA_NOPE + MLA_ROPE) ** -0.5
    scale_b = FOX_DIM ** -0.5

    def one(i):
        start = i * Q_BLOCK
        qpos = start + jnp.arange(Q_BLOCK)
        qa = lax.dynamic_slice_in_dim(q_a, start, Q_BLOCK, axis=2)
        s_a = jnp.einsum('bhqd,bhkd->bhqk', qa, k_a, preferred_element_type=jnp.float32) * scale_a
        chunk_mask = (kpos[None, :] // CHUNK) <= (qpos[:, None] // CHUNK)
        p_a = jax.nn.softmax(jnp.where(chunk_mask, s_a, -jnp.inf), axis=-1)
        o_a = jnp.einsum('bhqk,bhkd->bhqd', p_a.astype(v_a.dtype), v_a)
        qb = lax.dynamic_slice_in_dim(q_b, start, Q_BLOCK, axis=2)
        cq = lax.dynamic_slice_in_dim(cum, start, Q_BLOCK, axis=2)
        s_b = (jnp.einsum('bhqd,bhkd->bhqk', qb, k_b, preferred_element_type=jnp.float32) * scale_b
               + cq[..., :, None] - cum[..., None, :])
        causal = kpos[None, :] <= qpos[:, None]
        p_b = jax.nn.softmax(jnp.where(causal, s_b, -jnp.inf), axis=-1)
        o_b = jnp.einsum('bhqk,bhkd->bhqd', p_b.astype(v_b.dtype), v_b)
        return o_a, o_b

    o_a, o_b = lax.map(one, jnp.arange(n_blk))

    def merge(o):
        nb, b, h, qb_, d = o.shape
        return o.transpose(1, 0, 3, 2, 4).reshape(b, nb * qb_, h * d)

    return merge(o_a), merge(o_b)


def even_mixer(h, w_in, ln_q, w_uq, ln_kv, w_ukv, b_f, w_out):
    b, s, _ = h.shape
    z = h @ w_in
    cuts = np.cumsum([MLA_Q_LORA, MLA_KV_LORA, MLA_ROPE, FOX_HEADS * FOX_DIM,
                      FOX_HEADS * FOX_DIM, FOX_HEADS * FOX_DIM]).tolist()
    c_q, c_kv, k_pe, q_f, k_f, v_f, f_logit = jnp.split(z, cuts, axis=-1)
    q = (rmsnorm(c_q, ln_q) @ w_uq).reshape(b, s, MLA_HEADS, MLA_NOPE + MLA_ROPE)
    q = jnp.concatenate([q[..., :MLA_NOPE], rope(q[..., MLA_NOPE:])], axis=-1)
    kv = (rmsnorm(c_kv, ln_kv) @ w_ukv).reshape(b, s, MLA_HEADS, MLA_NOPE + MLA_V)
    k_pe = jnp.broadcast_to(rope(k_pe[:, :, None, :]), (b, s, MLA_HEADS, MLA_ROPE))
    k = jnp.concatenate([kv[..., :MLA_NOPE], k_pe], axis=-1)
    v = kv[..., MLA_NOPE:]
    log_f = jax.nn.log_sigmoid(f_logit.astype(jnp.float32) + b_f.astype(jnp.float32))
    cum = jnp.cumsum(log_f, axis=1).transpose(0, 2, 1)
    fh = lambda t: t.reshape(b, s, FOX_HEADS, FOX_DIM).transpose(0, 2, 1, 3)
    tr = lambda t: t.transpose(0, 2, 1, 3)
    o_a, o_b = block_attention(tr(q), tr(k), tr(v), fh(q_f), fh(k_f), fh(v_f), cum)
    return jnp.concatenate([o_a, o_b], axis=-1) @ w_out


def retention(q, k, v):
    b, s, h, dk = q.shape
    dv = v.shape[-1]
    n = s // CHUNK
    q, k, v = (t.astype(jnp.float32) for t in (q, k, v))
    log_g = jnp.log(1.0 - jnp.exp2(-5.0 - jnp.arange(h, dtype=jnp.float32)))
    j = jnp.arange(CHUNK, dtype=jnp.float32)
    diff = j[:, None] - j[None, :]
    dmask = jnp.where(diff >= 0, jnp.exp(jnp.maximum(diff, 0.0)[None] * log_g[:, None, None]), 0.0)
    xi = jnp.exp((j[:, None] + 1.0) * log_g[None, :])
    zeta = jnp.exp((CHUNK - 1.0 - j)[:, None] * log_g[None, :])
    g_c = jnp.exp(CHUNK * log_g)
    qc = q.reshape(b, n, CHUNK, h, dk)
    kc = k.reshape(b, n, CHUNK, h, dk)
    vc = v.reshape(b, n, CHUNK, h, dv)
    inner = jnp.einsum('bnchd,bnmhd->bnhcm', qc, kc) * dmask
    o_inner = jnp.einsum('bnhcm,bnmhe->bnche', inner, vc)

    def step(state, xs):
        qn, kn, vn = xs
        cross = jnp.einsum('bchd,bhde->bche', qn, state) * xi[None, :, :, None]
        state = g_c[None, :, None, None] * state + jnp.einsum(
            'bchd,bche->bhde', kn * zeta[None, :, :, None], vn)
        return state, cross

    state0 = jnp.zeros((b, h, dk, dv), jnp.float32)
    lead = lambda t: t.transpose(1, 0, 2, 3, 4)
    _, cross = lax.scan(step, state0, (lead(qc), lead(kc), lead(vc)))
    return (o_inner + lead(cross)).reshape(b, s, h, dv)


def odd_mixer(h, w_in, w_out):
    b, s, _ = h.shape
    z = h @ w_in
    cuts = np.cumsum([RET_HEADS * RET_QK, RET_HEADS * RET_QK, RET_HEADS * RET_V]).tolist()
    q, k, v, g = jnp.split(z, cuts, axis=-1)
    q = rope(q.reshape(b, s, RET_HEADS, RET_QK))
    k = rope(k.reshape(b, s, RET_HEADS, RET_QK)) * (RET_QK ** -0.5)
    v = v.reshape(b, s, RET_HEADS, RET_V)
    o = _rms(retention(q, k, v)).reshape(b, s, ODD_MIX)
    return (o.astype(h.dtype) * jax.nn.silu(g)) @ w_out


def expert_dispatch(xt, eid, gate, w_gate, w_up, w_down):
    t, d = xt.shape
    a = t * TOP_K
    e_flat = eid.reshape(a)
    w_flat = gate.reshape(a)
    tok = jnp.arange(a, dtype=jnp.int32) // TOP_K
    order = jnp.argsort(e_flat)
    e_s, tok_s, w_s = e_flat[order], tok[order], w_flat[order]
    counts = jnp.bincount(e_flat, length=N_EXPERTS)
    start = jnp.cumsum(counts) - counts
    padded = ((counts + MOE_BLOCK - 1) // MOE_BLOCK) * MOE_BLOCK
    pend = jnp.cumsum(padded)
    pstart = pend - padded
    dest = pstart[e_s] + (jnp.arange(a) - start[e_s])
    cap = a + N_EXPERTS * MOE_BLOCK
    n_blk = cap // MOE_BLOCK
    slot_tok = jnp.full((cap,), t, jnp.int32).at[dest].set(tok_s)
    slot_w = jnp.zeros((cap,), jnp.float32).at[dest].set(w_s)
    blk_expert = jnp.minimum(
        jnp.searchsorted(pend, jnp.arange(n_blk) * MOE_BLOCK, side='right'), N_EXPERTS - 1)
    x_pad = jnp.concatenate([xt, jnp.zeros((1, d), xt.dtype)], axis=0)
    xs = x_pad[slot_tok].reshape(n_blk, MOE_BLOCK, d)

    def run(args):
        xb, e = args
        hid = jax.nn.silu(xb @ w_gate[e]) * (xb @ w_up[e])
        return hid @ w_down[e]

    ys = lax.map(run, (xs, blk_expert)).reshape(cap, d)
    out = jax.ops.segment_sum(ys * slot_w[:, None].astype(ys.dtype), slot_tok, num_segments=t + 1)
    return out[:t]


def hier_moe(h, w_rg, b_rg, w_re, b_re, w_gate, w_up, w_down):
    b, s, d = h.shape
    t = b * s
    xt = h.reshape(t, d)
    g_logits = (xt @ w_rg).astype(jnp.float32) + b_rg.astype(jnp.float32)
    g_prob = jax.nn.softmax(g_logits, axis=-1)
    g_sel = jnp.argmax(g_logits, axis=-1).astype(jnp.int32)
    p_g = jnp.take_along_axis(g_prob, g_sel[:, None], axis=-1)
    e_all = ((xt @ w_re).astype(jnp.float32) + b_re.astype(jnp.float32)).reshape(
        t, N_GROUPS, EXPERTS_PER_GROUP)
    e_logits = jnp.take_along_axis(e_all, g_sel[:, None, None], axis=1)[:, 0]
    top_p, top_j = lax.top_k(jax.nn.softmax(e_logits, axis=-1), TOP_K)
    gate = p_g * top_p / jnp.sum(top_p, axis=-1, keepdims=True)
    eid = (g_sel[:, None] * EXPERTS_PER_GROUP + top_j).astype(jnp.int32)
    out = expert_dispatch(xt, eid, gate, w_gate, w_up, w_down)
    return out.reshape(b, s, d)


def setup_inputs(seed: int = 0) -> dict:
    key = jax.random.key(seed)
    ks = jax.random.split(key, 24)
    nrm = lambda k, shape, fan_in: jax.random.normal(k, shape, jnp.float32) * (fan_in ** -0.5)
    gain = lambda k, shape: 1.0 + 0.02 * jax.random.normal(k, shape, jnp.float32)
    return {
        'x': jax.random.normal(ks[0], (BATCH, SEQ, D_MODEL), jnp.float32),
        'ln_mix_e': gain(ks[1], (N_EVEN, D_MODEL)),
        'w_in_e': nrm(ks[2], (N_EVEN, D_MODEL, EVEN_IN), D_MODEL),
        'ln_q_e': gain(ks[3], (N_EVEN, MLA_Q_LORA)),
        'w_uq_e': nrm(ks[4], (N_EVEN, MLA_Q_LORA, MLA_HEADS * (MLA_NOPE + MLA_ROPE)), MLA_Q_LORA),
        'ln_kv_e': gain(ks[5], (N_EVEN, MLA_KV_LORA)),
        'w_ukv_e': nrm(ks[6], (N_EVEN, MLA_KV_LORA, MLA_HEADS * (MLA_NOPE + MLA_V)), MLA_KV_LORA),
        'b_f_e': 3.0 + 0.5 * jax.random.normal(ks[7], (N_EVEN, FOX_HEADS), jnp.float32),
        'w_out_e': nrm(ks[8], (N_EVEN, EVEN_MIX, D_MODEL), EVEN_MIX),
        'ln_mix_o': gain(ks[9], (N_ODD, D_MODEL)),
        'w_in_o': nrm(ks[10], (N_ODD, D_MODEL, ODD_IN), D_MODEL),
        'w_out_o': nrm(ks[11], (N_ODD, ODD_MIX, D_MODEL), ODD_MIX),
        'ln_ffn': gain(ks[12], (DEPTH, D_MODEL)),
        'w_rg': nrm(ks[13], (DEPTH, D_MODEL, N_GROUPS), D_MODEL),
        'b_rg': 0.01 * jax.random.normal(ks[14], (DEPTH, N_GROUPS), jnp.float32),
        'w_re': nrm(ks[15], (DEPTH, D_MODEL, N_EXPERTS), D_MODEL),
        'b_re': 0.01 * jax.random.normal(ks[16], (DEPTH, N_EXPERTS), jnp.float32),
        'w_gate': nrm(ks[17], (DEPTH, N_EXPERTS, D_MODEL, D_EXPERT), D_MODEL),
        'w_up': nrm(ks[18], (DEPTH, N_EXPERTS, D_MODEL, D_EXPERT), D_MODEL),
        'w_down': nrm(ks[19], (DEPTH, N_EXPERTS, D_EXPERT, D_MODEL), D_EXPERT),
        'ln_f': gain(ks[20], (D_MODEL,)),
    }


def reference(x, ln_mix_e, w_in_e, ln_q_e, w_uq_e, ln_kv_e, w_ukv_e, b_f_e, w_out_e,
              ln_mix_o, w_in_o, w_out_o, ln_ffn, w_rg, b_rg, w_re, b_re,
              w_gate, w_up, w_down, ln_f):
    for layer in range(DEPTH):
        i = layer // 2
        if layer % 2 == 0:
            x = x + even_mixer(rmsnorm(x, ln_mix_e[i]), w_in_e[i], ln_q_e[i], w_uq_e[i],
                               ln_kv_e[i], w_ukv_e[i], b_f_e[i], w_out_e[i])
        else:
            x = x + odd_mixer(rmsnorm(x, ln_mix_o[i]), w_in_o[i], w_out_o[i])
        x = x + hier_moe(rmsnorm(x, ln_ffn[layer]), w_rg[layer], b_rg[layer], w_re[layer],
                         b_re[layer], w_gate[layer], w_up[layer], w_down[layer])
    return rmsnorm(x, ln_f)
```

```python
import functools
import math

import numpy as np
import jax
import jax.numpy as jnp
from jax import lax
from jax.experimental import pallas as pl
from jax.experimental.pallas import tpu as pltpu

D_MODEL = 1024
CHUNK = 64
ROPE_BASE = 10000.0
EPS = 1e-6
MLA_HEADS = 8
MLA_NOPE = 64
MLA_ROPE = 32
MLA_V = 64
MLA_Q_LORA = 256
MLA_KV_LORA = 128
FOX_HEADS = 8
FOX_DIM = 64
RET_HEADS = 4
RET_QK = 256
RET_V = 512
N_GROUPS = 4
EXPERTS_PER_GROUP = 8
N_EXPERTS = N_GROUPS * EXPERTS_PER_GROUP
TOP_K = 2
D_EXPERT = 512

LANES = 128
SUBLANES = 8
VMEM_LIMIT = 52 * 1024 * 1024

TM_PROJ = 256
TQ = 256
TK_BIG = 1024
RET_C = 256
TE = 256
ROWS_PER_TOK = D_MODEL // LANES

N_PAIRS = MLA_HEADS // 2
PAIR_W = 256
ROPE_H = MLA_ROPE // 2
NEG = -1e30

_X1_OFF = 0
_X2_OFF = 64
_AUG_W = 6

_C_Q = 0
_C_KV = _C_Q + MLA_Q_LORA
_C_KPE = _C_KV + MLA_KV_LORA
_C_QF = _C_KPE + LANES
_C_KF = _C_QF + FOX_HEADS * FOX_DIM
_C_VF = _C_KF + FOX_HEADS * FOX_DIM
_C_F = _C_VF + FOX_HEADS * FOX_DIM
EVEN_N = _C_F + LANES


def _cparams(sem):
    return pltpu.CompilerParams(dimension_semantics=sem, vmem_limit_bytes=VMEM_LIMIT)


def _rms_scale(x):
    return lax.rsqrt(jnp.mean(x * x, axis=-1, keepdims=True) + EPS)


def _split3(c):
    hi = c.astype(jnp.bfloat16)
    r1 = c - hi.astype(jnp.float32)
    mid = r1.astype(jnp.bfloat16)
    r2 = r1 - mid.astype(jnp.float32)
    lo = r2.astype(jnp.bfloat16)
    return hi, mid, lo


def _even_proj_kernel(tiles_per_seq, x_ref, g_ref, win_ref, lnq_ref, wuq_ref, lnkv_ref, wukv_ref,
                      bf_ref, cos_ref, sin_ref, tri_ref, pq_ref, pk_ref, oq_ref, ok_ref,
                      q_ref, k_ref, v_ref, carry_ref):
    i = pl.program_id(0)
    x = x_ref[...]
    xn = (x * _rms_scale(x) * g_ref[...]).astype(jnp.bfloat16)
    z = jnp.dot(xn, win_ref[...], preferred_element_type=jnp.float32)
    cos = cos_ref[...]
    sin = sin_ref[...]

    def rope_slab(s):
        return s * cos + pltpu.roll(s, 64, axis=1) * sin

    c_q = z[:, _C_Q:_C_Q + MLA_Q_LORA]
    cqn = (c_q * _rms_scale(c_q) * lnq_ref[...]).astype(jnp.bfloat16)
    q = jnp.dot(cqn, wuq_ref[...], preferred_element_type=jnp.float32)
    scale_a = (MLA_NOPE + MLA_ROPE) ** -0.5
    for p in range(N_PAIRS):
        lo = p * PAIR_W
        q_ref[0, :, lo:lo + LANES] = (q[:, lo:lo + LANES] * scale_a).astype(jnp.bfloat16)
        q_ref[0, :, lo + LANES:lo + PAIR_W] = (
            rope_slab(q[:, lo + LANES:lo + PAIR_W]) * scale_a).astype(jnp.bfloat16)

    c_kv = z[:, _C_KV:_C_KV + MLA_KV_LORA]
    ckn = (c_kv * _rms_scale(c_kv) * lnkv_ref[...]).astype(jnp.bfloat16)
    kv = jnp.dot(ckn, wukv_ref[...], preferred_element_type=jnp.float32)
    kpe = rope_slab(z[:, _C_KPE:_C_KPE + LANES]).astype(jnp.bfloat16)
    nk = MLA_HEADS * MLA_NOPE
    for p in range(N_PAIRS):
        lo = p * PAIR_W
        k_ref[0, :, lo:lo + LANES] = kv[:, p * LANES:(p + 1) * LANES].astype(jnp.bfloat16)
        k_ref[0, :, lo + LANES:lo + PAIR_W] = kpe
    v_ref[0] = kv[:, nk:nk + MLA_HEADS * MLA_V].astype(jnp.bfloat16)

    lane = lax.broadcasted_iota(jnp.int32, (1, LANES), 1)
    fz = z[:, _C_F:_C_F + LANES] + bf_ref[...]
    log_f = -(jnp.maximum(-fz, 0.0) + jnp.log1p(jnp.exp(-jnp.abs(fz))))
    log_f = jnp.where(lane < FOX_HEADS, log_f, 0.0)

    @pl.when(i % tiles_per_seq == 0)
    def _():
        carry_ref[...] = jnp.zeros_like(carry_ref)

    hi, mid, lo3 = _split3(log_f)
    tri = tri_ref[...]
    cum = (jnp.dot(tri, hi, preferred_element_type=jnp.float32)
           + jnp.dot(tri, mid, preferred_element_type=jnp.float32)
           + jnp.dot(tri, lo3, preferred_element_type=jnp.float32)) + carry_ref[...]
    tm = cum.shape[0]
    carry_ref[...] = cum[tm - 1:tm, :]

    parts = jnp.concatenate(_split3(cum), axis=1)
    aug_q = jnp.dot(parts, pq_ref[...], preferred_element_type=jnp.float32) + oq_ref[...]
    aug_k = jnp.dot(parts, pk_ref[...], preferred_element_type=jnp.float32) + ok_ref[...]
    scale_b = FOX_DIM ** -0.5
    for p in range(N_PAIRS):
        lo = p * PAIR_W
        q_ref[1, :, lo:lo + LANES] = (
            z[:, _C_QF + p * LANES:_C_QF + (p + 1) * LANES] * scale_b).astype(jnp.bfloat16)
        q_ref[1, :, lo + LANES:lo + PAIR_W] = aug_q[:, p * LANES:(p + 1) * LANES].astype(jnp.bfloat16)
        k_ref[1, :, lo:lo + LANES] = z[:, _C_KF + p * LANES:_C_KF + (p + 1) * LANES].astype(jnp.bfloat16)
        k_ref[1, :, lo + LANES:lo + PAIR_W] = aug_k[:, p * LANES:(p + 1) * LANES].astype(jnp.bfloat16)
    v_ref[1] = z[:, _C_VF:_C_VF + FOX_HEADS * FOX_DIM].astype(jnp.bfloat16)


def _even_weights(w_in, w_uq, w_ukv, b_f):
    d = w_in.shape[0]
    zeros = lambda n: jnp.zeros((d, n), w_in.dtype)
    o_cq, o_ckv = 0, MLA_Q_LORA
    o_kpe = o_ckv + MLA_KV_LORA
    o_qf = o_kpe + MLA_ROPE
    o_kf = o_qf + FOX_HEADS * FOX_DIM
    o_vf = o_kf + FOX_HEADS * FOX_DIM
    o_f = o_vf + FOX_HEADS * FOX_DIM
    kpe1 = w_in[:, o_kpe:o_kpe + ROPE_H]
    kpe2 = w_in[:, o_kpe + ROPE_H:o_kpe + MLA_ROPE]
    kpe_slab = jnp.concatenate([kpe1, kpe1, zeros(32), kpe2, kpe2, zeros(32)], axis=1)
    f_slab = jnp.concatenate([w_in[:, o_f:o_f + FOX_HEADS], zeros(LANES - FOX_HEADS)], axis=1)
    w_in_p = jnp.concatenate([w_in[:, o_cq:o_kpe], kpe_slab, w_in[:, o_qf:o_f], f_slab], axis=1)

    dq = w_uq.shape[0]
    zq = lambda n: jnp.zeros((dq, n), w_uq.dtype)
    hd = MLA_NOPE + MLA_ROPE
    blocks = []
    for p in range(N_PAIRS):
        h0, h1 = 2 * p, 2 * p + 1
        nope = lambda h: w_uq[:, h * hd:h * hd + MLA_NOPE]
        r1 = lambda h: w_uq[:, h * hd + MLA_NOPE:h * hd + MLA_NOPE + ROPE_H]
        r2 = lambda h: w_uq[:, h * hd + MLA_NOPE + ROPE_H:(h + 1) * hd]
        blocks += [nope(h0), nope(h1), r1(h0), r1(h1), zq(32), r2(h0), r2(h1), zq(32)]
    w_uq_p = jnp.concatenate(blocks, axis=1)

    kvd = MLA_NOPE + MLA_V
    k_cols = [w_ukv[:, h * kvd:h * kvd + MLA_NOPE] for h in range(MLA_HEADS)]
    v_cols = [w_ukv[:, h * kvd + MLA_NOPE:(h + 1) * kvd] for h in range(MLA_HEADS)]
    w_ukv_p = jnp.concatenate(k_cols + v_cols, axis=1)
    b_f_p = jnp.concatenate([b_f, jnp.zeros((LANES - FOX_HEADS,), b_f.dtype)])[None, :]
    bf16 = jnp.bfloat16
    return w_in_p.astype(bf16), w_uq_p.astype(bf16), w_ukv_p.astype(bf16), b_f_p


def _aug_placement():
    pq = np.zeros((3 * LANES, N_PAIRS * LANES), np.float32)
    pk = np.zeros((3 * LANES, N_PAIRS * LANES), np.float32)
    oq = np.zeros((1, N_PAIRS * LANES), np.float32)
    ok = np.zeros((1, N_PAIRS * LANES), np.float32)
    for p in range(N_PAIRS):
        for j in range(2):
            h = 2 * p + j
            base = p * LANES + _X1_OFF + j * ROPE_H
            for t in range(3):
                pq[t * LANES + h, base + 3 + t] = 1.0
                pk[t * LANES + h, base + t] = -1.0
                oq[0, base + t] = 1.0
                ok[0, base + 3 + t] = 1.0
    return (jnp.asarray(pq, jnp.bfloat16), jnp.asarray(pk, jnp.bfloat16),
            jnp.asarray(oq), jnp.asarray(ok))


def _mla_rope_tables(s):
    inv = ROPE_BASE ** (-jnp.arange(ROPE_H, dtype=jnp.float32) / ROPE_H)
    ang = jnp.arange(s, dtype=jnp.float32)[:, None] * inv[None, :]
    c, sn = jnp.cos(ang), jnp.sin(ang)
    z = jnp.zeros((s, 32), jnp.float32)
    cos = jnp.concatenate([c, c, z, c, c, z], axis=1)
    sin = jnp.concatenate([-sn, -sn, z, sn, sn, z], axis=1)
    return cos, sin


def _even_proj(x2d, seq, g, w_in_p, ln_q, w_uq_p, ln_kv, w_ukv_p, b_f_p):
    t = x2d.shape[0]
    tm = TM_PROJ
    tiles_per_seq = seq // tm
    cos, sin = _mla_rope_tables(seq)
    tri = jnp.asarray(np.tril(np.ones((tm, tm), np.float32)), jnp.bfloat16)
    pq, pk, oq, ok = _aug_placement()
    const = lambda shape: pl.BlockSpec(shape, lambda i: (0,) * len(shape))
    return pl.pallas_call(
        functools.partial(_even_proj_kernel, tiles_per_seq),
        grid=(t // tm,),
        in_specs=[
            pl.BlockSpec((tm, D_MODEL), lambda i: (i, 0)),
            const((1, D_MODEL)), const((D_MODEL, EVEN_N)),
            const((1, MLA_Q_LORA)), const((MLA_Q_LORA, N_PAIRS * PAIR_W)),
            const((1, MLA_KV_LORA)), const((MLA_KV_LORA, 2 * MLA_HEADS * MLA_NOPE)),
            const((1, LANES)),
            pl.BlockSpec((tm, LANES), lambda i: (i % tiles_per_seq, 0)),
            pl.BlockSpec((tm, LANES), lambda i: (i % tiles_per_seq, 0)),
            const((tm, tm)), const(pq.shape), const(pk.shape), const(oq.shape), const(ok.shape),
        ],
        out_specs=[
            pl.BlockSpec((2, tm, N_PAIRS * PAIR_W), lambda i: (0, i, 0)),
            pl.BlockSpec((2, tm, N_PAIRS * PAIR_W), lambda i: (0, i, 0)),
            pl.BlockSpec((2, tm, MLA_HEADS * MLA_V), lambda i: (0, i, 0)),
        ],
        out_shape=[
            jax.ShapeDtypeStruct((2, t, N_PAIRS * PAIR_W), jnp.bfloat16),
            jax.ShapeDtypeStruct((2, t, N_PAIRS * PAIR_W), jnp.bfloat16),
            jax.ShapeDtypeStruct((2, t, MLA_HEADS * MLA_V), jnp.bfloat16),
        ],
        scratch_shapes=[pltpu.VMEM((1, LANES), jnp.float32)],
        compiler_params=_cparams(("arbitrary",)),
        name="even_proj",
    )(x2d, g, w_in_p, ln_q, w_uq_p, ln_kv, w_ukv_p, b_f_p, cos, sin, tri, pq, pk, oq, ok)


def _attn_kernel(q_ref, k_ref, vt_ref, o_ref, m_ref, l_ref, acc_ref):
    g = pl.program_id(0)
    i = pl.program_id(3)
    tq = q_ref.shape[0]

    lane = lax.broadcasted_iota(jnp.int32, (1, PAIR_W), 1)

    def head_mask(j):
        a = (lane >= j * MLA_NOPE) & (lane < (j + 1) * MLA_NOPE)
        b = (lane >= LANES + _X1_OFF + j * ROPE_H) & (lane < LANES + _X1_OFF + (j + 1) * ROPE_H)
        c = (lane >= LANES + _X2_OFF + j * ROPE_H) & (lane < LANES + _X2_OFF + (j + 1) * ROPE_H)
        return a | b | c

    q = q_ref[...]
    zero = jnp.zeros_like(q)
    qcat = jnp.concatenate([jnp.where(head_mask(0), q, zero), jnp.where(head_mask(1), q, zero)],
                           axis=0)

    m_ref[...] = jnp.full_like(m_ref, NEG)
    l_ref[...] = jnp.zeros_like(l_ref)
    acc_ref[...] = jnp.zeros_like(acc_ref)

    def step(kstart, tk, masked):
        kc = k_ref[pl.ds(kstart, tk), :]
        s = lax.dot_general(kc, qcat, (((1,), (1,)), ((), ())),
                            preferred_element_type=jnp.float32)
        if masked:
            col = lax.broadcasted_iota(jnp.int32, (1, 2 * tq), 1)
            qpos = kstart + jnp.where(col >= tq, col - tq, col)
            qlim = jnp.where(g == 0, qpos | (CHUNK - 1), qpos)
            kpos = kstart + lax.broadcasted_iota(jnp.int32, (tk, 1), 0)
            s = jnp.where(kpos <= qlim, s, NEG)
        m_old = m_ref[...]
        m_new = jnp.maximum(m_old, jnp.max(s, axis=0, keepdims=True))
        alpha = jnp.exp(m_old - m_new)
        p = jnp.exp(s - m_new)
        l_ref[...] = alpha * l_ref[...] + jnp.sum(p, axis=0, keepdims=True)
        m_ref[...] = m_new
        vt = vt_ref[:, pl.ds(kstart, tk)]
        acc_ref[...] = alpha * acc_ref[...] + jnp.dot(
            vt, p.astype(jnp.bfloat16), preferred_element_type=jnp.float32)

    per_big = TK_BIG // tq
    n_big = i // per_big

    def big_body(c, carry):
        step(pl.multiple_of(c * TK_BIG, TK_BIG), TK_BIG, False)
        return carry

    lax.fori_loop(0, n_big, big_body, 0)

    def small_body(c, carry):
        step(pl.multiple_of(c * tq, tq), tq, False)
        return carry

    lax.fori_loop(n_big * per_big, i, small_body, 0)
    step(pl.multiple_of(i * tq, tq), tq, True)

    o = acc_ref[...] / l_ref[...]
    row = lax.broadcasted_iota(jnp.int32, (LANES, 1), 0)
    o_ref[...] = jnp.where(row < MLA_V, o[:, :tq], o[:, tq:]).astype(o_ref.dtype)


def _attention(q, k, vt, batch, seq):
    nq = seq // TQ
    return pl.pallas_call(
        _attn_kernel,
        grid=(2, batch, N_PAIRS, nq),
        in_specs=[
            pl.BlockSpec((None, TQ, PAIR_W), lambda g, b, p, i: (g, b * nq + i, p)),
            pl.BlockSpec((None, seq, PAIR_W), lambda g, b, p, i: (g, b, p)),
            pl.BlockSpec((None, None, LANES, seq), lambda g, b, p, i: (g, b, p, 0)),
        ],
        out_specs=pl.BlockSpec((None, None, LANES, TQ), lambda g, b, p, i: (g, b, p, i)),
        out_shape=jax.ShapeDtypeStruct((2, batch, N_PAIRS * LANES, seq), jnp.bfloat16),
        scratch_shapes=[pltpu.VMEM((1, 2 * TQ), jnp.float32),
                        pltpu.VMEM((1, 2 * TQ), jnp.float32),
                        pltpu.VMEM((LANES, 2 * TQ), jnp.float32)],
        compiler_params=_cparams(("arbitrary",) * 4),
        name="attention",
    )(q, k, vt)


def _router(logits):
    lane = lax.broadcasted_iota(jnp.int32, logits.shape, 1)
    big = jnp.int32(LANES)
    ninf = -jnp.inf
    gl = jnp.where(lane < N_GROUPS, logits, ninf)
    gmax = jnp.max(gl, axis=1, keepdims=True)
    gsel = jnp.min(jnp.where(gl == gmax, lane, big), axis=1, keepdims=True)
    gsum = jnp.sum(jnp.where(lane < N_GROUPS, jnp.exp(logits - gmax), 0.0), axis=1, keepdims=True)
    p_g = 1.0 / gsum
    lo = N_GROUPS + EXPERTS_PER_GROUP * gsel
    e = jnp.where((lane >= lo) & (lane < lo + EXPERTS_PER_GROUP), logits, ninf)
    m1 = jnp.max(e, axis=1, keepdims=True)
    i1 = jnp.min(jnp.where(e == m1, lane, big), axis=1, keepdims=True)
    e2 = jnp.where(lane == i1, ninf, e)
    m2 = jnp.max(e2, axis=1, keepdims=True)
    i2 = jnp.min(jnp.where(e2 == m2, lane, big), axis=1, keepdims=True)
    a2 = jnp.exp(m2 - m1)
    den = 1.0 + a2
    g1 = p_g / den
    g2 = p_g * a2 / den
    f = lambda v: v.astype(jnp.float32)
    return jnp.where(lane == 0, f(i1 - N_GROUPS),
                     jnp.where(lane == 1, f(i2 - N_GROUPS),
                               jnp.where(lane == 2, g1, jnp.where(lane == 3, g2, 0.0))))


def _out_proj_kernel(x_ref, o_ref, w_ref, ln_ref, wr_ref, br_ref, xo_ref, xg_ref, route_ref):
    tm = x_ref.shape[0]
    y = x_ref[...] + jnp.dot(o_ref[...], w_ref[...], preferred_element_type=jnp.float32)
    xo_ref[...] = y
    xn = y * _rms_scale(y) * ln_ref[...]
    logits = jnp.dot(xn, wr_ref[...], preferred_element_type=jnp.float32,
                     precision=lax.Precision.HIGHEST) + br_ref[...]
    route_ref[...] = _router(logits)
    for c in range(ROWS_PER_TOK):
        xg_ref[pl.ds(c, tm, stride=ROWS_PER_TOK), :] = xn[:, c * LANES:(c + 1) * LANES]


def _out_proj(x2d, o, w_out, ln, w_r, b_r):
    t = x2d.shape[0]
    kdim = o.shape[1]
    tm = TM_PROJ
    const = lambda shape: pl.BlockSpec(shape, lambda i: (0,) * len(shape))
    return pl.pallas_call(
        _out_proj_kernel,
        grid=(t // tm,),
        in_specs=[
            pl.BlockSpec((tm, D_MODEL), lambda i: (i, 0)),
            pl.BlockSpec((tm, kdim), lambda i: (i, 0)),
            const((kdim, D_MODEL)), const((1, D_MODEL)), const((D_MODEL, LANES)), const((1, LANES)),
        ],
        out_specs=[
            pl.BlockSpec((tm, D_MODEL), lambda i: (i, 0)),
            pl.BlockSpec((tm * ROWS_PER_TOK, LANES), lambda i: (i, 0)),
            pl.BlockSpec((tm, LANES), lambda i: (i, 0)),
        ],
        out_shape=[
            jax.ShapeDtypeStruct((t, D_MODEL), jnp.float32),
            jax.ShapeDtypeStruct((t * ROWS_PER_TOK, LANES), jnp.float32),
            jax.ShapeDtypeStruct((t, LANES), jnp.float32),
        ],
        compiler_params=_cparams(("arbitrary",)),
        name="out_proj_router",
    )(x2d, o, w_out, ln, w_r, b_r)


def _router_weights(w_rg, b_rg, w_re, b_re):
    d = w_rg.shape[0]
    pad = LANES - N_GROUPS - N_EXPERTS
    w_r = jnp.concatenate([w_rg, w_re, jnp.zeros((d, pad), w_rg.dtype)], axis=1)
    b_r = jnp.concatenate([b_rg, b_re, jnp.zeros((pad,), b_rg.dtype)])[None, :]
    return w_r, b_r


def _dispatch_plan(eid, t):
    a = t * TOP_K
    nb = a // TE + N_EXPERTS
    e_flat = eid.reshape(a)
    order = jnp.argsort(e_flat).astype(jnp.int32)
    e_s = e_flat[order]
    bounds = jnp.searchsorted(e_s, jnp.arange(N_EXPERTS + 1, dtype=jnp.int32), side="left").astype(jnp.int32)
    start = bounds[:-1]
    counts = bounds[1:] - start
    padded = ((counts + TE - 1) // TE) * TE
    pend = jnp.cumsum(padded)
    pstart = pend - padded
    blk0 = jnp.arange(nb, dtype=jnp.int32) * TE
    blk_e = jnp.minimum(jnp.searchsorted(pend, blk0, side="right"), N_EXPERTS - 1).astype(jnp.int32)
    blk_used = (blk0 < pend[-1]).astype(jnp.int32)
    r = jnp.arange(TE, dtype=jnp.int32)[None, :]
    off = blk0[:, None] + r - pstart[blk_e][:, None]
    valid = (off < counts[blk_e][:, None]) & (blk_used[:, None] > 0)
    src = jnp.where(valid, start[blk_e][:, None] + off, 0)
    assign = order[src]
    src_tok = jnp.where(valid, assign // TOP_K, 0).astype(jnp.int32)
    dst_row = jnp.where(valid, assign, a + r).astype(jnp.int32)
    return blk_e, blk_used, src_tok[:, None, :], dst_row[:, None, :]


def _moe_kernel(blk_e_ref, blk_used_ref, src_ref, src_next_ref, dst_ref, xg_ref, wg_ref, wu_ref, wd_ref,
                ys_ref, xbuf, ybuf, wg_bf, wu_bf, wd_bf, gsem, ssem):
    i = pl.program_id(0)
    nb = pl.num_programs(0)
    used = blk_used_ref[i] > 0
    slot = i % 2

    def gather_row_copy(idx_ref, r, s):
        tok = idx_ref[0, 0, r]
        return pltpu.make_async_copy(
            xg_ref.at[pl.ds(pl.multiple_of(tok * ROWS_PER_TOK, ROWS_PER_TOK), ROWS_PER_TOK), :],
            xbuf.at[s, pl.ds(pl.multiple_of(r * ROWS_PER_TOK, ROWS_PER_TOK), ROWS_PER_TOK), :],
            gsem.at[s])

    def start_gather(idx_ref, s):
        def body(r, carry):
            gather_row_copy(idx_ref, r, s).start()
            return carry
        lax.fori_loop(0, TE, body, 0)

    def scatter_row_copy(r):
        row = dst_ref[0, 0, r]
        return pltpu.make_async_copy(
            ybuf.at[pl.ds(pl.multiple_of(r * ROWS_PER_TOK, ROWS_PER_TOK), ROWS_PER_TOK), :],
            ys_ref.at[pl.ds(pl.multiple_of(row * ROWS_PER_TOK, ROWS_PER_TOK), ROWS_PER_TOK), :],
            ssem.at[0])

    def wait_scatter():
        pltpu.make_async_copy(ybuf, ys_ref.at[pl.ds(0, TE * ROWS_PER_TOK), :], ssem.at[0]).wait()

    @pl.when(i == 0)
    def _():
        ybuf[...] = jnp.zeros_like(ybuf)
        spare = pltpu.make_async_copy(
            ybuf, ys_ref.at[pl.ds(ys_ref.shape[0] - TE * ROWS_PER_TOK, TE * ROWS_PER_TOK), :], ssem.at[0])
        spare.start()
        spare.wait()

    @pl.when((i == 0) & used)
    def _():
        start_gather(src_ref, 0)

    next_used = blk_used_ref[jnp.minimum(i + 1, nb - 1)] > 0

    @pl.when((i + 1 < nb) & next_used)
    def _():
        start_gather(src_next_ref, 1 - slot)

    @pl.when(used)
    def _():
        prev_e = blk_e_ref[jnp.maximum(i - 1, 0)]

        @pl.when((i == 0) | (blk_e_ref[i] != prev_e))
        def _():
            wg_bf[...] = wg_ref[...].astype(jnp.bfloat16)
            wu_bf[...] = wu_ref[...].astype(jnp.bfloat16)
            wd_bf[...] = wd_ref[...].astype(jnp.bfloat16)

        pltpu.make_async_copy(xg_ref.at[pl.ds(0, TE * ROWS_PER_TOK), :], xbuf.at[slot], gsem.at[slot]).wait()
        xb = xbuf.at[slot]
        x = jnp.concatenate([xb[pl.ds(c, TE, stride=ROWS_PER_TOK), :] for c in range(ROWS_PER_TOK)],
                            axis=1).astype(jnp.bfloat16)
        hg = jnp.dot(x, wg_bf[...], preferred_element_type=jnp.float32)
        hu = jnp.dot(x, wu_bf[...], preferred_element_type=jnp.float32)
        hid = (hg * jax.nn.sigmoid(hg) * hu).astype(jnp.bfloat16)
        y = jnp.dot(hid, wd_bf[...], preferred_element_type=jnp.float32)

        @pl.when(i > 0)
        def _():
            wait_scatter()

        for c in range(ROWS_PER_TOK):
            ybuf[pl.ds(c, TE, stride=ROWS_PER_TOK), :] = y[:, c * LANES:(c + 1) * LANES]

        def body(r, carry):
            scatter_row_copy(r).start()
            return carry
        lax.fori_loop(0, TE, body, 0)

        @pl.when((i + 1 >= nb) | jnp.logical_not(next_used))
        def _():
            wait_scatter()


def _moe(xg, eid, w_gate, w_up, w_down, t):
    a = t * TOP_K
    nb = a // TE + N_EXPERTS
    blk_e, blk_used, src_tok, dst_row = _dispatch_plan(eid, t)
    idx_spec = lambda f: pl.BlockSpec((1, 1, TE), f, memory_space=pltpu.SMEM)
    grid_spec = pltpu.PrefetchScalarGridSpec(
        num_scalar_prefetch=2,
        grid=(nb,),
        in_specs=[
            idx_spec(lambda i, be, bu: (i, 0, 0)),
            idx_spec(lambda i, be, bu: (jnp.minimum(i + 1, nb - 1), 0, 0)),
            idx_spec(lambda i, be, bu: (i, 0, 0)),
            pl.BlockSpec(memory_space=pl.ANY),
            pl.BlockSpec((None, D_MODEL, D_EXPERT), lambda i, be, bu: (be[i], 0, 0)),
            pl.BlockSpec((None, D_MODEL, D_EXPERT), lambda i, be, bu: (be[i], 0, 0)),
            pl.BlockSpec((None, D_EXPERT, D_MODEL), lambda i, be, bu: (be[i], 0, 0)),
        ],
        out_specs=pl.BlockSpec(memory_space=pl.ANY),
        scratch_shapes=[
            pltpu.VMEM((2, TE * ROWS_PER_TOK, LANES), jnp.float32),
            pltpu.VMEM((TE * ROWS_PER_TOK, LANES), jnp.float32),
            pltpu.VMEM((D_MODEL, D_EXPERT), jnp.bfloat16),
            pltpu.VMEM((D_MODEL, D_EXPERT), jnp.bfloat16),
            pltpu.VMEM((D_EXPERT, D_MODEL), jnp.bfloat16),
            pltpu.SemaphoreType.DMA((2,)),
            pltpu.SemaphoreType.DMA((1,)),
        ],
    )
    return pl.pallas_call(
        _moe_kernel,
        grid_spec=grid_spec,
        out_shape=jax.ShapeDtypeStruct(((a + TE) * ROWS_PER_TOK, LANES), jnp.float32),
        compiler_params=_cparams(("arbitrary",)),
        name="moe_experts",
    )(blk_e, blk_used, src_tok, src_tok, dst_row, xg, w_gate, w_up, w_down)


def _combine(x, ys_ref, route_ref, tm):
    stride = TOP_K * ROWS_PER_TOK
    y0 = jnp.concatenate([ys_ref[pl.ds(c, tm, stride=stride), :] for c in range(ROWS_PER_TOK)], axis=1)
    y1 = jnp.concatenate([ys_ref[pl.ds(ROWS_PER_TOK + c, tm, stride=stride), :]
                          for c in range(ROWS_PER_TOK)], axis=1)
    route = route_ref[...]
    return x + route[:, 2:3] * y0 + route[:, 3:4] * y1


def _odd_proj_kernel(x_ref, ys_ref, route_ref, g_ref, w_ref, cos_ref, sin_ref,
                     xo_ref, q_ref, k_ref, v_ref, gate_ref):
    tm = x_ref.shape[0]
    x = _combine(x_ref[...], ys_ref, route_ref, tm)
    xo_ref[...] = x
    xn = (x * _rms_scale(x) * g_ref[...]).astype(jnp.bfloat16)
    cos = cos_ref[...]
    sin = sin_ref[...]
    nqk = RET_HEADS * RET_QK
    half = RET_QK // 2

    def rope_store(z, dst, scale):
        for h in range(RET_HEADS):
            x1 = z[:, h * RET_QK:h * RET_QK + half]
            x2 = z[:, h * RET_QK + half:(h + 1) * RET_QK]
            dst[:, h * RET_QK:h * RET_QK + half] = ((x1 * cos - x2 * sin) * scale).astype(dst.dtype)
            dst[:, h * RET_QK + half:(h + 1) * RET_QK] = ((x1 * sin + x2 * cos) * scale).astype(dst.dtype)

    zq = jnp.dot(xn, w_ref[:, 0:nqk], preferred_element_type=jnp.float32)
    rope_store(zq, q_ref, 1.0)
    zk = jnp.dot(xn, w_ref[:, nqk:2 * nqk], preferred_element_type=jnp.float32)
    rope_store(zk, k_ref, RET_QK ** -0.5)
    nv = RET_HEADS * RET_V
    zv = jnp.dot(xn, w_ref[:, 2 * nqk:2 * nqk + nv], preferred_element_type=jnp.float32)
    v_ref[...] = zv.astype(v_ref.dtype)
    zg = jnp.dot(xn, w_ref[:, 2 * nqk + nv:2 * nqk + 2 * nv], preferred_element_type=jnp.float32)
    gate_ref[...] = (zg * jax.nn.sigmoid(zg)).astype(gate_ref.dtype)


def _ret_rope_tables(s):
    half = RET_QK // 2
    inv = ROPE_BASE ** (-jnp.arange(half, dtype=jnp.float32) / half)
    ang = jnp.arange(s, dtype=jnp.float32)[:, None] * inv[None, :]
    return jnp.cos(ang), jnp.sin(ang)


def _odd_proj(x2d, ys, route, seq, g, w_in):
    t = x2d.shape[0]
    tm = TM_PROJ
    tiles_per_seq = seq // tm
    cos, sin = _ret_rope_tables(seq)
    nqk = RET_HEADS * RET_QK
    nv = RET_HEADS * RET_V
    n_in = w_in.shape[1]
    const = lambda shape: pl.BlockSpec(shape, lambda i: (0,) * len(shape))
    tok = lambda n: pl.BlockSpec((tm, n), lambda i: (i, 0))
    return pl.pallas_call(
        _odd_proj_kernel,
        grid=(t // tm,),
        in_specs=[
            tok(D_MODEL),
            pl.BlockSpec((tm * TOP_K * ROWS_PER_TOK, LANES), lambda i: (i, 0)),
            tok(LANES),
            const((1, D_MODEL)), const((D_MODEL, n_in)),
            pl.BlockSpec((tm, LANES), lambda i: (i % tiles_per_seq, 0)),
            pl.BlockSpec((tm, LANES), lambda i: (i % tiles_per_seq, 0)),
        ],
        out_specs=[tok(D_MODEL), tok(nqk), tok(nqk), tok(nv), tok(nv)],
        out_shape=[
            jax.ShapeDtypeStruct((t, D_MODEL), jnp.float32),
            jax.ShapeDtypeStruct((t, nqk), jnp.bfloat16),
            jax.ShapeDtypeStruct((t, nqk), jnp.bfloat16),
            jax.ShapeDtypeStruct((t, nv), jnp.bfloat16),
            jax.ShapeDtypeStruct((t, nv), jnp.bfloat16),
        ],
        compiler_params=_cparams(("arbitrary",)),
        name="odd_proj",
    )(x2d, ys, route, g, w_in, cos, sin)


def _retention_kernel(q_ref, k_ref, v_ref, gate_ref, dmask_ref, xi_ref, zeta_ref, gc_ref, o_ref, state_ref):
    n = pl.program_id(2)

    @pl.when(n == 0)
    def _():
        state_ref[...] = jnp.zeros_like(state_ref)

    q = q_ref[...]
    k = k_ref[...]
    v = v_ref[...]
    inner = lax.dot_general(q, k, (((1,), (1,)), ((), ())),
                            preferred_element_type=jnp.float32) * dmask_ref[...]
    o = jnp.dot(inner.astype(jnp.bfloat16), v, preferred_element_type=jnp.float32)
    state = state_ref[...]
    cross = jnp.dot(q, state.astype(jnp.bfloat16), preferred_element_type=jnp.float32)
    o = o + cross * xi_ref[...][:, 0:1]
    kz = (k.astype(jnp.float32) * zeta_ref[...][:, 0:1]).astype(jnp.bfloat16)
    upd = lax.dot_general(kz, v, (((0,), (0,)), ((), ())), preferred_element_type=jnp.float32)
    state_ref[...] = gc_ref[...][:, 0:1] * state + upd
    o = o * _rms_scale(o)
    o_ref[...] = (o * gate_ref[...].astype(jnp.float32)).astype(o_ref.dtype)


def _retention_tables():
    c = RET_C
    log_g = jnp.log(1.0 - jnp.exp2(-5.0 - jnp.arange(RET_HEADS, dtype=jnp.float32)))
    j = jnp.arange(c, dtype=jnp.float32)
    diff = j[:, None] - j[None, :]
    dmask = jnp.where(diff >= 0, jnp.exp(jnp.maximum(diff, 0.0)[None] * log_g[:, None, None]), 0.0)
    xi = jnp.exp((j[None, :] + 1.0) * log_g[:, None])
    zeta = jnp.exp((c - 1.0 - j)[None, :] * log_g[:, None])
    g_c = jnp.exp(c * log_g)
    bc = lambda a: jnp.broadcast_to(a[..., None], a.shape + (LANES,))
    return dmask, bc(xi), bc(zeta), bc(g_c[:, None])


def _retention(q, k, v, gate, batch, seq):
    nc = seq // RET_C
    dmask, xi, zeta, g_c = _retention_tables()
    row = lambda n: pl.BlockSpec((RET_C, n), lambda b, h, c: (b * nc + c, h))
    head = lambda shape: pl.BlockSpec((None,) + shape, lambda b, h, c: (h, 0, 0))
    return pl.pallas_call(
        _retention_kernel,
        grid=(batch, RET_HEADS, nc),
        in_specs=[row(RET_QK), row(RET_QK), row(RET_V), row(RET_V),
                  head((RET_C, RET_C)), head((RET_C, LANES)), head((RET_C, LANES)), head((1, LANES))],
        out_specs=row(RET_V),
        out_shape=jax.ShapeDtypeStruct((batch * seq, RET_HEADS * RET_V), jnp.bfloat16),
        scratch_shapes=[pltpu.VMEM((RET_QK, RET_V), jnp.float32)],
        compiler_params=_cparams(("arbitrary",) * 3),
        name="retention",
    )(q, k, v, gate, dmask, xi, zeta, g_c)


def _final_kernel(x_ref, ys_ref, route_ref, g_ref, o_ref):
    x = _combine(x_ref[...], ys_ref, route_ref, x_ref.shape[0])
    o_ref[...] = x * _rms_scale(x) * g_ref[...]


def _final(x2d, ys, route, g):
    t = x2d.shape[0]
    tm = TM_PROJ
    return pl.pallas_call(
        _final_kernel,
        grid=(t // tm,),
        in_specs=[
            pl.BlockSpec((tm, D_MODEL), lambda i: (i, 0)),
            pl.BlockSpec((tm * TOP_K * ROWS_PER_TOK, LANES), lambda i: (i, 0)),
            pl.BlockSpec((tm, LANES), lambda i: (i, 0)),
            pl.BlockSpec((1, D_MODEL), lambda i: (0, 0)),
        ],
        out_specs=pl.BlockSpec((tm, D_MODEL), lambda i: (i, 0)),
        out_shape=jax.ShapeDtypeStruct((t, D_MODEL), jnp.float32),
        compiler_params=_cparams(("arbitrary",)),
        name="final_norm",
    )(x2d, ys, route, g)


def kernel(x, ln_mix_e, w_in_e, ln_q_e, w_uq_e, ln_kv_e, w_ukv_e, b_f_e, w_out_e, ln_mix_o, w_in_o,
           w_out_o, ln_ffn, w_rg, b_rg, w_re, b_re, w_gate, w_up, w_down, ln_f):
    batch, seq, d = x.shape
    t = batch * seq
    assert d == D_MODEL and seq % TK_BIG == 0 and t % TE == 0
    bf16 = jnp.bfloat16
    x2d = x.reshape(t, d)

    w_in_p, w_uq_p, w_ukv_p, b_f_p = _even_weights(w_in_e[0], w_uq_e[0], w_ukv_e[0], b_f_e[0])
    q, k, v = _even_proj(x2d, seq, ln_mix_e[0][None, :], w_in_p, ln_q_e[0][None, :], w_uq_p,
                         ln_kv_e[0][None, :], w_ukv_p, b_f_p)
    vt = v.reshape(2, batch, seq, MLA_HEADS * MLA_V).transpose(0, 1, 3, 2)
    o_t = _attention(q, k, vt, batch, seq)
    o = o_t.transpose(1, 3, 0, 2).reshape(t, 2 * MLA_HEADS * MLA_V)
    w_r, b_r = _router_weights(w_rg[0], b_rg[0], w_re[0], b_re[0])
    x2d, xg, route = _out_proj(x2d, o, w_out_e[0].astype(bf16), ln_ffn[0][None, :], w_r, b_r)
    ys = _moe(xg, route[:, 0:TOP_K].astype(jnp.int32), w_gate[0], w_up[0], w_down[0], t)

    x2d, rq, rk, rv, rg = _odd_proj(x2d, ys, route, seq, ln_mix_o[0][None, :], w_in_o[0].astype(bf16))
    og = _retention(rq, rk, rv, rg, batch, seq)
    w_r, b_r = _router_weights(w_rg[1], b_rg[1], w_re[1], b_re[1])
    x2d, xg, route = _out_proj(x2d, og, w_out_o[0].astype(bf16), ln_ffn[1][None, :], w_r, b_r)
    ys = _moe(xg, route[:, 0:TOP_K].astype(jnp.int32), w_gate[1], w_up[1], w_down[1], t)

    out = _final(x2d, ys, route, ln_f[None, :])
    return out.reshape(batch, seq, d)
```

```python
import functools
import math

import numpy as np
import jax
import jax.numpy as jnp
from jax import lax
from jax.experimental import pallas as pl
from jax.experimental.pallas import tpu as pltpu

D_MODEL = 1024
CHUNK = 64
ROPE_BASE = 10000.0
EPS = 1e-6
MLA_HEADS = 8
MLA_NOPE = 64
MLA_ROPE = 32
MLA_V = 64
MLA_Q_LORA = 256
MLA_KV_LORA = 128
FOX_HEADS = 8
FOX_DIM = 64
RET_HEADS = 4
RET_QK = 256
RET_V = 512
N_GROUPS = 4
EXPERTS_PER_GROUP = 8
N_EXPERTS = N_GROUPS * EXPERTS_PER_GROUP
TOP_K = 2
D_EXPERT = 512

LANES = 128
SUBLANES = 8
VMEM_LIMIT = 52 * 1024 * 1024

TM_PROJ = 256
TQ = 512
VT_ROWS = 80
TK = 512
LOG2E = math.log2(math.e)
RET_C = 256
TE = 256
ROWS_PER_TOK = D_MODEL // LANES

N_PAIRS = MLA_HEADS // 2
PAIR_W = 256
ROPE_H = MLA_ROPE // 2
NEG = -1e30

_X1_OFF = 0
_X2_OFF = 64
_AUG_W = 6

_C_Q = 0
_C_KV = _C_Q + MLA_Q_LORA
_C_KPE = _C_KV + MLA_KV_LORA
_C_QF = _C_KPE + LANES
_C_KF = _C_QF + FOX_HEADS * FOX_DIM
_C_VF = _C_KF + FOX_HEADS * FOX_DIM
_C_F = _C_VF + FOX_HEADS * FOX_DIM
EVEN_N = _C_F + LANES


def _cparams(sem):
    return pltpu.CompilerParams(dimension_semantics=sem, vmem_limit_bytes=VMEM_LIMIT)


def _rms_scale(x):
    return lax.rsqrt(jnp.mean(x * x, axis=-1, keepdims=True) + EPS)


def _split3(c):
    hi = c.astype(jnp.bfloat16)
    r1 = c - hi.astype(jnp.float32)
    mid = r1.astype(jnp.bfloat16)
    r2 = r1 - mid.astype(jnp.float32)
    lo = r2.astype(jnp.bfloat16)
    return hi, mid, lo


def _even_proj_kernel(tiles_per_seq, x_ref, g_ref, win_ref, lnq_ref, wuq_ref, lnkv_ref, wukv_ref,
                      bf_ref, cos_ref, sin_ref, tri_ref, pq_ref, pk_ref, oq_ref, ok_ref,
                      q_ref, k_ref, v_ref, carry_ref):
    i = pl.program_id(0)
    x = x_ref[...]
    xn = (x * _rms_scale(x) * g_ref[...]).astype(jnp.bfloat16)
    z = jnp.dot(xn, win_ref[...], preferred_element_type=jnp.float32)
    cos = cos_ref[...]
    sin = sin_ref[...]

    def rope_slab(s):
        return s * cos + pltpu.roll(s, 64, axis=1) * sin

    c_q = z[:, _C_Q:_C_Q + MLA_Q_LORA]
    cqn = (c_q * _rms_scale(c_q) * lnq_ref[...]).astype(jnp.bfloat16)
    q = jnp.dot(cqn, wuq_ref[...], preferred_element_type=jnp.float32)
    scale_a = (MLA_NOPE + MLA_ROPE) ** -0.5 * LOG2E
    for p in range(N_PAIRS):
        lo = p * PAIR_W
        q_ref[0, :, lo:lo + LANES] = (q[:, lo:lo + LANES] * scale_a).astype(jnp.bfloat16)
        q_ref[0, :, lo + LANES:lo + PAIR_W] = (
            rope_slab(q[:, lo + LANES:lo + PAIR_W]) * scale_a).astype(jnp.bfloat16)

    c_kv = z[:, _C_KV:_C_KV + MLA_KV_LORA]
    ckn = (c_kv * _rms_scale(c_kv) * lnkv_ref[...]).astype(jnp.bfloat16)
    kv = jnp.dot(ckn, wukv_ref[...], preferred_element_type=jnp.float32)
    kpe = rope_slab(z[:, _C_KPE:_C_KPE + LANES]).astype(jnp.bfloat16)
    nk = MLA_HEADS * MLA_NOPE
    for p in range(N_PAIRS):
        lo = p * PAIR_W
        k_ref[0, :, lo:lo + LANES] = kv[:, p * LANES:(p + 1) * LANES].astype(jnp.bfloat16)
        k_ref[0, :, lo + LANES:lo + PAIR_W] = kpe
    v_ref[0] = kv[:, nk:nk + MLA_HEADS * MLA_V].astype(jnp.bfloat16)

    lane = lax.broadcasted_iota(jnp.int32, (1, LANES), 1)
    fz = z[:, _C_F:_C_F + LANES] + bf_ref[...]
    log_f = -(jnp.maximum(-fz, 0.0) + jnp.log1p(jnp.exp(-jnp.abs(fz))))
    log_f = jnp.where(lane < FOX_HEADS, log_f, 0.0)

    @pl.when(i % tiles_per_seq == 0)
    def _():
        carry_ref[...] = jnp.zeros_like(carry_ref)

    hi, mid, lo3 = _split3(log_f)
    tri = tri_ref[...]
    cum = (jnp.dot(tri, hi, preferred_element_type=jnp.float32)
           + jnp.dot(tri, mid, preferred_element_type=jnp.float32)
           + jnp.dot(tri, lo3, preferred_element_type=jnp.float32)) + carry_ref[...]
    tm = cum.shape[0]
    carry_ref[...] = cum[tm - 1:tm, :]

    parts = jnp.concatenate(_split3(cum * LOG2E), axis=1)
    aug_q = jnp.dot(parts, pq_ref[...], preferred_element_type=jnp.float32) + oq_ref[...]
    aug_k = jnp.dot(parts, pk_ref[...], preferred_element_type=jnp.float32) + ok_ref[...]
    scale_b = FOX_DIM ** -0.5 * LOG2E
    for p in range(N_PAIRS):
        lo = p * PAIR_W
        q_ref[1, :, lo:lo + LANES] = (
            z[:, _C_QF + p * LANES:_C_QF + (p + 1) * LANES] * scale_b).astype(jnp.bfloat16)
        q_ref[1, :, lo + LANES:lo + PAIR_W] = aug_q[:, p * LANES:(p + 1) * LANES].astype(jnp.bfloat16)
        k_ref[1, :, lo:lo + LANES] = z[:, _C_KF + p * LANES:_C_KF + (p + 1) * LANES].astype(jnp.bfloat16)
        k_ref[1, :, lo + LANES:lo + PAIR_W] = aug_k[:, p * LANES:(p + 1) * LANES].astype(jnp.bfloat16)
    v_ref[1] = z[:, _C_VF:_C_VF + FOX_HEADS * FOX_DIM].astype(jnp.bfloat16)


def _even_weights(w_in, w_uq, w_ukv, b_f):
    d = w_in.shape[0]
    zeros = lambda n: jnp.zeros((d, n), w_in.dtype)
    o_cq, o_ckv = 0, MLA_Q_LORA
    o_kpe = o_ckv + MLA_KV_LORA
    o_qf = o_kpe + MLA_ROPE
    o_kf = o_qf + FOX_HEADS * FOX_DIM
    o_vf = o_kf + FOX_HEADS * FOX_DIM
    o_f = o_vf + FOX_HEADS * FOX_DIM
    kpe1 = w_in[:, o_kpe:o_kpe + ROPE_H]
    kpe2 = w_in[:, o_kpe + ROPE_H:o_kpe + MLA_ROPE]
    kpe_slab = jnp.concatenate([kpe1, kpe1, zeros(32), kpe2, kpe2, zeros(32)], axis=1)
    f_slab = jnp.concatenate([w_in[:, o_f:o_f + FOX_HEADS], zeros(LANES - FOX_HEADS)], axis=1)
    w_in_p = jnp.concatenate([w_in[:, o_cq:o_kpe], kpe_slab, w_in[:, o_qf:o_f], f_slab], axis=1)

    dq = w_uq.shape[0]
    zq = lambda n: jnp.zeros((dq, n), w_uq.dtype)
    hd = MLA_NOPE + MLA_ROPE
    blocks = []
    for p in range(N_PAIRS):
        h0, h1 = 2 * p, 2 * p + 1
        nope = lambda h: w_uq[:, h * hd:h * hd + MLA_NOPE]
        r1 = lambda h: w_uq[:, h * hd + MLA_NOPE:h * hd + MLA_NOPE + ROPE_H]
        r2 = lambda h: w_uq[:, h * hd + MLA_NOPE + ROPE_H:(h + 1) * hd]
        blocks += [nope(h0), nope(h1), r1(h0), r1(h1), zq(32), r2(h0), r2(h1), zq(32)]
    w_uq_p = jnp.concatenate(blocks, axis=1)

    kvd = MLA_NOPE + MLA_V
    k_cols = [w_ukv[:, h * kvd:h * kvd + MLA_NOPE] for h in range(MLA_HEADS)]
    v_cols = [w_ukv[:, h * kvd + MLA_NOPE:(h + 1) * kvd] for h in range(MLA_HEADS)]
    w_ukv_p = jnp.concatenate(k_cols + v_cols, axis=1)
    b_f_p = jnp.concatenate([b_f, jnp.zeros((LANES - FOX_HEADS,), b_f.dtype)])[None, :]
    bf16 = jnp.bfloat16
    return w_in_p.astype(bf16), w_uq_p.astype(bf16), w_ukv_p.astype(bf16), b_f_p


def _aug_placement():
    pq = np.zeros((3 * LANES, N_PAIRS * LANES), np.float32)
    pk = np.zeros((3 * LANES, N_PAIRS * LANES), np.float32)
    oq = np.zeros((1, N_PAIRS * LANES), np.float32)
    ok = np.zeros((1, N_PAIRS * LANES), np.float32)
    for p in range(N_PAIRS):
        for j in range(2):
            h = 2 * p + j
            base = p * LANES + _X1_OFF + j * ROPE_H
            for t in range(3):
                pq[t * LANES + h, base + 3 + t] = 1.0
                pk[t * LANES + h, base + t] = -1.0
                oq[0, base + t] = 1.0
                ok[0, base + 3 + t] = 1.0
    return (jnp.asarray(pq, jnp.bfloat16), jnp.asarray(pk, jnp.bfloat16),
            jnp.asarray(oq), jnp.asarray(ok))


def _mla_rope_tables(s):
    inv = ROPE_BASE ** (-jnp.arange(ROPE_H, dtype=jnp.float32) / ROPE_H)
    ang = jnp.arange(s, dtype=jnp.float32)[:, None] * inv[None, :]
    c, sn = jnp.cos(ang), jnp.sin(ang)
    z = jnp.zeros((s, 32), jnp.float32)
    cos = jnp.concatenate([c, c, z, c, c, z], axis=1)
    sin = jnp.concatenate([-sn, -sn, z, sn, sn, z], axis=1)
    return cos, sin


def _even_proj(x2d, seq, g, w_in_p, ln_q, w_uq_p, ln_kv, w_ukv_p, b_f_p):
    t = x2d.shape[0]
    tm = TM_PROJ
    tiles_per_seq = seq // tm
    cos, sin = _mla_rope_tables(seq)
    tri = jnp.asarray(np.tril(np.ones((tm, tm), np.float32)), jnp.bfloat16)
    pq, pk, oq, ok = _aug_placement()
    const = lambda shape: pl.BlockSpec(shape, lambda i: (0,) * len(shape))
    return pl.pallas_call(
        functools.partial(_even_proj_kernel, tiles_per_seq),
        grid=(t // tm,),
        in_specs=[
            pl.BlockSpec((tm, D_MODEL), lambda i: (i, 0)),
            const((1, D_MODEL)), const((D_MODEL, EVEN_N)),
            const((1, MLA_Q_LORA)), const((MLA_Q_LORA, N_PAIRS * PAIR_W)),
            const((1, MLA_KV_LORA)), const((MLA_KV_LORA, 2 * MLA_HEADS * MLA_NOPE)),
            const((1, LANES)),
            pl.BlockSpec((tm, LANES), lambda i: (i % tiles_per_seq, 0)),
            pl.BlockSpec((tm, LANES), lambda i: (i % tiles_per_seq, 0)),
            const((tm, tm)), const(pq.shape), const(pk.shape), const(oq.shape), const(ok.shape),
        ],
        out_specs=[
            pl.BlockSpec((2, tm, N_PAIRS * PAIR_W), lambda i: (0, i, 0)),
            pl.BlockSpec((2, tm, N_PAIRS * PAIR_W), lambda i: (0, i, 0)),
            pl.BlockSpec((2, tm, MLA_HEADS * MLA_V), lambda i: (0, i, 0)),
        ],
        out_shape=[
            jax.ShapeDtypeStruct((2, t, N_PAIRS * PAIR_W), jnp.bfloat16),
            jax.ShapeDtypeStruct((2, t, N_PAIRS * PAIR_W), jnp.bfloat16),
            jax.ShapeDtypeStruct((2, t, MLA_HEADS * MLA_V), jnp.bfloat16),
        ],
        scratch_shapes=[pltpu.VMEM((1, LANES), jnp.float32)],
        compiler_params=_cparams(("arbitrary",)),
        name="even_proj",
    )(x2d, g, w_in_p, ln_q, w_uq_p, ln_kv, w_ukv_p, b_f_p, cos, sin, tri, pq, pk, oq, ok)


def _attn_kernel(q_ref, k_ref, vt_ref, o_ref, qcat_ref, s0_ref, s1_ref, m0_ref, m1_ref,
                 a0_ref, a1_ref, acc_ref):
    g = pl.program_id(0)
    i = pl.program_id(3)
    tq = q_ref.shape[0]
    s_bufs, m_bufs, a_bufs = (s0_ref, s1_ref), (m0_ref, m1_ref), (a0_ref, a1_ref)

    lane = lax.broadcasted_iota(jnp.int32, (1, PAIR_W), 1)

    def head_mask(j):
        a = (lane >= j * MLA_NOPE) & (lane < (j + 1) * MLA_NOPE)
        b = (lane >= LANES + _X1_OFF + j * ROPE_H) & (lane < LANES + _X1_OFF + (j + 1) * ROPE_H)
        c = (lane >= LANES + _X2_OFF + j * ROPE_H) & (lane < LANES + _X2_OFF + (j + 1) * ROPE_H)
        return a | b | c

    q = q_ref[...]
    zero = jnp.zeros_like(q)
    qcat_ref[0:tq, :] = jnp.where(head_mask(0), q, zero)
    qcat_ref[tq:2 * tq, :] = jnp.where(head_mask(1), q, zero)
    acc_ref[...] = jnp.zeros_like(acc_ref)

    q0 = i * tq
    n_chunks = q0 // TK + 1

    def scores(c, m_run, par, masked):
        kstart = pl.multiple_of(c * TK, TK)
        s = lax.dot_general(k_ref[pl.ds(kstart, TK), :], qcat_ref[...], (((1,), (1,)), ((), ())),
                            preferred_element_type=jnp.float32)
        if masked:
            col = lax.broadcasted_iota(jnp.int32, (1, 2 * tq), 1)
            qpos = q0 + jnp.where(col >= tq, col - tq, col)
            qlim = jnp.where(g == 0, qpos | (CHUNK - 1), qpos)
            kpos = kstart + lax.broadcasted_iota(jnp.int32, (TK, 1), 0)
            s = jnp.where(kpos <= qlim, s, NEG)
        s_bufs[par][...] = s
        m_new = jnp.maximum(m_run, jnp.max(s, axis=0, keepdims=True))
        m_bufs[par][...] = m_new
        a_bufs[par][...] = jnp.exp2(m_run - m_new)

    def accumulate(c, par):
        kstart = pl.multiple_of(c * TK, TK)
        alpha = a_bufs[par][...]
        p = jnp.exp2(s_bufs[par][...] - m_bufs[par][...]).astype(jnp.bfloat16)
        for h in range(2):
            cols = slice(h * tq, (h + 1) * tq)
            acc_ref[h] = alpha[:, cols] * acc_ref[h] + jnp.dot(
                vt_ref[h * VT_ROWS:(h + 1) * VT_ROWS, pl.ds(kstart, TK)], p[:, cols],
                preferred_element_type=jnp.float32)

    def stage(c, par, masked):
        scores(c + 1, m_bufs[par][...], 1 - par, masked)
        accumulate(c, par)

    m_init = jnp.full((1, 2 * tq), NEG, jnp.float32)

    @pl.when(n_chunks == 1)
    def _():
        scores(0, m_init, 0, True)
        accumulate(0, 0)

    @pl.when(n_chunks > 1)
    def _():
        scores(0, m_init, 0, False)

    n_pairs = jnp.maximum(n_chunks - 2, 0) // 2

    def body(j, carry):
        stage(2 * j, 0, False)
        stage(2 * j + 1, 1, False)
        return carry

    lax.fori_loop(0, n_pairs, body, 0)
    c0 = 2 * n_pairs
    left = n_chunks - 1 - c0

    @pl.when((n_chunks > 1) & (left == 1))
    def _():
        stage(c0, 0, True)
        accumulate(c0 + 1, 1)

    @pl.when((n_chunks > 1) & (left == 2))
    def _():
        stage(c0, 0, False)
        stage(c0 + 1, 1, True)
        accumulate(c0 + 2, 0)

    for h in range(2):
        acc = acc_ref[h]
        o_ref[h * MLA_V:(h + 1) * MLA_V, :] = (acc[0:MLA_V] / acc[MLA_V:MLA_V + 1]).astype(o_ref.dtype)


def _values_transposed(v, batch, seq):
    vt = v.reshape(2, batch, seq, MLA_HEADS, MLA_V).transpose(0, 1, 3, 4, 2)
    ones = jnp.ones((2, batch, MLA_HEADS, VT_ROWS - MLA_V, seq), v.dtype)
    return jnp.concatenate([vt, ones], axis=3).reshape(2, batch, MLA_HEADS * VT_ROWS, seq)


def _attention(q, k, vt, batch, seq):
    nq = seq // TQ
    f32 = jnp.float32
    return pl.pallas_call(
        _attn_kernel,
        grid=(2, batch, N_PAIRS, nq),
        in_specs=[
            pl.BlockSpec((None, TQ, PAIR_W), lambda g, b, p, i: (g, b * nq + i, p)),
            pl.BlockSpec((None, seq, PAIR_W), lambda g, b, p, i: (g, b, p)),
            pl.BlockSpec((None, None, 2 * VT_ROWS, seq), lambda g, b, p, i: (g, b, p, 0)),
        ],
        out_specs=pl.BlockSpec((None, None, LANES, TQ), lambda g, b, p, i: (g, b, p, i)),
        out_shape=jax.ShapeDtypeStruct((2, batch, N_PAIRS * LANES, seq), jnp.bfloat16),
        scratch_shapes=[pltpu.VMEM((2 * TQ, PAIR_W), jnp.bfloat16),
                        pltpu.VMEM((TK, 2 * TQ), f32), pltpu.VMEM((TK, 2 * TQ), f32),
                        pltpu.VMEM((1, 2 * TQ), f32), pltpu.VMEM((1, 2 * TQ), f32),
                        pltpu.VMEM((1, 2 * TQ), f32), pltpu.VMEM((1, 2 * TQ), f32),
                        pltpu.VMEM((2, VT_ROWS, TQ), f32)],
        compiler_params=_cparams(("arbitrary",) * 4),
        name="attention",
    )(q, k, vt)


def _router(logits):
    lane = lax.broadcasted_iota(jnp.int32, logits.shape, 1)
    big = jnp.int32(LANES)
    ninf = -jnp.inf
    gl = jnp.where(lane < N_GROUPS, logits, ninf)
    gmax = jnp.max(gl, axis=1, keepdims=True)
    gsel = jnp.min(jnp.where(gl == gmax, lane, big), axis=1, keepdims=True)
    gsum = jnp.sum(jnp.where(lane < N_GROUPS, jnp.exp(logits - gmax), 0.0), axis=1, keepdims=True)
    p_g = 1.0 / gsum
    lo = N_GROUPS + EXPERTS_PER_GROUP * gsel
    e = jnp.where((lane >= lo) & (lane < lo + EXPERTS_PER_GROUP), logits, ninf)
    m1 = jnp.max(e, axis=1, keepdims=True)
    i1 = jnp.min(jnp.where(e == m1, lane, big), axis=1, keepdims=True)
    e2 = jnp.where(lane == i1, ninf, e)
    m2 = jnp.max(e2, axis=1, keepdims=True)
    i2 = jnp.min(jnp.where(e2 == m2, lane, big), axis=1, keepdims=True)
    a2 = jnp.exp(m2 - m1)
    den = 1.0 + a2
    g1 = p_g / den
    g2 = p_g * a2 / den
    f = lambda v: v.astype(jnp.float32)
    return jnp.where(lane == 0, f(i1 - N_GROUPS),
                     jnp.where(lane == 1, f(i2 - N_GROUPS),
                               jnp.where(lane == 2, g1, jnp.where(lane == 3, g2, 0.0))))


def _out_proj_kernel(x_ref, o_ref, w_ref, ln_ref, wr_ref, br_ref, xo_ref, xg_ref, route_ref):
    tm = x_ref.shape[0]
    y = x_ref[...] + jnp.dot(o_ref[...], w_ref[...], preferred_element_type=jnp.float32)
    xo_ref[...] = y
    xn = y * _rms_scale(y) * ln_ref[...]
    logits = jnp.dot(xn, wr_ref[...], preferred_element_type=jnp.float32,
                     precision=lax.Precision.HIGHEST) + br_ref[...]
    route_ref[...] = _router(logits)
    for c in range(ROWS_PER_TOK):
        xg_ref[pl.ds(c, tm, stride=ROWS_PER_TOK), :] = xn[:, c * LANES:(c + 1) * LANES]


def _out_proj(x2d, o, w_out, ln, w_r, b_r):
    t = x2d.shape[0]
    kdim = o.shape[1]
    tm = TM_PROJ
    const = lambda shape: pl.BlockSpec(shape, lambda i: (0,) * len(shape))
    return pl.pallas_call(
        _out_proj_kernel,
        grid=(t // tm,),
        in_specs=[
            pl.BlockSpec((tm, D_MODEL), lambda i: (i, 0)),
            pl.BlockSpec((tm, kdim), lambda i: (i, 0)),
            const((kdim, D_MODEL)), const((1, D_MODEL)), const((D_MODEL, LANES)), const((1, LANES)),
        ],
        out_specs=[
            pl.BlockSpec((tm, D_MODEL), lambda i: (i, 0)),
            pl.BlockSpec((tm * ROWS_PER_TOK, LANES), lambda i: (i, 0)),
            pl.BlockSpec((tm, LANES), lambda i: (i, 0)),
        ],
        out_shape=[
            jax.ShapeDtypeStruct((t, D_MODEL), jnp.float32),
            jax.ShapeDtypeStruct((t * ROWS_PER_TOK, LANES), jnp.float32),
            jax.ShapeDtypeStruct((t, LANES), jnp.float32),
        ],
        compiler_params=_cparams(("arbitrary",)),
        name="out_proj_router",
    )(x2d, o, w_out, ln, w_r, b_r)


def _router_weights(w_rg, b_rg, w_re, b_re):
    d = w_rg.shape[0]
    pad = LANES - N_GROUPS - N_EXPERTS
    w_r = jnp.concatenate([w_rg, w_re, jnp.zeros((d, pad), w_rg.dtype)], axis=1)
    b_r = jnp.concatenate([b_rg, b_re, jnp.zeros((pad,), b_rg.dtype)])[None, :]
    return w_r, b_r


def _dispatch_plan(eid, t):
    a = t * TOP_K
    nb = a // TE + N_EXPERTS
    e_flat = eid.reshape(a)
    order = jnp.argsort(e_flat).astype(jnp.int32)
    e_s = e_flat[order]
    bounds = jnp.searchsorted(e_s, jnp.arange(N_EXPERTS + 1, dtype=jnp.int32), side="left").astype(jnp.int32)
    start = bounds[:-1]
    counts = bounds[1:] - start
    padded = ((counts + TE - 1) // TE) * TE
    pend = jnp.cumsum(padded)
    pstart = pend - padded
    blk0 = jnp.arange(nb, dtype=jnp.int32) * TE
    blk_e = jnp.minimum(jnp.searchsorted(pend, blk0, side="right"), N_EXPERTS - 1).astype(jnp.int32)
    blk_used = (blk0 < pend[-1]).astype(jnp.int32)
    r = jnp.arange(TE, dtype=jnp.int32)[None, :]
    off = blk0[:, None] + r - pstart[blk_e][:, None]
    valid = (off < counts[blk_e][:, None]) & (blk_used[:, None] > 0)
    src = jnp.where(valid, start[blk_e][:, None] + off, 0)
    assign = order[src]
    src_tok = jnp.where(valid, assign // TOP_K, 0).astype(jnp.int32)
    dst_row = jnp.where(valid, assign, a + r).astype(jnp.int32)
    return blk_e, blk_used, src_tok[:, None, :], dst_row[:, None, :]


def _moe_kernel(blk_e_ref, blk_used_ref, src_ref, src_next_ref, dst_ref, xg_ref, wg_ref, wu_ref, wd_ref,
                ys_ref, xbuf, ybuf, wg_bf, wu_bf, wd_bf, gsem, ssem):
    i = pl.program_id(0)
    nb = pl.num_programs(0)
    used = blk_used_ref[i] > 0
    slot = i % 2

    def gather_row_copy(idx_ref, r, s):
        tok = idx_ref[0, 0, r]
        return pltpu.make_async_copy(
            xg_ref.at[pl.ds(pl.multiple_of(tok * ROWS_PER_TOK, ROWS_PER_TOK), ROWS_PER_TOK), :],
            xbuf.at[s, pl.ds(pl.multiple_of(r * ROWS_PER_TOK, ROWS_PER_TOK), ROWS_PER_TOK), :],
            gsem.at[s])

    def start_gather(idx_ref, s):
        def body(r, carry):
            gather_row_copy(idx_ref, r, s).start()
            return carry
        lax.fori_loop(0, TE, body, 0)

    def scatter_row_copy(r):
        row = dst_ref[0, 0, r]
        return pltpu.make_async_copy(
            ybuf.at[pl.ds(pl.multiple_of(r * ROWS_PER_TOK, ROWS_PER_TOK), ROWS_PER_TOK), :],
            ys_ref.at[pl.ds(pl.multiple_of(row * ROWS_PER_TOK, ROWS_PER_TOK), ROWS_PER_TOK), :],
            ssem.at[0])

    def wait_scatter():
        pltpu.make_async_copy(ybuf, ys_ref.at[pl.ds(0, TE * ROWS_PER_TOK), :], ssem.at[0]).wait()

    @pl.when(i == 0)
    def _():
        ybuf[...] = jnp.zeros_like(ybuf)
        spare = pltpu.make_async_copy(
            ybuf, ys_ref.at[pl.ds(ys_ref.shape[0] - TE * ROWS_PER_TOK, TE * ROWS_PER_TOK), :], ssem.at[0])
        spare.start()
        spare.wait()

    @pl.when((i == 0) & used)
    def _():
        start_gather(src_ref, 0)

    next_used = blk_used_ref[jnp.minimum(i + 1, nb - 1)] > 0

    @pl.when((i + 1 < nb) & next_used)
    def _():
        start_gather(src_next_ref, 1 - slot)

    @pl.when(used)
    def _():
        prev_e = blk_e_ref[jnp.maximum(i - 1, 0)]

        @pl.when((i == 0) | (blk_e_ref[i] != prev_e))
        def _():
            wg_bf[...] = wg_ref[...].astype(jnp.bfloat16)
            wu_bf[...] = wu_ref[...].astype(jnp.bfloat16)
            wd_bf[...] = wd_ref[...].astype(jnp.bfloat16)

        pltpu.make_async_copy(xg_ref.at[pl.ds(0, TE * ROWS_PER_TOK), :], xbuf.at[slot], gsem.at[slot]).wait()
        xb = xbuf.at[slot]
        x = jnp.concatenate([xb[pl.ds(c, TE, stride=ROWS_PER_TOK), :] for c in range(ROWS_PER_TOK)],
                            axis=1).astype(jnp.bfloat16)
        hg = jnp.dot(x, wg_bf[...], preferred_element_type=jnp.float32)
        hu = jnp.dot(x, wu_bf[...], preferred_element_type=jnp.float32)
        hid = (hg * jax.nn.sigmoid(hg) * hu).astype(jnp.bfloat16)
        y = jnp.dot(hid, wd_bf[...], preferred_element_type=jnp.float32)

        @pl.when(i > 0)
        def _():
            wait_scatter()

        for c in range(ROWS_PER_TOK):
            ybuf[pl.ds(c, TE, stride=ROWS_PER_TOK), :] = y[:, c * LANES:(c + 1) * LANES]

        def body(r, carry):
            scatter_row_copy(r).start()
            return carry
        lax.fori_loop(0, TE, body, 0)

        @pl.when((i + 1 >= nb) | jnp.logical_not(next_used))
        def _():
            wait_scatter()


def _moe(xg, eid, w_gate, w_up, w_down, t):
    a = t * TOP_K
    nb = a // TE + N_EXPERTS
    blk_e, blk_used, src_tok, dst_row = _dispatch_plan(eid, t)
    idx_spec = lambda f: pl.BlockSpec((1, 1, TE), f, memory_space=pltpu.SMEM)
    grid_spec = pltpu.PrefetchScalarGridSpec(
        num_scalar_prefetch=2,
        grid=(nb,),
        in_specs=[
            idx_spec(lambda i, be, bu: (i, 0, 0)),
            idx_spec(lambda i, be, bu: (jnp.minimum(i + 1, nb - 1), 0, 0)),
            idx_spec(lambda i, be, bu: (i, 0, 0)),
            pl.BlockSpec(memory_space=pl.ANY),
            pl.BlockSpec((None, D_MODEL, D_EXPERT), lambda i, be, bu: (be[i], 0, 0)),
            pl.BlockSpec((None, D_MODEL, D_EXPERT), lambda i, be, bu: (be[i], 0, 0)),
            pl.BlockSpec((None, D_EXPERT, D_MODEL), lambda i, be, bu: (be[i], 0, 0)),
        ],
        out_specs=pl.BlockSpec(memory_space=pl.ANY),
        scratch_shapes=[
            pltpu.VMEM((2, TE * ROWS_PER_TOK, LANES), jnp.float32),
            pltpu.VMEM((TE * ROWS_PER_TOK, LANES), jnp.float32),
            pltpu.VMEM((D_MODEL, D_EXPERT), jnp.bfloat16),
            pltpu.VMEM((D_MODEL, D_EXPERT), jnp.bfloat16),
            pltpu.VMEM((D_EXPERT, D_MODEL), jnp.bfloat16),
            pltpu.SemaphoreType.DMA((2,)),
            pltpu.SemaphoreType.DMA((1,)),
        ],
    )
    return pl.pallas_call(
        _moe_kernel,
        grid_spec=grid_spec,
        out_shape=jax.ShapeDtypeStruct(((a + TE) * ROWS_PER_TOK, LANES), jnp.float32),
        compiler_params=_cparams(("arbitrary",)),
        name="moe_experts",
    )(blk_e, blk_used, src_tok, src_tok, dst_row, xg, w_gate, w_up, w_down)


def _combine(x, ys_ref, route_ref, tm):
    stride = TOP_K * ROWS_PER_TOK
    y0 = jnp.concatenate([ys_ref[pl.ds(c, tm, stride=stride), :] for c in range(ROWS_PER_TOK)], axis=1)
    y1 = jnp.concatenate([ys_ref[pl.ds(ROWS_PER_TOK + c, tm, stride=stride), :]
                          for c in range(ROWS_PER_TOK)], axis=1)
    route = route_ref[...]
    return x + route[:, 2:3] * y0 + route[:, 3:4] * y1


def _odd_proj_kernel(x_ref, ys_ref, route_ref, g_ref, w_ref, cos_ref, sin_ref,
                     xo_ref, q_ref, k_ref, v_ref, gate_ref):
    tm = x_ref.shape[0]
    x = _combine(x_ref[...], ys_ref, route_ref, tm)
    xo_ref[...] = x
    xn = (x * _rms_scale(x) * g_ref[...]).astype(jnp.bfloat16)
    cos = cos_ref[...]
    sin = sin_ref[...]
    nqk = RET_HEADS * RET_QK
    half = RET_QK // 2

    def rope_store(z, dst, scale):
        for h in range(RET_HEADS):
            x1 = z[:, h * RET_QK:h * RET_QK + half]
            x2 = z[:, h * RET_QK + half:(h + 1) * RET_QK]
            dst[:, h * RET_QK:h * RET_QK + half] = ((x1 * cos - x2 * sin) * scale).astype(dst.dtype)
            dst[:, h * RET_QK + half:(h + 1) * RET_QK] = ((x1 * sin + x2 * cos) * scale).astype(dst.dtype)

    zq = jnp.dot(xn, w_ref[:, 0:nqk], preferred_element_type=jnp.float32)
    rope_store(zq, q_ref, 1.0)
    zk = jnp.dot(xn, w_ref[:, nqk:2 * nqk], preferred_element_type=jnp.float32)
    rope_store(zk, k_ref, RET_QK ** -0.5)
    nv = RET_HEADS * RET_V
    zv = jnp.dot(xn, w_ref[:, 2 * nqk:2 * nqk + nv], preferred_element_type=jnp.float32)
    v_ref[...] = zv.astype(v_ref.dtype)
    zg = jnp.dot(xn, w_ref[:, 2 * nqk + nv:2 * nqk + 2 * nv], preferred_element_type=jnp.float32)
    gate_ref[...] = (zg * jax.nn.sigmoid(zg)).astype(gate_ref.dtype)


def _ret_rope_tables(s):
    half = RET_QK // 2
    inv = ROPE_BASE ** (-jnp.arange(half, dtype=jnp.float32) / half)
    ang = jnp.arange(s, dtype=jnp.float32)[:, None] * inv[None, :]
    return jnp.cos(ang), jnp.sin(ang)


def _odd_proj(x2d, ys, route, seq, g, w_in):
    t = x2d.shape[0]
    tm = TM_PROJ
    tiles_per_seq = seq // tm
    cos, sin = _ret_rope_tables(seq)
    nqk = RET_HEADS * RET_QK
    nv = RET_HEADS * RET_V
    n_in = w_in.shape[1]
    const = lambda shape: pl.BlockSpec(shape, lambda i: (0,) * len(shape))
    tok = lambda n: pl.BlockSpec((tm, n), lambda i: (i, 0))
    return pl.pallas_call(
        _odd_proj_kernel,
        grid=(t // tm,),
        in_specs=[
            tok(D_MODEL),
            pl.BlockSpec((tm * TOP_K * ROWS_PER_TOK, LANES), lambda i: (i, 0)),
            tok(LANES),
            const((1, D_MODEL)), const((D_MODEL, n_in)),
            pl.BlockSpec((tm, LANES), lambda i: (i % tiles_per_seq, 0)),
            pl.BlockSpec((tm, LANES), lambda i: (i % tiles_per_seq, 0)),
        ],
        out_specs=[tok(D_MODEL), tok(nqk), tok(nqk), tok(nv), tok(nv)],
        out_shape=[
            jax.ShapeDtypeStruct((t, D_MODEL), jnp.float32),
            jax.ShapeDtypeStruct((t, nqk), jnp.bfloat16),
            jax.ShapeDtypeStruct((t, nqk), jnp.bfloat16),
            jax.ShapeDtypeStruct((t, nv), jnp.bfloat16),
            jax.ShapeDtypeStruct((t, nv), jnp.bfloat16),
        ],
        compiler_params=_cparams(("arbitrary",)),
        name="odd_proj",
    )(x2d, ys, route, g, w_in, cos, sin)


def _retention_kernel(q_ref, k_ref, v_ref, gate_ref, dmask_ref, xi_ref, zeta_ref, gc_ref, o_ref, state_ref):
    n = pl.program_id(2)

    @pl.when(n == 0)
    def _():
        state_ref[...] = jnp.zeros_like(state_ref)

    q = q_ref[...]
    k = k_ref[...]
    v = v_ref[...]
    inner = lax.dot_general(q, k, (((1,), (1,)), ((), ())),
                            preferred_element_type=jnp.float32) * dmask_ref[...]
    o = jnp.dot(inner.astype(jnp.bfloat16), v, preferred_element_type=jnp.float32)
    state = state_ref[...]
    cross = jnp.dot(q, state.astype(jnp.bfloat16), preferred_element_type=jnp.float32)
    o = o + cross * xi_ref[...][:, 0:1]
    kz = (k.astype(jnp.float32) * zeta_ref[...][:, 0:1]).astype(jnp.bfloat16)
    upd = lax.dot_general(kz, v, (((0,), (0,)), ((), ())), preferred_element_type=jnp.float32)
    state_ref[...] = gc_ref[...][:, 0:1] * state + upd
    o = o * _rms_scale(o)
    o_ref[...] = (o * gate_ref[...].astype(jnp.float32)).astype(o_ref.dtype)


def _retention_tables():
    c = RET_C
    log_g = jnp.log(1.0 - jnp.exp2(-5.0 - jnp.arange(RET_HEADS, dtype=jnp.float32)))
    j = jnp.arange(c, dtype=jnp.float32)
    diff = j[:, None] - j[None, :]
    dmask = jnp.where(diff >= 0, jnp.exp(jnp.maximum(diff, 0.0)[None] * log_g[:, None, None]), 0.0)
    xi = jnp.exp((j[None, :] + 1.0) * log_g[:, None])
    zeta = jnp.exp((c - 1.0 - j)[None, :] * log_g[:, None])
    g_c = jnp.exp(c * log_g)
    bc = lambda a: jnp.broadcast_to(a[..., None], a.shape + (LANES,))
    return dmask, bc(xi), bc(zeta), bc(g_c[:, None])


def _retention(q, k, v, gate, batch, seq):
    nc = seq // RET_C
    dmask, xi, zeta, g_c = _retention_tables()
    row = lambda n: pl.BlockSpec((RET_C, n), lambda b, h, c: (b * nc + c, h))
    head = lambda shape: pl.BlockSpec((None,) + shape, lambda b, h, c: (h, 0, 0))
    return pl.pallas_call(
        _retention_kernel,
        grid=(batch, RET_HEADS, nc),
        in_specs=[row(RET_QK), row(RET_QK), row(RET_V), row(RET_V),
                  head((RET_C, RET_C)), head((RET_C, LANES)), head((RET_C, LANES)), head((1, LANES))],
        out_specs=row(RET_V),
        out_shape=jax.ShapeDtypeStruct((batch * seq, RET_HEADS * RET_V), jnp.bfloat16),
        scratch_shapes=[pltpu.VMEM((RET_QK, RET_V), jnp.float32)],
        compiler_params=_cparams(("arbitrary",) * 3),
        name="retention",
    )(q, k, v, gate, dmask, xi, zeta, g_c)


def _final_kernel(x_ref, ys_ref, route_ref, g_ref, o_ref):
    x = _combine(x_ref[...], ys_ref, route_ref, x_ref.shape[0])
    o_ref[...] = x * _rms_scale(x) * g_ref[...]


def _final(x2d, ys, route, g):
    t = x2d.shape[0]
    tm = TM_PROJ
    return pl.pallas_call(
        _final_kernel,
        grid=(t // tm,),
        in_specs=[
            pl.BlockSpec((tm, D_MODEL), lambda i: (i, 0)),
            pl.BlockSpec((tm * TOP_K * ROWS_PER_TOK, LANES), lambda i: (i, 0)),
            pl.BlockSpec((tm, LANES), lambda i: (i, 0)),
            pl.BlockSpec((1, D_MODEL), lambda i: (0, 0)),
        ],
        out_specs=pl.BlockSpec((tm, D_MODEL), lambda i: (i, 0)),
        out_shape=jax.ShapeDtypeStruct((t, D_MODEL), jnp.float32),
        compiler_params=_cparams(("arbitrary",)),
        name="final_norm",
    )(x2d, ys, route, g)


def kernel(x, ln_mix_e, w_in_e, ln_q_e, w_uq_e, ln_kv_e, w_ukv_e, b_f_e, w_out_e, ln_mix_o, w_in_o,
           w_out_o, ln_ffn, w_rg, b_rg, w_re, b_re, w_gate, w_up, w_down, ln_f):
    batch, seq, d = x.shape
    t = batch * seq
    assert d == D_MODEL and seq % TK == 0 and seq % TM_PROJ == 0 and t % TE == 0
    bf16 = jnp.bfloat16
    x2d = x.reshape(t, d)

    w_in_p, w_uq_p, w_ukv_p, b_f_p = _even_weights(w_in_e[0], w_uq_e[0], w_ukv_e[0], b_f_e[0])
    q, k, v = _even_proj(x2d, seq, ln_mix_e[0][None, :], w_in_p, ln_q_e[0][None, :], w_uq_p,
                         ln_kv_e[0][None, :], w_ukv_p, b_f_p)
    o_t = _attention(q, k, _values_transposed(v, batch, seq), batch, seq)
    o = o_t.transpose(1, 3, 0, 2).reshape(t, 2 * MLA_HEADS * MLA_V)
    w_r, b_r = _router_weights(w_rg[0], b_rg[0], w_re[0], b_re[0])
    x2d, xg, route = _out_proj(x2d, o, w_out_e[0].astype(bf16), ln_ffn[0][None, :], w_r, b_r)
    ys = _moe(xg, route[:, 0:TOP_K].astype(jnp.int32), w_gate[0], w_up[0], w_down[0], t)

    x2d, rq, rk, rv, rg = _odd_proj(x2d, ys, route, seq, ln_mix_o[0][None, :], w_in_o[0].astype(bf16))
    og = _retention(rq, rk, rv, rg, batch, seq)
    w_r, b_r = _router_weights(w_rg[1], b_rg[1], w_re[1], b_re[1])
    x2d, xg, route = _out_proj(x2d, og, w_out_o[0].astype(bf16), ln_ffn[1][None, :], w_r, b_r)
    ys = _moe(xg, route[:, 0:TOP_K].astype(jnp.int32), w_gate[1], w_up[1], w_down[1], t)

    out = _final(x2d, ys, route, ln_f[None, :])
    return out.reshape(batch, seq, d)
```

```python
import functools
import math

import numpy as np
import jax
import jax.numpy as jnp
from jax import lax
from jax.experimental import pallas as pl
from jax.experimental.pallas import tpu as pltpu

D_MODEL = 1024
CHUNK = 64
ROPE_BASE = 10000.0
EPS = 1e-6
MLA_HEADS = 8
MLA_NOPE = 64
MLA_ROPE = 32
MLA_V = 64
MLA_Q_LORA = 256
MLA_KV_LORA = 128
FOX_HEADS = 8
FOX_DIM = 64
RET_HEADS = 4
RET_QK = 256
RET_V = 512
N_GROUPS = 4
EXPERTS_PER_GROUP = 8
N_EXPERTS = N_GROUPS * EXPERTS_PER_GROUP
TOP_K = 2
D_EXPERT = 512

LANES = 128
SUBLANES = 8
VMEM_LIMIT = 52 * 1024 * 1024

TM_PROJ = 256
TM_OUT = 512
TQ = 512
VT_ROWS = 80
TK = 512
LOG2E = math.log2(math.e)
RET_C = 256
RET_BLK = 1024
TE = 256
ROWS_PER_TOK = D_MODEL // LANES

N_PAIRS = MLA_HEADS // 2
PAIR_W = 256
ROPE_H = MLA_ROPE // 2
NEG = -1e30

_X1_OFF = 0
_X2_OFF = 64
_AUG_W = 6

_C_Q = 0
_C_KV = _C_Q + MLA_Q_LORA
_C_KPE = _C_KV + MLA_KV_LORA
_C_QF = _C_KPE + LANES
_C_KF = _C_QF + FOX_HEADS * FOX_DIM
_C_VF = _C_KF + FOX_HEADS * FOX_DIM
_C_F = _C_VF + FOX_HEADS * FOX_DIM
EVEN_N = _C_F + LANES


def _cparams(sem):
    return pltpu.CompilerParams(dimension_semantics=sem, vmem_limit_bytes=VMEM_LIMIT)


def _rms_scale(x):
    return lax.rsqrt(jnp.mean(x * x, axis=-1, keepdims=True) + EPS)


def _split3(c):
    hi = c.astype(jnp.bfloat16)
    r1 = c - hi.astype(jnp.float32)
    mid = r1.astype(jnp.bfloat16)
    r2 = r1 - mid.astype(jnp.float32)
    lo = r2.astype(jnp.bfloat16)
    return hi, mid, lo


def _even_proj_kernel(tiles_per_seq, x_ref, g_ref, win_ref, lnq_ref, wuq_ref, lnkv_ref, wukv_ref,
                      bf_ref, cos_ref, sin_ref, tri_ref, pq_ref, pk_ref, oq_ref, ok_ref,
                      q_ref, k_ref, vt_ref, carry_ref):
    i = pl.program_id(0)
    x = x_ref[...]
    xn = (x * _rms_scale(x) * g_ref[...]).astype(jnp.bfloat16)
    z = jnp.dot(xn, win_ref[...], preferred_element_type=jnp.float32)
    cos = cos_ref[...]
    sin = sin_ref[...]

    def rope_slab(s):
        return s * cos + pltpu.roll(s, 64, axis=1) * sin

    c_q = z[:, _C_Q:_C_Q + MLA_Q_LORA]
    cqn = (c_q * _rms_scale(c_q) * lnq_ref[...]).astype(jnp.bfloat16)
    q = jnp.dot(cqn, wuq_ref[...], preferred_element_type=jnp.float32)
    scale_a = (MLA_NOPE + MLA_ROPE) ** -0.5 * LOG2E
    for p in range(N_PAIRS):
        lo = p * PAIR_W
        q_ref[0, :, lo:lo + LANES] = (q[:, lo:lo + LANES] * scale_a).astype(jnp.bfloat16)
        q_ref[0, :, lo + LANES:lo + PAIR_W] = (
            rope_slab(q[:, lo + LANES:lo + PAIR_W]) * scale_a).astype(jnp.bfloat16)

    c_kv = z[:, _C_KV:_C_KV + MLA_KV_LORA]
    ckn = (c_kv * _rms_scale(c_kv) * lnkv_ref[...]).astype(jnp.bfloat16)
    kv = jnp.dot(ckn, wukv_ref[...], preferred_element_type=jnp.float32)
    kpe = rope_slab(z[:, _C_KPE:_C_KPE + LANES]).astype(jnp.bfloat16)
    nk = MLA_HEADS * MLA_NOPE
    for p in range(N_PAIRS):
        lo = p * PAIR_W
        k_ref[0, :, lo:lo + LANES] = kv[:, p * LANES:(p + 1) * LANES].astype(jnp.bfloat16)
        k_ref[0, :, lo + LANES:lo + PAIR_W] = kpe
    _store_values_transposed(vt_ref, 0, kv[:, nk:nk + MLA_HEADS * MLA_V])

    lane = lax.broadcasted_iota(jnp.int32, (1, LANES), 1)
    fz = z[:, _C_F:_C_F + LANES] + bf_ref[...]
    log_f = -(jnp.maximum(-fz, 0.0) + jnp.log1p(jnp.exp(-jnp.abs(fz))))
    log_f = jnp.where(lane < FOX_HEADS, log_f, 0.0)

    @pl.when(i % tiles_per_seq == 0)
    def _():
        carry_ref[...] = jnp.zeros_like(carry_ref)

    hi, mid, lo3 = _split3(log_f)
    tri = tri_ref[...]
    cum = (jnp.dot(tri, hi, preferred_element_type=jnp.float32)
           + jnp.dot(tri, mid, preferred_element_type=jnp.float32)
           + jnp.dot(tri, lo3, preferred_element_type=jnp.float32)) + carry_ref[...]
    tm = cum.shape[0]
    carry_ref[...] = cum[tm - 1:tm, :]

    parts = jnp.concatenate(_split3(cum * LOG2E), axis=1)
    aug_q = jnp.dot(parts, pq_ref[...], preferred_element_type=jnp.float32) + oq_ref[...]
    aug_k = jnp.dot(parts, pk_ref[...], preferred_element_type=jnp.float32) + ok_ref[...]
    scale_b = FOX_DIM ** -0.5 * LOG2E
    for p in range(N_PAIRS):
        lo = p * PAIR_W
        q_ref[1, :, lo:lo + LANES] = (
            z[:, _C_QF + p * LANES:_C_QF + (p + 1) * LANES] * scale_b).astype(jnp.bfloat16)
        q_ref[1, :, lo + LANES:lo + PAIR_W] = aug_q[:, p * LANES:(p + 1) * LANES].astype(jnp.bfloat16)
        k_ref[1, :, lo:lo + LANES] = z[:, _C_KF + p * LANES:_C_KF + (p + 1) * LANES].astype(jnp.bfloat16)
        k_ref[1, :, lo + LANES:lo + PAIR_W] = aug_k[:, p * LANES:(p + 1) * LANES].astype(jnp.bfloat16)
    _store_values_transposed(vt_ref, 1, z[:, _C_VF:_C_VF + FOX_HEADS * FOX_DIM])


def _store_values_transposed(vt_ref, g, v):
    vt = v.T
    ones = jnp.ones((VT_ROWS - MLA_V, vt.shape[1]), vt_ref.dtype)
    for h in range(MLA_HEADS):
        vt_ref[g, h * VT_ROWS:h * VT_ROWS + MLA_V, :] = vt[h * MLA_V:(h + 1) * MLA_V].astype(vt_ref.dtype)
        vt_ref[g, h * VT_ROWS + MLA_V:(h + 1) * VT_ROWS, :] = ones


def _even_weights(w_in, w_uq, w_ukv, b_f):
    d = w_in.shape[0]
    zeros = lambda n: jnp.zeros((d, n), w_in.dtype)
    o_cq, o_ckv = 0, MLA_Q_LORA
    o_kpe = o_ckv + MLA_KV_LORA
    o_qf = o_kpe + MLA_ROPE
    o_kf = o_qf + FOX_HEADS * FOX_DIM
    o_vf = o_kf + FOX_HEADS * FOX_DIM
    o_f = o_vf + FOX_HEADS * FOX_DIM
    kpe1 = w_in[:, o_kpe:o_kpe + ROPE_H]
    kpe2 = w_in[:, o_kpe + ROPE_H:o_kpe + MLA_ROPE]
    kpe_slab = jnp.concatenate([kpe1, kpe1, zeros(32), kpe2, kpe2, zeros(32)], axis=1)
    f_slab = jnp.concatenate([w_in[:, o_f:o_f + FOX_HEADS], zeros(LANES - FOX_HEADS)], axis=1)
    w_in_p = jnp.concatenate([w_in[:, o_cq:o_kpe], kpe_slab, w_in[:, o_qf:o_f], f_slab], axis=1)

    dq = w_uq.shape[0]
    zq = lambda n: jnp.zeros((dq, n), w_uq.dtype)
    hd = MLA_NOPE + MLA_ROPE
    blocks = []
    for p in range(N_PAIRS):
        h0, h1 = 2 * p, 2 * p + 1
        nope = lambda h: w_uq[:, h * hd:h * hd + MLA_NOPE]
        r1 = lambda h: w_uq[:, h * hd + MLA_NOPE:h * hd + MLA_NOPE + ROPE_H]
        r2 = lambda h: w_uq[:, h * hd + MLA_NOPE + ROPE_H:(h + 1) * hd]
        blocks += [nope(h0), nope(h1), r1(h0), r1(h1), zq(32), r2(h0), r2(h1), zq(32)]
    w_uq_p = jnp.concatenate(blocks, axis=1)

    kvd = MLA_NOPE + MLA_V
    k_cols = [w_ukv[:, h * kvd:h * kvd + MLA_NOPE] for h in range(MLA_HEADS)]
    v_cols = [w_ukv[:, h * kvd + MLA_NOPE:(h + 1) * kvd] for h in range(MLA_HEADS)]
    w_ukv_p = jnp.concatenate(k_cols + v_cols, axis=1)
    b_f_p = jnp.concatenate([b_f, jnp.zeros((LANES - FOX_HEADS,), b_f.dtype)])[None, :]
    bf16 = jnp.bfloat16
    return w_in_p.astype(bf16), w_uq_p.astype(bf16), w_ukv_p.astype(bf16), b_f_p


def _aug_placement():
    pq = np.zeros((3 * LANES, N_PAIRS * LANES), np.float32)
    pk = np.zeros((3 * LANES, N_PAIRS * LANES), np.float32)
    oq = np.zeros((1, N_PAIRS * LANES), np.float32)
    ok = np.zeros((1, N_PAIRS * LANES), np.float32)
    for p in range(N_PAIRS):
        for j in range(2):
            h = 2 * p + j
            base = p * LANES + _X1_OFF + j * ROPE_H
            for t in range(3):
                pq[t * LANES + h, base + 3 + t] = 1.0
                pk[t * LANES + h, base + t] = -1.0
                oq[0, base + t] = 1.0
                ok[0, base + 3 + t] = 1.0
    return (jnp.asarray(pq, jnp.bfloat16), jnp.asarray(pk, jnp.bfloat16),
            jnp.asarray(oq), jnp.asarray(ok))


def _mla_rope_tables(s):
    inv = ROPE_BASE ** (-jnp.arange(ROPE_H, dtype=jnp.float32) / ROPE_H)
    ang = jnp.arange(s, dtype=jnp.float32)[:, None] * inv[None, :]
    c, sn = jnp.cos(ang), jnp.sin(ang)
    z = jnp.zeros((s, 32), jnp.float32)
    cos = jnp.concatenate([c, c, z, c, c, z], axis=1)
    sin = jnp.concatenate([-sn, -sn, z, sn, sn, z], axis=1)
    return cos, sin


def _even_proj(x2d, seq, g, w_in_p, ln_q, w_uq_p, ln_kv, w_ukv_p, b_f_p):
    t = x2d.shape[0]
    tm = TM_PROJ
    tiles_per_seq = seq // tm
    cos, sin = _mla_rope_tables(seq)
    tri = jnp.asarray(np.tril(np.ones((tm, tm), np.float32)), jnp.bfloat16)
    pq, pk, oq, ok = _aug_placement()
    const = lambda shape: pl.BlockSpec(shape, lambda i: (0,) * len(shape))
    return pl.pallas_call(
        functools.partial(_even_proj_kernel, tiles_per_seq),
        grid=(t // tm,),
        in_specs=[
            pl.BlockSpec((tm, D_MODEL), lambda i: (i, 0)),
            const((1, D_MODEL)), const((D_MODEL, EVEN_N)),
            const((1, MLA_Q_LORA)), const((MLA_Q_LORA, N_PAIRS * PAIR_W)),
            const((1, MLA_KV_LORA)), const((MLA_KV_LORA, 2 * MLA_HEADS * MLA_NOPE)),
            const((1, LANES)),
            pl.BlockSpec((tm, LANES), lambda i: (i % tiles_per_seq, 0)),
            pl.BlockSpec((tm, LANES), lambda i: (i % tiles_per_seq, 0)),
            const((tm, tm)), const(pq.shape), const(pk.shape), const(oq.shape), const(ok.shape),
        ],
        out_specs=[
            pl.BlockSpec((2, tm, N_PAIRS * PAIR_W), lambda i: (0, i, 0)),
            pl.BlockSpec((2, tm, N_PAIRS * PAIR_W), lambda i: (0, i, 0)),
            pl.BlockSpec((2, None, MLA_HEADS * VT_ROWS, tm),
                         lambda i: (0, i // tiles_per_seq, 0, i % tiles_per_seq)),
        ],
        out_shape=[
            jax.ShapeDtypeStruct((2, t, N_PAIRS * PAIR_W), jnp.bfloat16),
            jax.ShapeDtypeStruct((2, t, N_PAIRS * PAIR_W), jnp.bfloat16),
            jax.ShapeDtypeStruct((2, t // seq, MLA_HEADS * VT_ROWS, seq), jnp.bfloat16),
        ],
        scratch_shapes=[pltpu.VMEM((1, LANES), jnp.float32)],
        compiler_params=_cparams(("arbitrary",)),
        name="even_proj",
    )(x2d, g, w_in_p, ln_q, w_uq_p, ln_kv, w_ukv_p, b_f_p, cos, sin, tri, pq, pk, oq, ok)


def _attn_kernel(q_ref, k_ref, vt_ref, o_ref, qcat_ref, s0_ref, s1_ref, m0_ref, m1_ref,
                 a0_ref, a1_ref, acc_ref):
    g = pl.program_id(0)
    i = pl.program_id(3)
    tq = q_ref.shape[0]
    s_bufs, m_bufs, a_bufs = (s0_ref, s1_ref), (m0_ref, m1_ref), (a0_ref, a1_ref)

    lane = lax.broadcasted_iota(jnp.int32, (1, PAIR_W), 1)

    def head_mask(j):
        a = (lane >= j * MLA_NOPE) & (lane < (j + 1) * MLA_NOPE)
        b = (lane >= LANES + _X1_OFF + j * ROPE_H) & (lane < LANES + _X1_OFF + (j + 1) * ROPE_H)
        c = (lane >= LANES + _X2_OFF + j * ROPE_H) & (lane < LANES + _X2_OFF + (j + 1) * ROPE_H)
        return a | b | c

    q = q_ref[...]
    zero = jnp.zeros_like(q)
    qcat_ref[0:tq, :] = jnp.where(head_mask(0), q, zero)
    qcat_ref[tq:2 * tq, :] = jnp.where(head_mask(1), q, zero)
    acc_ref[...] = jnp.zeros_like(acc_ref)

    q0 = i * tq
    n_chunks = q0 // TK + 1

    def scores(c, m_run, par, masked):
        kstart = pl.multiple_of(c * TK, TK)
        s = lax.dot_general(k_ref[pl.ds(kstart, TK), :], qcat_ref[...], (((1,), (1,)), ((), ())),
                            preferred_element_type=jnp.float32)
        if masked:
            col = lax.broadcasted_iota(jnp.int32, (1, 2 * tq), 1)
            qpos = q0 + jnp.where(col >= tq, col - tq, col)
            qlim = jnp.where(g == 0, qpos | (CHUNK - 1), qpos)
            kpos = kstart + lax.broadcasted_iota(jnp.int32, (TK, 1), 0)
            s = jnp.where(kpos <= qlim, s, NEG)
        s_bufs[par][...] = s
        m_new = jnp.maximum(m_run, jnp.max(s, axis=0, keepdims=True))
        m_bufs[par][...] = m_new
        a_bufs[par][...] = jnp.exp2(m_run - m_new)

    def accumulate(c, par):
        kstart = pl.multiple_of(c * TK, TK)
        alpha = a_bufs[par][...]
        p = jnp.exp2(s_bufs[par][...] - m_bufs[par][...]).astype(jnp.bfloat16)
        for h in range(2):
            cols = slice(h * tq, (h + 1) * tq)
            acc_ref[h] = alpha[:, cols] * acc_ref[h] + jnp.dot(
                vt_ref[h * VT_ROWS:(h + 1) * VT_ROWS, pl.ds(kstart, TK)], p[:, cols],
                preferred_element_type=jnp.float32)

    def stage(c, par, masked):
        scores(c + 1, m_bufs[par][...], 1 - par, masked)
        accumulate(c, par)

    m_init = jnp.full((1, 2 * tq), NEG, jnp.float32)

    @pl.when(n_chunks == 1)
    def _():
        scores(0, m_init, 0, True)
        accumulate(0, 0)

    @pl.when(n_chunks > 1)
    def _():
        scores(0, m_init, 0, False)

    n_pairs = jnp.maximum(n_chunks - 2, 0) // 2

    def body(j, carry):
        stage(2 * j, 0, False)
        stage(2 * j + 1, 1, False)
        return carry

    lax.fori_loop(0, n_pairs, body, 0)
    c0 = 2 * n_pairs
    left = n_chunks - 1 - c0

    @pl.when((n_chunks > 1) & (left == 1))
    def _():
        stage(c0, 0, True)
        accumulate(c0 + 1, 1)

    @pl.when((n_chunks > 1) & (left == 2))
    def _():
        stage(c0, 0, False)
        stage(c0 + 1, 1, True)
        accumulate(c0 + 2, 0)

    for h in range(2):
        acc = acc_ref[h]
        o_ref[h * MLA_V:(h + 1) * MLA_V, :] = (acc[0:MLA_V] / acc[MLA_V:MLA_V + 1]).astype(o_ref.dtype)


def _attention(q, k, vt, batch, seq):
    nq = seq // TQ
    f32 = jnp.float32
    return pl.pallas_call(
        _attn_kernel,
        grid=(2, batch, N_PAIRS, nq),
        in_specs=[
            pl.BlockSpec((None, TQ, PAIR_W), lambda g, b, p, i: (g, b * nq + i, p)),
            pl.BlockSpec((None, seq, PAIR_W), lambda g, b, p, i: (g, b, p)),
            pl.BlockSpec((None, None, 2 * VT_ROWS, seq), lambda g, b, p, i: (g, b, p, 0)),
        ],
        out_specs=pl.BlockSpec((None, None, LANES, TQ), lambda g, b, p, i: (g, b, p, i)),
        out_shape=jax.ShapeDtypeStruct((2, batch, N_PAIRS * LANES, seq), jnp.bfloat16),
        scratch_shapes=[pltpu.VMEM((2 * TQ, PAIR_W), jnp.bfloat16),
                        pltpu.VMEM((TK, 2 * TQ), f32), pltpu.VMEM((TK, 2 * TQ), f32),
                        pltpu.VMEM((1, 2 * TQ), f32), pltpu.VMEM((1, 2 * TQ), f32),
                        pltpu.VMEM((1, 2 * TQ), f32), pltpu.VMEM((1, 2 * TQ), f32),
                        pltpu.VMEM((2, VT_ROWS, TQ), f32)],
        compiler_params=_cparams(("arbitrary",) * 4),
        name="attention",
    )(q, k, vt)


def _router(logits):
    lane = lax.broadcasted_iota(jnp.int32, logits.shape, 1)
    big = jnp.int32(LANES)
    ninf = -jnp.inf
    gl = jnp.where(lane < N_GROUPS, logits, ninf)
    gmax = jnp.max(gl, axis=1, keepdims=True)
    gsel = jnp.min(jnp.where(gl == gmax, lane, big), axis=1, keepdims=True)
    gsum = jnp.sum(jnp.where(lane < N_GROUPS, jnp.exp(logits - gmax), 0.0), axis=1, keepdims=True)
    p_g = 1.0 / gsum
    lo = N_GROUPS + EXPERTS_PER_GROUP * gsel
    e = jnp.where((lane >= lo) & (lane < lo + EXPERTS_PER_GROUP), logits, ninf)
    m1 = jnp.max(e, axis=1, keepdims=True)
    i1 = jnp.min(jnp.where(e == m1, lane, big), axis=1, keepdims=True)
    e2 = jnp.where(lane == i1, ninf, e)
    m2 = jnp.max(e2, axis=1, keepdims=True)
    i2 = jnp.min(jnp.where(e2 == m2, lane, big), axis=1, keepdims=True)
    a2 = jnp.exp(m2 - m1)
    den = 1.0 + a2
    g1 = p_g / den
    g2 = p_g * a2 / den
    f = lambda v: v.astype(jnp.float32)
    return jnp.where(lane == 0, f(i1 - N_GROUPS),
                     jnp.where(lane == 1, f(i2 - N_GROUPS),
                               jnp.where(lane == 2, g1, jnp.where(lane == 3, g2, 0.0))))


def _out_proj_kernel(transposed, x_ref, o_ref, w_ref, ln_ref, wr_ref, br_ref, xo_ref, xg_ref, route_ref):
    tm = x_ref.shape[0]
    if transposed:
        y = x_ref[...]
        nf = o_ref.shape[1]
        for g in range(o_ref.shape[0]):
            y = y + lax.dot_general(o_ref[g], w_ref[g * nf:(g + 1) * nf, :], (((0,), (0,)), ((), ())),
                                    preferred_element_type=jnp.float32)
    else:
        y = x_ref[...] + jnp.dot(o_ref[...], w_ref[...], preferred_element_type=jnp.float32)
    xo_ref[...] = y
    xn = y * _rms_scale(y) * ln_ref[...]
    hi = xn.astype(jnp.bfloat16)
    lo = (xn - hi.astype(jnp.float32)).astype(jnp.bfloat16)
    l2 = jnp.dot(jnp.concatenate([hi, lo], axis=1), wr_ref[...], preferred_element_type=jnp.float32)
    logits = l2[:, :LANES] + l2[:, LANES:] + br_ref[...]
    route_ref[...] = _router(logits)
    for c in range(ROWS_PER_TOK):
        xg_ref[pl.ds(c, tm, stride=ROWS_PER_TOK), :] = xn[:, c * LANES:(c + 1) * LANES]


def _out_proj(x2d, o, w_out, ln, w_r, b_r, seq):
    t = x2d.shape[0]
    kdim = w_out.shape[0]
    tm = TM_OUT
    transposed = o.ndim == 4
    tiles_per_seq = seq // tm
    if transposed:
        o_spec = pl.BlockSpec((o.shape[0], None, o.shape[2], tm),
                              lambda i: (0, i // tiles_per_seq, 0, i % tiles_per_seq))
    else:
        o_spec = pl.BlockSpec((tm, kdim), lambda i: (i, 0))
    const = lambda shape: pl.BlockSpec(shape, lambda i: (0,) * len(shape))
    return pl.pallas_call(
        functools.partial(_out_proj_kernel, transposed),
        grid=(t // tm,),
        in_specs=[
            pl.BlockSpec((tm, D_MODEL), lambda i: (i, 0)),
            o_spec,
            const((kdim, D_MODEL)), const((1, D_MODEL)), const((2 * D_MODEL, 2 * LANES)), const((1, LANES)),
        ],
        out_specs=[
            pl.BlockSpec((tm, D_MODEL), lambda i: (i, 0)),
            pl.BlockSpec((tm * ROWS_PER_TOK, LANES), lambda i: (i, 0)),
            pl.BlockSpec((tm, LANES), lambda i: (i, 0)),
        ],
        out_shape=[
            jax.ShapeDtypeStruct((t, D_MODEL), jnp.float32),
            jax.ShapeDtypeStruct((t * ROWS_PER_TOK, LANES), jnp.float32),
            jax.ShapeDtypeStruct((t, LANES), jnp.float32),
        ],
        compiler_params=_cparams(("arbitrary",)),
        name="out_proj_router",
    )(x2d, o, w_out, ln, w_r, b_r)


def _router_weights(w_rg, b_rg, w_re, b_re):
    d = w_rg.shape[0]
    pad = LANES - N_GROUPS - N_EXPERTS
    w_r = jnp.concatenate([w_rg, w_re, jnp.zeros((d, pad), w_rg.dtype)], axis=1)
    b_r = jnp.concatenate([b_rg, b_re, jnp.zeros((pad,), b_rg.dtype)])[None, :]
    w_hi = w_r.astype(jnp.bfloat16)
    w_lo = (w_r - w_hi.astype(jnp.float32)).astype(jnp.bfloat16)
    half = jnp.concatenate([w_hi, w_lo], axis=1)
    return jnp.concatenate([half, half], axis=0), b_r


def _dispatch_plan(eid, t):
    a = t * TOP_K
    nb = a // TE + N_EXPERTS
    e_flat = eid.reshape(a)
    experts = jnp.arange(N_EXPERTS, dtype=jnp.int32)
    _, order = lax.sort((e_flat, jnp.arange(a, dtype=jnp.int32)), num_keys=1, is_stable=True)
    counts = jnp.sum((e_flat[:, None] == experts[None, :]).astype(jnp.int32), axis=0)
    start = jnp.cumsum(counts) - counts
    padded = ((counts + TE - 1) // TE) * TE
    pend = jnp.cumsum(padded)
    pstart = pend - padded
    blk0 = jnp.arange(nb, dtype=jnp.int32) * TE
    blk_e = jnp.minimum(jnp.sum((blk0[:, None] >= pend[None, :]).astype(jnp.int32), axis=1), N_EXPERTS - 1)
    blk_used = (blk0 < pend[-1]).astype(jnp.int32)
    onehot = (blk_e[:, None] == experts[None, :]).astype(jnp.int32)
    per_blk = lambda v: jnp.sum(onehot * v[None, :], axis=1)[:, None]
    r = jnp.arange(TE, dtype=jnp.int32)[None, :]
    off = blk0[:, None] + r - per_blk(pstart)
    valid = (off < per_blk(counts)) & (blk_used[:, None] > 0)
    src = jnp.where(valid, per_blk(start) + off, 0)
    assign = order[src]
    src_tok = jnp.where(valid, assign // TOP_K, 0).astype(jnp.int32)
    dst_row = jnp.where(valid, assign, a + r).astype(jnp.int32)
    return blk_e, blk_used, src_tok[:, None, :], dst_row[:, None, :]


def _moe_kernel(blk_e_ref, blk_used_ref, src_ref, src_next_ref, dst_ref, xg_ref, wg_ref, wu_ref, wd_ref,
                ys_ref, xbuf, ybuf, wg_bf, wu_bf, wd_bf, gsem, ssem):
    i = pl.program_id(0)
    nb = pl.num_programs(0)
    used = blk_used_ref[i] > 0
    slot = i % 2
    block_rows = TE * ROWS_PER_TOK

    def start_gather(idx_ref, s):
        for r in range(TE):
            tok = idx_ref[0, 0, r]
            pltpu.make_async_copy(
                xg_ref.at[pl.ds(pl.multiple_of(tok * ROWS_PER_TOK, ROWS_PER_TOK), ROWS_PER_TOK), :],
                xbuf.at[s, pl.ds(r * ROWS_PER_TOK, ROWS_PER_TOK), :],
                gsem.at[s]).start(priority=r % 2)

    def wait_gather(s):
        pltpu.make_async_copy(xg_ref.at[pl.ds(0, block_rows), :], xbuf.at[s], gsem.at[s]).wait()

    def start_scatter():
        for r in range(TE):
            row = dst_ref[0, 0, r]
            pltpu.make_async_copy(
                ybuf.at[pl.ds(r * ROWS_PER_TOK, ROWS_PER_TOK), :],
                ys_ref.at[pl.ds(pl.multiple_of(row * ROWS_PER_TOK, ROWS_PER_TOK), ROWS_PER_TOK), :],
                ssem.at[0]).start(priority=r % 2)

    def wait_scatter():
        pltpu.make_async_copy(ybuf, ys_ref.at[pl.ds(0, block_rows), :], ssem.at[0]).wait()

    @pl.when(i == 0)
    def _():
        ybuf[...] = jnp.zeros_like(ybuf)
        pltpu.make_async_copy(
            ybuf, ys_ref.at[pl.ds(ys_ref.shape[0] - block_rows, block_rows), :], ssem.at[0]).start()
        start_gather(src_ref, 0)

    @pl.when(used)
    def _():
        start_gather(src_next_ref, 1 - slot)
        prev_e = blk_e_ref[jnp.maximum(i - 1, 0)]

        @pl.when((i == 0) | (blk_e_ref[i] != prev_e))
        def _():
            wg_bf[...] = wg_ref[...].astype(jnp.bfloat16)
            wu_bf[...] = wu_ref[...].astype(jnp.bfloat16)
            wd_bf[...] = wd_ref[...].astype(jnp.bfloat16)

        wait_gather(slot)
        xb = xbuf.at[slot]
        x = jnp.concatenate([xb[pl.ds(c, TE, stride=ROWS_PER_TOK), :] for c in range(ROWS_PER_TOK)],
                            axis=1).astype(jnp.bfloat16)
        hg = jnp.dot(x, wg_bf[...], preferred_element_type=jnp.float32)
        hu = jnp.dot(x, wu_bf[...], preferred_element_type=jnp.float32)
        hid = (hg * jax.nn.sigmoid(hg) * hu).astype(jnp.bfloat16)
        y = jnp.dot(hid, wd_bf[...], preferred_element_type=jnp.float32)

        wait_scatter()
        for c in range(ROWS_PER_TOK):
            ybuf[pl.ds(c, TE, stride=ROWS_PER_TOK), :] = y[:, c * LANES:(c + 1) * LANES]
        start_scatter()

        next_used = (i + 1 < nb) & (blk_used_ref[jnp.minimum(i + 1, nb - 1)] > 0)

        @pl.when(jnp.logical_not(next_used))
        def _():
            wait_gather(1 - slot)
            wait_scatter()


def _moe(xg, eid, w_gate, w_up, w_down, t):
    a = t * TOP_K
    nb = a // TE + N_EXPERTS
    blk_e, blk_used, src_tok, dst_row = _dispatch_plan(eid, t)
    idx_spec = lambda f: pl.BlockSpec((1, 1, TE), f, memory_space=pltpu.SMEM)
    grid_spec = pltpu.PrefetchScalarGridSpec(
        num_scalar_prefetch=2,
        grid=(nb,),
        in_specs=[
            idx_spec(lambda i, be, bu: (i, 0, 0)),
            idx_spec(lambda i, be, bu: (jnp.minimum(i + 1, nb - 1), 0, 0)),
            idx_spec(lambda i, be, bu: (i, 0, 0)),
            pl.BlockSpec(memory_space=pl.ANY),
            pl.BlockSpec((None, D_MODEL, D_EXPERT), lambda i, be, bu: (be[i], 0, 0)),
            pl.BlockSpec((None, D_MODEL, D_EXPERT), lambda i, be, bu: (be[i], 0, 0)),
            pl.BlockSpec((None, D_EXPERT, D_MODEL), lambda i, be, bu: (be[i], 0, 0)),
        ],
        out_specs=pl.BlockSpec(memory_space=pl.ANY),
        scratch_shapes=[
            pltpu.VMEM((2, TE * ROWS_PER_TOK, LANES), jnp.float32),
            pltpu.VMEM((TE * ROWS_PER_TOK, LANES), jnp.float32),
            pltpu.VMEM((D_MODEL, D_EXPERT), jnp.bfloat16),
            pltpu.VMEM((D_MODEL, D_EXPERT), jnp.bfloat16),
            pltpu.VMEM((D_EXPERT, D_MODEL), jnp.bfloat16),
            pltpu.SemaphoreType.DMA((2,)),
            pltpu.SemaphoreType.DMA((1,)),
        ],
    )
    return pl.pallas_call(
        _moe_kernel,
        grid_spec=grid_spec,
        out_shape=jax.ShapeDtypeStruct(((a + TE) * ROWS_PER_TOK, LANES), jnp.float32),
        compiler_params=_cparams(("arbitrary",)),
        name="moe_experts",
    )(blk_e, blk_used, src_tok, src_tok, dst_row, xg, w_gate, w_up, w_down)


def _combine(x, ys_ref, route_ref, tm):
    stride = TOP_K * ROWS_PER_TOK
    y0 = jnp.concatenate([ys_ref[pl.ds(c, tm, stride=stride), :] for c in range(ROWS_PER_TOK)], axis=1)
    y1 = jnp.concatenate([ys_ref[pl.ds(ROWS_PER_TOK + c, tm, stride=stride), :]
                          for c in range(ROWS_PER_TOK)], axis=1)
    route = route_ref[...]
    return x + route[:, 2:3] * y0 + route[:, 3:4] * y1


def _odd_proj_kernel(x_ref, ys_ref, route_ref, g_ref, w_ref, cos_ref, sin_ref,
                     xo_ref, q_ref, k_ref, v_ref, gate_ref):
    tm = x_ref.shape[0]
    x = _combine(x_ref[...], ys_ref, route_ref, tm)
    xo_ref[...] = x
    xn = (x * _rms_scale(x) * g_ref[...]).astype(jnp.bfloat16)
    cos = cos_ref[...]
    sin = sin_ref[...]
    nqk = RET_HEADS * RET_QK
    half = RET_QK // 2

    def rope_store(z, dst, scale):
        for h in range(RET_HEADS):
            x1 = z[:, h * RET_QK:h * RET_QK + half]
            x2 = z[:, h * RET_QK + half:(h + 1) * RET_QK]
            dst[:, h * RET_QK:h * RET_QK + half] = ((x1 * cos - x2 * sin) * scale).astype(dst.dtype)
            dst[:, h * RET_QK + half:(h + 1) * RET_QK] = ((x1 * sin + x2 * cos) * scale).astype(dst.dtype)

    zq = jnp.dot(xn, w_ref[:, 0:nqk], preferred_element_type=jnp.float32)
    rope_store(zq, q_ref, 1.0)
    zk = jnp.dot(xn, w_ref[:, nqk:2 * nqk], preferred_element_type=jnp.float32)
    rope_store(zk, k_ref, RET_QK ** -0.5)
    nv = RET_HEADS * RET_V
    zv = jnp.dot(xn, w_ref[:, 2 * nqk:2 * nqk + nv], preferred_element_type=jnp.float32)
    v_ref[...] = zv.astype(v_ref.dtype)
    zg = jnp.dot(xn, w_ref[:, 2 * nqk + nv:2 * nqk + 2 * nv], preferred_element_type=jnp.float32)
    gate_ref[...] = (zg * jax.nn.sigmoid(zg)).astype(gate_ref.dtype)


def _ret_rope_tables(s):
    half = RET_QK // 2
    inv = ROPE_BASE ** (-jnp.arange(half, dtype=jnp.float32) / half)
    ang = jnp.arange(s, dtype=jnp.float32)[:, None] * inv[None, :]
    return jnp.cos(ang), jnp.sin(ang)


def _odd_proj(x2d, ys, route, seq, g, w_in):
    t = x2d.shape[0]
    tm = TM_PROJ
    tiles_per_seq = seq // tm
    cos, sin = _ret_rope_tables(seq)
    nqk = RET_HEADS * RET_QK
    nv = RET_HEADS * RET_V
    n_in = w_in.shape[1]
    const = lambda shape: pl.BlockSpec(shape, lambda i: (0,) * len(shape))
    tok = lambda n: pl.BlockSpec((tm, n), lambda i: (i, 0))
    return pl.pallas_call(
        _odd_proj_kernel,
        grid=(t // tm,),
        in_specs=[
            tok(D_MODEL),
            pl.BlockSpec((tm * TOP_K * ROWS_PER_TOK, LANES), lambda i: (i, 0)),
            tok(LANES),
            const((1, D_MODEL)), const((D_MODEL, n_in)),
            pl.BlockSpec((tm, LANES), lambda i: (i % tiles_per_seq, 0)),
            pl.BlockSpec((tm, LANES), lambda i: (i % tiles_per_seq, 0)),
        ],
        out_specs=[tok(D_MODEL), tok(nqk), tok(nqk), tok(nv), tok(nv)],
        out_shape=[
            jax.ShapeDtypeStruct((t, D_MODEL), jnp.float32),
            jax.ShapeDtypeStruct((t, nqk), jnp.bfloat16),
            jax.ShapeDtypeStruct((t, nqk), jnp.bfloat16),
            jax.ShapeDtypeStruct((t, nv), jnp.bfloat16),
            jax.ShapeDtypeStruct((t, nv), jnp.bfloat16),
        ],
        compiler_params=_cparams(("arbitrary",)),
        name="odd_proj",
    )(x2d, ys, route, g, w_in, cos, sin)


def _retention_kernel(q_ref, k_ref, v_ref, gate_ref, dmask_ref, xi_ref, zeta_ref, gc_ref, o_ref, state_ref):
    n = pl.program_id(2)

    @pl.when(n == 0)
    def _():
        state_ref[...] = jnp.zeros_like(state_ref)

    xi = xi_ref[...][:, 0:1]
    zeta = zeta_ref[...][:, 0:1]
    g_c = gc_ref[...][:, 0:1]
    for j in range(q_ref.shape[0] // RET_C):
        rows = slice(j * RET_C, (j + 1) * RET_C)
        q = q_ref[rows, :]
        k = k_ref[rows, :]
        v = v_ref[rows, :]
        inner = lax.dot_general(q, k, (((1,), (1,)), ((), ())),
                                preferred_element_type=jnp.float32) * dmask_ref[...]
        o = jnp.dot(inner.astype(jnp.bfloat16), v, preferred_element_type=jnp.float32)
        state = state_ref[...]
        cross = jnp.dot(q, state.astype(jnp.bfloat16), preferred_element_type=jnp.float32)
        o = o + cross * xi
        kz = (k.astype(jnp.float32) * zeta).astype(jnp.bfloat16)
        upd = lax.dot_general(kz, v, (((0,), (0,)), ((), ())), preferred_element_type=jnp.float32)
        state_ref[...] = g_c * state + upd
        o = o * _rms_scale(o)
        o_ref[rows, :] = (o * gate_ref[rows, :].astype(jnp.float32)).astype(o_ref.dtype)


def _retention_tables():
    c = RET_C
    log_g = jnp.log(1.0 - jnp.exp2(-5.0 - jnp.arange(RET_HEADS, dtype=jnp.float32)))
    j = jnp.arange(c, dtype=jnp.float32)
    diff = j[:, None] - j[None, :]
    dmask = jnp.where(diff >= 0, jnp.exp(jnp.maximum(diff, 0.0)[None] * log_g[:, None, None]), 0.0)
    xi = jnp.exp((j[None, :] + 1.0) * log_g[:, None])
    zeta = jnp.exp((c - 1.0 - j)[None, :] * log_g[:, None])
    g_c = jnp.exp(c * log_g)
    bc = lambda a: jnp.broadcast_to(a[..., None], a.shape + (LANES,))
    return dmask, bc(xi), bc(zeta), bc(g_c[:, None])


def _retention(q, k, v, gate, batch, seq):
    nc = seq // RET_BLK
    dmask, xi, zeta, g_c = _retention_tables()
    row = lambda n: pl.BlockSpec((RET_BLK, n), lambda b, h, c: (b * nc + c, h))
    head = lambda shape: pl.BlockSpec((None,) + shape, lambda b, h, c: (h, 0, 0))
    return pl.pallas_call(
        _retention_kernel,
        grid=(batch, RET_HEADS, nc),
        in_specs=[row(RET_QK), row(RET_QK), row(RET_V), row(RET_V),
                  head((RET_C, RET_C)), head((RET_C, LANES)), head((RET_C, LANES)), head((1, LANES))],
        out_specs=row(RET_V),
        out_shape=jax.ShapeDtypeStruct((batch * seq, RET_HEADS * RET_V), jnp.bfloat16),
        scratch_shapes=[pltpu.VMEM((RET_QK, RET_V), jnp.float32)],
        compiler_params=_cparams(("arbitrary",) * 3),
        name="retention",
    )(q, k, v, gate, dmask, xi, zeta, g_c)


def _final_kernel(x_ref, ys_ref, route_ref, g_ref, o_ref):
    x = _combine(x_ref[...], ys_ref, route_ref, x_ref.shape[0])
    o_ref[...] = x * _rms_scale(x) * g_ref[...]


def _final(x2d, ys, route, g):
    t = x2d.shape[0]
    tm = TM_PROJ
    return pl.pallas_call(
        _final_kernel,
        grid=(t // tm,),
        in_specs=[
            pl.BlockSpec((tm, D_MODEL), lambda i: (i, 0)),
            pl.BlockSpec((tm * TOP_K * ROWS_PER_TOK, LANES), lambda i: (i, 0)),
            pl.BlockSpec((tm, LANES), lambda i: (i, 0)),
            pl.BlockSpec((1, D_MODEL), lambda i: (0, 0)),
        ],
        out_specs=pl.BlockSpec((tm, D_MODEL), lambda i: (i, 0)),
        out_shape=jax.ShapeDtypeStruct((t, D_MODEL), jnp.float32),
        compiler_params=_cparams(("arbitrary",)),
        name="final_norm",
    )(x2d, ys, route, g)


def kernel(x, ln_mix_e, w_in_e, ln_q_e, w_uq_e, ln_kv_e, w_ukv_e, b_f_e, w_out_e, ln_mix_o, w_in_o,
           w_out_o, ln_ffn, w_rg, b_rg, w_re, b_re, w_gate, w_up, w_down, ln_f):
    batch, seq, d = x.shape
    t = batch * seq
    assert d == D_MODEL and seq % TK == 0 and seq % TQ == 0 and seq % RET_BLK == 0
    assert seq % TM_PROJ == 0 and t % TM_OUT == 0 and t % TE == 0
    bf16 = jnp.bfloat16
    x2d = x.reshape(t, d)

    w_in_p, w_uq_p, w_ukv_p, b_f_p = _even_weights(w_in_e[0], w_uq_e[0], w_ukv_e[0], b_f_e[0])
    q, k, vt = _even_proj(x2d, seq, ln_mix_e[0][None, :], w_in_p, ln_q_e[0][None, :], w_uq_p,
                         ln_kv_e[0][None, :], w_ukv_p, b_f_p)
    o_t = _attention(q, k, vt, batch, seq)
    w_r, b_r = _router_weights(w_rg[0], b_rg[0], w_re[0], b_re[0])
    x2d, xg, route = _out_proj(x2d, o_t, w_out_e[0].astype(bf16), ln_ffn[0][None, :], w_r, b_r, seq)
    ys = _moe(xg, route[:, 0:TOP_K].astype(jnp.int32), w_gate[0], w_up[0], w_down[0], t)

    x2d, rq, rk, rv, rg = _odd_proj(x2d, ys, route, seq, ln_mix_o[0][None, :], w_in_o[0].astype(bf16))
    og = _retention(rq, rk, rv, rg, batch, seq)
    w_r, b_r = _router_weights(w_rg[1], b_rg[1], w_re[1], b_re[1])
    x2d, xg, route = _out_proj(x2d, og, w_out_o[0].astype(bf16), ln_ffn[1][None, :], w_r, b_r, seq)
    ys = _moe(xg, route[:, 0:TOP_K].astype(jnp.int32), w_gate[1], w_up[1], w_down[1], t)

    out = _final(x2d, ys, route, ln_f[None, :])
    return out.reshape(batch, seq, d)
```

```python
import functools
import math

import numpy as np
import jax
import jax.numpy as jnp
from jax import lax
from jax.experimental import pallas as pl
from jax.experimental.pallas import tpu as pltpu

D_MODEL = 1024
CHUNK = 64
ROPE_BASE = 10000.0
EPS = 1e-6
MLA_HEADS = 8
MLA_NOPE = 64
MLA_ROPE = 32
MLA_V = 64
MLA_Q_LORA = 256
MLA_KV_LORA = 128
FOX_HEADS = 8
FOX_DIM = 64
RET_HEADS = 4
RET_QK = 256
RET_V = 512
N_GROUPS = 4
EXPERTS_PER_GROUP = 8
N_EXPERTS = N_GROUPS * EXPERTS_PER_GROUP
TOP_K = 2
D_EXPERT = 512

LANES = 128
SUBLANES = 8
VMEM_LIMIT = 52 * 1024 * 1024

TM_PROJ = 256
TM_OUT = 512
OUT_SPLIT = 2
TQ = 512
VT_ROWS = 80
TK = 512
LOG2E = math.log2(math.e)
RET_C = 256
RET_BLK = 1024
TE = 256
ROWS_PER_TOK = D_MODEL // LANES

N_PAIRS = MLA_HEADS // 2
PAIR_W = 256
ROPE_H = MLA_ROPE // 2
NEG = -1e30

_X1_OFF = 0
_X2_OFF = 64
_AUG_W = 6

_C_Q = 0
_C_KV = _C_Q + MLA_Q_LORA
_C_KPE = _C_KV + MLA_KV_LORA
_C_QF = _C_KPE + LANES
_C_KF = _C_QF + FOX_HEADS * FOX_DIM
_C_VF = _C_KF + FOX_HEADS * FOX_DIM
_C_F = _C_VF + FOX_HEADS * FOX_DIM
EVEN_N = _C_F + LANES


def _cparams(sem):
    return pltpu.CompilerParams(dimension_semantics=sem, vmem_limit_bytes=VMEM_LIMIT)


def _rms_scale(x):
    return lax.rsqrt(jnp.mean(x * x, axis=-1, keepdims=True) + EPS)


def _split3(c):
    hi = c.astype(jnp.bfloat16)
    r1 = c - hi.astype(jnp.float32)
    mid = r1.astype(jnp.bfloat16)
    r2 = r1 - mid.astype(jnp.float32)
    lo = r2.astype(jnp.bfloat16)
    return hi, mid, lo


def _even_proj_kernel(tiles_per_seq, x_ref, g_ref, win_ref, lnq_ref, wuq_ref, lnkv_ref, wukv_ref,
                      bf_ref, cos_ref, sin_ref, tri_ref, pq_ref, pk_ref, oq_ref, ok_ref,
                      q_ref, k_ref, vt_ref, carry_ref):
    i = pl.program_id(0)
    x = x_ref[...]
    xn = (x * _rms_scale(x) * g_ref[...]).astype(jnp.bfloat16)
    z = jnp.dot(xn, win_ref[...], preferred_element_type=jnp.float32)
    cos = cos_ref[...]
    sin = sin_ref[...]

    def rope_slab(s):
        return s * cos + pltpu.roll(s, 64, axis=1) * sin

    c_q = z[:, _C_Q:_C_Q + MLA_Q_LORA]
    cqn = (c_q * _rms_scale(c_q) * lnq_ref[...]).astype(jnp.bfloat16)
    q = jnp.dot(cqn, wuq_ref[...], preferred_element_type=jnp.float32)
    scale_a = (MLA_NOPE + MLA_ROPE) ** -0.5 * LOG2E
    for p in range(N_PAIRS):
        lo = p * PAIR_W
        q_ref[0, :, lo:lo + LANES] = (q[:, lo:lo + LANES] * scale_a).astype(jnp.bfloat16)
        q_ref[0, :, lo + LANES:lo + PAIR_W] = (
            rope_slab(q[:, lo + LANES:lo + PAIR_W]) * scale_a).astype(jnp.bfloat16)

    c_kv = z[:, _C_KV:_C_KV + MLA_KV_LORA]
    ckn = (c_kv * _rms_scale(c_kv) * lnkv_ref[...]).astype(jnp.bfloat16)
    kv = jnp.dot(ckn, wukv_ref[...], preferred_element_type=jnp.float32)
    kpe = rope_slab(z[:, _C_KPE:_C_KPE + LANES]).astype(jnp.bfloat16)
    nk = MLA_HEADS * MLA_NOPE
    for p in range(N_PAIRS):
        lo = p * PAIR_W
        k_ref[0, :, lo:lo + LANES] = kv[:, p * LANES:(p + 1) * LANES].astype(jnp.bfloat16)
        k_ref[0, :, lo + LANES:lo + PAIR_W] = kpe
    _store_values_transposed(vt_ref, 0, kv[:, nk:nk + MLA_HEADS * MLA_V])

    lane = lax.broadcasted_iota(jnp.int32, (1, LANES), 1)
    fz = z[:, _C_F:_C_F + LANES] + bf_ref[...]
    log_f = -(jnp.maximum(-fz, 0.0) + jnp.log1p(jnp.exp(-jnp.abs(fz))))
    log_f = jnp.where(lane < FOX_HEADS, log_f, 0.0)

    @pl.when(i % tiles_per_seq == 0)
    def _():
        carry_ref[...] = jnp.zeros_like(carry_ref)

    hi, mid, lo3 = _split3(log_f)
    tri = tri_ref[...]
    cum = (jnp.dot(tri, hi, preferred_element_type=jnp.float32)
           + jnp.dot(tri, mid, preferred_element_type=jnp.float32)
           + jnp.dot(tri, lo3, preferred_element_type=jnp.float32)) + carry_ref[...]
    tm = cum.shape[0]
    carry_ref[...] = cum[tm - 1:tm, :]

    parts = jnp.concatenate(_split3(cum * LOG2E), axis=1)
    aug_q = jnp.dot(parts, pq_ref[...], preferred_element_type=jnp.float32) + oq_ref[...]
    aug_k = jnp.dot(parts, pk_ref[...], preferred_element_type=jnp.float32) + ok_ref[...]
    scale_b = FOX_DIM ** -0.5 * LOG2E
    for p in range(N_PAIRS):
        lo = p * PAIR_W
        q_ref[1, :, lo:lo + LANES] = (
            z[:, _C_QF + p * LANES:_C_QF + (p + 1) * LANES] * scale_b).astype(jnp.bfloat16)
        q_ref[1, :, lo + LANES:lo + PAIR_W] = aug_q[:, p * LANES:(p + 1) * LANES].astype(jnp.bfloat16)
        k_ref[1, :, lo:lo + LANES] = z[:, _C_KF + p * LANES:_C_KF + (p + 1) * LANES].astype(jnp.bfloat16)
        k_ref[1, :, lo + LANES:lo + PAIR_W] = aug_k[:, p * LANES:(p + 1) * LANES].astype(jnp.bfloat16)
    _store_values_transposed(vt_ref, 1, z[:, _C_VF:_C_VF + FOX_HEADS * FOX_DIM])


def _store_values_transposed(vt_ref, g, v):
    vt = v.T
    ones = jnp.ones((VT_ROWS - MLA_V, vt.shape[1]), vt_ref.dtype)
    for h in range(MLA_HEADS):
        vt_ref[g, h * VT_ROWS:h * VT_ROWS + MLA_V, :] = vt[h * MLA_V:(h + 1) * MLA_V].astype(vt_ref.dtype)
        vt_ref[g, h * VT_ROWS + MLA_V:(h + 1) * VT_ROWS, :] = ones


def _even_weights(w_in, w_uq, w_ukv, b_f):
    d = w_in.shape[0]
    zeros = lambda n: jnp.zeros((d, n), w_in.dtype)
    o_cq, o_ckv = 0, MLA_Q_LORA
    o_kpe = o_ckv + MLA_KV_LORA
    o_qf = o_kpe + MLA_ROPE
    o_kf = o_qf + FOX_HEADS * FOX_DIM
    o_vf = o_kf + FOX_HEADS * FOX_DIM
    o_f = o_vf + FOX_HEADS * FOX_DIM
    kpe1 = w_in[:, o_kpe:o_kpe + ROPE_H]
    kpe2 = w_in[:, o_kpe + ROPE_H:o_kpe + MLA_ROPE]
    kpe_slab = jnp.concatenate([kpe1, kpe1, zeros(32), kpe2, kpe2, zeros(32)], axis=1)
    f_slab = jnp.concatenate([w_in[:, o_f:o_f + FOX_HEADS], zeros(LANES - FOX_HEADS)], axis=1)
    w_in_p = jnp.concatenate([w_in[:, o_cq:o_kpe], kpe_slab, w_in[:, o_qf:o_f], f_slab], axis=1)

    dq = w_uq.shape[0]
    zq = lambda n: jnp.zeros((dq, n), w_uq.dtype)
    hd = MLA_NOPE + MLA_ROPE
    blocks = []
    for p in range(N_PAIRS):
        h0, h1 = 2 * p, 2 * p + 1
        nope = lambda h: w_uq[:, h * hd:h * hd + MLA_NOPE]
        r1 = lambda h: w_uq[:, h * hd + MLA_NOPE:h * hd + MLA_NOPE + ROPE_H]
        r2 = lambda h: w_uq[:, h * hd + MLA_NOPE + ROPE_H:(h + 1) * hd]
        blocks += [nope(h0), nope(h1), r1(h0), r1(h1), zq(32), r2(h0), r2(h1), zq(32)]
    w_uq_p = jnp.concatenate(blocks, axis=1)

    kvd = MLA_NOPE + MLA_V
    k_cols = [w_ukv[:, h * kvd:h * kvd + MLA_NOPE] for h in range(MLA_HEADS)]
    v_cols = [w_ukv[:, h * kvd + MLA_NOPE:(h + 1) * kvd] for h in range(MLA_HEADS)]
    w_ukv_p = jnp.concatenate(k_cols + v_cols, axis=1)
    b_f_p = jnp.concatenate([b_f, jnp.zeros((LANES - FOX_HEADS,), b_f.dtype)])[None, :]
    bf16 = jnp.bfloat16
    return w_in_p.astype(bf16), w_uq_p.astype(bf16), w_ukv_p.astype(bf16), b_f_p


def _aug_placement():
    pq = np.zeros((3 * LANES, N_PAIRS * LANES), np.float32)
    pk = np.zeros((3 * LANES, N_PAIRS * LANES), np.float32)
    oq = np.zeros((1, N_PAIRS * LANES), np.float32)
    ok = np.zeros((1, N_PAIRS * LANES), np.float32)
    for p in range(N_PAIRS):
        for j in range(2):
            h = 2 * p + j
            base = p * LANES + _X1_OFF + j * ROPE_H
            for t in range(3):
                pq[t * LANES + h, base + 3 + t] = 1.0
                pk[t * LANES + h, base + t] = -1.0
                oq[0, base + t] = 1.0
                ok[0, base + 3 + t] = 1.0
    return (jnp.asarray(pq, jnp.bfloat16), jnp.asarray(pk, jnp.bfloat16),
            jnp.asarray(oq), jnp.asarray(ok))


def _mla_rope_tables(s):
    inv = ROPE_BASE ** (-jnp.arange(ROPE_H, dtype=jnp.float32) / ROPE_H)
    ang = jnp.arange(s, dtype=jnp.float32)[:, None] * inv[None, :]
    c, sn = jnp.cos(ang), jnp.sin(ang)
    z = jnp.zeros((s, 32), jnp.float32)
    cos = jnp.concatenate([c, c, z, c, c, z], axis=1)
    sin = jnp.concatenate([-sn, -sn, z, sn, sn, z], axis=1)
    return cos, sin


def _even_proj(x2d, seq, g, w_in_p, ln_q, w_uq_p, ln_kv, w_ukv_p, b_f_p):
    t = x2d.shape[0]
    tm = TM_PROJ
    tiles_per_seq = seq // tm
    cos, sin = _mla_rope_tables(seq)
    tri = jnp.asarray(np.tril(np.ones((tm, tm), np.float32)), jnp.bfloat16)
    pq, pk, oq, ok = _aug_placement()
    const = lambda shape: pl.BlockSpec(shape, lambda i: (0,) * len(shape))
    return pl.pallas_call(
        functools.partial(_even_proj_kernel, tiles_per_seq),
        grid=(t // tm,),
        in_specs=[
            pl.BlockSpec((tm, D_MODEL), lambda i: (i, 0)),
            const((1, D_MODEL)), const((D_MODEL, EVEN_N)),
            const((1, MLA_Q_LORA)), const((MLA_Q_LORA, N_PAIRS * PAIR_W)),
            const((1, MLA_KV_LORA)), const((MLA_KV_LORA, 2 * MLA_HEADS * MLA_NOPE)),
            const((1, LANES)),
            pl.BlockSpec((tm, LANES), lambda i: (i % tiles_per_seq, 0)),
            pl.BlockSpec((tm, LANES), lambda i: (i % tiles_per_seq, 0)),
            const((tm, tm)), const(pq.shape), const(pk.shape), const(oq.shape), const(ok.shape),
        ],
        out_specs=[
            pl.BlockSpec((2, tm, N_PAIRS * PAIR_W), lambda i: (0, i, 0)),
            pl.BlockSpec((2, tm, N_PAIRS * PAIR_W), lambda i: (0, i, 0)),
            pl.BlockSpec((2, None, MLA_HEADS * VT_ROWS, tm),
                         lambda i: (0, i // tiles_per_seq, 0, i % tiles_per_seq)),
        ],
        out_shape=[
            jax.ShapeDtypeStruct((2, t, N_PAIRS * PAIR_W), jnp.bfloat16),
            jax.ShapeDtypeStruct((2, t, N_PAIRS * PAIR_W), jnp.bfloat16),
            jax.ShapeDtypeStruct((2, t // seq, MLA_HEADS * VT_ROWS, seq), jnp.bfloat16),
        ],
        scratch_shapes=[pltpu.VMEM((1, LANES), jnp.float32)],
        compiler_params=_cparams(("arbitrary",)),
        name="even_proj",
    )(x2d, g, w_in_p, ln_q, w_uq_p, ln_kv, w_ukv_p, b_f_p, cos, sin, tri, pq, pk, oq, ok)


def _attn_kernel(q_ref, k_ref, vt_ref, o_ref, qcat_ref, s0_ref, s1_ref, m0_ref, m1_ref,
                 a0_ref, a1_ref, acc_ref):
    g = pl.program_id(0)
    i = pl.program_id(3)
    tq = q_ref.shape[0]
    s_bufs, m_bufs, a_bufs = (s0_ref, s1_ref), (m0_ref, m1_ref), (a0_ref, a1_ref)

    lane = lax.broadcasted_iota(jnp.int32, (1, PAIR_W), 1)

    def head_mask(j):
        a = (lane >= j * MLA_NOPE) & (lane < (j + 1) * MLA_NOPE)
        b = (lane >= LANES + _X1_OFF + j * ROPE_H) & (lane < LANES + _X1_OFF + (j + 1) * ROPE_H)
        c = (lane >= LANES + _X2_OFF + j * ROPE_H) & (lane < LANES + _X2_OFF + (j + 1) * ROPE_H)
        return a | b | c

    q = q_ref[...]
    zero = jnp.zeros_like(q)
    qcat_ref[0:tq, :] = jnp.where(head_mask(0), q, zero)
    qcat_ref[tq:2 * tq, :] = jnp.where(head_mask(1), q, zero)
    acc_ref[...] = jnp.zeros_like(acc_ref)

    q0 = i * tq
    n_chunks = q0 // TK + 1

    def scores(c, m_run, par, masked):
        kstart = pl.multiple_of(c * TK, TK)
        s = lax.dot_general(k_ref[pl.ds(kstart, TK), :], qcat_ref[...], (((1,), (1,)), ((), ())),
                            preferred_element_type=jnp.float32)
        if masked:
            col = lax.broadcasted_iota(jnp.int32, (1, 2 * tq), 1)
            qpos = q0 + jnp.where(col >= tq, col - tq, col)
            qlim = jnp.where(g == 0, qpos | (CHUNK - 1), qpos)
            kpos = kstart + lax.broadcasted_iota(jnp.int32, (TK, 1), 0)
            s = jnp.where(kpos <= qlim, s, NEG)
        s_bufs[par][...] = s
        m_new = jnp.maximum(m_run, jnp.max(s, axis=0, keepdims=True))
        m_bufs[par][...] = m_new
        a_bufs[par][...] = jnp.exp2(m_run - m_new)

    def accumulate(c, par):
        kstart = pl.multiple_of(c * TK, TK)
        alpha = a_bufs[par][...]
        p = jnp.exp2(s_bufs[par][...] - m_bufs[par][...]).astype(jnp.bfloat16)
        for h in range(2):
            cols = slice(h * tq, (h + 1) * tq)
            acc_ref[h] = alpha[:, cols] * acc_ref[h] + jnp.dot(
                vt_ref[h * VT_ROWS:(h + 1) * VT_ROWS, pl.ds(kstart, TK)], p[:, cols],
                preferred_element_type=jnp.float32)

    def stage(c, par, masked):
        scores(c + 1, m_bufs[par][...], 1 - par, masked)
        accumulate(c, par)

    m_init = jnp.full((1, 2 * tq), NEG, jnp.float32)

    @pl.when(n_chunks == 1)
    def _():
        scores(0, m_init, 0, True)
        accumulate(0, 0)

    @pl.when(n_chunks > 1)
    def _():
        scores(0, m_init, 0, False)

    n_pairs = jnp.maximum(n_chunks - 2, 0) // 2

    def body(j, carry):
        for u in range(4):
            stage(4 * j + u, u % 2, False)
        return carry

    lax.fori_loop(0, n_pairs // 2, body, 0)

    @pl.when(n_pairs % 2 == 1)
    def _():
        stage(2 * n_pairs - 2, 0, False)
        stage(2 * n_pairs - 1, 1, False)
    c0 = 2 * n_pairs
    left = n_chunks - 1 - c0

    @pl.when((n_chunks > 1) & (left == 1))
    def _():
        stage(c0, 0, True)
        accumulate(c0 + 1, 1)

    @pl.when((n_chunks > 1) & (left == 2))
    def _():
        stage(c0, 0, False)
        stage(c0 + 1, 1, True)
        accumulate(c0 + 2, 0)

    for h in range(2):
        acc = acc_ref[h]
        o_ref[h * MLA_V:(h + 1) * MLA_V, :] = (acc[0:MLA_V] / acc[MLA_V:MLA_V + 1]).astype(o_ref.dtype)


def _attention(q, k, vt, batch, seq):
    nq = seq // TQ
    f32 = jnp.float32
    return pl.pallas_call(
        _attn_kernel,
        grid=(2, batch, N_PAIRS, nq),
        in_specs=[
            pl.BlockSpec((None, TQ, PAIR_W), lambda g, b, p, i: (g, b * nq + i, p)),
            pl.BlockSpec((None, seq, PAIR_W), lambda g, b, p, i: (g, b, p)),
            pl.BlockSpec((None, None, 2 * VT_ROWS, seq), lambda g, b, p, i: (g, b, p, 0)),
        ],
        out_specs=pl.BlockSpec((None, None, LANES, TQ), lambda g, b, p, i: (g, b, p, i)),
        out_shape=jax.ShapeDtypeStruct((2, batch, N_PAIRS * LANES, seq), jnp.bfloat16),
        scratch_shapes=[pltpu.VMEM((2 * TQ, PAIR_W), jnp.bfloat16),
                        pltpu.VMEM((TK, 2 * TQ), f32), pltpu.VMEM((TK, 2 * TQ), f32),
                        pltpu.VMEM((1, 2 * TQ), f32), pltpu.VMEM((1, 2 * TQ), f32),
                        pltpu.VMEM((1, 2 * TQ), f32), pltpu.VMEM((1, 2 * TQ), f32),
                        pltpu.VMEM((2, VT_ROWS, TQ), f32)],
        compiler_params=_cparams(("arbitrary",) * 4),
        name="attention",
    )(q, k, vt)


def _router_t(lt):
    row = lax.broadcasted_iota(jnp.int32, lt.shape, 0)
    big = jnp.int32(LANES)
    ninf = -jnp.inf
    gl = jnp.where(row < N_GROUPS, lt, ninf)
    gmax = jnp.max(gl, axis=0, keepdims=True)
    gsel = jnp.min(jnp.where(gl == gmax, row, big), axis=0, keepdims=True)
    gsum = jnp.sum(jnp.where(row < N_GROUPS, jnp.exp(lt - gmax), 0.0), axis=0, keepdims=True)
    p_g = 1.0 / gsum
    lo = N_GROUPS + EXPERTS_PER_GROUP * gsel
    e = jnp.where((row >= lo) & (row < lo + EXPERTS_PER_GROUP), lt, ninf)
    m1 = jnp.max(e, axis=0, keepdims=True)
    i1 = jnp.min(jnp.where(e == m1, row, big), axis=0, keepdims=True)
    e2 = jnp.where(row == i1, ninf, e)
    m2 = jnp.max(e2, axis=0, keepdims=True)
    i2 = jnp.min(jnp.where(e2 == m2, row, big), axis=0, keepdims=True)
    a2 = jnp.exp(m2 - m1)
    den = 1.0 + a2
    g1 = p_g / den
    g2 = p_g * a2 / den
    f = lambda v: v.astype(jnp.float32)
    return jnp.where(row == 0, f(i1 - N_GROUPS),
                     jnp.where(row == 1, f(i2 - N_GROUPS),
                               jnp.where(row == 2, g1, jnp.where(row == 3, g2, 0.0))))


def _out_proj_kernel(transposed, x_ref, o_ref, w_ref, ln_ref, wr_ref, br_ref, xo_ref, xg_ref, route_ref):
    tm = x_ref.shape[0]
    hm = tm // OUT_SPLIT
    for part in range(OUT_SPLIT):
        rows = slice(part * hm, (part + 1) * hm)
        if transposed:
            y = x_ref[rows, :]
            nf = o_ref.shape[1]
            for g in range(o_ref.shape[0]):
                y = y + lax.dot_general(o_ref[g, :, rows], w_ref[g * nf:(g + 1) * nf, :],
                                        (((0,), (0,)), ((), ())), preferred_element_type=jnp.float32)
        else:
            y = x_ref[rows, :] + jnp.dot(o_ref[rows, :], w_ref[...], preferred_element_type=jnp.float32)
        xo_ref[rows, :] = y
        xn = y * _rms_scale(y) * ln_ref[...]
        hi = xn.astype(jnp.bfloat16)
        lo = (xn - hi.astype(jnp.float32)).astype(jnp.bfloat16)
        l2 = jnp.dot(jnp.concatenate([hi, lo], axis=1), wr_ref[...], preferred_element_type=jnp.float32)
        lt = (l2[:, :LANES] + l2[:, LANES:] + br_ref[...]).T
        route_ref[rows, :] = _router_t(lt).T
        for c in range(ROWS_PER_TOK):
            xg_ref[pl.ds(part * hm * ROWS_PER_TOK + c, hm, stride=ROWS_PER_TOK), :] = (
                xn[:, c * LANES:(c + 1) * LANES])


def _out_proj(x2d, o, w_out, ln, w_r, b_r, seq):
    t = x2d.shape[0]
    kdim = w_out.shape[0]
    tm = TM_OUT
    transposed = o.ndim == 4
    tiles_per_seq = seq // tm
    if transposed:
        o_spec = pl.BlockSpec((o.shape[0], None, o.shape[2], tm),
                              lambda i: (0, i // tiles_per_seq, 0, i % tiles_per_seq))
    else:
        o_spec = pl.BlockSpec((tm, kdim), lambda i: (i, 0))
    const = lambda shape: pl.BlockSpec(shape, lambda i: (0,) * len(shape))
    return pl.pallas_call(
        functools.partial(_out_proj_kernel, transposed),
        grid=(t // tm,),
        in_specs=[
            pl.BlockSpec((tm, D_MODEL), lambda i: (i, 0)),
            o_spec,
            const((kdim, D_MODEL)), const((1, D_MODEL)), const((2 * D_MODEL, 2 * LANES)), const((1, LANES)),
        ],
        out_specs=[
            pl.BlockSpec((tm, D_MODEL), lambda i: (i, 0)),
            pl.BlockSpec((tm * ROWS_PER_TOK, LANES), lambda i: (i, 0)),
            pl.BlockSpec((tm, LANES), lambda i: (i, 0)),
        ],
        out_shape=[
            jax.ShapeDtypeStruct((t, D_MODEL), jnp.float32),
            jax.ShapeDtypeStruct((t * ROWS_PER_TOK, LANES), jnp.float32),
            jax.ShapeDtypeStruct((t, LANES), jnp.float32),
        ],
        compiler_params=_cparams(("arbitrary",)),
        name="out_proj_router",
    )(x2d, o, w_out, ln, w_r, b_r)


def _router_weights(w_rg, b_rg, w_re, b_re):
    d = w_rg.shape[0]
    pad = LANES - N_GROUPS - N_EXPERTS
    w_r = jnp.concatenate([w_rg, w_re, jnp.zeros((d, pad), w_rg.dtype)], axis=1)
    b_r = jnp.concatenate([b_rg, b_re, jnp.zeros((pad,), b_rg.dtype)])[None, :]
    w_hi = w_r.astype(jnp.bfloat16)
    w_lo = (w_r - w_hi.astype(jnp.float32)).astype(jnp.bfloat16)
    half = jnp.concatenate([w_hi, w_lo], axis=1)
    return jnp.concatenate([half, half], axis=0), b_r


def _dispatch_plan(eid, t):
    a = t * TOP_K
    nb = a // TE + N_EXPERTS
    e_flat = eid.reshape(a)
    experts = jnp.arange(N_EXPERTS, dtype=jnp.int32)
    _, order = lax.sort((e_flat, jnp.arange(a, dtype=jnp.int32)), num_keys=1, is_stable=True)
    counts = jnp.sum((e_flat[:, None] == experts[None, :]).astype(jnp.int32), axis=0)
    start = jnp.cumsum(counts) - counts
    padded = ((counts + TE - 1) // TE) * TE
    pend = jnp.cumsum(padded)
    pstart = pend - padded
    blk0 = jnp.arange(nb, dtype=jnp.int32) * TE
    blk_e = jnp.minimum(jnp.sum((blk0[:, None] >= pend[None, :]).astype(jnp.int32), axis=1), N_EXPERTS - 1)
    blk_used = (blk0 < pend[-1]).astype(jnp.int32)
    onehot = (blk_e[:, None] == experts[None, :]).astype(jnp.int32)
    per_blk = lambda v: jnp.sum(onehot * v[None, :], axis=1)[:, None]
    r = jnp.arange(TE, dtype=jnp.int32)[None, :]
    off = blk0[:, None] + r - per_blk(pstart)
    valid = (off < per_blk(counts)) & (blk_used[:, None] > 0)
    src = jnp.where(valid, per_blk(start) + off, 0)
    assign = order[src]
    src_tok = jnp.where(valid, assign // TOP_K, 0).astype(jnp.int32)
    dst_row = jnp.where(valid, assign, a + r).astype(jnp.int32)
    return blk_e, blk_used, src_tok[:, None, :], dst_row[:, None, :]


def _moe_kernel(blk_e_ref, blk_used_ref, src_ref, src_next_ref, dst_ref, xg_ref, wg_ref, wu_ref, wd_ref,
                ys_ref, xbuf, ybuf, wg_bf, wu_bf, wd_bf, gsem, ssem):
    i = pl.program_id(0)
    nb = pl.num_programs(0)
    used = blk_used_ref[i] > 0
    slot = i % 2
    block_rows = TE * ROWS_PER_TOK

    def start_gather(idx_ref, s):
        for r in range(TE):
            tok = idx_ref[0, 0, r]
            pltpu.make_async_copy(
                xg_ref.at[pl.ds(pl.multiple_of(tok * ROWS_PER_TOK, ROWS_PER_TOK), ROWS_PER_TOK), :],
                xbuf.at[s, pl.ds(r * ROWS_PER_TOK, ROWS_PER_TOK), :],
                gsem.at[s]).start(priority=r % 2)

    def wait_gather(s):
        pltpu.make_async_copy(xg_ref.at[pl.ds(0, block_rows), :], xbuf.at[s], gsem.at[s]).wait()

    def start_scatter():
        for r in range(TE):
            row = dst_ref[0, 0, r]
            pltpu.make_async_copy(
                ybuf.at[pl.ds(r * ROWS_PER_TOK, ROWS_PER_TOK), :],
                ys_ref.at[pl.ds(pl.multiple_of(row * ROWS_PER_TOK, ROWS_PER_TOK), ROWS_PER_TOK), :],
                ssem.at[0]).start(priority=r % 2)

    def wait_scatter():
        pltpu.make_async_copy(ybuf, ys_ref.at[pl.ds(0, block_rows), :], ssem.at[0]).wait()

    @pl.when(i == 0)
    def _():
        ybuf[...] = jnp.zeros_like(ybuf)
        pltpu.make_async_copy(
            ybuf, ys_ref.at[pl.ds(ys_ref.shape[0] - block_rows, block_rows), :], ssem.at[0]).start()
        start_gather(src_ref, 0)

    @pl.when(used)
    def _():
        start_gather(src_next_ref, 1 - slot)
        prev_e = blk_e_ref[jnp.maximum(i - 1, 0)]

        @pl.when((i == 0) | (blk_e_ref[i] != prev_e))
        def _():
            wg_bf[...] = wg_ref[...].astype(jnp.bfloat16)
            wu_bf[...] = wu_ref[...].astype(jnp.bfloat16)
            wd_bf[...] = wd_ref[...].astype(jnp.bfloat16)

        wait_gather(slot)
        xb = xbuf.at[slot]
        x = jnp.concatenate([xb[pl.ds(c, TE, stride=ROWS_PER_TOK), :] for c in range(ROWS_PER_TOK)],
                            axis=1).astype(jnp.bfloat16)
        hg = jnp.dot(x, wg_bf[...], preferred_element_type=jnp.float32)
        hu = jnp.dot(x, wu_bf[...], preferred_element_type=jnp.float32)
        hid = (hg * jax.nn.sigmoid(hg) * hu).astype(jnp.bfloat16)
        y = jnp.dot(hid, wd_bf[...], preferred_element_type=jnp.float32)

        wait_scatter()
        for c in range(ROWS_PER_TOK):
            ybuf[pl.ds(c, TE, stride=ROWS_PER_TOK), :] = y[:, c * LANES:(c + 1) * LANES]
        start_scatter()

        next_used = (i + 1 < nb) & (blk_used_ref[jnp.minimum(i + 1, nb - 1)] > 0)

        @pl.when(jnp.logical_not(next_used))
        def _():
            wait_gather(1 - slot)
            wait_scatter()


def _moe(xg, eid, w_gate, w_up, w_down, layer, t):
    a = t * TOP_K
    nb = a // TE + N_EXPERTS
    blk_e, blk_used, src_tok, dst_row = _dispatch_plan(eid, t)
    idx_spec = lambda f: pl.BlockSpec((1, 1, TE), f, memory_space=pltpu.SMEM)
    grid_spec = pltpu.PrefetchScalarGridSpec(
        num_scalar_prefetch=2,
        grid=(nb,),
        in_specs=[
            idx_spec(lambda i, be, bu: (i, 0, 0)),
            idx_spec(lambda i, be, bu: (jnp.minimum(i + 1, nb - 1), 0, 0)),
            idx_spec(lambda i, be, bu: (i, 0, 0)),
            pl.BlockSpec(memory_space=pl.ANY),
            pl.BlockSpec((None, None, D_MODEL, D_EXPERT), lambda i, be, bu: (layer, be[i], 0, 0)),
            pl.BlockSpec((None, None, D_MODEL, D_EXPERT), lambda i, be, bu: (layer, be[i], 0, 0)),
            pl.BlockSpec((None, None, D_EXPERT, D_MODEL), lambda i, be, bu: (layer, be[i], 0, 0)),
        ],
        out_specs=pl.BlockSpec(memory_space=pl.ANY),
        scratch_shapes=[
            pltpu.VMEM((2, TE * ROWS_PER_TOK, LANES), jnp.float32),
            pltpu.VMEM((TE * ROWS_PER_TOK, LANES), jnp.float32),
            pltpu.VMEM((D_MODEL, D_EXPERT), jnp.bfloat16),
            pltpu.VMEM((D_MODEL, D_EXPERT), jnp.bfloat16),
            pltpu.VMEM((D_EXPERT, D_MODEL), jnp.bfloat16),
            pltpu.SemaphoreType.DMA((2,)),
            pltpu.SemaphoreType.DMA((1,)),
        ],
    )
    return pl.pallas_call(
        _moe_kernel,
        grid_spec=grid_spec,
        out_shape=jax.ShapeDtypeStruct(((a + TE) * ROWS_PER_TOK, LANES), jnp.float32),
        compiler_params=_cparams(("arbitrary",)),
        name="moe_experts",
    )(blk_e, blk_used, src_tok, src_tok, dst_row, xg, w_gate, w_up, w_down)


def _combine(x, ys_ref, route_ref, tm):
    stride = TOP_K * ROWS_PER_TOK
    y0 = jnp.concatenate([ys_ref[pl.ds(c, tm, stride=stride), :] for c in range(ROWS_PER_TOK)], axis=1)
    y1 = jnp.concatenate([ys_ref[pl.ds(ROWS_PER_TOK + c, tm, stride=stride), :]
                          for c in range(ROWS_PER_TOK)], axis=1)
    route = route_ref[...]
    return x + route[:, 2:3] * y0 + route[:, 3:4] * y1


def _odd_proj_kernel(x_ref, ys_ref, route_ref, g_ref, w_ref, cos_ref, sin_ref,
                     xo_ref, q_ref, k_ref, v_ref, gate_ref):
    tm = x_ref.shape[0]
    x = _combine(x_ref[...], ys_ref, route_ref, tm)
    xo_ref[...] = x
    xn = (x * _rms_scale(x) * g_ref[...]).astype(jnp.bfloat16)
    cos = cos_ref[...]
    sin = sin_ref[...]
    nqk = RET_HEADS * RET_QK
    half = RET_QK // 2

    def rope_store(z, dst, scale):
        for h in range(RET_HEADS):
            x1 = z[:, h * RET_QK:h * RET_QK + half]
            x2 = z[:, h * RET_QK + half:(h + 1) * RET_QK]
            dst[:, h * RET_QK:h * RET_QK + half] = ((x1 * cos - x2 * sin) * scale).astype(dst.dtype)
            dst[:, h * RET_QK + half:(h + 1) * RET_QK] = ((x1 * sin + x2 * cos) * scale).astype(dst.dtype)

    zq = jnp.dot(xn, w_ref[:, 0:nqk], preferred_element_type=jnp.float32)
    rope_store(zq, q_ref, 1.0)
    zk = jnp.dot(xn, w_ref[:, nqk:2 * nqk], preferred_element_type=jnp.float32)
    rope_store(zk, k_ref, RET_QK ** -0.5)
    nv = RET_HEADS * RET_V
    zv = jnp.dot(xn, w_ref[:, 2 * nqk:2 * nqk + nv], preferred_element_type=jnp.float32)
    v_ref[...] = zv.astype(v_ref.dtype)
    zg = jnp.dot(xn, w_ref[:, 2 * nqk + nv:2 * nqk + 2 * nv], preferred_element_type=jnp.float32)
    gate_ref[...] = (zg * jax.nn.sigmoid(zg)).astype(gate_ref.dtype)


def _ret_rope_tables(s):
    half = RET_QK // 2
    inv = ROPE_BASE ** (-jnp.arange(half, dtype=jnp.float32) / half)
    ang = jnp.arange(s, dtype=jnp.float32)[:, None] * inv[None, :]
    return jnp.cos(ang), jnp.sin(ang)


def _odd_proj(x2d, ys, route, seq, g, w_in):
    t = x2d.shape[0]
    tm = TM_PROJ
    tiles_per_seq = seq // tm
    cos, sin = _ret_rope_tables(seq)
    nqk = RET_HEADS * RET_QK
    nv = RET_HEADS * RET_V
    n_in = w_in.shape[1]
    const = lambda shape: pl.BlockSpec(shape, lambda i: (0,) * len(shape))
    tok = lambda n: pl.BlockSpec((tm, n), lambda i: (i, 0))
    return pl.pallas_call(
        _odd_proj_kernel,
        grid=(t // tm,),
        in_specs=[
            tok(D_MODEL),
            pl.BlockSpec((tm * TOP_K * ROWS_PER_TOK, LANES), lambda i: (i, 0)),
            tok(LANES),
            const((1, D_MODEL)), const((D_MODEL, n_in)),
            pl.BlockSpec((tm, LANES), lambda i: (i % tiles_per_seq, 0)),
            pl.BlockSpec((tm, LANES), lambda i: (i % tiles_per_seq, 0)),
        ],
        out_specs=[tok(D_MODEL), tok(nqk), tok(nqk), tok(nv), tok(nv)],
        out_shape=[
            jax.ShapeDtypeStruct((t, D_MODEL), jnp.float32),
            jax.ShapeDtypeStruct((t, nqk), jnp.bfloat16),
            jax.ShapeDtypeStruct((t, nqk), jnp.bfloat16),
            jax.ShapeDtypeStruct((t, nv), jnp.bfloat16),
            jax.ShapeDtypeStruct((t, nv), jnp.bfloat16),
        ],
        compiler_params=_cparams(("arbitrary",)),
        name="odd_proj",
    )(x2d, ys, route, g, w_in, cos, sin)


def _retention_kernel(q_ref, k_ref, v_ref, gate_ref, dmask_ref, xi_ref, zeta_ref, gc_ref, o_ref, state_ref):
    n = pl.program_id(2)

    @pl.when(n == 0)
    def _():
        state_ref[...] = jnp.zeros_like(state_ref)

    xi = xi_ref[...][:, 0:1]
    zeta = zeta_ref[...][:, 0:1]
    g_c = gc_ref[...][:, 0:1]
    for j in range(q_ref.shape[0] // RET_C):
        rows = slice(j * RET_C, (j + 1) * RET_C)
        q = q_ref[rows, :]
        k = k_ref[rows, :]
        v = v_ref[rows, :]
        inner = lax.dot_general(q, k, (((1,), (1,)), ((), ())),
                                preferred_element_type=jnp.float32) * dmask_ref[...]
        o = jnp.dot(inner.astype(jnp.bfloat16), v, preferred_element_type=jnp.float32)
        state = state_ref[...]
        cross = jnp.dot(q, state.astype(jnp.bfloat16), preferred_element_type=jnp.float32)
        o = o + cross * xi
        kz = (k.astype(jnp.float32) * zeta).astype(jnp.bfloat16)
        upd = lax.dot_general(kz, v, (((0,), (0,)), ((), ())), preferred_element_type=jnp.float32)
        state_ref[...] = g_c * state + upd
        o = o * _rms_scale(o)
        o_ref[rows, :] = (o * gate_ref[rows, :].astype(jnp.float32)).astype(o_ref.dtype)


def _retention_tables():
    c = RET_C
    log_g = jnp.log(1.0 - jnp.exp2(-5.0 - jnp.arange(RET_HEADS, dtype=jnp.float32)))
    j = jnp.arange(c, dtype=jnp.float32)
    diff = j[:, None] - j[None, :]
    dmask = jnp.where(diff >= 0, jnp.exp(jnp.maximum(diff, 0.0)[None] * log_g[:, None, None]), 0.0)
    xi = jnp.exp((j[None, :] + 1.0) * log_g[:, None])
    zeta = jnp.exp((c - 1.0 - j)[None, :] * log_g[:, None])
    g_c = jnp.exp(c * log_g)
    bc = lambda a: jnp.broadcast_to(a[..., None], a.shape + (LANES,))
    return dmask, bc(xi), bc(zeta), bc(g_c[:, None])


def _retention(q, k, v, gate, batch, seq):
    nc = seq // RET_BLK
    dmask, xi, zeta, g_c = _retention_tables()
    row = lambda n: pl.BlockSpec((RET_BLK, n), lambda b, h, c: (b * nc + c, h))
    head = lambda shape: pl.BlockSpec((None,) + shape, lambda b, h, c: (h, 0, 0))
    return pl.pallas_call(
        _retention_kernel,
        grid=(batch, RET_HEADS, nc),
        in_specs=[row(RET_QK), row(RET_QK), row(RET_V), row(RET_V),
                  head((RET_C, RET_C)), head((RET_C, LANES)), head((RET_C, LANES)), head((1, LANES))],
        out_specs=row(RET_V),
        out_shape=jax.ShapeDtypeStruct((batch * seq, RET_HEADS * RET_V), jnp.bfloat16),
        scratch_shapes=[pltpu.VMEM((RET_QK, RET_V), jnp.float32)],
        compiler_params=_cparams(("arbitrary",) * 3),
        name="retention",
    )(q, k, v, gate, dmask, xi, zeta, g_c)


def _final_kernel(x_ref, ys_ref, route_ref, g_ref, o_ref):
    x = _combine(x_ref[...], ys_ref, route_ref, x_ref.shape[0])
    o_ref[...] = x * _rms_scale(x) * g_ref[...]


def _final(x2d, ys, route, g):
    t = x2d.shape[0]
    tm = TM_OUT
    return pl.pallas_call(
        _final_kernel,
        grid=(t // tm,),
        in_specs=[
            pl.BlockSpec((tm, D_MODEL), lambda i: (i, 0)),
            pl.BlockSpec((tm * TOP_K * ROWS_PER_TOK, LANES), lambda i: (i, 0)),
            pl.BlockSpec((tm, LANES), lambda i: (i, 0)),
            pl.BlockSpec((1, D_MODEL), lambda i: (0, 0)),
        ],
        out_specs=pl.BlockSpec((tm, D_MODEL), lambda i: (i, 0)),
        out_shape=jax.ShapeDtypeStruct((t, D_MODEL), jnp.float32),
        compiler_params=_cparams(("arbitrary",)),
        name="final_norm",
    )(x2d, ys, route, g)


def kernel(x, ln_mix_e, w_in_e, ln_q_e, w_uq_e, ln_kv_e, w_ukv_e, b_f_e, w_out_e, ln_mix_o, w_in_o,
           w_out_o, ln_ffn, w_rg, b_rg, w_re, b_re, w_gate, w_up, w_down, ln_f):
    batch, seq, d = x.shape
    t = batch * seq
    assert d == D_MODEL and seq % TK == 0 and seq % TQ == 0 and seq % RET_BLK == 0
    assert seq % TM_PROJ == 0 and t % TM_OUT == 0 and t % TE == 0
    bf16 = jnp.bfloat16
    x2d = x.reshape(t, d)

    w_in_p, w_uq_p, w_ukv_p, b_f_p = _even_weights(w_in_e[0], w_uq_e[0], w_ukv_e[0], b_f_e[0])
    q, k, vt = _even_proj(x2d, seq, ln_mix_e[0][None, :], w_in_p, ln_q_e[0][None, :], w_uq_p,
                         ln_kv_e[0][None, :], w_ukv_p, b_f_p)
    o_t = _attention(q, k, vt, batch, seq)
    w_r, b_r = _router_weights(w_rg[0], b_rg[0], w_re[0], b_re[0])
    x2d, xg, route = _out_proj(x2d, o_t, w_out_e[0].astype(bf16), ln_ffn[0][None, :], w_r, b_r, seq)
    ys = _moe(xg, route[:, 0:TOP_K].astype(jnp.int32), w_gate, w_up, w_down, 0, t)

    x2d, rq, rk, rv, rg = _odd_proj(x2d, ys, route, seq, ln_mix_o[0][None, :], w_in_o[0].astype(bf16))
    og = _retention(rq, rk, rv, rg, batch, seq)
    w_r, b_r = _router_weights(w_rg[1], b_rg[1], w_re[1], b_re[1])
    x2d, xg, route = _out_proj(x2d, og, w_out_o[0].astype(bf16), ln_ffn[1][None, :], w_r, b_r, seq)
    ys = _moe(xg, route[:, 0:TOP_K].astype(jnp.int32), w_gate, w_up, w_down, 1, t)

    out = _final(x2d, ys, route, ln_f[None, :])
    return out.reshape(batch, seq, d)
```

```python
import functools
import math

import numpy as np
import jax
import jax.numpy as jnp
from jax import lax
from jax.experimental import pallas as pl
from jax.experimental.pallas import tpu as pltpu

D_MODEL = 1024
CHUNK = 64
ROPE_BASE = 10000.0
EPS = 1e-6
MLA_HEADS = 8
MLA_NOPE = 64
MLA_ROPE = 32
MLA_V = 64
MLA_Q_LORA = 256
MLA_KV_LORA = 128
FOX_HEADS = 8
FOX_DIM = 64
RET_HEADS = 4
RET_QK = 256
RET_V = 512
N_GROUPS = 4
EXPERTS_PER_GROUP = 8
N_EXPERTS = N_GROUPS * EXPERTS_PER_GROUP
TOP_K = 2
D_EXPERT = 512
N_PAIR_CLASSES = N_GROUPS * EXPERTS_PER_GROUP * (EXPERTS_PER_GROUP - 1) // 2
N_CLASS_IDS = N_GROUPS * EXPERTS_PER_GROUP * EXPERTS_PER_GROUP

LANES = 128
SUBLANES = 8
VMEM_LIMIT = 52 * 1024 * 1024

TM_PROJ = 256
TM_OUT = 512
OUT_SPLIT = 2
TQ = 512
VT_ROWS = 80
TK = 512
LOG2E = math.log2(math.e)
RET_C = 256
RET_BLK = 1024
TE = 128
ROWS_PER_TOK = D_MODEL // LANES

N_PAIRS = MLA_HEADS // 2
PAIR_W = 256
ROPE_H = MLA_ROPE // 2
NEG = -1e30

_X1_OFF = 0
_X2_OFF = 64
_AUG_W = 6

_C_Q = 0
_C_KV = _C_Q + MLA_Q_LORA
_C_KPE = _C_KV + MLA_KV_LORA
_C_QF = _C_KPE + LANES
_C_KF = _C_QF + FOX_HEADS * FOX_DIM
_C_VF = _C_KF + FOX_HEADS * FOX_DIM
_C_F = _C_VF + FOX_HEADS * FOX_DIM
EVEN_N = _C_F + LANES


def _cparams(sem):
    return pltpu.CompilerParams(dimension_semantics=sem, vmem_limit_bytes=VMEM_LIMIT)


def _rms_scale(x):
    return lax.rsqrt(jnp.mean(x * x, axis=-1, keepdims=True) + EPS)


def _split3(c):
    hi = c.astype(jnp.bfloat16)
    r1 = c - hi.astype(jnp.float32)
    mid = r1.astype(jnp.bfloat16)
    r2 = r1 - mid.astype(jnp.float32)
    lo = r2.astype(jnp.bfloat16)
    return hi, mid, lo


def _even_proj_kernel(tiles_per_seq, x_ref, g_ref, win_ref, lnq_ref, wuq_ref, lnkv_ref, wukv_ref,
                      bf_ref, cos_ref, sin_ref, tri_ref, pq_ref, pk_ref, oq_ref, ok_ref,
                      q_ref, k_ref, vt_ref, carry_ref):
    i = pl.program_id(0)
    x = x_ref[...]
    xn = (x * _rms_scale(x) * g_ref[...]).astype(jnp.bfloat16)
    z = jnp.dot(xn, win_ref[...], preferred_element_type=jnp.float32)
    cos = cos_ref[...]
    sin = sin_ref[...]

    def rope_slab(s):
        return s * cos + pltpu.roll(s, 64, axis=1) * sin

    c_q = z[:, _C_Q:_C_Q + MLA_Q_LORA]
    cqn = (c_q * _rms_scale(c_q) * lnq_ref[...]).astype(jnp.bfloat16)
    q = jnp.dot(cqn, wuq_ref[...], preferred_element_type=jnp.float32)
    scale_a = (MLA_NOPE + MLA_ROPE) ** -0.5 * LOG2E
    for p in range(N_PAIRS):
        lo = p * PAIR_W
        q_ref[0, :, lo:lo + LANES] = (q[:, lo:lo + LANES] * scale_a).astype(jnp.bfloat16)
        q_ref[0, :, lo + LANES:lo + PAIR_W] = (
            rope_slab(q[:, lo + LANES:lo + PAIR_W]) * scale_a).astype(jnp.bfloat16)

    c_kv = z[:, _C_KV:_C_KV + MLA_KV_LORA]
    ckn = (c_kv * _rms_scale(c_kv) * lnkv_ref[...]).astype(jnp.bfloat16)
    kv = jnp.dot(ckn, wukv_ref[...], preferred_element_type=jnp.float32)
    kpe = rope_slab(z[:, _C_KPE:_C_KPE + LANES]).astype(jnp.bfloat16)
    nk = MLA_HEADS * MLA_NOPE
    for p in range(N_PAIRS):
        lo = p * PAIR_W
        k_ref[0, :, lo:lo + LANES] = kv[:, p * LANES:(p + 1) * LANES].astype(jnp.bfloat16)
        k_ref[0, :, lo + LANES:lo + PAIR_W] = kpe
    _store_values_transposed(vt_ref, 0, kv[:, nk:nk + MLA_HEADS * MLA_V])

    lane = lax.broadcasted_iota(jnp.int32, (1, LANES), 1)
    fz = z[:, _C_F:_C_F + LANES] + bf_ref[...]
    log_f = -(jnp.maximum(-fz, 0.0) + jnp.log1p(jnp.exp(-jnp.abs(fz))))
    log_f = jnp.where(lane < FOX_HEADS, log_f, 0.0)

    @pl.when(i % tiles_per_seq == 0)
    def _():
        carry_ref[...] = jnp.zeros_like(carry_ref)

    hi, mid, lo3 = _split3(log_f)
    tri = tri_ref[...]
    cum = (jnp.dot(tri, hi, preferred_element_type=jnp.float32)
           + jnp.dot(tri, mid, preferred_element_type=jnp.float32)
           + jnp.dot(tri, lo3, preferred_element_type=jnp.float32)) + carry_ref[...]
    tm = cum.shape[0]
    carry_ref[...] = cum[tm - 1:tm, :]

    parts = jnp.concatenate(_split3(cum * LOG2E), axis=1)
    aug_q = jnp.dot(parts, pq_ref[...], preferred_element_type=jnp.float32) + oq_ref[...]
    aug_k = jnp.dot(parts, pk_ref[...], preferred_element_type=jnp.float32) + ok_ref[...]
    scale_b = FOX_DIM ** -0.5 * LOG2E
    for p in range(N_PAIRS):
        lo = p * PAIR_W
        q_ref[1, :, lo:lo + LANES] = (
            z[:, _C_QF + p * LANES:_C_QF + (p + 1) * LANES] * scale_b).astype(jnp.bfloat16)
        q_ref[1, :, lo + LANES:lo + PAIR_W] = aug_q[:, p * LANES:(p + 1) * LANES].astype(jnp.bfloat16)
        k_ref[1, :, lo:lo + LANES] = z[:, _C_KF + p * LANES:_C_KF + (p + 1) * LANES].astype(jnp.bfloat16)
        k_ref[1, :, lo + LANES:lo + PAIR_W] = aug_k[:, p * LANES:(p + 1) * LANES].astype(jnp.bfloat16)
    _store_values_transposed(vt_ref, 1, z[:, _C_VF:_C_VF + FOX_HEADS * FOX_DIM])


def _store_values_transposed(vt_ref, g, v):
    vt = v.T
    ones = jnp.ones((VT_ROWS - MLA_V, vt.shape[1]), vt_ref.dtype)
    for h in range(MLA_HEADS):
        vt_ref[g, h * VT_ROWS:h * VT_ROWS + MLA_V, :] = vt[h * MLA_V:(h + 1) * MLA_V].astype(vt_ref.dtype)
        vt_ref[g, h * VT_ROWS + MLA_V:(h + 1) * VT_ROWS, :] = ones


def _even_weights(w_in, w_uq, w_ukv, b_f):
    d = w_in.shape[0]
    zeros = lambda n: jnp.zeros((d, n), w_in.dtype)
    o_cq, o_ckv = 0, MLA_Q_LORA
    o_kpe = o_ckv + MLA_KV_LORA
    o_qf = o_kpe + MLA_ROPE
    o_kf = o_qf + FOX_HEADS * FOX_DIM
    o_vf = o_kf + FOX_HEADS * FOX_DIM
    o_f = o_vf + FOX_HEADS * FOX_DIM
    kpe1 = w_in[:, o_kpe:o_kpe + ROPE_H]
    kpe2 = w_in[:, o_kpe + ROPE_H:o_kpe + MLA_ROPE]
    kpe_slab = jnp.concatenate([kpe1, kpe1, zeros(32), kpe2, kpe2, zeros(32)], axis=1)
    f_slab = jnp.concatenate([w_in[:, o_f:o_f + FOX_HEADS], zeros(LANES - FOX_HEADS)], axis=1)
    w_in_p = jnp.concatenate([w_in[:, o_cq:o_kpe], kpe_slab, w_in[:, o_qf:o_f], f_slab], axis=1)

    dq = w_uq.shape[0]
    zq = lambda n: jnp.zeros((dq, n), w_uq.dtype)
    hd = MLA_NOPE + MLA_ROPE
    blocks = []
    for p in range(N_PAIRS):
        h0, h1 = 2 * p, 2 * p + 1
        nope = lambda h: w_uq[:, h * hd:h * hd + MLA_NOPE]
        r1 = lambda h: w_uq[:, h * hd + MLA_NOPE:h * hd + MLA_NOPE + ROPE_H]
        r2 = lambda h: w_uq[:, h * hd + MLA_NOPE + ROPE_H:(h + 1) * hd]
        blocks += [nope(h0), nope(h1), r1(h0), r1(h1), zq(32), r2(h0), r2(h1), zq(32)]
    w_uq_p = jnp.concatenate(blocks, axis=1)

    kvd = MLA_NOPE + MLA_V
    k_cols = [w_ukv[:, h * kvd:h * kvd + MLA_NOPE] for h in range(MLA_HEADS)]
    v_cols = [w_ukv[:, h * kvd + MLA_NOPE:(h + 1) * kvd] for h in range(MLA_HEADS)]
    w_ukv_p = jnp.concatenate(k_cols + v_cols, axis=1)
    b_f_p = jnp.concatenate([b_f, jnp.zeros((LANES - FOX_HEADS,), b_f.dtype)])[None, :]
    bf16 = jnp.bfloat16
    return w_in_p.astype(bf16), w_uq_p.astype(bf16), w_ukv_p.astype(bf16), b_f_p


def _aug_placement():
    pq = np.zeros((3 * LANES, N_PAIRS * LANES), np.float32)
    pk = np.zeros((3 * LANES, N_PAIRS * LANES), np.float32)
    oq = np.zeros((1, N_PAIRS * LANES), np.float32)
    ok = np.zeros((1, N_PAIRS * LANES), np.float32)
    for p in range(N_PAIRS):
        for j in range(2):
            h = 2 * p + j
            base = p * LANES + _X1_OFF + j * ROPE_H
            for t in range(3):
                pq[t * LANES + h, base + 3 + t] = 1.0
                pk[t * LANES + h, base + t] = -1.0
                oq[0, base + t] = 1.0
                ok[0, base + 3 + t] = 1.0
    return (jnp.asarray(pq, jnp.bfloat16), jnp.asarray(pk, jnp.bfloat16),
            jnp.asarray(oq), jnp.asarray(ok))


def _mla_rope_tables(s):
    inv = ROPE_BASE ** (-jnp.arange(ROPE_H, dtype=jnp.float32) / ROPE_H)
    ang = jnp.arange(s, dtype=jnp.float32)[:, None] * inv[None, :]
    c, sn = jnp.cos(ang), jnp.sin(ang)
    z = jnp.zeros((s, 32), jnp.float32)
    cos = jnp.concatenate([c, c, z, c, c, z], axis=1)
    sin = jnp.concatenate([-sn, -sn, z, sn, sn, z], axis=1)
    return cos, sin


def _even_proj(x2d, seq, g, w_in_p, ln_q, w_uq_p, ln_kv, w_ukv_p, b_f_p):
    t = x2d.shape[0]
    tm = TM_PROJ
    tiles_per_seq = seq // tm
    cos, sin = _mla_rope_tables(seq)
    tri = jnp.asarray(np.tril(np.ones((tm, tm), np.float32)), jnp.bfloat16)
    pq, pk, oq, ok = _aug_placement()
    const = lambda shape: pl.BlockSpec(shape, lambda i: (0,) * len(shape))
    return pl.pallas_call(
        functools.partial(_even_proj_kernel, tiles_per_seq),
        grid=(t // tm,),
        in_specs=[
            pl.BlockSpec((tm, D_MODEL), lambda i: (i, 0)),
            const((1, D_MODEL)), const((D_MODEL, EVEN_N)),
            const((1, MLA_Q_LORA)), const((MLA_Q_LORA, N_PAIRS * PAIR_W)),
            const((1, MLA_KV_LORA)), const((MLA_KV_LORA, 2 * MLA_HEADS * MLA_NOPE)),
            const((1, LANES)),
            pl.BlockSpec((tm, LANES), lambda i: (i % tiles_per_seq, 0)),
            pl.BlockSpec((tm, LANES), lambda i: (i % tiles_per_seq, 0)),
            const((tm, tm)), const(pq.shape), const(pk.shape), const(oq.shape), const(ok.shape),
        ],
        out_specs=[
            pl.BlockSpec((2, tm, N_PAIRS * PAIR_W), lambda i: (0, i, 0)),
            pl.BlockSpec((2, tm, N_PAIRS * PAIR_W), lambda i: (0, i, 0)),
            pl.BlockSpec((2, None, MLA_HEADS * VT_ROWS, tm),
                         lambda i: (0, i // tiles_per_seq, 0, i % tiles_per_seq)),
        ],
        out_shape=[
            jax.ShapeDtypeStruct((2, t, N_PAIRS * PAIR_W), jnp.bfloat16),
            jax.ShapeDtypeStruct((2, t, N_PAIRS * PAIR_W), jnp.bfloat16),
            jax.ShapeDtypeStruct((2, t // seq, MLA_HEADS * VT_ROWS, seq), jnp.bfloat16),
        ],
        scratch_shapes=[pltpu.VMEM((1, LANES), jnp.float32)],
        compiler_params=_cparams(("arbitrary",)),
        name="even_proj",
    )(x2d, g, w_in_p, ln_q, w_uq_p, ln_kv, w_ukv_p, b_f_p, cos, sin, tri, pq, pk, oq, ok)


def _attn_kernel(group, q_ref, k_ref, vt_ref, o_ref, qcat_ref, s0_ref, s1_ref, m0_ref, m1_ref,
                 a0_ref, a1_ref, acc_ref):
    i = pl.program_id(2)
    tq = q_ref.shape[0]
    s_bufs, m_bufs, a_bufs = (s0_ref, s1_ref), (m0_ref, m1_ref), (a0_ref, a1_ref)

    lane = lax.broadcasted_iota(jnp.int32, (1, PAIR_W), 1)

    def head_mask(j):
        a = (lane >= j * MLA_NOPE) & (lane < (j + 1) * MLA_NOPE)
        b = (lane >= LANES + _X1_OFF + j * ROPE_H) & (lane < LANES + _X1_OFF + (j + 1) * ROPE_H)
        c = (lane >= LANES + _X2_OFF + j * ROPE_H) & (lane < LANES + _X2_OFF + (j + 1) * ROPE_H)
        return a | b | c

    q = q_ref[...]
    zero = jnp.zeros_like(q)
    qcat_ref[0:tq, :] = jnp.where(head_mask(0), q, zero)
    qcat_ref[tq:2 * tq, :] = jnp.where(head_mask(1), q, zero)
    acc_ref[...] = jnp.zeros_like(acc_ref)

    q0 = i * tq
    n_chunks = q0 // TK + 1

    def scores(c, m_run, par, masked):
        kstart = pl.multiple_of(c * TK, TK)
        s = lax.dot_general(k_ref[pl.ds(kstart, TK), :], qcat_ref[...], (((1,), (1,)), ((), ())),
                            preferred_element_type=jnp.float32)
        if masked:
            col = lax.broadcasted_iota(jnp.int32, (1, 2 * tq), 1)
            qpos = q0 + jnp.where(col >= tq, col - tq, col)
            qlim = (qpos | (CHUNK - 1)) if group == 0 else qpos
            kpos = kstart + lax.broadcasted_iota(jnp.int32, (TK, 1), 0)
            s = jnp.where(kpos <= qlim, s, NEG)
        s_bufs[par][...] = s
        m_new = jnp.maximum(m_run, jnp.max(s, axis=0, keepdims=True))
        m_bufs[par][...] = m_new
        a_bufs[par][...] = jnp.exp2(m_run - m_new)

    def accumulate(c, par):
        kstart = pl.multiple_of(c * TK, TK)
        alpha = a_bufs[par][...]
        p = jnp.exp2(s_bufs[par][...] - m_bufs[par][...]).astype(jnp.bfloat16)
        for h in range(2):
            cols = slice(h * tq, (h + 1) * tq)
            acc_ref[h] = alpha[:, cols] * acc_ref[h] + jnp.dot(
                vt_ref[h * VT_ROWS:(h + 1) * VT_ROWS, pl.ds(kstart, TK)], p[:, cols],
                preferred_element_type=jnp.float32)

    def stage(c, par, masked):
        scores(c + 1, m_bufs[par][...], 1 - par, masked)
        accumulate(c, par)

    m_init = jnp.full((1, 2 * tq), NEG, jnp.float32)

    @pl.when(n_chunks == 1)
    def _():
        scores(0, m_init, 0, True)
        accumulate(0, 0)

    @pl.when(n_chunks > 1)
    def _():
        scores(0, m_init, 0, False)

    n_pairs = jnp.maximum(n_chunks - 2, 0) // 2

    def body(j, carry):
        for u in range(4):
            stage(4 * j + u, u % 2, False)
        return carry

    lax.fori_loop(0, n_pairs // 2, body, 0)

    @pl.when(n_pairs % 2 == 1)
    def _():
        stage(2 * n_pairs - 2, 0, False)
        stage(2 * n_pairs - 1, 1, False)
    c0 = 2 * n_pairs
    left = n_chunks - 1 - c0

    @pl.when((n_chunks > 1) & (left == 1))
    def _():
        stage(c0, 0, True)
        accumulate(c0 + 1, 1)

    @pl.when((n_chunks > 1) & (left == 2))
    def _():
        stage(c0, 0, False)
        stage(c0 + 1, 1, True)
        accumulate(c0 + 2, 0)

    for h in range(2):
        acc = acc_ref[h]
        o_ref[h * MLA_V:(h + 1) * MLA_V, :] = (acc[0:MLA_V] / acc[MLA_V:MLA_V + 1]).astype(o_ref.dtype)


def _attention(q, k, vt, batch, seq, group):
    nq = seq // TQ
    f32 = jnp.float32
    return pl.pallas_call(
        functools.partial(_attn_kernel, group),
        grid=(batch, N_PAIRS, nq),
        in_specs=[
            pl.BlockSpec((None, TQ, PAIR_W), lambda b, p, i: (group, b * nq + i, p)),
            pl.BlockSpec((None, seq, PAIR_W), lambda b, p, i: (group, b, p)),
            pl.BlockSpec((None, None, 2 * VT_ROWS, seq), lambda b, p, i: (group, b, p, 0)),
        ],
        out_specs=pl.BlockSpec((None, LANES, TQ), lambda b, p, i: (b, p, i)),
        out_shape=jax.ShapeDtypeStruct((batch, N_PAIRS * LANES, seq), jnp.bfloat16),
        scratch_shapes=[pltpu.VMEM((2 * TQ, PAIR_W), jnp.bfloat16),
                        pltpu.VMEM((TK, 2 * TQ), f32), pltpu.VMEM((TK, 2 * TQ), f32),
                        pltpu.VMEM((1, 2 * TQ), f32), pltpu.VMEM((1, 2 * TQ), f32),
                        pltpu.VMEM((1, 2 * TQ), f32), pltpu.VMEM((1, 2 * TQ), f32),
                        pltpu.VMEM((2, VT_ROWS, TQ), f32)],
        compiler_params=_cparams(("arbitrary",) * 3),
        name="attention_g%d" % group,
    )(q, k, vt)


def _router_t(lt):
    row = lax.broadcasted_iota(jnp.int32, lt.shape, 0)
    big = jnp.int32(LANES)
    ninf = -jnp.inf
    gl = jnp.where(row < N_GROUPS, lt, ninf)
    gmax = jnp.max(gl, axis=0, keepdims=True)
    gsel = jnp.min(jnp.where(gl == gmax, row, big), axis=0, keepdims=True)
    gsum = jnp.sum(jnp.where(row < N_GROUPS, jnp.exp(lt - gmax), 0.0), axis=0, keepdims=True)
    p_g = 1.0 / gsum
    lo = N_GROUPS + EXPERTS_PER_GROUP * gsel
    e = jnp.where((row >= lo) & (row < lo + EXPERTS_PER_GROUP), lt, ninf)
    m1 = jnp.max(e, axis=0, keepdims=True)
    i1 = jnp.min(jnp.where(e == m1, row, big), axis=0, keepdims=True)
    e2 = jnp.where(row == i1, ninf, e)
    m2 = jnp.max(e2, axis=0, keepdims=True)
    i2 = jnp.min(jnp.where(e2 == m2, row, big), axis=0, keepdims=True)
    a2 = jnp.exp(m2 - m1)
    den = 1.0 + a2
    g1 = p_g / den
    g2 = p_g * a2 / den
    f = lambda v: v.astype(jnp.float32)
    return jnp.where(row == 0, f(i1 - N_GROUPS),
                     jnp.where(row == 1, f(i2 - N_GROUPS),
                               jnp.where(row == 2, g1, jnp.where(row == 3, g2, 0.0))))


def _out_proj_kernel(n_o, transposed, x_ref, *refs):
    o_refs = refs[:n_o]
    w_ref, ln_ref, wr_ref, br_ref, xo_ref, xg_ref, route_ref = refs[n_o:]
    tm = x_ref.shape[0]
    hm = tm // OUT_SPLIT
    for part in range(OUT_SPLIT):
        rows = slice(part * hm, (part + 1) * hm)
        if transposed:
            y = x_ref[rows, :]
            nf = o_refs[0].shape[0]
            for g, o_ref in enumerate(o_refs):
                y = y + lax.dot_general(o_ref[:, rows], w_ref[g * nf:(g + 1) * nf, :],
                                        (((0,), (0,)), ((), ())), preferred_element_type=jnp.float32)
        else:
            y = x_ref[rows, :] + jnp.dot(o_refs[0][rows, :], w_ref[...], preferred_element_type=jnp.float32)
        xo_ref[rows, :] = y
        xn = y * _rms_scale(y) * ln_ref[...]
        hi = xn.astype(jnp.bfloat16)
        lo = (xn - hi.astype(jnp.float32)).astype(jnp.bfloat16)
        l2 = jnp.dot(jnp.concatenate([hi, lo], axis=1), wr_ref[...], preferred_element_type=jnp.float32)
        lt = (l2[:, :LANES] + l2[:, LANES:] + br_ref[...]).T
        route_ref[rows, :] = _router_t(lt).T
        for c in range(ROWS_PER_TOK):
            xg_ref[pl.ds(part * hm * ROWS_PER_TOK + c, hm, stride=ROWS_PER_TOK), :] = (
                xn[:, c * LANES:(c + 1) * LANES])


def _out_proj(x2d, o, w_out, ln, w_r, b_r, seq):
    t = x2d.shape[0]
    kdim = w_out.shape[0]
    tm = TM_OUT
    transposed = isinstance(o, tuple)
    tiles_per_seq = seq // tm
    if transposed:
        o_specs = [pl.BlockSpec((None, a.shape[1], tm), lambda i: (i // tiles_per_seq, 0, i % tiles_per_seq))
                   for a in o]
    else:
        o, o_specs = (o,), [pl.BlockSpec((tm, kdim), lambda i: (i, 0))]
    const = lambda shape: pl.BlockSpec(shape, lambda i: (0,) * len(shape))
    return pl.pallas_call(
        functools.partial(_out_proj_kernel, len(o), transposed),
        grid=(t // tm,),
        in_specs=[
            pl.BlockSpec((tm, D_MODEL), lambda i: (i, 0)),
            *o_specs,
            const((kdim, D_MODEL)), const((1, D_MODEL)), const((2 * D_MODEL, 2 * LANES)), const((1, LANES)),
        ],
        out_specs=[
            pl.BlockSpec((tm, D_MODEL), lambda i: (i, 0)),
            pl.BlockSpec((tm * ROWS_PER_TOK, LANES), lambda i: (i, 0)),
            pl.BlockSpec((tm, LANES), lambda i: (i, 0)),
        ],
        out_shape=[
            jax.ShapeDtypeStruct((t, D_MODEL), jnp.float32),
            jax.ShapeDtypeStruct((t * ROWS_PER_TOK, LANES), jnp.float32),
            jax.ShapeDtypeStruct((t, LANES), jnp.float32),
        ],
        compiler_params=_cparams(("arbitrary",)),
        name="out_proj_router",
    )(x2d, *o, w_out, ln, w_r, b_r)


def _router_weights(w_rg, b_rg, w_re, b_re):
    d = w_rg.shape[0]
    pad = LANES - N_GROUPS - N_EXPERTS
    w_r = jnp.concatenate([w_rg, w_re, jnp.zeros((d, pad), w_rg.dtype)], axis=1)
    b_r = jnp.concatenate([b_rg, b_re, jnp.zeros((pad,), b_rg.dtype)])[None, :]
    w_hi = w_r.astype(jnp.bfloat16)
    w_lo = (w_r - w_hi.astype(jnp.float32)).astype(jnp.bfloat16)
    half = jnp.concatenate([w_hi, w_lo], axis=1)
    return jnp.concatenate([half, half], axis=0), b_r


def _dispatch_plan(route, t):
    nb = t // TE + N_PAIR_CLASSES
    epg = EXPERTS_PER_GROUP
    e0, e1 = route[:, 0].astype(jnp.int32), route[:, 1].astype(jnp.int32)
    swap = e1 < e0
    ea, eb = jnp.where(swap, e1, e0), jnp.where(swap, e0, e1)
    gates = jnp.stack([jnp.where(swap, route[:, 3], route[:, 2]),
                       jnp.where(swap, route[:, 2], route[:, 3])], axis=1)
    cls = (ea // epg) * (epg * epg) + (ea % epg) * epg + (eb % epg)
    ids = jnp.arange(N_CLASS_IDS, dtype=jnp.int32)
    _, order = lax.sort((cls, jnp.arange(t, dtype=jnp.int32)), num_keys=1, is_stable=True)
    counts = jnp.sum((cls[:, None] == ids[None, :]).astype(jnp.int32), axis=0)
    start = jnp.cumsum(counts) - counts
    padded = ((counts + TE - 1) // TE) * TE
    pend = jnp.cumsum(padded)
    pstart = pend - padded
    blk0 = jnp.arange(nb, dtype=jnp.int32) * TE
    blk_cls = jnp.minimum(jnp.sum((blk0[:, None] >= pend[None, :]).astype(jnp.int32), axis=1), N_CLASS_IDS - 1)
    blk_used = (blk0 < pend[-1]).astype(jnp.int32)
    onehot = (blk_cls[:, None] == ids[None, :]).astype(jnp.int32)
    per_blk = lambda v: jnp.sum(onehot * v[None, :], axis=1)[:, None]
    r = jnp.arange(TE, dtype=jnp.int32)[None, :]
    off = blk0[:, None] + r - per_blk(pstart)
    valid = (off < per_blk(counts)) & (blk_used[:, None] > 0)
    tok = order[jnp.where(valid, per_blk(start) + off, 0)]
    src_tok = jnp.where(valid, tok, 0).astype(jnp.int32)
    dst_row = jnp.where(valid, tok, t + r).astype(jnp.int32)
    slot_gates = jnp.where(valid[..., None], gates[src_tok], 0.0).reshape(nb * TE, 2)
    slab = lambda g: jnp.broadcast_to(g[:, None], (nb * TE, LANES))
    blk_ea = (blk_cls // (epg * epg)) * epg + (blk_cls % (epg * epg)) // epg
    blk_eb = (blk_cls // (epg * epg)) * epg + blk_cls % epg
    return (blk_ea, blk_eb, blk_used, src_tok[:, None, :], dst_row[:, None, :],
            slab(slot_gates[:, 0]), slab(slot_gates[:, 1]))


def _moe_kernel(ea_ref, eb_ref, blk_used_ref, src_ref, src_next_ref, dst_ref, xg_ref, ga_ref, gb_ref,
                wga_ref, wua_ref, wda_ref, wgb_ref, wub_ref, wdb_ref, ys_ref,
                xbuf, ybuf, wga_bf, wua_bf, wda_bf, wgb_bf, wub_bf, wdb_bf, gsem, ssem):
    i = pl.program_id(0)
    nb = pl.num_programs(0)
    used = blk_used_ref[i] > 0
    slot = i % 2
    block_rows = TE * ROWS_PER_TOK

    def start_gather(idx_ref, s):
        for r in range(TE):
            tok = idx_ref[0, 0, r]
            pltpu.make_async_copy(
                xg_ref.at[pl.ds(pl.multiple_of(tok * ROWS_PER_TOK, ROWS_PER_TOK), ROWS_PER_TOK), :],
                xbuf.at[s, pl.ds(r * ROWS_PER_TOK, ROWS_PER_TOK), :],
                gsem.at[s]).start(priority=r % 2)

    def wait_gather(s):
        pltpu.make_async_copy(xg_ref.at[pl.ds(0, block_rows), :], xbuf.at[s], gsem.at[s]).wait()

    def start_scatter():
        for r in range(TE):
            row = dst_ref[0, 0, r]
            pltpu.make_async_copy(
                ybuf.at[pl.ds(r * ROWS_PER_TOK, ROWS_PER_TOK), :],
                ys_ref.at[pl.ds(pl.multiple_of(row * ROWS_PER_TOK, ROWS_PER_TOK), ROWS_PER_TOK), :],
                ssem.at[0]).start(priority=r % 2)

    def wait_scatter():
        pltpu.make_async_copy(ybuf, ys_ref.at[pl.ds(0, block_rows), :], ssem.at[0]).wait()

    @pl.when(i == 0)
    def _():
        ybuf[...] = jnp.zeros_like(ybuf)
        pltpu.make_async_copy(
            ybuf, ys_ref.at[pl.ds(ys_ref.shape[0] - block_rows, block_rows), :], ssem.at[0]).start()
        start_gather(src_ref, 0)

    @pl.when(used)
    def _():
        start_gather(src_next_ref, 1 - slot)
        prev = jnp.maximum(i - 1, 0)

        @pl.when((i == 0) | (ea_ref[i] != ea_ref[prev]))
        def _():
            wga_bf[...] = wga_ref[...].astype(jnp.bfloat16)
            wua_bf[...] = wua_ref[...].astype(jnp.bfloat16)
            wda_bf[...] = wda_ref[...].astype(jnp.bfloat16)

        @pl.when((i == 0) | (eb_ref[i] != eb_ref[prev]))
        def _():
            wgb_bf[...] = wgb_ref[...].astype(jnp.bfloat16)
            wub_bf[...] = wub_ref[...].astype(jnp.bfloat16)
            wdb_bf[...] = wdb_ref[...].astype(jnp.bfloat16)

        wait_gather(slot)
        xb = xbuf.at[slot]
        x = jnp.concatenate([xb[pl.ds(c, TE, stride=ROWS_PER_TOK), :] for c in range(ROWS_PER_TOK)],
                            axis=1).astype(jnp.bfloat16)

        def ffn(wg, wu, wd):
            hg = jnp.dot(x, wg[...], preferred_element_type=jnp.float32)
            hu = jnp.dot(x, wu[...], preferred_element_type=jnp.float32)
            hid = (hg * jax.nn.sigmoid(hg) * hu).astype(jnp.bfloat16)
            return jnp.dot(hid, wd[...], preferred_element_type=jnp.float32)

        ya = ffn(wga_bf, wua_bf, wda_bf)
        yb = ffn(wgb_bf, wub_bf, wdb_bf)
        ga = ga_ref[...]
        gb = gb_ref[...]
        wait_scatter()
        for c in range(ROWS_PER_TOK):
            cs = slice(c * LANES, (c + 1) * LANES)
            ybuf[pl.ds(c, TE, stride=ROWS_PER_TOK), :] = ga * ya[:, cs] + gb * yb[:, cs]
        start_scatter()

        next_used = (i + 1 < nb) & (blk_used_ref[jnp.minimum(i + 1, nb - 1)] > 0)

        @pl.when(jnp.logical_not(next_used))
        def _():
            wait_gather(1 - slot)
            wait_scatter()


def _moe(xg, route, w_gate, w_up, w_down, layer, t):
    nb = t // TE + N_PAIR_CLASSES
    blk_ea, blk_eb, blk_used, src_tok, dst_row, gate_a, gate_b = _dispatch_plan(route, t)
    idx_spec = lambda f: pl.BlockSpec((1, 1, TE), f, memory_space=pltpu.SMEM)
    up = lambda which: pl.BlockSpec((None, None, D_MODEL, D_EXPERT),
                                    lambda i, ea, eb, bu: (layer, (ea, eb)[which][i], 0, 0))
    down = lambda which: pl.BlockSpec((None, None, D_EXPERT, D_MODEL),
                                      lambda i, ea, eb, bu: (layer, (ea, eb)[which][i], 0, 0))
    gate_spec = pl.BlockSpec((TE, LANES), lambda i, ea, eb, bu: (i, 0))
    bf16 = jnp.bfloat16
    grid_spec = pltpu.PrefetchScalarGridSpec(
        num_scalar_prefetch=3,
        grid=(nb,),
        in_specs=[
            idx_spec(lambda i, ea, eb, bu: (i, 0, 0)),
            idx_spec(lambda i, ea, eb, bu: (jnp.minimum(i + 1, nb - 1), 0, 0)),
            idx_spec(lambda i, ea, eb, bu: (i, 0, 0)),
            pl.BlockSpec(memory_space=pl.ANY),
            gate_spec, gate_spec,
            up(0), up(0), down(0), up(1), up(1), down(1),
        ],
        out_specs=pl.BlockSpec(memory_space=pl.ANY),
        scratch_shapes=[
            pltpu.VMEM((2, TE * ROWS_PER_TOK, LANES), jnp.float32),
            pltpu.VMEM((TE * ROWS_PER_TOK, LANES), jnp.float32),
            pltpu.VMEM((D_MODEL, D_EXPERT), bf16), pltpu.VMEM((D_MODEL, D_EXPERT), bf16),
            pltpu.VMEM((D_EXPERT, D_MODEL), bf16),
            pltpu.VMEM((D_MODEL, D_EXPERT), bf16), pltpu.VMEM((D_MODEL, D_EXPERT), bf16),
            pltpu.VMEM((D_EXPERT, D_MODEL), bf16),
            pltpu.SemaphoreType.DMA((2,)),
            pltpu.SemaphoreType.DMA((1,)),
        ],
    )
    return pl.pallas_call(
        _moe_kernel,
        grid_spec=grid_spec,
        out_shape=jax.ShapeDtypeStruct(((t + TE) * ROWS_PER_TOK, LANES), jnp.float32),
        compiler_params=_cparams(("arbitrary",)),
        name="moe_experts",
    )(blk_ea, blk_eb, blk_used, src_tok, src_tok, dst_row, xg, gate_a, gate_b,
      w_gate, w_up, w_down, w_gate, w_up, w_down)


def _combine(x, ys_ref, tm):
    return x + jnp.concatenate([ys_ref[pl.ds(c, tm, stride=ROWS_PER_TOK), :] for c in range(ROWS_PER_TOK)],
                               axis=1)


def _odd_proj_kernel(x_ref, ys_ref, g_ref, w_ref, cos_ref, sin_ref,
                     xo_ref, q_ref, k_ref, v_ref, gate_ref):
    tm = x_ref.shape[0]
    x = _combine(x_ref[...], ys_ref, tm)
    xo_ref[...] = x
    xn = (x * _rms_scale(x) * g_ref[...]).astype(jnp.bfloat16)
    cos = cos_ref[...]
    sin = sin_ref[...]
    nqk = RET_HEADS * RET_QK
    half = RET_QK // 2

    def rope_store(z, dst, scale):
        for h in range(RET_HEADS):
            x1 = z[:, h * RET_QK:h * RET_QK + half]
            x2 = z[:, h * RET_QK + half:(h + 1) * RET_QK]
            dst[:, h * RET_QK:h * RET_QK + half] = ((x1 * cos - x2 * sin) * scale).astype(dst.dtype)
            dst[:, h * RET_QK + half:(h + 1) * RET_QK] = ((x1 * sin + x2 * cos) * scale).astype(dst.dtype)

    zq = jnp.dot(xn, w_ref[:, 0:nqk], preferred_element_type=jnp.float32)
    rope_store(zq, q_ref, 1.0)
    zk = jnp.dot(xn, w_ref[:, nqk:2 * nqk], preferred_element_type=jnp.float32)
    rope_store(zk, k_ref, RET_QK ** -0.5)
    nv = RET_HEADS * RET_V
    zv = jnp.dot(xn, w_ref[:, 2 * nqk:2 * nqk + nv], preferred_element_type=jnp.float32)
    v_ref[...] = zv.astype(v_ref.dtype)
    zg = jnp.dot(xn, w_ref[:, 2 * nqk + nv:2 * nqk + 2 * nv], preferred_element_type=jnp.float32)
    gate_ref[...] = (zg * jax.nn.sigmoid(zg)).astype(gate_ref.dtype)


def _ret_rope_tables(s):
    half = RET_QK // 2
    inv = ROPE_BASE ** (-jnp.arange(half, dtype=jnp.float32) / half)
    ang = jnp.arange(s, dtype=jnp.float32)[:, None] * inv[None, :]
    return jnp.cos(ang), jnp.sin(ang)


def _odd_proj(x2d, ys, seq, g, w_in):
    t = x2d.shape[0]
    tm = TM_PROJ
    tiles_per_seq = seq // tm
    cos, sin = _ret_rope_tables(seq)
    nqk = RET_HEADS * RET_QK
    nv = RET_HEADS * RET_V
    n_in = w_in.shape[1]
    const = lambda shape: pl.BlockSpec(shape, lambda i: (0,) * len(shape))
    tok = lambda n: pl.BlockSpec((tm, n), lambda i: (i, 0))
    return pl.pallas_call(
        _odd_proj_kernel,
        grid=(t // tm,),
        in_specs=[
            tok(D_MODEL),
            pl.BlockSpec((tm * ROWS_PER_TOK, LANES), lambda i: (i, 0)),
            const((1, D_MODEL)), const((D_MODEL, n_in)),
            pl.BlockSpec((tm, LANES), lambda i: (i % tiles_per_seq, 0)),
            pl.BlockSpec((tm, LANES), lambda i: (i % tiles_per_seq, 0)),
        ],
        out_specs=[tok(D_MODEL), tok(nqk), tok(nqk), tok(nv), tok(nv)],
        out_shape=[
            jax.ShapeDtypeStruct((t, D_MODEL), jnp.float32),
            jax.ShapeDtypeStruct((t, nqk), jnp.bfloat16),
            jax.ShapeDtypeStruct((t, nqk), jnp.bfloat16),
            jax.ShapeDtypeStruct((t, nv), jnp.bfloat16),
            jax.ShapeDtypeStruct((t, nv), jnp.bfloat16),
        ],
        compiler_params=_cparams(("arbitrary",)),
        name="odd_proj",
    )(x2d, ys, g, w_in, cos, sin)


def _retention_kernel(q_ref, k_ref, v_ref, gate_ref, dmask_ref, xi_ref, zeta_ref, gc_ref, o_ref, state_ref):
    n = pl.program_id(2)

    @pl.when(n == 0)
    def _():
        state_ref[...] = jnp.zeros_like(state_ref)

    xi = xi_ref[...][:, 0:1]
    zeta = zeta_ref[...][:, 0:1]
    g_c = gc_ref[...][:, 0:1]
    for j in range(q_ref.shape[0] // RET_C):
        rows = slice(j * RET_C, (j + 1) * RET_C)
        q = q_ref[rows, :]
        k = k_ref[rows, :]
        v = v_ref[rows, :]
        inner = lax.dot_general(q, k, (((1,), (1,)), ((), ())),
                                preferred_element_type=jnp.float32) * dmask_ref[...]
        o = jnp.dot(inner.astype(jnp.bfloat16), v, preferred_element_type=jnp.float32)
        state = state_ref[...]
        cross = jnp.dot(q, state.astype(jnp.bfloat16), preferred_element_type=jnp.float32)
        o = o + cross * xi
        kz = (k.astype(jnp.float32) * zeta).astype(jnp.bfloat16)
        upd = lax.dot_general(kz, v, (((0,), (0,)), ((), ())), preferred_element_type=jnp.float32)
        state_ref[...] = g_c * state + upd
        o = o * _rms_scale(o)
        o_ref[rows, :] = (o * gate_ref[rows, :].astype(jnp.float32)).astype(o_ref.dtype)


def _retention_tables():
    c = RET_C
    log_g = jnp.log(1.0 - jnp.exp2(-5.0 - jnp.arange(RET_HEADS, dtype=jnp.float32)))
    j = jnp.arange(c, dtype=jnp.float32)
    diff = j[:, None] - j[None, :]
    dmask = jnp.where(diff >= 0, jnp.exp(jnp.maximum(diff, 0.0)[None] * log_g[:, None, None]), 0.0)
    xi = jnp.exp((j[None, :] + 1.0) * log_g[:, None])
    zeta = jnp.exp((c - 1.0 - j)[None, :] * log_g[:, None])
    g_c = jnp.exp(c * log_g)
    bc = lambda a: jnp.broadcast_to(a[..., None], a.shape + (LANES,))
    return dmask, bc(xi), bc(zeta), bc(g_c[:, None])


def _retention(q, k, v, gate, batch, seq):
    nc = seq // RET_BLK
    dmask, xi, zeta, g_c = _retention_tables()
    row = lambda n: pl.BlockSpec((RET_BLK, n), lambda b, h, c: (b * nc + c, h))
    head = lambda shape: pl.BlockSpec((None,) + shape, lambda b, h, c: (h, 0, 0))
    return pl.pallas_call(
        _retention_kernel,
        grid=(batch, RET_HEADS, nc),
        in_specs=[row(RET_QK), row(RET_QK), row(RET_V), row(RET_V),
                  head((RET_C, RET_C)), head((RET_C, LANES)), head((RET_C, LANES)), head((1, LANES))],
        out_specs=row(RET_V),
        out_shape=jax.ShapeDtypeStruct((batch * seq, RET_HEADS * RET_V), jnp.bfloat16),
        scratch_shapes=[pltpu.VMEM((RET_QK, RET_V), jnp.float32)],
        compiler_params=_cparams(("arbitrary",) * 3),
        name="retention",
    )(q, k, v, gate, dmask, xi, zeta, g_c)


def _final_kernel(x_ref, ys_ref, g_ref, o_ref):
    x = _combine(x_ref[...], ys_ref, x_ref.shape[0])
    o_ref[...] = x * _rms_scale(x) * g_ref[...]


def _final(x2d, ys, g):
    t = x2d.shape[0]
    tm = TM_OUT
    return pl.pallas_call(
        _final_kernel,
        grid=(t // tm,),
        in_specs=[
            pl.BlockSpec((tm, D_MODEL), lambda i: (i, 0)),
            pl.BlockSpec((tm * ROWS_PER_TOK, LANES), lambda i: (i, 0)),
            pl.BlockSpec((1, D_MODEL), lambda i: (0, 0)),
        ],
        out_specs=pl.BlockSpec((tm, D_MODEL), lambda i: (i, 0)),
        out_shape=jax.ShapeDtypeStruct((t, D_MODEL), jnp.float32),
        compiler_params=_cparams(("arbitrary",)),
        name="final_norm",
    )(x2d, ys, g)


def kernel(x, ln_mix_e, w_in_e, ln_q_e, w_uq_e, ln_kv_e, w_ukv_e, b_f_e, w_out_e, ln_mix_o, w_in_o,
           w_out_o, ln_ffn, w_rg, b_rg, w_re, b_re, w_gate, w_up, w_down, ln_f):
    batch, seq, d = x.shape
    t = batch * seq
    assert d == D_MODEL and seq % TK == 0 and seq % TQ == 0 and seq % RET_BLK == 0
    assert seq % TM_PROJ == 0 and t % TM_OUT == 0 and t % TE == 0
    bf16 = jnp.bfloat16
    x2d = x.reshape(t, d)

    w_in_p, w_uq_p, w_ukv_p, b_f_p = _even_weights(w_in_e[0], w_uq_e[0], w_ukv_e[0], b_f_e[0])
    q, k, vt = _even_proj(x2d, seq, ln_mix_e[0][None, :], w_in_p, ln_q_e[0][None, :], w_uq_p,
                         ln_kv_e[0][None, :], w_ukv_p, b_f_p)
    o_t = (_attention(q, k, vt, batch, seq, 0), _attention(q, k, vt, batch, seq, 1))
    w_r, b_r = _router_weights(w_rg[0], b_rg[0], w_re[0], b_re[0])
    x2d, xg, route = _out_proj(x2d, o_t, w_out_e[0].astype(bf16), ln_ffn[0][None, :], w_r, b_r, seq)
    ys = _moe(xg, route, w_gate, w_up, w_down, 0, t)

    x2d, rq, rk, rv, rg = _odd_proj(x2d, ys, seq, ln_mix_o[0][None, :], w_in_o[0].astype(bf16))
    og = _retention(rq, rk, rv, rg, batch, seq)
    w_r, b_r = _router_weights(w_rg[1], b_rg[1], w_re[1], b_re[1])
    x2d, xg, route = _out_proj(x2d, og, w_out_o[0].astype(bf16), ln_ffn[1][None, :], w_r, b_r, seq)
    ys = _moe(xg, route, w_gate, w_up, w_down, 1, t)

    out = _final(x2d, ys, ln_f[None, :])
    return out.reshape(batch, seq, d)
```

```python
import functools
import math

import numpy as np
import jax
import jax.numpy as jnp
from jax import lax
from jax.experimental import pallas as pl
from jax.experimental.pallas import tpu as pltpu

D_MODEL = 1024
CHUNK = 64
ROPE_BASE = 10000.0
EPS = 1e-6
MLA_HEADS = 8
MLA_NOPE = 64
MLA_ROPE = 32
MLA_V = 64
MLA_Q_LORA = 256
MLA_KV_LORA = 128
FOX_HEADS = 8
FOX_DIM = 64
RET_HEADS = 4
RET_QK = 256
RET_V = 512
N_GROUPS = 4
EXPERTS_PER_GROUP = 8
N_EXPERTS = N_GROUPS * EXPERTS_PER_GROUP
TOP_K = 2
D_EXPERT = 512

LANES = 128
SUBLANES = 8
VMEM_LIMIT = 52 * 1024 * 1024

TM_PROJ = 256
TM_OUT = 512
OUT_SPLIT = 2
TQ = 512
VT_ROWS = 80
TK = 512
LOG2E = math.log2(math.e)
RET_C = 256
RET_BLK = 2048
TE = 256
ROWS_PER_TOK = D_MODEL // LANES

N_PAIRS = MLA_HEADS // 2
PAIR_W = 256
ROPE_H = MLA_ROPE // 2
NEG = -1e30

_X1_OFF = 0
_X2_OFF = 64
_AUG_W = 6

_C_Q = 0
_C_KV = _C_Q + MLA_Q_LORA
_C_KPE = _C_KV + MLA_KV_LORA
_C_QF = _C_KPE + LANES
_C_KF = _C_QF + FOX_HEADS * FOX_DIM
_C_VF = _C_KF + FOX_HEADS * FOX_DIM
_C_F = _C_VF + FOX_HEADS * FOX_DIM
EVEN_N = _C_F + LANES


def _cparams(sem):
    return pltpu.CompilerParams(dimension_semantics=sem, vmem_limit_bytes=VMEM_LIMIT)


def _rms_scale(x):
    return lax.rsqrt(jnp.mean(x * x, axis=-1, keepdims=True) + EPS)


def _split3(c):
    hi = c.astype(jnp.bfloat16)
    r1 = c - hi.astype(jnp.float32)
    mid = r1.astype(jnp.bfloat16)
    r2 = r1 - mid.astype(jnp.float32)
    lo = r2.astype(jnp.bfloat16)
    return hi, mid, lo


def _even_proj_kernel(tiles_per_seq, x_ref, g_ref, win_ref, lnq_ref, wuq_ref, lnkv_ref, wukv_ref,
                      bf_ref, cos_ref, sin_ref, tri_ref, pq_ref, pk_ref, oq_ref, ok_ref,
                      q_ref, k_ref, vt_ref, carry_ref):
    i = pl.program_id(0)
    x = x_ref[...]
    xn = (x * _rms_scale(x) * g_ref[...]).astype(jnp.bfloat16)
    z = jnp.dot(xn, win_ref[...], preferred_element_type=jnp.float32)
    cos = cos_ref[...]
    sin = sin_ref[...]

    def rope_slab(s):
        return s * cos + pltpu.roll(s, 64, axis=1) * sin

    c_q = z[:, _C_Q:_C_Q + MLA_Q_LORA]
    cqn = (c_q * _rms_scale(c_q) * lnq_ref[...]).astype(jnp.bfloat16)
    q = jnp.dot(cqn, wuq_ref[...], preferred_element_type=jnp.float32)
    scale_a = (MLA_NOPE + MLA_ROPE) ** -0.5 * LOG2E
    for p in range(N_PAIRS):
        lo = p * PAIR_W
        q_ref[0, :, lo:lo + LANES] = (q[:, lo:lo + LANES] * scale_a).astype(jnp.bfloat16)
        q_ref[0, :, lo + LANES:lo + PAIR_W] = (
            rope_slab(q[:, lo + LANES:lo + PAIR_W]) * scale_a).astype(jnp.bfloat16)

    c_kv = z[:, _C_KV:_C_KV + MLA_KV_LORA]
    ckn = (c_kv * _rms_scale(c_kv) * lnkv_ref[...]).astype(jnp.bfloat16)
    kv = jnp.dot(ckn, wukv_ref[...], preferred_element_type=jnp.float32)
    kpe = rope_slab(z[:, _C_KPE:_C_KPE + LANES]).astype(jnp.bfloat16)
    nk = MLA_HEADS * MLA_NOPE
    for p in range(N_PAIRS):
        lo = p * PAIR_W
        k_ref[0, :, lo:lo + LANES] = kv[:, p * LANES:(p + 1) * LANES].astype(jnp.bfloat16)
        k_ref[0, :, lo + LANES:lo + PAIR_W] = kpe
    _store_values_transposed(vt_ref, 0, kv[:, nk:nk + MLA_HEADS * MLA_V])

    lane = lax.broadcasted_iota(jnp.int32, (1, LANES), 1)
    fz = z[:, _C_F:_C_F + LANES] + bf_ref[...]
    log_f = -(jnp.maximum(-fz, 0.0) + jnp.log1p(jnp.exp(-jnp.abs(fz))))
    log_f = jnp.where(lane < FOX_HEADS, log_f, 0.0)

    @pl.when(i % tiles_per_seq == 0)
    def _():
        carry_ref[...] = jnp.zeros_like(carry_ref)

    hi, mid, lo3 = _split3(log_f)
    tri = tri_ref[...]
    cum = (jnp.dot(tri, hi, preferred_element_type=jnp.float32)
           + jnp.dot(tri, mid, preferred_element_type=jnp.float32)
           + jnp.dot(tri, lo3, preferred_element_type=jnp.float32)) + carry_ref[...]
    tm = cum.shape[0]
    carry_ref[...] = cum[tm - 1:tm, :]

    parts = jnp.concatenate(_split3(cum * LOG2E), axis=1)
    aug_q = jnp.dot(parts, pq_ref[...], preferred_element_type=jnp.float32) + oq_ref[...]
    aug_k = jnp.dot(parts, pk_ref[...], preferred_element_type=jnp.float32) + ok_ref[...]
    scale_b = FOX_DIM ** -0.5 * LOG2E
    for p in range(N_PAIRS):
        lo = p * PAIR_W
        q_ref[1, :, lo:lo + LANES] = (
            z[:, _C_QF + p * LANES:_C_QF + (p + 1) * LANES] * scale_b).astype(jnp.bfloat16)
        q_ref[1, :, lo + LANES:lo + PAIR_W] = aug_q[:, p * LANES:(p + 1) * LANES].astype(jnp.bfloat16)
        k_ref[1, :, lo:lo + LANES] = z[:, _C_KF + p * LANES:_C_KF + (p + 1) * LANES].astype(jnp.bfloat16)
        k_ref[1, :, lo + LANES:lo + PAIR_W] = aug_k[:, p * LANES:(p + 1) * LANES].astype(jnp.bfloat16)
    _store_values_transposed(vt_ref, 1, z[:, _C_VF:_C_VF + FOX_HEADS * FOX_DIM])


def _store_values_transposed(vt_ref, g, v):
    vt = v.T
    ones = jnp.ones((VT_ROWS - MLA_V, vt.shape[1]), vt_ref.dtype)
    for h in range(MLA_HEADS):
        vt_ref[g, h * VT_ROWS:h * VT_ROWS + MLA_V, :] = vt[h * MLA_V:(h + 1) * MLA_V].astype(vt_ref.dtype)
        vt_ref[g, h * VT_ROWS + MLA_V:(h + 1) * VT_ROWS, :] = ones


def _even_weights(w_in, w_uq, w_ukv, b_f):
    d = w_in.shape[0]
    zeros = lambda n: jnp.zeros((d, n), w_in.dtype)
    o_cq, o_ckv = 0, MLA_Q_LORA
    o_kpe = o_ckv + MLA_KV_LORA
    o_qf = o_kpe + MLA_ROPE
    o_kf = o_qf + FOX_HEADS * FOX_DIM
    o_vf = o_kf + FOX_HEADS * FOX_DIM
    o_f = o_vf + FOX_HEADS * FOX_DIM
    kpe1 = w_in[:, o_kpe:o_kpe + ROPE_H]
    kpe2 = w_in[:, o_kpe + ROPE_H:o_kpe + MLA_ROPE]
    kpe_slab = jnp.concatenate([kpe1, kpe1, zeros(32), kpe2, kpe2, zeros(32)], axis=1)
    f_slab = jnp.concatenate([w_in[:, o_f:o_f + FOX_HEADS], zeros(LANES - FOX_HEADS)], axis=1)
    w_in_p = jnp.concatenate([w_in[:, o_cq:o_kpe], kpe_slab, w_in[:, o_qf:o_f], f_slab], axis=1)

    dq = w_uq.shape[0]
    zq = lambda n: jnp.zeros((dq, n), w_uq.dtype)
    hd = MLA_NOPE + MLA_ROPE
    blocks = []
    for p in range(N_PAIRS):
        h0, h1 = 2 * p, 2 * p + 1
        nope = lambda h: w_uq[:, h * hd:h * hd + MLA_NOPE]
        r1 = lambda h: w_uq[:, h * hd + MLA_NOPE:h * hd + MLA_NOPE + ROPE_H]
        r2 = lambda h: w_uq[:, h * hd + MLA_NOPE + ROPE_H:(h + 1) * hd]
        blocks += [nope(h0), nope(h1), r1(h0), r1(h1), zq(32), r2(h0), r2(h1), zq(32)]
    w_uq_p = jnp.concatenate(blocks, axis=1)

    kvd = MLA_NOPE + MLA_V
    k_cols = [w_ukv[:, h * kvd:h * kvd + MLA_NOPE] for h in range(MLA_HEADS)]
    v_cols = [w_ukv[:, h * kvd + MLA_NOPE:(h + 1) * kvd] for h in range(MLA_HEADS)]
    w_ukv_p = jnp.concatenate(k_cols + v_cols, axis=1)
    b_f_p = jnp.concatenate([b_f, jnp.zeros((LANES - FOX_HEADS,), b_f.dtype)])[None, :]
    bf16 = jnp.bfloat16
    return w_in_p.astype(bf16), w_uq_p.astype(bf16), w_ukv_p.astype(bf16), b_f_p


def _aug_placement():
    pq = np.zeros((3 * LANES, N_PAIRS * LANES), np.float32)
    pk = np.zeros((3 * LANES, N_PAIRS * LANES), np.float32)
    oq = np.zeros((1, N_PAIRS * LANES), np.float32)
    ok = np.zeros((1, N_PAIRS * LANES), np.float32)
    for p in range(N_PAIRS):
        for j in range(2):
            h = 2 * p + j
            base = p * LANES + _X1_OFF + j * ROPE_H
            for t in range(3):
                pq[t * LANES + h, base + 3 + t] = 1.0
                pk[t * LANES + h, base + t] = -1.0
                oq[0, base + t] = 1.0
                ok[0, base + 3 + t] = 1.0
    return (jnp.asarray(pq, jnp.bfloat16), jnp.asarray(pk, jnp.bfloat16),
            jnp.asarray(oq), jnp.asarray(ok))


def _mla_rope_tables(s):
    inv = ROPE_BASE ** (-jnp.arange(ROPE_H, dtype=jnp.float32) / ROPE_H)
    ang = jnp.arange(s, dtype=jnp.float32)[:, None] * inv[None, :]
    c, sn = jnp.cos(ang), jnp.sin(ang)
    z = jnp.zeros((s, 32), jnp.float32)
    cos = jnp.concatenate([c, c, z, c, c, z], axis=1)
    sin = jnp.concatenate([-sn, -sn, z, sn, sn, z], axis=1)
    return cos, sin


def _even_proj(x2d, seq, g, w_in_p, ln_q, w_uq_p, ln_kv, w_ukv_p, b_f_p):
    t = x2d.shape[0]
    tm = TM_PROJ
    tiles_per_seq = seq // tm
    cos, sin = _mla_rope_tables(seq)
    tri = jnp.asarray(np.tril(np.ones((tm, tm), np.float32)), jnp.bfloat16)
    pq, pk, oq, ok = _aug_placement()
    const = lambda shape: pl.BlockSpec(shape, lambda i: (0,) * len(shape))
    return pl.pallas_call(
        functools.partial(_even_proj_kernel, tiles_per_seq),
        grid=(t // tm,),
        in_specs=[
            pl.BlockSpec((tm, D_MODEL), lambda i: (i, 0)),
            const((1, D_MODEL)), const((D_MODEL, EVEN_N)),
            const((1, MLA_Q_LORA)), const((MLA_Q_LORA, N_PAIRS * PAIR_W)),
            const((1, MLA_KV_LORA)), const((MLA_KV_LORA, 2 * MLA_HEADS * MLA_NOPE)),
            const((1, LANES)),
            pl.BlockSpec((tm, LANES), lambda i: (i % tiles_per_seq, 0)),
            pl.BlockSpec((tm, LANES), lambda i: (i % tiles_per_seq, 0)),
            const((tm, tm)), const(pq.shape), const(pk.shape), const(oq.shape), const(ok.shape),
        ],
        out_specs=[
            pl.BlockSpec((2, tm, N_PAIRS * PAIR_W), lambda i: (0, i, 0)),
            pl.BlockSpec((2, tm, N_PAIRS * PAIR_W), lambda i: (0, i, 0)),
            pl.BlockSpec((2, None, MLA_HEADS * VT_ROWS, tm),
                         lambda i: (0, i // tiles_per_seq, 0, i % tiles_per_seq)),
        ],
        out_shape=[
            jax.ShapeDtypeStruct((2, t, N_PAIRS * PAIR_W), jnp.bfloat16),
            jax.ShapeDtypeStruct((2, t, N_PAIRS * PAIR_W), jnp.bfloat16),
            jax.ShapeDtypeStruct((2, t // seq, MLA_HEADS * VT_ROWS, seq), jnp.bfloat16),
        ],
        scratch_shapes=[pltpu.VMEM((1, LANES), jnp.float32)],
        compiler_params=_cparams(("arbitrary",)),
        name="even_proj",
    )(x2d, g, w_in_p, ln_q, w_uq_p, ln_kv, w_ukv_p, b_f_p, cos, sin, tri, pq, pk, oq, ok)


def _attn_kernel(group, tk, q_ref, k_ref, vt_ref, o_ref, qcat_ref, s0_ref, s1_ref, m0_ref, m1_ref,
                 a0_ref, a1_ref, acc_ref):
    i = pl.program_id(2)
    tq = q_ref.shape[0]
    s_bufs, m_bufs, a_bufs = (s0_ref, s1_ref), (m0_ref, m1_ref), (a0_ref, a1_ref)

    lane = lax.broadcasted_iota(jnp.int32, (1, PAIR_W), 1)

    def head_mask(j):
        a = (lane >= j * MLA_NOPE) & (lane < (j + 1) * MLA_NOPE)
        b = (lane >= LANES + _X1_OFF + j * ROPE_H) & (lane < LANES + _X1_OFF + (j + 1) * ROPE_H)
        c = (lane >= LANES + _X2_OFF + j * ROPE_H) & (lane < LANES + _X2_OFF + (j + 1) * ROPE_H)
        return a | b | c

    q = q_ref[...]
    zero = jnp.zeros_like(q)
    qcat_ref[0:tq, :] = jnp.where(head_mask(0), q, zero)
    qcat_ref[tq:2 * tq, :] = jnp.where(head_mask(1), q, zero)
    acc_ref[...] = jnp.zeros_like(acc_ref)

    q0 = i * tq
    n_chunks = q0 // tk + 1

    def scores(c, m_run, par, masked):
        kstart = pl.multiple_of(c * tk, tk)
        s = lax.dot_general(k_ref[pl.ds(kstart, tk), :], qcat_ref[...], (((1,), (1,)), ((), ())),
                            preferred_element_type=jnp.float32)
        if masked:
            col = lax.broadcasted_iota(jnp.int32, (1, 2 * tq), 1)
            qpos = q0 + jnp.where(col >= tq, col - tq, col)
            qlim = (qpos | (CHUNK - 1)) if group == 0 else qpos
            kpos = kstart + lax.broadcasted_iota(jnp.int32, (tk, 1), 0)
            s = jnp.where(kpos <= qlim, s, NEG)
        s_bufs[par][...] = s
        m_new = jnp.maximum(m_run, jnp.max(s, axis=0, keepdims=True))
        m_bufs[par][...] = m_new
        a_bufs[par][...] = jnp.exp2(m_run - m_new)

    def accumulate(c, par):
        kstart = pl.multiple_of(c * tk, tk)
        alpha = a_bufs[par][...]
        p = jnp.exp2(s_bufs[par][...] - m_bufs[par][...]).astype(jnp.bfloat16)
        for h in range(2):
            cols = slice(h * tq, (h + 1) * tq)
            acc_ref[h] = alpha[:, cols] * acc_ref[h] + jnp.dot(
                vt_ref[h * VT_ROWS:(h + 1) * VT_ROWS, pl.ds(kstart, tk)], p[:, cols],
                preferred_element_type=jnp.float32)

    def stage(c, par, masked):
        scores(c + 1, m_bufs[par][...], 1 - par, masked)
        accumulate(c, par)

    m_init = jnp.full((1, 2 * tq), NEG, jnp.float32)

    @pl.when(n_chunks == 1)
    def _():
        scores(0, m_init, 0, True)
        accumulate(0, 0)

    @pl.when(n_chunks > 1)
    def _():
        scores(0, m_init, 0, False)

    n_pairs = jnp.maximum(n_chunks - 2, 0) // 2

    def body(j, carry):
        for u in range(4):
            stage(4 * j + u, u % 2, False)
        return carry

    lax.fori_loop(0, n_pairs // 2, body, 0)

    @pl.when(n_pairs % 2 == 1)
    def _():
        stage(2 * n_pairs - 2, 0, False)
        stage(2 * n_pairs - 1, 1, False)
    c0 = 2 * n_pairs
    left = n_chunks - 1 - c0

    @pl.when((n_chunks > 1) & (left == 1))
    def _():
        stage(c0, 0, True)
        accumulate(c0 + 1, 1)

    @pl.when((n_chunks > 1) & (left == 2))
    def _():
        stage(c0, 0, False)
        stage(c0 + 1, 1, True)
        accumulate(c0 + 2, 0)

    for h in range(2):
        acc = acc_ref[h]
        o_ref[h * MLA_V:(h + 1) * MLA_V, :] = (acc[0:MLA_V] / acc[MLA_V:MLA_V + 1]).astype(o_ref.dtype)


def _attention(q, k, vt, batch, seq, group, tk):
    nq = seq // TQ
    f32 = jnp.float32
    return pl.pallas_call(
        functools.partial(_attn_kernel, group, tk),
        grid=(batch, N_PAIRS, nq),
        in_specs=[
            pl.BlockSpec((None, TQ, PAIR_W), lambda b, p, i: (group, b * nq + i, p)),
            pl.BlockSpec((None, seq, PAIR_W), lambda b, p, i: (group, b, p)),
            pl.BlockSpec((None, None, 2 * VT_ROWS, seq), lambda b, p, i: (group, b, p, 0)),
        ],
        out_specs=pl.BlockSpec((None, LANES, TQ), lambda b, p, i: (b, p, i)),
        out_shape=jax.ShapeDtypeStruct((batch, N_PAIRS * LANES, seq), jnp.bfloat16),
        scratch_shapes=[pltpu.VMEM((2 * TQ, PAIR_W), jnp.bfloat16),
                        pltpu.VMEM((tk, 2 * TQ), f32), pltpu.VMEM((tk, 2 * TQ), f32),
                        pltpu.VMEM((1, 2 * TQ), f32), pltpu.VMEM((1, 2 * TQ), f32),
                        pltpu.VMEM((1, 2 * TQ), f32), pltpu.VMEM((1, 2 * TQ), f32),
                        pltpu.VMEM((2, VT_ROWS, TQ), f32)],
        compiler_params=_cparams(("arbitrary",) * 3),
        name="attention_g%d" % group,
    )(q, k, vt)


def _router_t(lt):
    row = lax.broadcasted_iota(jnp.int32, lt.shape, 0)
    big = jnp.int32(LANES)
    ninf = -jnp.inf
    gl = jnp.where(row < N_GROUPS, lt, ninf)
    gmax = jnp.max(gl, axis=0, keepdims=True)
    gsel = jnp.min(jnp.where(gl == gmax, row, big), axis=0, keepdims=True)
    gsum = jnp.sum(jnp.where(row < N_GROUPS, jnp.exp(lt - gmax), 0.0), axis=0, keepdims=True)
    p_g = 1.0 / gsum
    lo = N_GROUPS + EXPERTS_PER_GROUP * gsel
    e = jnp.where((row >= lo) & (row < lo + EXPERTS_PER_GROUP), lt, ninf)
    m1 = jnp.max(e, axis=0, keepdims=True)
    i1 = jnp.min(jnp.where(e == m1, row, big), axis=0, keepdims=True)
    e2 = jnp.where(row == i1, ninf, e)
    m2 = jnp.max(e2, axis=0, keepdims=True)
    i2 = jnp.min(jnp.where(e2 == m2, row, big), axis=0, keepdims=True)
    a2 = jnp.exp(m2 - m1)
    den = 1.0 + a2
    g1 = p_g / den
    g2 = p_g * a2 / den
    f = lambda v: v.astype(jnp.float32)
    return jnp.where(row == 0, f(i1 - N_GROUPS),
                     jnp.where(row == 1, f(i2 - N_GROUPS),
                               jnp.where(row == 2, g1, jnp.where(row == 3, g2, 0.0))))


def _out_proj_kernel(n_o, transposed, x_ref, *refs):
    o_refs = refs[:n_o]
    w_ref, ln_ref, wr_ref, br_ref, xo_ref, xg_ref, route_ref = refs[n_o:]
    tm = x_ref.shape[0]
    hm = tm // OUT_SPLIT
    for part in range(OUT_SPLIT):
        rows = slice(part * hm, (part + 1) * hm)
        if transposed:
            y = x_ref[rows, :]
            nf = o_refs[0].shape[0]
            for g, o_ref in enumerate(o_refs):
                y = y + lax.dot_general(o_ref[:, rows], w_ref[g * nf:(g + 1) * nf, :],
                                        (((0,), (0,)), ((), ())), preferred_element_type=jnp.float32)
        else:
            y = x_ref[rows, :] + jnp.dot(o_refs[0][rows, :], w_ref[...], preferred_element_type=jnp.float32)
        xo_ref[rows, :] = y
        xn = y * _rms_scale(y) * ln_ref[...]
        hi = xn.astype(jnp.bfloat16)
        lo = (xn - hi.astype(jnp.float32)).astype(jnp.bfloat16)
        l2 = jnp.dot(jnp.concatenate([hi, lo], axis=1), wr_ref[...], preferred_element_type=jnp.float32)
        lt = (l2[:, :LANES] + l2[:, LANES:] + br_ref[...]).T
        route_ref[rows, :] = _router_t(lt).T
        for c in range(ROWS_PER_TOK):
            xg_ref[pl.ds(part * hm * ROWS_PER_TOK + c, hm, stride=ROWS_PER_TOK), :] = (
                xn[:, c * LANES:(c + 1) * LANES])


def _out_proj(x2d, o, w_out, ln, w_r, b_r, seq):
    t = x2d.shape[0]
    kdim = w_out.shape[0]
    tm = TM_OUT
    transposed = isinstance(o, tuple)
    tiles_per_seq = seq // tm
    if transposed:
        o_specs = [pl.BlockSpec((None, a.shape[1], tm), lambda i: (i // tiles_per_seq, 0, i % tiles_per_seq))
                   for a in o]
    else:
        o, o_specs = (o,), [pl.BlockSpec((tm, kdim), lambda i: (i, 0))]
    const = lambda shape: pl.BlockSpec(shape, lambda i: (0,) * len(shape))
    return pl.pallas_call(
        functools.partial(_out_proj_kernel, len(o), transposed),
        grid=(t // tm,),
        in_specs=[
            pl.BlockSpec((tm, D_MODEL), lambda i: (i, 0)),
            *o_specs,
            const((kdim, D_MODEL)), const((1, D_MODEL)), const((2 * D_MODEL, 2 * LANES)), const((1, LANES)),
        ],
        out_specs=[
            pl.BlockSpec((tm, D_MODEL), lambda i: (i, 0)),
            pl.BlockSpec((tm * ROWS_PER_TOK, LANES), lambda i: (i, 0)),
            pl.BlockSpec((tm, LANES), lambda i: (i, 0)),
        ],
        out_shape=[
            jax.ShapeDtypeStruct((t, D_MODEL), jnp.float32),
            jax.ShapeDtypeStruct((t * ROWS_PER_TOK, LANES), jnp.float32),
            jax.ShapeDtypeStruct((t, LANES), jnp.float32),
        ],
        compiler_params=_cparams(("arbitrary",)),
        name="out_proj_router",
    )(x2d, *o, w_out, ln, w_r, b_r)


def _router_weights(w_rg, b_rg, w_re, b_re):
    d = w_rg.shape[0]
    pad = LANES - N_GROUPS - N_EXPERTS
    w_r = jnp.concatenate([w_rg, w_re, jnp.zeros((d, pad), w_rg.dtype)], axis=1)
    b_r = jnp.concatenate([b_rg, b_re, jnp.zeros((pad,), b_rg.dtype)])[None, :]
    w_hi = w_r.astype(jnp.bfloat16)
    w_lo = (w_r - w_hi.astype(jnp.float32)).astype(jnp.bfloat16)
    half = jnp.concatenate([w_hi, w_lo], axis=1)
    return jnp.concatenate([half, half], axis=0), b_r


def _dispatch_plan(eid, t):
    a = t * TOP_K
    nb = a // TE + N_EXPERTS
    e_flat = eid.reshape(a)
    experts = jnp.arange(N_EXPERTS, dtype=jnp.int32)
    _, order = lax.sort((e_flat, jnp.arange(a, dtype=jnp.int32)), num_keys=1, is_stable=True)
    counts = jnp.sum((e_flat[:, None] == experts[None, :]).astype(jnp.int32), axis=0)
    start = jnp.cumsum(counts) - counts
    padded = ((counts + TE - 1) // TE) * TE
    pend = jnp.cumsum(padded)
    pstart = pend - padded
    blk0 = jnp.arange(nb, dtype=jnp.int32) * TE
    blk_e = jnp.minimum(jnp.sum((blk0[:, None] >= pend[None, :]).astype(jnp.int32), axis=1), N_EXPERTS - 1)
    blk_used = (blk0 < pend[-1]).astype(jnp.int32)
    onehot = (blk_e[:, None] == experts[None, :]).astype(jnp.int32)
    per_blk = lambda v: jnp.sum(onehot * v[None, :], axis=1)[:, None]
    r = jnp.arange(TE, dtype=jnp.int32)[None, :]
    off = blk0[:, None] + r - per_blk(pstart)
    valid = (off < per_blk(counts)) & (blk_used[:, None] > 0)
    src = jnp.where(valid, per_blk(start) + off, 0)
    assign = order[src]
    src_tok = jnp.where(valid, assign // TOP_K, 0).astype(jnp.int32)
    dst_row = jnp.where(valid, assign, a + r).astype(jnp.int32)
    nxt = jnp.minimum(jnp.arange(nb) + 1, nb - 1)
    return blk_e, blk_used, jnp.concatenate([src_tok, src_tok[nxt], dst_row], axis=1)[:, None, :]


def _moe_kernel(blk_e_ref, blk_used_ref, idx_ref, xg_ref, wg_ref, wu_ref, wd_ref,
                ys_ref, xbuf, ybuf, wg_bf, wu_bf, wd_bf, gsem, ssem):
    i = pl.program_id(0)
    nb = pl.num_programs(0)
    used = blk_used_ref[i] > 0
    slot = i % 2
    block_rows = TE * ROWS_PER_TOK

    def start_gather(first, s):
        for r in range(TE):
            tok = idx_ref[0, 0, first + r]
            pltpu.make_async_copy(
                xg_ref.at[pl.ds(pl.multiple_of(tok * ROWS_PER_TOK, ROWS_PER_TOK), ROWS_PER_TOK), :],
                xbuf.at[s, pl.ds(r * ROWS_PER_TOK, ROWS_PER_TOK), :],
                gsem.at[s]).start(priority=r % 2)

    def wait_gather(s):
        pltpu.make_async_copy(xg_ref.at[pl.ds(0, block_rows), :], xbuf.at[s], gsem.at[s]).wait()

    def start_scatter():
        for r in range(TE):
            row = idx_ref[0, 0, 2 * TE + r]
            pltpu.make_async_copy(
                ybuf.at[pl.ds(r * ROWS_PER_TOK, ROWS_PER_TOK), :],
                ys_ref.at[pl.ds(pl.multiple_of(row * ROWS_PER_TOK, ROWS_PER_TOK), ROWS_PER_TOK), :],
                ssem.at[0]).start(priority=r % 2)

    def wait_scatter():
        pltpu.make_async_copy(ybuf, ys_ref.at[pl.ds(0, block_rows), :], ssem.at[0]).wait()

    @pl.when(i == 0)
    def _():
        ybuf[...] = jnp.zeros_like(ybuf)
        pltpu.make_async_copy(
            ybuf, ys_ref.at[pl.ds(ys_ref.shape[0] - block_rows, block_rows), :], ssem.at[0]).start()
        start_gather(0, 0)

    @pl.when(used)
    def _():
        start_gather(TE, 1 - slot)
        prev_e = blk_e_ref[jnp.maximum(i - 1, 0)]

        @pl.when((i == 0) | (blk_e_ref[i] != prev_e))
        def _():
            wg_bf[...] = wg_ref[...].astype(jnp.bfloat16)
            wu_bf[...] = wu_ref[...].astype(jnp.bfloat16)
            wd_bf[...] = wd_ref[...].astype(jnp.bfloat16)

        wait_gather(slot)
        xb = xbuf.at[slot]
        x = jnp.concatenate([xb[pl.ds(c, TE, stride=ROWS_PER_TOK), :] for c in range(ROWS_PER_TOK)],
                            axis=1).astype(jnp.bfloat16)
        hg = jnp.dot(x, wg_bf[...], preferred_element_type=jnp.float32)
        hu = jnp.dot(x, wu_bf[...], preferred_element_type=jnp.float32)
        hid = (hg * jax.nn.sigmoid(hg) * hu).astype(jnp.bfloat16)
        y = jnp.dot(hid, wd_bf[...], preferred_element_type=jnp.float32)

        wait_scatter()
        for c in range(ROWS_PER_TOK):
            ybuf[pl.ds(c, TE, stride=ROWS_PER_TOK), :] = y[:, c * LANES:(c + 1) * LANES]
        start_scatter()

        next_used = (i + 1 < nb) & (blk_used_ref[jnp.minimum(i + 1, nb - 1)] > 0)

        @pl.when(jnp.logical_not(next_used))
        def _():
            wait_gather(1 - slot)
            wait_scatter()


def _moe(xg, eid, w_gate, w_up, w_down, layer, t):
    a = t * TOP_K
    nb = a // TE + N_EXPERTS
    blk_e, blk_used, idx = _dispatch_plan(eid, t)
    grid_spec = pltpu.PrefetchScalarGridSpec(
        num_scalar_prefetch=2,
        grid=(nb,),
        in_specs=[
            pl.BlockSpec((1, 1, 3 * TE), lambda i, be, bu: (i, 0, 0), memory_space=pltpu.SMEM),
            pl.BlockSpec(memory_space=pl.ANY),
            pl.BlockSpec((None, None, D_MODEL, D_EXPERT), lambda i, be, bu: (layer, be[i], 0, 0)),
            pl.BlockSpec((None, None, D_MODEL, D_EXPERT), lambda i, be, bu: (layer, be[i], 0, 0)),
            pl.BlockSpec((None, None, D_EXPERT, D_MODEL), lambda i, be, bu: (layer, be[i], 0, 0)),
        ],
        out_specs=pl.BlockSpec(memory_space=pl.ANY),
        scratch_shapes=[
            pltpu.VMEM((2, TE * ROWS_PER_TOK, LANES), jnp.float32),
            pltpu.VMEM((TE * ROWS_PER_TOK, LANES), jnp.float32),
            pltpu.VMEM((D_MODEL, D_EXPERT), jnp.bfloat16),
            pltpu.VMEM((D_MODEL, D_EXPERT), jnp.bfloat16),
            pltpu.VMEM((D_EXPERT, D_MODEL), jnp.bfloat16),
            pltpu.SemaphoreType.DMA((2,)),
            pltpu.SemaphoreType.DMA((1,)),
        ],
    )
    return pl.pallas_call(
        _moe_kernel,
        grid_spec=grid_spec,
        out_shape=jax.ShapeDtypeStruct(((a + TE) * ROWS_PER_TOK, LANES), jnp.float32),
        compiler_params=_cparams(("arbitrary",)),
        name="moe_experts",
    )(blk_e, blk_used, idx, xg, w_gate, w_up, w_down)


def _combine(x, ys_ref, route_ref, tm):
    stride = TOP_K * ROWS_PER_TOK
    y0 = jnp.concatenate([ys_ref[pl.ds(c, tm, stride=stride), :] for c in range(ROWS_PER_TOK)], axis=1)
    y1 = jnp.concatenate([ys_ref[pl.ds(ROWS_PER_TOK + c, tm, stride=stride), :]
                          for c in range(ROWS_PER_TOK)], axis=1)
    route = route_ref[...]
    return x + route[:, 2:3] * y0 + route[:, 3:4] * y1


def _odd_proj_kernel(x_ref, ys_ref, route_ref, g_ref, w_ref, cos_ref, sin_ref,
                     xo_ref, q_ref, k_ref, v_ref, gate_ref):
    tm = x_ref.shape[0]
    x = _combine(x_ref[...], ys_ref, route_ref, tm)
    xo_ref[...] = x
    xn = (x * _rms_scale(x) * g_ref[...]).astype(jnp.bfloat16)
    cos = cos_ref[...]
    sin = sin_ref[...]
    nqk = RET_HEADS * RET_QK
    half = RET_QK // 2

    def rope_store(z, dst, scale):
        for h in range(RET_HEADS):
            x1 = z[:, h * RET_QK:h * RET_QK + half]
            x2 = z[:, h * RET_QK + half:(h + 1) * RET_QK]
            dst[:, h * RET_QK:h * RET_QK + half] = ((x1 * cos - x2 * sin) * scale).astype(dst.dtype)
            dst[:, h * RET_QK + half:(h + 1) * RET_QK] = ((x1 * sin + x2 * cos) * scale).astype(dst.dtype)

    zq = jnp.dot(xn, w_ref[:, 0:nqk], preferred_element_type=jnp.float32)
    rope_store(zq, q_ref, 1.0)
    zk = jnp.dot(xn, w_ref[:, nqk:2 * nqk], preferred_element_type=jnp.float32)
    rope_store(zk, k_ref, RET_QK ** -0.5)
    nv = RET_HEADS * RET_V
    zv = jnp.dot(xn, w_ref[:, 2 * nqk:2 * nqk + nv], preferred_element_type=jnp.float32)
    v_ref[...] = zv.astype(v_ref.dtype)
    zg = jnp.dot(xn, w_ref[:, 2 * nqk + nv:2 * nqk + 2 * nv], preferred_element_type=jnp.float32)
    gate_ref[...] = (zg * jax.nn.sigmoid(zg)).astype(gate_ref.dtype)


def _ret_rope_tables(s):
    half = RET_QK // 2
    inv = ROPE_BASE ** (-jnp.arange(half, dtype=jnp.float32) / half)
    ang = jnp.arange(s, dtype=jnp.float32)[:, None] * inv[None, :]
    return jnp.cos(ang), jnp.sin(ang)


def _odd_proj(x2d, ys, route, seq, g, w_in):
    t = x2d.shape[0]
    tm = TM_PROJ
    tiles_per_seq = seq // tm
    cos, sin = _ret_rope_tables(seq)
    nqk = RET_HEADS * RET_QK
    nv = RET_HEADS * RET_V
    n_in = w_in.shape[1]
    const = lambda shape: pl.BlockSpec(shape, lambda i: (0,) * len(shape))
    tok = lambda n: pl.BlockSpec((tm, n), lambda i: (i, 0))
    return pl.pallas_call(
        _odd_proj_kernel,
        grid=(t // tm,),
        in_specs=[
            tok(D_MODEL),
            pl.BlockSpec((tm * TOP_K * ROWS_PER_TOK, LANES), lambda i: (i, 0)),
            tok(LANES),
            const((1, D_MODEL)), const((D_MODEL, n_in)),
            pl.BlockSpec((tm, LANES), lambda i: (i % tiles_per_seq, 0)),
            pl.BlockSpec((tm, LANES), lambda i: (i % tiles_per_seq, 0)),
        ],
        out_specs=[tok(D_MODEL), tok(nqk), tok(nqk), tok(nv), tok(nv)],
        out_shape=[
            jax.ShapeDtypeStruct((t, D_MODEL), jnp.float32),
            jax.ShapeDtypeStruct((t, nqk), jnp.bfloat16),
            jax.ShapeDtypeStruct((t, nqk), jnp.bfloat16),
            jax.ShapeDtypeStruct((t, nv), jnp.bfloat16),
            jax.ShapeDtypeStruct((t, nv), jnp.bfloat16),
        ],
        compiler_params=_cparams(("arbitrary",)),
        name="odd_proj",
    )(x2d, ys, route, g, w_in, cos, sin)


def _retention_kernel(q_ref, k_ref, v_ref, gate_ref, dmask_ref, xi_ref, zeta_ref, gc_ref, o_ref, state_ref):
    n = pl.program_id(2)

    @pl.when(n == 0)
    def _():
        state_ref[...] = jnp.zeros_like(state_ref)

    xi = xi_ref[...][:, 0:1]
    zeta = zeta_ref[...][:, 0:1]
    g_c = gc_ref[...][:, 0:1]
    for j in range(q_ref.shape[0] // RET_C):
        rows = slice(j * RET_C, (j + 1) * RET_C)
        q = q_ref[rows, :]
        k = k_ref[rows, :]
        v = v_ref[rows, :]
        inner = lax.dot_general(q, k, (((1,), (1,)), ((), ())),
                                preferred_element_type=jnp.float32) * dmask_ref[...]
        o = jnp.dot(inner.astype(jnp.bfloat16), v, preferred_element_type=jnp.float32)
        state = state_ref[...]
        cross = jnp.dot(q, state.astype(jnp.bfloat16), preferred_element_type=jnp.float32)
        o = o + cross * xi
        kz = (k.astype(jnp.float32) * zeta).astype(jnp.bfloat16)
        upd = lax.dot_general(kz, v, (((0,), (0,)), ((), ())), preferred_element_type=jnp.float32)
        state_ref[...] = g_c * state + upd
        o = o * _rms_scale(o)
        o_ref[rows, :] = (o * gate_ref[rows, :].astype(jnp.float32)).astype(o_ref.dtype)


def _retention_tables():
    c = RET_C
    log_g = jnp.log(1.0 - jnp.exp2(-5.0 - jnp.arange(RET_HEADS, dtype=jnp.float32)))
    j = jnp.arange(c, dtype=jnp.float32)
    diff = j[:, None] - j[None, :]
    dmask = jnp.where(diff >= 0, jnp.exp(jnp.maximum(diff, 0.0)[None] * log_g[:, None, None]), 0.0)
    xi = jnp.exp((j[None, :] + 1.0) * log_g[:, None])
    zeta = jnp.exp((c - 1.0 - j)[None, :] * log_g[:, None])
    g_c = jnp.exp(c * log_g)
    bc = lambda a: jnp.broadcast_to(a[..., None], a.shape + (LANES,))
    return dmask, bc(xi), bc(zeta), bc(g_c[:, None])


def _retention(q, k, v, gate, batch, seq):
    nc = seq // RET_BLK
    dmask, xi, zeta, g_c = _retention_tables()
    row = lambda n: pl.BlockSpec((RET_BLK, n), lambda b, h, c: (b * nc + c, h))
    head = lambda shape: pl.BlockSpec((None,) + shape, lambda b, h, c: (h, 0, 0))
    return pl.pallas_call(
        _retention_kernel,
        grid=(batch, RET_HEADS, nc),
        in_specs=[row(RET_QK), row(RET_QK), row(RET_V), row(RET_V),
                  head((RET_C, RET_C)), head((RET_C, LANES)), head((RET_C, LANES)), head((1, LANES))],
        out_specs=row(RET_V),
        out_shape=jax.ShapeDtypeStruct((batch * seq, RET_HEADS * RET_V), jnp.bfloat16),
        scratch_shapes=[pltpu.VMEM((RET_QK, RET_V), jnp.float32)],
        compiler_params=_cparams(("arbitrary",) * 3),
        name="retention",
    )(q, k, v, gate, dmask, xi, zeta, g_c)


def _final_kernel(x_ref, ys_ref, route_ref, g_ref, o_ref):
    x = _combine(x_ref[...], ys_ref, route_ref, x_ref.shape[0])
    o_ref[...] = x * _rms_scale(x) * g_ref[...]


def _final(x2d, ys, route, g):
    t = x2d.shape[0]
    tm = TM_OUT
    return pl.pallas_call(
        _final_kernel,
        grid=(t // tm,),
        in_specs=[
            pl.BlockSpec((tm, D_MODEL), lambda i: (i, 0)),
            pl.BlockSpec((tm * TOP_K * ROWS_PER_TOK, LANES), lambda i: (i, 0)),
            pl.BlockSpec((tm, LANES), lambda i: (i, 0)),
            pl.BlockSpec((1, D_MODEL), lambda i: (0, 0)),
        ],
        out_specs=pl.BlockSpec((tm, D_MODEL), lambda i: (i, 0)),
        out_shape=jax.ShapeDtypeStruct((t, D_MODEL), jnp.float32),
        compiler_params=_cparams(("arbitrary",)),
        name="final_norm",
    )(x2d, ys, route, g)


def kernel(x, ln_mix_e, w_in_e, ln_q_e, w_uq_e, ln_kv_e, w_ukv_e, b_f_e, w_out_e, ln_mix_o, w_in_o,
           w_out_o, ln_ffn, w_rg, b_rg, w_re, b_re, w_gate, w_up, w_down, ln_f):
    batch, seq, d = x.shape
    t = batch * seq
    assert d == D_MODEL and seq % (2 * TK) == 0 and seq % TQ == 0 and seq % RET_BLK == 0
    assert seq % TM_PROJ == 0 and t % TM_OUT == 0 and t % TE == 0
    bf16 = jnp.bfloat16
    x2d = x.reshape(t, d)

    w_in_p, w_uq_p, w_ukv_p, b_f_p = _even_weights(w_in_e[0], w_uq_e[0], w_ukv_e[0], b_f_e[0])
    q, k, vt = _even_proj(x2d, seq, ln_mix_e[0][None, :], w_in_p, ln_q_e[0][None, :], w_uq_p,
                         ln_kv_e[0][None, :], w_ukv_p, b_f_p)
    o_t = (_attention(q, k, vt, batch, seq, 0, 2 * TK), _attention(q, k, vt, batch, seq, 1, TK))
    w_r, b_r = _router_weights(w_rg[0], b_rg[0], w_re[0], b_re[0])
    x2d, xg, route = _out_proj(x2d, o_t, w_out_e[0].astype(bf16), ln_ffn[0][None, :], w_r, b_r, seq)
    ys = _moe(xg, route[:, 0:TOP_K].astype(jnp.int32), w_gate, w_up, w_down, 0, t)

    x2d, rq, rk, rv, rg = _odd_proj(x2d, ys, route, seq, ln_mix_o[0][None, :], w_in_o[0].astype(bf16))
    og = _retention(rq, rk, rv, rg, batch, seq)
    w_r, b_r = _router_weights(w_rg[1], b_rg[1], w_re[1], b_re[1])
    x2d, xg, route = _out_proj(x2d, og, w_out_o[0].astype(bf16), ln_ffn[1][None, :], w_r, b_r, seq)
    ys = _moe(xg, route[:, 0:TOP_K].astype(jnp.int32), w_gate, w_up, w_down, 1, t)

    out = _final(x2d, ys, route, ln_f[None, :])
    return out.reshape(batch, seq, d)
```

```python
import functools
import math

import numpy as np
import jax
import jax.numpy as jnp
from jax import lax
from jax.experimental import pallas as pl
from jax.experimental.pallas import tpu as pltpu

D_MODEL = 1024
CHUNK = 64
ROPE_BASE = 10000.0
EPS = 1e-6
MLA_HEADS = 8
MLA_NOPE = 64
MLA_ROPE = 32
MLA_V = 64
MLA_Q_LORA = 256
MLA_KV_LORA = 128
FOX_HEADS = 8
FOX_DIM = 64
RET_HEADS = 4
RET_QK = 256
RET_V = 512
N_GROUPS = 4
EXPERTS_PER_GROUP = 8
N_EXPERTS = N_GROUPS * EXPERTS_PER_GROUP
TOP_K = 2
D_EXPERT = 512

LANES = 128
SUBLANES = 8
VMEM_LIMIT = 52 * 1024 * 1024

TM_PROJ = 256
TM_OUT = 512
OUT_SPLIT = 2
TQ = 512
VT_ROWS = 80
TK = 1024
LOG2E = math.log2(math.e)
RET_C = 256
RET_BLK = 2048
TE = 256
ROWS_PER_TOK = D_MODEL // LANES

N_PAIRS = MLA_HEADS // 2
PAIR_W = 256
ROPE_H = MLA_ROPE // 2
NEG = -1e30

_X1_OFF = 0
_X2_OFF = 64
_AUG_W = 6

_C_Q = 0
_C_KV = _C_Q + MLA_Q_LORA
_C_KPE = _C_KV + MLA_KV_LORA
_C_QF = _C_KPE + LANES
_C_KF = _C_QF + FOX_HEADS * FOX_DIM
_C_VF = _C_KF + FOX_HEADS * FOX_DIM
_C_F = _C_VF + FOX_HEADS * FOX_DIM
EVEN_N = _C_F + LANES


def _cparams(sem):
    return pltpu.CompilerParams(dimension_semantics=sem, vmem_limit_bytes=VMEM_LIMIT)


def _rms_scale(x):
    return lax.rsqrt(jnp.mean(x * x, axis=-1, keepdims=True) + EPS)


def _split3(c):
    hi = c.astype(jnp.bfloat16)
    r1 = c - hi.astype(jnp.float32)
    mid = r1.astype(jnp.bfloat16)
    r2 = r1 - mid.astype(jnp.float32)
    lo = r2.astype(jnp.bfloat16)
    return hi, mid, lo


def _even_proj_kernel(tiles_per_seq, x_ref, g_ref, win_ref, lnq_ref, wuq_ref, lnkv_ref, wukv_ref,
                      bf_ref, cos_ref, sin_ref, tri_ref, pq_ref, pk_ref, oq_ref, ok_ref,
                      q_ref, k_ref, vt_ref, carry_ref):
    i = pl.program_id(0)
    x = x_ref[...]
    xn = (x * _rms_scale(x) * g_ref[...]).astype(jnp.bfloat16)
    z = jnp.dot(xn, win_ref[...], preferred_element_type=jnp.float32)
    cos = cos_ref[...]
    sin = sin_ref[...]

    def rope_slab(s):
        return s * cos + pltpu.roll(s, 64, axis=1) * sin

    c_q = z[:, _C_Q:_C_Q + MLA_Q_LORA]
    cqn = (c_q * _rms_scale(c_q) * lnq_ref[...]).astype(jnp.bfloat16)
    q = jnp.dot(cqn, wuq_ref[...], preferred_element_type=jnp.float32)
    scale_a = (MLA_NOPE + MLA_ROPE) ** -0.5 * LOG2E
    for p in range(N_PAIRS):
        lo = p * PAIR_W
        q_ref[0, :, lo:lo + LANES] = (q[:, lo:lo + LANES] * scale_a).astype(jnp.bfloat16)
        q_ref[0, :, lo + LANES:lo + PAIR_W] = (
            rope_slab(q[:, lo + LANES:lo + PAIR_W]) * scale_a).astype(jnp.bfloat16)

    c_kv = z[:, _C_KV:_C_KV + MLA_KV_LORA]
    ckn = (c_kv * _rms_scale(c_kv) * lnkv_ref[...]).astype(jnp.bfloat16)
    kv = jnp.dot(ckn, wukv_ref[...], preferred_element_type=jnp.float32)
    kpe = rope_slab(z[:, _C_KPE:_C_KPE + LANES]).astype(jnp.bfloat16)
    nk = MLA_HEADS * MLA_NOPE
    for p in range(N_PAIRS):
        lo = p * PAIR_W
        k_ref[0, :, lo:lo + LANES] = kv[:, p * LANES:(p + 1) * LANES].astype(jnp.bfloat16)
        k_ref[0, :, lo + LANES:lo + PAIR_W] = kpe
    _store_values_transposed(vt_ref, 0, kv[:, nk:nk + MLA_HEADS * MLA_V])

    lane = lax.broadcasted_iota(jnp.int32, (1, LANES), 1)
    fz = z[:, _C_F:_C_F + LANES] + bf_ref[...]
    log_f = -(jnp.maximum(-fz, 0.0) + jnp.log1p(jnp.exp(-jnp.abs(fz))))
    log_f = jnp.where(lane < FOX_HEADS, log_f, 0.0)

    @pl.when(i % tiles_per_seq == 0)
    def _():
        carry_ref[...] = jnp.zeros_like(carry_ref)

    hi, mid, lo3 = _split3(log_f)
    tri = tri_ref[...]
    cum = (jnp.dot(tri, hi, preferred_element_type=jnp.float32)
           + jnp.dot(tri, mid, preferred_element_type=jnp.float32)
           + jnp.dot(tri, lo3, preferred_element_type=jnp.float32)) + carry_ref[...]
    tm = cum.shape[0]
    carry_ref[...] = cum[tm - 1:tm, :]

    parts = jnp.concatenate(_split3(cum * LOG2E), axis=1)
    aug_q = jnp.dot(parts, pq_ref[...], preferred_element_type=jnp.float32) + oq_ref[...]
    aug_k = jnp.dot(parts, pk_ref[...], preferred_element_type=jnp.float32) + ok_ref[...]
    scale_b = FOX_DIM ** -0.5 * LOG2E
    for p in range(N_PAIRS):
        lo = p * PAIR_W
        q_ref[1, :, lo:lo + LANES] = (
            z[:, _C_QF + p * LANES:_C_QF + (p + 1) * LANES] * scale_b).astype(jnp.bfloat16)
        q_ref[1, :, lo + LANES:lo + PAIR_W] = aug_q[:, p * LANES:(p + 1) * LANES].astype(jnp.bfloat16)
        k_ref[1, :, lo:lo + LANES] = z[:, _C_KF + p * LANES:_C_KF + (p + 1) * LANES].astype(jnp.bfloat16)
        k_ref[1, :, lo + LANES:lo + PAIR_W] = aug_k[:, p * LANES:(p + 1) * LANES].astype(jnp.bfloat16)
    _store_values_transposed(vt_ref, 1, z[:, _C_VF:_C_VF + FOX_HEADS * FOX_DIM])


def _store_values_transposed(vt_ref, g, v):
    vt = v.T
    ones = jnp.ones((VT_ROWS - MLA_V, vt.shape[1]), vt_ref.dtype)
    for h in range(MLA_HEADS):
        vt_ref[g, h * VT_ROWS:h * VT_ROWS + MLA_V, :] = vt[h * MLA_V:(h + 1) * MLA_V].astype(vt_ref.dtype)
        vt_ref[g, h * VT_ROWS + MLA_V:(h + 1) * VT_ROWS, :] = ones


def _even_weights(w_in, w_uq, w_ukv, b_f):
    d = w_in.shape[0]
    zeros = lambda n: jnp.zeros((d, n), w_in.dtype)
    o_cq, o_ckv = 0, MLA_Q_LORA
    o_kpe = o_ckv + MLA_KV_LORA
    o_qf = o_kpe + MLA_ROPE
    o_kf = o_qf + FOX_HEADS * FOX_DIM
    o_vf = o_kf + FOX_HEADS * FOX_DIM
    o_f = o_vf + FOX_HEADS * FOX_DIM
    kpe1 = w_in[:, o_kpe:o_kpe + ROPE_H]
    kpe2 = w_in[:, o_kpe + ROPE_H:o_kpe + MLA_ROPE]
    kpe_slab = jnp.concatenate([kpe1, kpe1, zeros(32), kpe2, kpe2, zeros(32)], axis=1)
    f_slab = jnp.concatenate([w_in[:, o_f:o_f + FOX_HEADS], zeros(LANES - FOX_HEADS)], axis=1)
    w_in_p = jnp.concatenate([w_in[:, o_cq:o_kpe], kpe_slab, w_in[:, o_qf:o_f], f_slab], axis=1)

    dq = w_uq.shape[0]
    zq = lambda n: jnp.zeros((dq, n), w_uq.dtype)
    hd = MLA_NOPE + MLA_ROPE
    blocks = []
    for p in range(N_PAIRS):
        h0, h1 = 2 * p, 2 * p + 1
        nope = lambda h: w_uq[:, h * hd:h * hd + MLA_NOPE]
        r1 = lambda h: w_uq[:, h * hd + MLA_NOPE:h * hd + MLA_NOPE + ROPE_H]
        r2 = lambda h: w_uq[:, h * hd + MLA_NOPE + ROPE_H:(h + 1) * hd]
        blocks += [nope(h0), nope(h1), r1(h0), r1(h1), zq(32), r2(h0), r2(h1), zq(32)]
    w_uq_p = jnp.concatenate(blocks, axis=1)

    kvd = MLA_NOPE + MLA_V
    k_cols = [w_ukv[:, h * kvd:h * kvd + MLA_NOPE] for h in range(MLA_HEADS)]
    v_cols = [w_ukv[:, h * kvd + MLA_NOPE:(h + 1) * kvd] for h in range(MLA_HEADS)]
    w_ukv_p = jnp.concatenate(k_cols + v_cols, axis=1)
    b_f_p = jnp.concatenate([b_f, jnp.zeros((LANES - FOX_HEADS,), b_f.dtype)])[None, :]
    bf16 = jnp.bfloat16
    return w_in_p.astype(bf16), w_uq_p.astype(bf16), w_ukv_p.astype(bf16), b_f_p


def _aug_placement():
    pq = np.zeros((3 * LANES, N_PAIRS * LANES), np.float32)
    pk = np.zeros((3 * LANES, N_PAIRS * LANES), np.float32)
    oq = np.zeros((1, N_PAIRS * LANES), np.float32)
    ok = np.zeros((1, N_PAIRS * LANES), np.float32)
    for p in range(N_PAIRS):
        for j in range(2):
            h = 2 * p + j
            base = p * LANES + _X1_OFF + j * ROPE_H
            for t in range(3):
                pq[t * LANES + h, base + 3 + t] = 1.0
                pk[t * LANES + h, base + t] = -1.0
                oq[0, base + t] = 1.0
                ok[0, base + 3 + t] = 1.0
    return (jnp.asarray(pq, jnp.bfloat16), jnp.asarray(pk, jnp.bfloat16),
            jnp.asarray(oq), jnp.asarray(ok))


def _mla_rope_tables(s):
    inv = ROPE_BASE ** (-jnp.arange(ROPE_H, dtype=jnp.float32) / ROPE_H)
    ang = jnp.arange(s, dtype=jnp.float32)[:, None] * inv[None, :]
    c, sn = jnp.cos(ang), jnp.sin(ang)
    z = jnp.zeros((s, 32), jnp.float32)
    cos = jnp.concatenate([c, c, z, c, c, z], axis=1)
    sin = jnp.concatenate([-sn, -sn, z, sn, sn, z], axis=1)
    return cos, sin


def _even_proj(x2d, seq, g, w_in_p, ln_q, w_uq_p, ln_kv, w_ukv_p, b_f_p):
    t = x2d.shape[0]
    tm = TM_OUT
    tiles_per_seq = seq // tm
    cos, sin = _mla_rope_tables(seq)
    tri = jnp.asarray(np.tril(np.ones((tm, tm), np.float32)), jnp.bfloat16)
    pq, pk, oq, ok = _aug_placement()
    const = lambda shape: pl.BlockSpec(shape, lambda i: (0,) * len(shape))
    return pl.pallas_call(
        functools.partial(_even_proj_kernel, tiles_per_seq),
        grid=(t // tm,),
        in_specs=[
            pl.BlockSpec((tm, D_MODEL), lambda i: (i, 0)),
            const((1, D_MODEL)), const((D_MODEL, EVEN_N)),
            const((1, MLA_Q_LORA)), const((MLA_Q_LORA, N_PAIRS * PAIR_W)),
            const((1, MLA_KV_LORA)), const((MLA_KV_LORA, 2 * MLA_HEADS * MLA_NOPE)),
            const((1, LANES)),
            pl.BlockSpec((tm, LANES), lambda i: (i % tiles_per_seq, 0)),
            pl.BlockSpec((tm, LANES), lambda i: (i % tiles_per_seq, 0)),
            const((tm, tm)), const(pq.shape), const(pk.shape), const(oq.shape), const(ok.shape),
        ],
        out_specs=[
            pl.BlockSpec((2, tm, N_PAIRS * PAIR_W), lambda i: (0, i, 0)),
            pl.BlockSpec((2, tm, N_PAIRS * PAIR_W), lambda i: (0, i, 0)),
            pl.BlockSpec((2, None, MLA_HEADS * VT_ROWS, tm),
                         lambda i: (0, i // tiles_per_seq, 0, i % tiles_per_seq)),
        ],
        out_shape=[
            jax.ShapeDtypeStruct((2, t, N_PAIRS * PAIR_W), jnp.bfloat16),
            jax.ShapeDtypeStruct((2, t, N_PAIRS * PAIR_W), jnp.bfloat16),
            jax.ShapeDtypeStruct((2, t // seq, MLA_HEADS * VT_ROWS, seq), jnp.bfloat16),
        ],
        scratch_shapes=[pltpu.VMEM((1, LANES), jnp.float32)],
        compiler_params=_cparams(("arbitrary",)),
        name="even_proj",
    )(x2d, g, w_in_p, ln_q, w_uq_p, ln_kv, w_ukv_p, b_f_p, cos, sin, tri, pq, pk, oq, ok)


def _attn_kernel(group, tk, q_ref, k_ref, vt_ref, o_ref, qcat_ref, s0_ref, s1_ref, m0_ref, m1_ref,
                 a0_ref, a1_ref, acc_ref):
    i = pl.program_id(2)
    tq = q_ref.shape[0]
    s_bufs, m_bufs, a_bufs = (s0_ref, s1_ref), (m0_ref, m1_ref), (a0_ref, a1_ref)

    lane = lax.broadcasted_iota(jnp.int32, (1, PAIR_W), 1)

    def head_mask(j):
        a = (lane >= j * MLA_NOPE) & (lane < (j + 1) * MLA_NOPE)
        b = (lane >= LANES + _X1_OFF + j * ROPE_H) & (lane < LANES + _X1_OFF + (j + 1) * ROPE_H)
        c = (lane >= LANES + _X2_OFF + j * ROPE_H) & (lane < LANES + _X2_OFF + (j + 1) * ROPE_H)
        return a | b | c

    q = q_ref[...]
    zero = jnp.zeros_like(q)
    qcat_ref[0:tq, :] = jnp.where(head_mask(0), q, zero)
    qcat_ref[tq:2 * tq, :] = jnp.where(head_mask(1), q, zero)
    acc_ref[...] = jnp.zeros_like(acc_ref)

    q0 = i * tq
    n_chunks = q0 // tk + 1

    def scores(c, m_run, par, masked):
        kstart = pl.multiple_of(c * tk, tk)
        s = lax.dot_general(k_ref[pl.ds(kstart, tk), :], qcat_ref[...], (((1,), (1,)), ((), ())),
                            preferred_element_type=jnp.float32)
        if masked:
            col = lax.broadcasted_iota(jnp.int32, (1, 2 * tq), 1)
            qpos = q0 + jnp.where(col >= tq, col - tq, col)
            qlim = (qpos | (CHUNK - 1)) if group == 0 else qpos
            kpos = kstart + lax.broadcasted_iota(jnp.int32, (tk, 1), 0)
            s = jnp.where(kpos <= qlim, s, NEG)
        s_bufs[par][...] = s
        m_new = jnp.maximum(m_run, jnp.max(s, axis=0, keepdims=True))
        m_bufs[par][...] = m_new
        a_bufs[par][...] = jnp.exp2(m_run - m_new)

    def accumulate(c, par):
        kstart = pl.multiple_of(c * tk, tk)
        alpha = a_bufs[par][...]
        p = jnp.exp2(s_bufs[par][...] - m_bufs[par][...]).astype(jnp.bfloat16)
        for h in range(2):
            cols = slice(h * tq, (h + 1) * tq)
            acc_ref[h] = alpha[:, cols] * acc_ref[h] + jnp.dot(
                vt_ref[h * VT_ROWS:(h + 1) * VT_ROWS, pl.ds(kstart, tk)], p[:, cols],
                preferred_element_type=jnp.float32)

    def stage(c, par, masked):
        scores(c + 1, m_bufs[par][...], 1 - par, masked)
        accumulate(c, par)

    m_init = jnp.full((1, 2 * tq), NEG, jnp.float32)

    @pl.when(n_chunks == 1)
    def _():
        scores(0, m_init, 0, True)
        accumulate(0, 0)

    @pl.when(n_chunks > 1)
    def _():
        scores(0, m_init, 0, False)

    n_pairs = jnp.maximum(n_chunks - 2, 0) // 2

    def body(j, carry):
        for u in range(4):
            stage(4 * j + u, u % 2, False)
        return carry

    lax.fori_loop(0, n_pairs // 2, body, 0)

    @pl.when(n_pairs % 2 == 1)
    def _():
        stage(2 * n_pairs - 2, 0, False)
        stage(2 * n_pairs - 1, 1, False)
    c0 = 2 * n_pairs
    left = n_chunks - 1 - c0

    @pl.when((n_chunks > 1) & (left == 1))
    def _():
        stage(c0, 0, True)
        accumulate(c0 + 1, 1)

    @pl.when((n_chunks > 1) & (left == 2))
    def _():
        stage(c0, 0, False)
        stage(c0 + 1, 1, True)
        accumulate(c0 + 2, 0)

    for h in range(2):
        acc = acc_ref[h]
        o_ref[h * MLA_V:(h + 1) * MLA_V, :] = (acc[0:MLA_V] / acc[MLA_V:MLA_V + 1]).astype(o_ref.dtype)


def _attention(q, k, vt, batch, seq, group, tq, tk):
    nq = seq // tq
    f32 = jnp.float32
    return pl.pallas_call(
        functools.partial(_attn_kernel, group, tk),
        grid=(batch, N_PAIRS, nq),
        in_specs=[
            pl.BlockSpec((None, tq, PAIR_W), lambda b, p, i: (group, b * nq + i, p)),
            pl.BlockSpec((None, seq, PAIR_W), lambda b, p, i: (group, b, p)),
            pl.BlockSpec((None, None, 2 * VT_ROWS, seq), lambda b, p, i: (group, b, p, 0)),
        ],
        out_specs=pl.BlockSpec((None, LANES, tq), lambda b, p, i: (b, p, i)),
        out_shape=jax.ShapeDtypeStruct((batch, N_PAIRS * LANES, seq), jnp.bfloat16),
        scratch_shapes=[pltpu.VMEM((2 * tq, PAIR_W), jnp.bfloat16),
                        pltpu.VMEM((tk, 2 * tq), f32), pltpu.VMEM((tk, 2 * tq), f32),
                        pltpu.VMEM((1, 2 * tq), f32), pltpu.VMEM((1, 2 * tq), f32),
                        pltpu.VMEM((1, 2 * tq), f32), pltpu.VMEM((1, 2 * tq), f32),
                        pltpu.VMEM((2, VT_ROWS, tq), f32)],
        compiler_params=_cparams(("arbitrary",) * 3),
        name="attention_g%d" % group,
    )(q, k, vt)


def _router_t(lt):
    row = lax.broadcasted_iota(jnp.int32, lt.shape, 0)
    big = jnp.int32(LANES)
    ninf = -jnp.inf
    gl = jnp.where(row < N_GROUPS, lt, ninf)
    gmax = jnp.max(gl, axis=0, keepdims=True)
    gsel = jnp.min(jnp.where(gl == gmax, row, big), axis=0, keepdims=True)
    gsum = jnp.sum(jnp.where(row < N_GROUPS, jnp.exp(lt - gmax), 0.0), axis=0, keepdims=True)
    p_g = 1.0 / gsum
    lo = N_GROUPS + EXPERTS_PER_GROUP * gsel
    e = jnp.where((row >= lo) & (row < lo + EXPERTS_PER_GROUP), lt, ninf)
    m1 = jnp.max(e, axis=0, keepdims=True)
    i1 = jnp.min(jnp.where(e == m1, row, big), axis=0, keepdims=True)
    e2 = jnp.where(row == i1, ninf, e)
    m2 = jnp.max(e2, axis=0, keepdims=True)
    i2 = jnp.min(jnp.where(e2 == m2, row, big), axis=0, keepdims=True)
    a2 = jnp.exp(m2 - m1)
    den = 1.0 + a2
    g1 = p_g / den
    g2 = p_g * a2 / den
    f = lambda v: v.astype(jnp.float32)
    return jnp.where(row == 0, f(i1 - N_GROUPS),
                     jnp.where(row == 1, f(i2 - N_GROUPS),
                               jnp.where(row == 2, g1, jnp.where(row == 3, g2, 0.0))))


def _out_proj_kernel(n_o, transposed, x_ref, *refs):
    o_refs = refs[:n_o]
    w_ref, ln_ref, wr_ref, br_ref, xo_ref, xg_ref, route_ref = refs[n_o:]
    tm = x_ref.shape[0]
    hm = tm // OUT_SPLIT
    for part in range(OUT_SPLIT):
        rows = slice(part * hm, (part + 1) * hm)
        if transposed:
            y = x_ref[rows, :]
            nf = o_refs[0].shape[0]
            for g, o_ref in enumerate(o_refs):
                y = y + lax.dot_general(o_ref[:, rows], w_ref[g * nf:(g + 1) * nf, :],
                                        (((0,), (0,)), ((), ())), preferred_element_type=jnp.float32)
        else:
            y = x_ref[rows, :] + jnp.dot(o_refs[0][rows, :], w_ref[...], preferred_element_type=jnp.float32)
        xo_ref[rows, :] = y
        xn = y * _rms_scale(y) * ln_ref[...]
        hi = xn.astype(jnp.bfloat16)
        lo = (xn - hi.astype(jnp.float32)).astype(jnp.bfloat16)
        l2 = jnp.dot(jnp.concatenate([hi, lo], axis=1), wr_ref[...], preferred_element_type=jnp.float32)
        lt = (l2[:, :LANES] + l2[:, LANES:] + br_ref[...]).T
        route_ref[rows, :] = _router_t(lt).T
        for c in range(ROWS_PER_TOK):
            xg_ref[pl.ds(part * hm * ROWS_PER_TOK + c, hm, stride=ROWS_PER_TOK), :] = (
                xn[:, c * LANES:(c + 1) * LANES])


def _out_proj(x2d, o, w_out, ln, w_r, b_r, seq):
    t = x2d.shape[0]
    kdim = w_out.shape[0]
    tm = TM_OUT
    transposed = isinstance(o, tuple)
    tiles_per_seq = seq // tm
    if transposed:
        o_specs = [pl.BlockSpec((None, a.shape[1], tm), lambda i: (i // tiles_per_seq, 0, i % tiles_per_seq))
                   for a in o]
    else:
        o, o_specs = (o,), [pl.BlockSpec((tm, kdim), lambda i: (i, 0))]
    const = lambda shape: pl.BlockSpec(shape, lambda i: (0,) * len(shape))
    return pl.pallas_call(
        functools.partial(_out_proj_kernel, len(o), transposed),
        grid=(t // tm,),
        in_specs=[
            pl.BlockSpec((tm, D_MODEL), lambda i: (i, 0)),
            *o_specs,
            const((kdim, D_MODEL)), const((1, D_MODEL)), const((2 * D_MODEL, 2 * LANES)), const((1, LANES)),
        ],
        out_specs=[
            pl.BlockSpec((tm, D_MODEL), lambda i: (i, 0)),
            pl.BlockSpec((tm * ROWS_PER_TOK, LANES), lambda i: (i, 0)),
            pl.BlockSpec((tm, LANES), lambda i: (i, 0)),
        ],
        out_shape=[
            jax.ShapeDtypeStruct((t, D_MODEL), jnp.float32),
            jax.ShapeDtypeStruct((t * ROWS_PER_TOK, LANES), jnp.float32),
            jax.ShapeDtypeStruct((t, LANES), jnp.float32),
        ],
        compiler_params=_cparams(("arbitrary",)),
        name="out_proj_router",
    )(x2d, *o, w_out, ln, w_r, b_r)


def _router_weights(w_rg, b_rg, w_re, b_re):
    d = w_rg.shape[0]
    pad = LANES - N_GROUPS - N_EXPERTS
    w_r = jnp.concatenate([w_rg, w_re, jnp.zeros((d, pad), w_rg.dtype)], axis=1)
    b_r = jnp.concatenate([b_rg, b_re, jnp.zeros((pad,), b_rg.dtype)])[None, :]
    w_hi = w_r.astype(jnp.bfloat16)
    w_lo = (w_r - w_hi.astype(jnp.float32)).astype(jnp.bfloat16)
    half = jnp.concatenate([w_hi, w_lo], axis=1)
    return jnp.concatenate([half, half], axis=0), b_r


def _dispatch_plan(eid, t):
    a = t * TOP_K
    nb = a // TE + N_EXPERTS
    e_flat = eid.reshape(a)
    experts = jnp.arange(N_EXPERTS, dtype=jnp.int32)
    _, order = lax.sort((e_flat, jnp.arange(a, dtype=jnp.int32)), num_keys=1, is_stable=True)
    counts = jnp.sum((e_flat[:, None] == experts[None, :]).astype(jnp.int32), axis=0)
    start = jnp.cumsum(counts) - counts
    padded = ((counts + TE - 1) // TE) * TE
    pend = jnp.cumsum(padded)
    pstart = pend - padded
    blk0 = jnp.arange(nb, dtype=jnp.int32) * TE
    blk_e = jnp.minimum(jnp.sum((blk0[:, None] >= pend[None, :]).astype(jnp.int32), axis=1), N_EXPERTS - 1)
    blk_used = (blk0 < pend[-1]).astype(jnp.int32)
    onehot = (blk_e[:, None] == experts[None, :]).astype(jnp.int32)
    per_blk = lambda v: jnp.sum(onehot * v[None, :], axis=1)[:, None]
    r = jnp.arange(TE, dtype=jnp.int32)[None, :]
    off = blk0[:, None] + r - per_blk(pstart)
    valid = (off < per_blk(counts)) & (blk_used[:, None] > 0)
    src = jnp.where(valid, per_blk(start) + off, 0)
    assign = order[src]
    src_tok = jnp.where(valid, assign // TOP_K, 0).astype(jnp.int32)
    dst_row = jnp.where(valid, assign, a + r).astype(jnp.int32)
    nxt = jnp.minimum(jnp.arange(nb) + 1, nb - 1)
    return blk_e, blk_used, jnp.concatenate([src_tok, src_tok[nxt], dst_row], axis=1)[:, None, :]


def _moe_kernel(blk_e_ref, blk_used_ref, idx_ref, xg_ref, wg_ref, wu_ref, wd_ref,
                ys_ref, xbuf, ybuf, wg_bf, wu_bf, wd_bf, gsem, ssem):
    i = pl.program_id(0)
    nb = pl.num_programs(0)
    used = blk_used_ref[i] > 0
    slot = i % 2
    block_rows = TE * ROWS_PER_TOK

    def start_gather(first, s):
        for r in range(TE):
            tok = idx_ref[0, 0, first + r]
            pltpu.make_async_copy(
                xg_ref.at[pl.ds(pl.multiple_of(tok * ROWS_PER_TOK, ROWS_PER_TOK), ROWS_PER_TOK), :],
                xbuf.at[s, pl.ds(r * ROWS_PER_TOK, ROWS_PER_TOK), :],
                gsem.at[s]).start(priority=r % 2)

    def wait_gather(s):
        pltpu.make_async_copy(xg_ref.at[pl.ds(0, block_rows), :], xbuf.at[s], gsem.at[s]).wait()

    def start_scatter():
        for r in range(TE):
            row = idx_ref[0, 0, 2 * TE + r]
            pltpu.make_async_copy(
                ybuf.at[pl.ds(r * ROWS_PER_TOK, ROWS_PER_TOK), :],
                ys_ref.at[pl.ds(pl.multiple_of(row * ROWS_PER_TOK, ROWS_PER_TOK), ROWS_PER_TOK), :],
                ssem.at[0]).start(priority=r % 2)

    def wait_scatter():
        pltpu.make_async_copy(ybuf, ys_ref.at[pl.ds(0, block_rows), :], ssem.at[0]).wait()

    @pl.when(i == 0)
    def _():
        ybuf[...] = jnp.zeros_like(ybuf)
        pltpu.make_async_copy(
            ybuf, ys_ref.at[pl.ds(ys_ref.shape[0] - block_rows, block_rows), :], ssem.at[0]).start()
        start_gather(0, 0)

    @pl.when(used)
    def _():
        start_gather(TE, 1 - slot)
        prev_e = blk_e_ref[jnp.maximum(i - 1, 0)]

        @pl.when((i == 0) | (blk_e_ref[i] != prev_e))
        def _():
            wg_bf[...] = wg_ref[...].astype(jnp.bfloat16)
            wu_bf[...] = wu_ref[...].astype(jnp.bfloat16)
            wd_bf[...] = wd_ref[...].astype(jnp.bfloat16)

        wait_gather(slot)
        xb = xbuf.at[slot]
        x = jnp.concatenate([xb[pl.ds(c, TE, stride=ROWS_PER_TOK), :] for c in range(ROWS_PER_TOK)],
                            axis=1).astype(jnp.bfloat16)
        hg = jnp.dot(x, wg_bf[...], preferred_element_type=jnp.float32)
        hu = jnp.dot(x, wu_bf[...], preferred_element_type=jnp.float32)
        hid = (hg * jax.nn.sigmoid(hg) * hu).astype(jnp.bfloat16)
        y = jnp.dot(hid, wd_bf[...], preferred_element_type=jnp.float32)

        wait_scatter()
        for c in range(ROWS_PER_TOK):
            ybuf[pl.ds(c, TE, stride=ROWS_PER_TOK), :] = y[:, c * LANES:(c + 1) * LANES]
        start_scatter()

        next_used = (i + 1 < nb) & (blk_used_ref[jnp.minimum(i + 1, nb - 1)] > 0)

        @pl.when(jnp.logical_not(next_used))
        def _():
            wait_gather(1 - slot)
            wait_scatter()


def _moe(xg, eid, w_gate, w_up, w_down, layer, t):
    a = t * TOP_K
    nb = a // TE + N_EXPERTS
    blk_e, blk_used, idx = _dispatch_plan(eid, t)
    grid_spec = pltpu.PrefetchScalarGridSpec(
        num_scalar_prefetch=2,
        grid=(nb,),
        in_specs=[
            pl.BlockSpec((1, 1, 3 * TE), lambda i, be, bu: (i, 0, 0), memory_space=pltpu.SMEM),
            pl.BlockSpec(memory_space=pl.ANY),
            pl.BlockSpec((None, None, D_MODEL, D_EXPERT), lambda i, be, bu: (layer, be[i], 0, 0)),
            pl.BlockSpec((None, None, D_MODEL, D_EXPERT), lambda i, be, bu: (layer, be[i], 0, 0)),
            pl.BlockSpec((None, None, D_EXPERT, D_MODEL), lambda i, be, bu: (layer, be[i], 0, 0)),
        ],
        out_specs=pl.BlockSpec(memory_space=pl.ANY),
        scratch_shapes=[
            pltpu.VMEM((2, TE * ROWS_PER_TOK, LANES), jnp.float32),
            pltpu.VMEM((TE * ROWS_PER_TOK, LANES), jnp.float32),
            pltpu.VMEM((D_MODEL, D_EXPERT), jnp.bfloat16),
            pltpu.VMEM((D_MODEL, D_EXPERT), jnp.bfloat16),
            pltpu.VMEM((D_EXPERT, D_MODEL), jnp.bfloat16),
            pltpu.SemaphoreType.DMA((2,)),
            pltpu.SemaphoreType.DMA((1,)),
        ],
    )
    return pl.pallas_call(
        _moe_kernel,
        grid_spec=grid_spec,
        out_shape=jax.ShapeDtypeStruct(((a + TE) * ROWS_PER_TOK, LANES), jnp.float32),
        compiler_params=_cparams(("arbitrary",)),
        name="moe_experts",
    )(blk_e, blk_used, idx, xg, w_gate, w_up, w_down)


def _combine(x, ys_ref, route_ref, tm):
    stride = TOP_K * ROWS_PER_TOK
    y0 = jnp.concatenate([ys_ref[pl.ds(c, tm, stride=stride), :] for c in range(ROWS_PER_TOK)], axis=1)
    y1 = jnp.concatenate([ys_ref[pl.ds(ROWS_PER_TOK + c, tm, stride=stride), :]
                          for c in range(ROWS_PER_TOK)], axis=1)
    route = route_ref[...]
    return x + route[:, 2:3] * y0 + route[:, 3:4] * y1


def _odd_proj_kernel(x_ref, ys_ref, route_ref, g_ref, w_ref, cos_ref, sin_ref,
                     xo_ref, q_ref, k_ref, v_ref, gate_ref):
    tm = x_ref.shape[0]
    x = _combine(x_ref[...], ys_ref, route_ref, tm)
    xo_ref[...] = x
    xn = (x * _rms_scale(x) * g_ref[...]).astype(jnp.bfloat16)
    cos = cos_ref[...]
    sin = sin_ref[...]
    nqk = RET_HEADS * RET_QK
    half = RET_QK // 2

    def rope_store(z, dst, scale):
        for h in range(RET_HEADS):
            x1 = z[:, h * RET_QK:h * RET_QK + half]
            x2 = z[:, h * RET_QK + half:(h + 1) * RET_QK]
            dst[:, h * RET_QK:h * RET_QK + half] = ((x1 * cos - x2 * sin) * scale).astype(dst.dtype)
            dst[:, h * RET_QK + half:(h + 1) * RET_QK] = ((x1 * sin + x2 * cos) * scale).astype(dst.dtype)

    zq = jnp.dot(xn, w_ref[:, 0:nqk], preferred_element_type=jnp.float32)
    rope_store(zq, q_ref, 1.0)
    zk = jnp.dot(xn, w_ref[:, nqk:2 * nqk], preferred_element_type=jnp.float32)
    rope_store(zk, k_ref, RET_QK ** -0.5)
    nv = RET_HEADS * RET_V
    zv = jnp.dot(xn, w_ref[:, 2 * nqk:2 * nqk + nv], preferred_element_type=jnp.float32)
    v_ref[...] = zv.astype(v_ref.dtype)
    zg = jnp.dot(xn, w_ref[:, 2 * nqk + nv:2 * nqk + 2 * nv], preferred_element_type=jnp.float32)
    gate_ref[...] = (zg * jax.nn.sigmoid(zg)).astype(gate_ref.dtype)


def _ret_rope_tables(s):
    half = RET_QK // 2
    inv = ROPE_BASE ** (-jnp.arange(half, dtype=jnp.float32) / half)
    ang = jnp.arange(s, dtype=jnp.float32)[:, None] * inv[None, :]
    return jnp.cos(ang), jnp.sin(ang)


def _odd_proj(x2d, ys, route, seq, g, w_in):
    t = x2d.shape[0]
    tm = TM_PROJ
    tiles_per_seq = seq // tm
    cos, sin = _ret_rope_tables(seq)
    nqk = RET_HEADS * RET_QK
    nv = RET_HEADS * RET_V
    n_in = w_in.shape[1]
    const = lambda shape: pl.BlockSpec(shape, lambda i: (0,) * len(shape))
    tok = lambda n: pl.BlockSpec((tm, n), lambda i: (i, 0))
    return pl.pallas_call(
        _odd_proj_kernel,
        grid=(t // tm,),
        in_specs=[
            tok(D_MODEL),
            pl.BlockSpec((tm * TOP_K * ROWS_PER_TOK, LANES), lambda i: (i, 0)),
            tok(LANES),
            const((1, D_MODEL)), const((D_MODEL, n_in)),
            pl.BlockSpec((tm, LANES), lambda i: (i % tiles_per_seq, 0)),
            pl.BlockSpec((tm, LANES), lambda i: (i % tiles_per_seq, 0)),
        ],
        out_specs=[tok(D_MODEL), tok(nqk), tok(nqk), tok(nv), tok(nv)],
        out_shape=[
            jax.ShapeDtypeStruct((t, D_MODEL), jnp.float32),
            jax.ShapeDtypeStruct((t, nqk), jnp.bfloat16),
            jax.ShapeDtypeStruct((t, nqk), jnp.bfloat16),
            jax.ShapeDtypeStruct((t, nv), jnp.bfloat16),
            jax.ShapeDtypeStruct((t, nv), jnp.bfloat16),
        ],
        compiler_params=_cparams(("arbitrary",)),
        name="odd_proj",
    )(x2d, ys, route, g, w_in, cos, sin)


def _retention_kernel(q_ref, k_ref, v_ref, gate_ref, dmask_ref, xi_ref, zeta_ref, gc_ref, o_ref, state_ref):
    n = pl.program_id(2)

    @pl.when(n == 0)
    def _():
        state_ref[...] = jnp.zeros_like(state_ref)

    xi = xi_ref[...][:, 0:1]
    zeta = zeta_ref[...][:, 0:1]
    g_c = gc_ref[...][:, 0:1]
    for j in range(q_ref.shape[0] // RET_C):
        rows = slice(j * RET_C, (j + 1) * RET_C)
        q = q_ref[rows, :]
        k = k_ref[rows, :]
        v = v_ref[rows, :]
        inner = lax.dot_general(q, k, (((1,), (1,)), ((), ())),
                                preferred_element_type=jnp.float32) * dmask_ref[...]
        o = jnp.dot(inner.astype(jnp.bfloat16), v, preferred_element_type=jnp.float32)
        state = state_ref[...]
        cross = jnp.dot(q, state.astype(jnp.bfloat16), preferred_element_type=jnp.float32)
        o = o + cross * xi
        kz = (k.astype(jnp.float32) * zeta).astype(jnp.bfloat16)
        upd = lax.dot_general(kz, v, (((0,), (0,)), ((), ())), preferred_element_type=jnp.float32)
        state_ref[...] = g_c * state + upd
        o = o * _rms_scale(o)
        o_ref[rows, :] = (o * gate_ref[rows, :].astype(jnp.float32)).astype(o_ref.dtype)


def _retention_tables():
    c = RET_C
    log_g = jnp.log(1.0 - jnp.exp2(-5.0 - jnp.arange(RET_HEADS, dtype=jnp.float32)))
    j = jnp.arange(c, dtype=jnp.float32)
    diff = j[:, None] - j[None, :]
    dmask = jnp.where(diff >= 0, jnp.exp(jnp.maximum(diff, 0.0)[None] * log_g[:, None, None]), 0.0)
    xi = jnp.exp((j[None, :] + 1.0) * log_g[:, None])
    zeta = jnp.exp((c - 1.0 - j)[None, :] * log_g[:, None])
    g_c = jnp.exp(c * log_g)
    bc = lambda a: jnp.broadcast_to(a[..., None], a.shape + (LANES,))
    return dmask, bc(xi), bc(zeta), bc(g_c[:, None])


def _retention(q, k, v, gate, batch, seq):
    nc = seq // RET_BLK
    dmask, xi, zeta, g_c = _retention_tables()
    row = lambda n: pl.BlockSpec((RET_BLK, n), lambda b, h, c: (b * nc + c, h))
    head = lambda shape: pl.BlockSpec((None,) + shape, lambda b, h, c: (h, 0, 0))
    return pl.pallas_call(
        _retention_kernel,
        grid=(batch, RET_HEADS, nc),
        in_specs=[row(RET_QK), row(RET_QK), row(RET_V), row(RET_V),
                  head((RET_C, RET_C)), head((RET_C, LANES)), head((RET_C, LANES)), head((1, LANES))],
        out_specs=row(RET_V),
        out_shape=jax.ShapeDtypeStruct((batch * seq, RET_HEADS * RET_V), jnp.bfloat16),
        scratch_shapes=[pltpu.VMEM((RET_QK, RET_V), jnp.float32)],
        compiler_params=_cparams(("arbitrary",) * 3),
        name="retention",
    )(q, k, v, gate, dmask, xi, zeta, g_c)


def _final_kernel(x_ref, ys_ref, route_ref, g_ref, o_ref):
    x = _combine(x_ref[...], ys_ref, route_ref, x_ref.shape[0])
    o_ref[...] = x * _rms_scale(x) * g_ref[...]


def _final(x2d, ys, route, g):
    t = x2d.shape[0]
    tm = TM_OUT
    return pl.pallas_call(
        _final_kernel,
        grid=(t // tm,),
        in_specs=[
            pl.BlockSpec((tm, D_MODEL), lambda i: (i, 0)),
            pl.BlockSpec((tm * TOP_K * ROWS_PER_TOK, LANES), lambda i: (i, 0)),
            pl.BlockSpec((tm, LANES), lambda i: (i, 0)),
            pl.BlockSpec((1, D_MODEL), lambda i: (0, 0)),
        ],
        out_specs=pl.BlockSpec((tm, D_MODEL), lambda i: (i, 0)),
        out_shape=jax.ShapeDtypeStruct((t, D_MODEL), jnp.float32),
        compiler_params=_cparams(("arbitrary",)),
        name="final_norm",
    )(x2d, ys, route, g)


def kernel(x, ln_mix_e, w_in_e, ln_q_e, w_uq_e, ln_kv_e, w_ukv_e, b_f_e, w_out_e, ln_mix_o, w_in_o,
           w_out_o, ln_ffn, w_rg, b_rg, w_re, b_re, w_gate, w_up, w_down, ln_f):
    batch, seq, d = x.shape
    t = batch * seq
    assert d == D_MODEL and seq % TK == 0 and seq % (2 * TQ) == 0 and seq % RET_BLK == 0
    assert seq % TM_PROJ == 0 and t % TM_OUT == 0 and t % TE == 0
    bf16 = jnp.bfloat16
    x2d = x.reshape(t, d)

    w_in_p, w_uq_p, w_ukv_p, b_f_p = _even_weights(w_in_e[0], w_uq_e[0], w_ukv_e[0], b_f_e[0])
    q, k, vt = _even_proj(x2d, seq, ln_mix_e[0][None, :], w_in_p, ln_q_e[0][None, :], w_uq_p,
                         ln_kv_e[0][None, :], w_ukv_p, b_f_p)
    o_t = (_attention(q, k, vt, batch, seq, 0, 2 * TQ, TK), _attention(q, k, vt, batch, seq, 1, TQ, TK))
    w_r, b_r = _router_weights(w_rg[0], b_rg[0], w_re[0], b_re[0])
    x2d, xg, route = _out_proj(x2d, o_t, w_out_e[0].astype(bf16), ln_ffn[0][None, :], w_r, b_r, seq)
    ys = _moe(xg, route[:, 0:TOP_K].astype(jnp.int32), w_gate, w_up, w_down, 0, t)

    x2d, rq, rk, rv, rg = _odd_proj(x2d, ys, route, seq, ln_mix_o[0][None, :], w_in_o[0].astype(bf16))
    og = _retention(rq, rk, rv, rg, batch, seq)
    w_r, b_r = _router_weights(w_rg[1], b_rg[1], w_re[1], b_re[1])
    x2d, xg, route = _out_proj(x2d, og, w_out_o[0].astype(bf16), ln_ffn[1][None, :], w_r, b_r, seq)
    ys = _moe(xg, route[:, 0:TOP_K].astype(jnp.int32), w_gate, w_up, w_down, 1, t)

    out = _final(x2d, ys, route, ln_f[None, :])
    return out.reshape(batch, seq, d)
```

```python
import functools
import math

import numpy as np
import jax
import jax.numpy as jnp
from jax import lax
from jax.experimental import pallas as pl
from jax.experimental.pallas import tpu as pltpu

D_MODEL = 1024
CHUNK = 64
ROPE_BASE = 10000.0
EPS = 1e-6
MLA_HEADS = 8
MLA_NOPE = 64
MLA_ROPE = 32
MLA_V = 64
MLA_Q_LORA = 256
MLA_KV_LORA = 128
FOX_HEADS = 8
FOX_DIM = 64
RET_HEADS = 4
RET_QK = 256
RET_V = 512
N_GROUPS = 4
EXPERTS_PER_GROUP = 8
N_EXPERTS = N_GROUPS * EXPERTS_PER_GROUP
TOP_K = 2
D_EXPERT = 512

LANES = 128
SUBLANES = 8
VMEM_LIMIT = 52 * 1024 * 1024

TM_PROJ = 256
TM_OUT = 512
OUT_SPLIT = 2
TQ = 1024
VT_ROWS = 80
TK = 1024
LOG2E = math.log2(math.e)
RET_C = 256
RET_BLK = 2048
TE = 512
ROWS_PER_TOK = D_MODEL // LANES

N_PAIRS = MLA_HEADS // 2
PAIR_W = 256
ROPE_H = MLA_ROPE // 2
NEG = -1e30

_X1_OFF = 0
_X2_OFF = 64
_AUG_W = 6

_C_Q = 0
_C_KV = _C_Q + MLA_Q_LORA
_C_KPE = _C_KV + MLA_KV_LORA
_C_QF = _C_KPE + LANES
_C_KF = _C_QF + FOX_HEADS * FOX_DIM
_C_VF = _C_KF + FOX_HEADS * FOX_DIM
_C_F = _C_VF + FOX_HEADS * FOX_DIM
EVEN_N = _C_F + LANES


def _cparams(sem):
    return pltpu.CompilerParams(dimension_semantics=sem, vmem_limit_bytes=VMEM_LIMIT)


def _rms_scale(x):
    return lax.rsqrt(jnp.mean(x * x, axis=-1, keepdims=True) + EPS)


def _split3(c):
    hi = c.astype(jnp.bfloat16)
    r1 = c - hi.astype(jnp.float32)
    mid = r1.astype(jnp.bfloat16)
    r2 = r1 - mid.astype(jnp.float32)
    lo = r2.astype(jnp.bfloat16)
    return hi, mid, lo


def _even_proj_kernel(tiles_per_seq, x_ref, g_ref, win_ref, lnq_ref, wuq_ref, lnkv_ref, wukv_ref,
                      bf_ref, cos_ref, sin_ref, tri_ref, pq_ref, pk_ref, oq_ref, ok_ref,
                      q_ref, k_ref, vt_ref, carry_ref):
    i = pl.program_id(0)
    x = x_ref[...]
    xn = (x * _rms_scale(x) * g_ref[...]).astype(jnp.bfloat16)
    z = jnp.dot(xn, win_ref[...], preferred_element_type=jnp.float32)
    cos = cos_ref[...]
    sin = sin_ref[...]

    def rope_slab(s):
        return s * cos + pltpu.roll(s, 64, axis=1) * sin

    c_q = z[:, _C_Q:_C_Q + MLA_Q_LORA]
    cqn = (c_q * _rms_scale(c_q) * lnq_ref[...]).astype(jnp.bfloat16)
    q = jnp.dot(cqn, wuq_ref[...], preferred_element_type=jnp.float32)
    scale_a = (MLA_NOPE + MLA_ROPE) ** -0.5 * LOG2E
    for p in range(N_PAIRS):
        lo = p * PAIR_W
        q_ref[0, :, lo:lo + LANES] = (q[:, lo:lo + LANES] * scale_a).astype(jnp.bfloat16)
        q_ref[0, :, lo + LANES:lo + PAIR_W] = (
            rope_slab(q[:, lo + LANES:lo + PAIR_W]) * scale_a).astype(jnp.bfloat16)

    c_kv = z[:, _C_KV:_C_KV + MLA_KV_LORA]
    ckn = (c_kv * _rms_scale(c_kv) * lnkv_ref[...]).astype(jnp.bfloat16)
    kv = jnp.dot(ckn, wukv_ref[...], preferred_element_type=jnp.float32)
    kpe = rope_slab(z[:, _C_KPE:_C_KPE + LANES]).astype(jnp.bfloat16)
    nk = MLA_HEADS * MLA_NOPE
    for p in range(N_PAIRS):
        lo = p * PAIR_W
        k_ref[0, :, lo:lo + LANES] = kv[:, p * LANES:(p + 1) * LANES].astype(jnp.bfloat16)
        k_ref[0, :, lo + LANES:lo + PAIR_W] = kpe
    _store_values_transposed(vt_ref, 0, kv[:, nk:nk + MLA_HEADS * MLA_V])

    lane = lax.broadcasted_iota(jnp.int32, (1, LANES), 1)
    fz = z[:, _C_F:_C_F + LANES] + bf_ref[...]
    log_f = -(jnp.maximum(-fz, 0.0) + jnp.log1p(jnp.exp(-jnp.abs(fz))))
    log_f = jnp.where(lane < FOX_HEADS, log_f, 0.0)

    @pl.when(i % tiles_per_seq == 0)
    def _():
        carry_ref[...] = jnp.zeros_like(carry_ref)

    hi, mid, lo3 = _split3(log_f)
    tri = tri_ref[...]
    cum = (jnp.dot(tri, hi, preferred_element_type=jnp.float32)
           + jnp.dot(tri, mid, preferred_element_type=jnp.float32)
           + jnp.dot(tri, lo3, preferred_element_type=jnp.float32)) + carry_ref[...]
    tm = cum.shape[0]
    carry_ref[...] = cum[tm - 1:tm, :]

    parts = jnp.concatenate(_split3(cum * LOG2E), axis=1)
    aug_q = jnp.dot(parts, pq_ref[...], preferred_element_type=jnp.float32) + oq_ref[...]
    aug_k = jnp.dot(parts, pk_ref[...], preferred_element_type=jnp.float32) + ok_ref[...]
    scale_b = FOX_DIM ** -0.5 * LOG2E
    for p in range(N_PAIRS):
        lo = p * PAIR_W
        q_ref[1, :, lo:lo + LANES] = (
            z[:, _C_QF + p * LANES:_C_QF + (p + 1) * LANES] * scale_b).astype(jnp.bfloat16)
        q_ref[1, :, lo + LANES:lo + PAIR_W] = aug_q[:, p * LANES:(p + 1) * LANES].astype(jnp.bfloat16)
        k_ref[1, :, lo:lo + LANES] = z[:, _C_KF + p * LANES:_C_KF + (p + 1) * LANES].astype(jnp.bfloat16)
        k_ref[1, :, lo + LANES:lo + PAIR_W] = aug_k[:, p * LANES:(p + 1) * LANES].astype(jnp.bfloat16)
    _store_values_transposed(vt_ref, 1, z[:, _C_VF:_C_VF + FOX_HEADS * FOX_DIM])


def _store_values_transposed(vt_ref, g, v):
    vt = v.T
    ones = jnp.ones((VT_ROWS - MLA_V, vt.shape[1]), vt_ref.dtype)
    for h in range(MLA_HEADS):
        vt_ref[g, h * VT_ROWS:h * VT_ROWS + MLA_V, :] = vt[h * MLA_V:(h + 1) * MLA_V].astype(vt_ref.dtype)
        vt_ref[g, h * VT_ROWS + MLA_V:(h + 1) * VT_ROWS, :] = ones


def _even_weights(w_in, w_uq, w_ukv, b_f):
    d = w_in.shape[0]
    zeros = lambda n: jnp.zeros((d, n), w_in.dtype)
    o_cq, o_ckv = 0, MLA_Q_LORA
    o_kpe = o_ckv + MLA_KV_LORA
    o_qf = o_kpe + MLA_ROPE
    o_kf = o_qf + FOX_HEADS * FOX_DIM
    o_vf = o_kf + FOX_HEADS * FOX_DIM
    o_f = o_vf + FOX_HEADS * FOX_DIM
    kpe1 = w_in[:, o_kpe:o_kpe + ROPE_H]
    kpe2 = w_in[:, o_kpe + ROPE_H:o_kpe + MLA_ROPE]
    kpe_slab = jnp.concatenate([kpe1, kpe1, zeros(32), kpe2, kpe2, zeros(32)], axis=1)
    f_slab = jnp.concatenate([w_in[:, o_f:o_f + FOX_HEADS], zeros(LANES - FOX_HEADS)], axis=1)
    w_in_p = jnp.concatenate([w_in[:, o_cq:o_kpe], kpe_slab, w_in[:, o_qf:o_f], f_slab], axis=1)

    dq = w_uq.shape[0]
    zq = lambda n: jnp.zeros((dq, n), w_uq.dtype)
    hd = MLA_NOPE + MLA_ROPE
    blocks = []
    for p in range(N_PAIRS):
        h0, h1 = 2 * p, 2 * p + 1
        nope = lambda h: w_uq[:, h * hd:h * hd + MLA_NOPE]
        r1 = lambda h: w_uq[:, h * hd + MLA_NOPE:h * hd + MLA_NOPE + ROPE_H]
        r2 = lambda h: w_uq[:, h * hd + MLA_NOPE + ROPE_H:(h + 1) * hd]
        blocks += [nope(h0), nope(h1), r1(h0), r1(h1), zq(32), r2(h0), r2(h1), zq(32)]
    w_uq_p = jnp.concatenate(blocks, axis=1)

    kvd = MLA_NOPE + MLA_V
    k_cols = [w_ukv[:, h * kvd:h * kvd + MLA_NOPE] for h in range(MLA_HEADS)]
    v_cols = [w_ukv[:, h * kvd + MLA_NOPE:(h + 1) * kvd] for h in range(MLA_HEADS)]
    w_ukv_p = jnp.concatenate(k_cols + v_cols, axis=1)
    b_f_p = jnp.concatenate([b_f, jnp.zeros((LANES - FOX_HEADS,), b_f.dtype)])[None, :]
    bf16 = jnp.bfloat16
    return w_in_p.astype(bf16), w_uq_p.astype(bf16), w_ukv_p.astype(bf16), b_f_p


def _aug_placement():
    pq = np.zeros((3 * LANES, N_PAIRS * LANES), np.float32)
    pk = np.zeros((3 * LANES, N_PAIRS * LANES), np.float32)
    oq = np.zeros((1, N_PAIRS * LANES), np.float32)
    ok = np.zeros((1, N_PAIRS * LANES), np.float32)
    for p in range(N_PAIRS):
        for j in range(2):
            h = 2 * p + j
            base = p * LANES + _X1_OFF + j * ROPE_H
            for t in range(3):
                pq[t * LANES + h, base + 3 + t] = 1.0
                pk[t * LANES + h, base + t] = -1.0
                oq[0, base + t] = 1.0
                ok[0, base + 3 + t] = 1.0
    return (jnp.asarray(pq, jnp.bfloat16), jnp.asarray(pk, jnp.bfloat16),
            jnp.asarray(oq), jnp.asarray(ok))


def _mla_rope_tables(s):
    inv = ROPE_BASE ** (-jnp.arange(ROPE_H, dtype=jnp.float32) / ROPE_H)
    ang = jnp.arange(s, dtype=jnp.float32)[:, None] * inv[None, :]
    c, sn = jnp.cos(ang), jnp.sin(ang)
    z = jnp.zeros((s, 32), jnp.float32)
    cos = jnp.concatenate([c, c, z, c, c, z], axis=1)
    sin = jnp.concatenate([-sn, -sn, z, sn, sn, z], axis=1)
    return cos, sin


def _even_proj(x2d, seq, g, w_in_p, ln_q, w_uq_p, ln_kv, w_ukv_p, b_f_p):
    t = x2d.shape[0]
    tm = TM_OUT
    tiles_per_seq = seq // tm
    cos, sin = _mla_rope_tables(seq)
    tri = jnp.asarray(np.tril(np.ones((tm, tm), np.float32)), jnp.bfloat16)
    pq, pk, oq, ok = _aug_placement()
    const = lambda shape: pl.BlockSpec(shape, lambda i: (0,) * len(shape))
    return pl.pallas_call(
        functools.partial(_even_proj_kernel, tiles_per_seq),
        grid=(t // tm,),
        in_specs=[
            pl.BlockSpec((tm, D_MODEL), lambda i: (i, 0)),
            const((1, D_MODEL)), const((D_MODEL, EVEN_N)),
            const((1, MLA_Q_LORA)), const((MLA_Q_LORA, N_PAIRS * PAIR_W)),
            const((1, MLA_KV_LORA)), const((MLA_KV_LORA, 2 * MLA_HEADS * MLA_NOPE)),
            const((1, LANES)),
            pl.BlockSpec((tm, LANES), lambda i: (i % tiles_per_seq, 0)),
            pl.BlockSpec((tm, LANES), lambda i: (i % tiles_per_seq, 0)),
            const((tm, tm)), const(pq.shape), const(pk.shape), const(oq.shape), const(ok.shape),
        ],
        out_specs=[
            pl.BlockSpec((2, tm, N_PAIRS * PAIR_W), lambda i: (0, i, 0)),
            pl.BlockSpec((2, tm, N_PAIRS * PAIR_W), lambda i: (0, i, 0)),
            pl.BlockSpec((2, None, MLA_HEADS * VT_ROWS, tm),
                         lambda i: (0, i // tiles_per_seq, 0, i % tiles_per_seq)),
        ],
        out_shape=[
            jax.ShapeDtypeStruct((2, t, N_PAIRS * PAIR_W), jnp.bfloat16),
            jax.ShapeDtypeStruct((2, t, N_PAIRS * PAIR_W), jnp.bfloat16),
            jax.ShapeDtypeStruct((2, t // seq, MLA_HEADS * VT_ROWS, seq), jnp.bfloat16),
        ],
        scratch_shapes=[pltpu.VMEM((1, LANES), jnp.float32)],
        compiler_params=_cparams(("arbitrary",)),
        name="even_proj",
    )(x2d, g, w_in_p, ln_q, w_uq_p, ln_kv, w_ukv_p, b_f_p, cos, sin, tri, pq, pk, oq, ok)


def _attn_kernel(group, tk, q_ref, k_ref, vt_ref, o_ref, qcat_ref, s0_ref, s1_ref, m0_ref, m1_ref,
                 a0_ref, a1_ref, acc_ref):
    i = pl.program_id(2)
    tq = q_ref.shape[0]
    s_bufs, m_bufs, a_bufs = (s0_ref, s1_ref), (m0_ref, m1_ref), (a0_ref, a1_ref)

    lane = lax.broadcasted_iota(jnp.int32, (1, PAIR_W), 1)

    def head_mask(j):
        a = (lane >= j * MLA_NOPE) & (lane < (j + 1) * MLA_NOPE)
        b = (lane >= LANES + _X1_OFF + j * ROPE_H) & (lane < LANES + _X1_OFF + (j + 1) * ROPE_H)
        c = (lane >= LANES + _X2_OFF + j * ROPE_H) & (lane < LANES + _X2_OFF + (j + 1) * ROPE_H)
        return a | b | c

    q = q_ref[...]
    zero = jnp.zeros_like(q)
    qcat_ref[0:tq, :] = jnp.where(head_mask(0), q, zero)
    qcat_ref[tq:2 * tq, :] = jnp.where(head_mask(1), q, zero)
    acc_ref[...] = jnp.zeros_like(acc_ref)

    q0 = i * tq
    n_chunks = q0 // tk + 1

    def scores(c, m_run, par, masked):
        kstart = pl.multiple_of(c * tk, tk)
        s = lax.dot_general(k_ref[pl.ds(kstart, tk), :], qcat_ref[...], (((1,), (1,)), ((), ())),
                            preferred_element_type=jnp.float32)
        if masked:
            col = lax.broadcasted_iota(jnp.int32, (1, 2 * tq), 1)
            qpos = q0 + jnp.where(col >= tq, col - tq, col)
            qlim = (qpos | (CHUNK - 1)) if group == 0 else qpos
            kpos = kstart + lax.broadcasted_iota(jnp.int32, (tk, 1), 0)
            s = jnp.where(kpos <= qlim, s, NEG)
        s_bufs[par][...] = s
        m_new = jnp.maximum(m_run, jnp.max(s, axis=0, keepdims=True))
        m_bufs[par][...] = m_new
        a_bufs[par][...] = jnp.exp2(m_run - m_new)

    def accumulate(c, par):
        kstart = pl.multiple_of(c * tk, tk)
        alpha = a_bufs[par][...]
        p = jnp.exp2(s_bufs[par][...] - m_bufs[par][...]).astype(jnp.bfloat16)
        for h in range(2):
            cols = slice(h * tq, (h + 1) * tq)
            acc_ref[h] = alpha[:, cols] * acc_ref[h] + jnp.dot(
                vt_ref[h * VT_ROWS:(h + 1) * VT_ROWS, pl.ds(kstart, tk)], p[:, cols],
                preferred_element_type=jnp.float32)

    def stage(c, par, masked):
        scores(c + 1, m_bufs[par][...], 1 - par, masked)
        accumulate(c, par)

    m_init = jnp.full((1, 2 * tq), NEG, jnp.float32)

    @pl.when(n_chunks == 1)
    def _():
        scores(0, m_init, 0, True)
        accumulate(0, 0)

    @pl.when(n_chunks > 1)
    def _():
        scores(0, m_init, 0, False)

    n_pairs = jnp.maximum(n_chunks - 2, 0) // 2

    def body(j, carry):
        for u in range(4):
            stage(4 * j + u, u % 2, False)
        return carry

    lax.fori_loop(0, n_pairs // 2, body, 0)

    @pl.when(n_pairs % 2 == 1)
    def _():
        stage(2 * n_pairs - 2, 0, False)
        stage(2 * n_pairs - 1, 1, False)
    c0 = 2 * n_pairs
    left = n_chunks - 1 - c0

    @pl.when((n_chunks > 1) & (left == 1))
    def _():
        stage(c0, 0, True)
        accumulate(c0 + 1, 1)

    @pl.when((n_chunks > 1) & (left == 2))
    def _():
        stage(c0, 0, False)
        stage(c0 + 1, 1, True)
        accumulate(c0 + 2, 0)

    for h in range(2):
        acc = acc_ref[h]
        o_ref[h * MLA_V:(h + 1) * MLA_V, :] = (acc[0:MLA_V] / acc[MLA_V:MLA_V + 1]).astype(o_ref.dtype)


def _attention(q, k, vt, batch, seq, group, tq, tk):
    nq = seq // tq
    f32 = jnp.float32
    return pl.pallas_call(
        functools.partial(_attn_kernel, group, tk),
        grid=(batch, N_PAIRS, nq),
        in_specs=[
            pl.BlockSpec((None, tq, PAIR_W), lambda b, p, i: (group, b * nq + i, p)),
            pl.BlockSpec((None, seq, PAIR_W), lambda b, p, i: (group, b, p)),
            pl.BlockSpec((None, None, 2 * VT_ROWS, seq), lambda b, p, i: (group, b, p, 0)),
        ],
        out_specs=pl.BlockSpec((None, LANES, tq), lambda b, p, i: (b, p, i)),
        out_shape=jax.ShapeDtypeStruct((batch, N_PAIRS * LANES, seq), jnp.bfloat16),
        scratch_shapes=[pltpu.VMEM((2 * tq, PAIR_W), jnp.bfloat16),
                        pltpu.VMEM((tk, 2 * tq), f32), pltpu.VMEM((tk, 2 * tq), f32),
                        pltpu.VMEM((1, 2 * tq), f32), pltpu.VMEM((1, 2 * tq), f32),
                        pltpu.VMEM((1, 2 * tq), f32), pltpu.VMEM((1, 2 * tq), f32),
                        pltpu.VMEM((2, VT_ROWS, tq), f32)],
        compiler_params=_cparams(("arbitrary",) * 3),
        name="attention_g%d" % group,
    )(q, k, vt)


def _router_t(lt):
    row = lax.broadcasted_iota(jnp.int32, lt.shape, 0)
    big = jnp.int32(LANES)
    ninf = -jnp.inf
    gl = jnp.where(row < N_GROUPS, lt, ninf)
    gmax = jnp.max(gl, axis=0, keepdims=True)
    gsel = jnp.min(jnp.where(gl == gmax, row, big), axis=0, keepdims=True)
    gsum = jnp.sum(jnp.where(row < N_GROUPS, jnp.exp(lt - gmax), 0.0), axis=0, keepdims=True)
    p_g = 1.0 / gsum
    lo = N_GROUPS + EXPERTS_PER_GROUP * gsel
    e = jnp.where((row >= lo) & (row < lo + EXPERTS_PER_GROUP), lt, ninf)
    m1 = jnp.max(e, axis=0, keepdims=True)
    i1 = jnp.min(jnp.where(e == m1, row, big), axis=0, keepdims=True)
    e2 = jnp.where(row == i1, ninf, e)
    m2 = jnp.max(e2, axis=0, keepdims=True)
    i2 = jnp.min(jnp.where(e2 == m2, row, big), axis=0, keepdims=True)
    a2 = jnp.exp(m2 - m1)
    den = 1.0 + a2
    g1 = p_g / den
    g2 = p_g * a2 / den
    f = lambda v: v.astype(jnp.float32)
    return jnp.where(row == 0, f(i1 - N_GROUPS),
                     jnp.where(row == 1, f(i2 - N_GROUPS),
                               jnp.where(row == 2, g1, jnp.where(row == 3, g2, 0.0))))


def _out_proj_kernel(n_o, transposed, x_ref, *refs):
    o_refs = refs[:n_o]
    w_ref, ln_ref, wr_ref, br_ref, xo_ref, xg_ref, route_ref = refs[n_o:]
    tm = x_ref.shape[0]
    hm = tm // OUT_SPLIT
    for part in range(OUT_SPLIT):
        rows = slice(part * hm, (part + 1) * hm)
        if transposed:
            y = x_ref[rows, :]
            nf = o_refs[0].shape[0]
            for g, o_ref in enumerate(o_refs):
                y = y + lax.dot_general(o_ref[:, rows], w_ref[g * nf:(g + 1) * nf, :],
                                        (((0,), (0,)), ((), ())), preferred_element_type=jnp.float32)
        else:
            y = x_ref[rows, :] + jnp.dot(o_refs[0][rows, :], w_ref[...], preferred_element_type=jnp.float32)
        xo_ref[rows, :] = y
        xn = y * _rms_scale(y) * ln_ref[...]
        hi = xn.astype(jnp.bfloat16)
        lo = (xn - hi.astype(jnp.float32)).astype(jnp.bfloat16)
        l2 = jnp.dot(jnp.concatenate([hi, lo], axis=1), wr_ref[...], preferred_element_type=jnp.float32)
        lt = (l2[:, :LANES] + l2[:, LANES:] + br_ref[...]).T
        route_ref[rows, :] = _router_t(lt).T
        for c in range(ROWS_PER_TOK):
            xg_ref[pl.ds(part * hm * ROWS_PER_TOK + c, hm, stride=ROWS_PER_TOK), :] = (
                xn[:, c * LANES:(c + 1) * LANES])


def _out_proj(x2d, o, w_out, ln, w_r, b_r, seq):
    t = x2d.shape[0]
    kdim = w_out.shape[0]
    tm = TM_OUT
    transposed = isinstance(o, tuple)
    tiles_per_seq = seq // tm
    if transposed:
        o_specs = [pl.BlockSpec((None, a.shape[1], tm), lambda i: (i // tiles_per_seq, 0, i % tiles_per_seq))
                   for a in o]
    else:
        o, o_specs = (o,), [pl.BlockSpec((tm, kdim), lambda i: (i, 0))]
    const = lambda shape: pl.BlockSpec(shape, lambda i: (0,) * len(shape))
    return pl.pallas_call(
        functools.partial(_out_proj_kernel, len(o), transposed),
        grid=(t // tm,),
        in_specs=[
            pl.BlockSpec((tm, D_MODEL), lambda i: (i, 0)),
            *o_specs,
            const((kdim, D_MODEL)), const((1, D_MODEL)), const((2 * D_MODEL, 2 * LANES)), const((1, LANES)),
        ],
        out_specs=[
            pl.BlockSpec((tm, D_MODEL), lambda i: (i, 0)),
            pl.BlockSpec((tm * ROWS_PER_TOK, LANES), lambda i: (i, 0)),
            pl.BlockSpec((tm, LANES), lambda i: (i, 0)),
        ],
        out_shape=[
            jax.ShapeDtypeStruct((t, D_MODEL), jnp.float32),
            jax.ShapeDtypeStruct((t * ROWS_PER_TOK, LANES), jnp.float32),
            jax.ShapeDtypeStruct((t, LANES), jnp.float32),
        ],
        compiler_params=_cparams(("arbitrary",)),
        name="out_proj_router",
    )(x2d, *o, w_out, ln, w_r, b_r)


def _router_weights(w_rg, b_rg, w_re, b_re):
    d = w_rg.shape[0]
    pad = LANES - N_GROUPS - N_EXPERTS
    w_r = jnp.concatenate([w_rg, w_re, jnp.zeros((d, pad), w_rg.dtype)], axis=1)
    b_r = jnp.concatenate([b_rg, b_re, jnp.zeros((pad,), b_rg.dtype)])[None, :]
    w_hi = w_r.astype(jnp.bfloat16)
    w_lo = (w_r - w_hi.astype(jnp.float32)).astype(jnp.bfloat16)
    half = jnp.concatenate([w_hi, w_lo], axis=1)
    return jnp.concatenate([half, half], axis=0), b_r


def _dispatch_plan(eid, t):
    a = t * TOP_K
    nb = a // TE + N_EXPERTS
    e_flat = eid.reshape(a)
    experts = jnp.arange(N_EXPERTS, dtype=jnp.int32)
    _, order = lax.sort((e_flat, jnp.arange(a, dtype=jnp.int32)), num_keys=1, is_stable=True)
    counts = jnp.sum((e_flat[:, None] == experts[None, :]).astype(jnp.int32), axis=0)
    start = jnp.cumsum(counts) - counts
    padded = ((counts + TE - 1) // TE) * TE
    pend = jnp.cumsum(padded)
    pstart = pend - padded
    blk0 = jnp.arange(nb, dtype=jnp.int32) * TE
    blk_e = jnp.minimum(jnp.sum((blk0[:, None] >= pend[None, :]).astype(jnp.int32), axis=1), N_EXPERTS - 1)
    blk_used = (blk0 < pend[-1]).astype(jnp.int32)
    onehot = (blk_e[:, None] == experts[None, :]).astype(jnp.int32)
    per_blk = lambda v: jnp.sum(onehot * v[None, :], axis=1)[:, None]
    r = jnp.arange(TE, dtype=jnp.int32)[None, :]
    off = blk0[:, None] + r - per_blk(pstart)
    valid = (off < per_blk(counts)) & (blk_used[:, None] > 0)
    src = jnp.where(valid, per_blk(start) + off, 0)
    assign = order[src]
    src_tok = jnp.where(valid, assign // TOP_K, 0).astype(jnp.int32)
    dst_row = jnp.where(valid, assign, a + r).astype(jnp.int32)
    nxt = jnp.minimum(jnp.arange(nb) + 1, nb - 1)
    return blk_e, blk_used, jnp.concatenate([src_tok, src_tok[nxt], dst_row], axis=1)[:, None, :]


def _moe_kernel(blk_e_ref, blk_used_ref, idx_ref, xg_ref, wg_ref, wu_ref, wd_ref,
                ys_ref, xbuf, ybuf, wg_bf, wu_bf, wd_bf, gsem, ssem):
    i = pl.program_id(0)
    nb = pl.num_programs(0)
    used = blk_used_ref[i] > 0
    slot = i % 2
    block_rows = TE * ROWS_PER_TOK

    def start_gather(first, s):
        for r in range(TE):
            tok = idx_ref[0, 0, first + r]
            pltpu.make_async_copy(
                xg_ref.at[pl.ds(pl.multiple_of(tok * ROWS_PER_TOK, ROWS_PER_TOK), ROWS_PER_TOK), :],
                xbuf.at[s, pl.ds(r * ROWS_PER_TOK, ROWS_PER_TOK), :],
                gsem.at[s]).start(priority=r % 2)

    def wait_gather(s):
        pltpu.make_async_copy(xg_ref.at[pl.ds(0, block_rows), :], xbuf.at[s], gsem.at[s]).wait()

    def start_scatter():
        for r in range(TE):
            row = idx_ref[0, 0, 2 * TE + r]
            pltpu.make_async_copy(
                ybuf.at[pl.ds(r * ROWS_PER_TOK, ROWS_PER_TOK), :],
                ys_ref.at[pl.ds(pl.multiple_of(row * ROWS_PER_TOK, ROWS_PER_TOK), ROWS_PER_TOK), :],
                ssem.at[0]).start(priority=r % 2)

    def wait_scatter():
        pltpu.make_async_copy(ybuf, ys_ref.at[pl.ds(0, block_rows), :], ssem.at[0]).wait()

    @pl.when(i == 0)
    def _():
        ybuf[...] = jnp.zeros_like(ybuf)
        pltpu.make_async_copy(
            ybuf, ys_ref.at[pl.ds(ys_ref.shape[0] - block_rows, block_rows), :], ssem.at[0]).start()
        start_gather(0, 0)

    @pl.when(used)
    def _():
        start_gather(TE, 1 - slot)
        prev_e = blk_e_ref[jnp.maximum(i - 1, 0)]

        @pl.when((i == 0) | (blk_e_ref[i] != prev_e))
        def _():
            wg_bf[...] = wg_ref[...].astype(jnp.bfloat16)
            wu_bf[...] = wu_ref[...].astype(jnp.bfloat16)
            wd_bf[...] = wd_ref[...].astype(jnp.bfloat16)

        wait_gather(slot)
        xb = xbuf.at[slot]
        x = jnp.concatenate([xb[pl.ds(c, TE, stride=ROWS_PER_TOK), :] for c in range(ROWS_PER_TOK)],
                            axis=1).astype(jnp.bfloat16)
        hg = jnp.dot(x, wg_bf[...], preferred_element_type=jnp.float32)
        hu = jnp.dot(x, wu_bf[...], preferred_element_type=jnp.float32)
        hid = (hg * jax.nn.sigmoid(hg) * hu).astype(jnp.bfloat16)
        y = jnp.dot(hid, wd_bf[...], preferred_element_type=jnp.float32)

        wait_scatter()
        for c in range(ROWS_PER_TOK):
            ybuf[pl.ds(c, TE, stride=ROWS_PER_TOK), :] = y[:, c * LANES:(c + 1) * LANES]
        start_scatter()

        next_used = (i + 1 < nb) & (blk_used_ref[jnp.minimum(i + 1, nb - 1)] > 0)

        @pl.when(jnp.logical_not(next_used))
        def _():
            wait_gather(1 - slot)
            wait_scatter()


def _moe(xg, eid, w_gate, w_up, w_down, layer, t):
    a = t * TOP_K
    nb = a // TE + N_EXPERTS
    blk_e, blk_used, idx = _dispatch_plan(eid, t)
    grid_spec = pltpu.PrefetchScalarGridSpec(
        num_scalar_prefetch=2,
        grid=(nb,),
        in_specs=[
            pl.BlockSpec((1, 1, 3 * TE), lambda i, be, bu: (i, 0, 0), memory_space=pltpu.SMEM),
            pl.BlockSpec(memory_space=pl.ANY),
            pl.BlockSpec((None, None, D_MODEL, D_EXPERT), lambda i, be, bu: (layer, be[i], 0, 0)),
            pl.BlockSpec((None, None, D_MODEL, D_EXPERT), lambda i, be, bu: (layer, be[i], 0, 0)),
            pl.BlockSpec((None, None, D_EXPERT, D_MODEL), lambda i, be, bu: (layer, be[i], 0, 0)),
        ],
        out_specs=pl.BlockSpec(memory_space=pl.ANY),
        scratch_shapes=[
            pltpu.VMEM((2, TE * ROWS_PER_TOK, LANES), jnp.float32),
            pltpu.VMEM((TE * ROWS_PER_TOK, LANES), jnp.float32),
            pltpu.VMEM((D_MODEL, D_EXPERT), jnp.bfloat16),
            pltpu.VMEM((D_MODEL, D_EXPERT), jnp.bfloat16),
            pltpu.VMEM((D_EXPERT, D_MODEL), jnp.bfloat16),
            pltpu.SemaphoreType.DMA((2,)),
            pltpu.SemaphoreType.DMA((1,)),
        ],
    )
    return pl.pallas_call(
        _moe_kernel,
        grid_spec=grid_spec,
        out_shape=jax.ShapeDtypeStruct(((a + TE) * ROWS_PER_TOK, LANES), jnp.float32),
        compiler_params=_cparams(("arbitrary",)),
        name="moe_experts",
    )(blk_e, blk_used, idx, xg, w_gate, w_up, w_down)


def _combine(x, ys_ref, route_ref, tm):
    stride = TOP_K * ROWS_PER_TOK
    y0 = jnp.concatenate([ys_ref[pl.ds(c, tm, stride=stride), :] for c in range(ROWS_PER_TOK)], axis=1)
    y1 = jnp.concatenate([ys_ref[pl.ds(ROWS_PER_TOK + c, tm, stride=stride), :]
                          for c in range(ROWS_PER_TOK)], axis=1)
    route = route_ref[...]
    return x + route[:, 2:3] * y0 + route[:, 3:4] * y1


def _odd_proj_kernel(x_ref, ys_ref, route_ref, g_ref, w_ref, cos_ref, sin_ref,
                     xo_ref, q_ref, k_ref, v_ref, gate_ref):
    tm = x_ref.shape[0]
    x = _combine(x_ref[...], ys_ref, route_ref, tm)
    xo_ref[...] = x
    xn = (x * _rms_scale(x) * g_ref[...]).astype(jnp.bfloat16)
    cos = cos_ref[...]
    sin = sin_ref[...]
    nqk = RET_HEADS * RET_QK
    half = RET_QK // 2

    def rope_store(z, dst, scale):
        for h in range(RET_HEADS):
            x1 = z[:, h * RET_QK:h * RET_QK + half]
            x2 = z[:, h * RET_QK + half:(h + 1) * RET_QK]
            dst[:, h * RET_QK:h * RET_QK + half] = ((x1 * cos - x2 * sin) * scale).astype(dst.dtype)
            dst[:, h * RET_QK + half:(h + 1) * RET_QK] = ((x1 * sin + x2 * cos) * scale).astype(dst.dtype)

    zq = jnp.dot(xn, w_ref[:, 0:nqk], preferred_element_type=jnp.float32)
    rope_store(zq, q_ref, 1.0)
    zk = jnp.dot(xn, w_ref[:, nqk:2 * nqk], preferred_element_type=jnp.float32)
    rope_store(zk, k_ref, RET_QK ** -0.5)
    nv = RET_HEADS * RET_V
    zv = jnp.dot(xn, w_ref[:, 2 * nqk:2 * nqk + nv], preferred_element_type=jnp.float32)
    v_ref[...] = zv.astype(v_ref.dtype)
    zg = jnp.dot(xn, w_ref[:, 2 * nqk + nv:2 * nqk + 2 * nv], preferred_element_type=jnp.float32)
    gate_ref[...] = (zg * jax.nn.sigmoid(zg)).astype(gate_ref.dtype)


def _ret_rope_tables(s):
    half = RET_QK // 2
    inv = ROPE_BASE ** (-jnp.arange(half, dtype=jnp.float32) / half)
    ang = jnp.arange(s, dtype=jnp.float32)[:, None] * inv[None, :]
    return jnp.cos(ang), jnp.sin(ang)


def _odd_proj(x2d, ys, route, seq, g, w_in):
    t = x2d.shape[0]
    tm = TM_PROJ
    tiles_per_seq = seq // tm
    cos, sin = _ret_rope_tables(seq)
    nqk = RET_HEADS * RET_QK
    nv = RET_HEADS * RET_V
    n_in = w_in.shape[1]
    const = lambda shape: pl.BlockSpec(shape, lambda i: (0,) * len(shape))
    tok = lambda n: pl.BlockSpec((tm, n), lambda i: (i, 0))
    return pl.pallas_call(
        _odd_proj_kernel,
        grid=(t // tm,),
        in_specs=[
            tok(D_MODEL),
            pl.BlockSpec((tm * TOP_K * ROWS_PER_TOK, LANES), lambda i: (i, 0)),
            tok(LANES),
            const((1, D_MODEL)), const((D_MODEL, n_in)),
            pl.BlockSpec((tm, LANES), lambda i: (i % tiles_per_seq, 0)),
            pl.BlockSpec((tm, LANES), lambda i: (i % tiles_per_seq, 0)),
        ],
        out_specs=[tok(D_MODEL), tok(nqk), tok(nqk), tok(nv), tok(nv)],
        out_shape=[
            jax.ShapeDtypeStruct((t, D_MODEL), jnp.float32),
            jax.ShapeDtypeStruct((t, nqk), jnp.bfloat16),
            jax.ShapeDtypeStruct((t, nqk), jnp.bfloat16),
            jax.ShapeDtypeStruct((t, nv), jnp.bfloat16),
            jax.ShapeDtypeStruct((t, nv), jnp.bfloat16),
        ],
        compiler_params=_cparams(("arbitrary",)),
        name="odd_proj",
    )(x2d, ys, route, g, w_in, cos, sin)


def _retention_kernel(q_ref, k_ref, v_ref, gate_ref, dmask_ref, xi_ref, zeta_ref, gc_ref, o_ref, state_ref):
    n = pl.program_id(2)

    @pl.when(n == 0)
    def _():
        state_ref[...] = jnp.zeros_like(state_ref)

    xi = xi_ref[...][:, 0:1]
    zeta = zeta_ref[...][:, 0:1]
    g_c = gc_ref[...][:, 0:1]
    for j in range(q_ref.shape[0] // RET_C):
        rows = slice(j * RET_C, (j + 1) * RET_C)
        q = q_ref[rows, :]
        k = k_ref[rows, :]
        v = v_ref[rows, :]
        inner = lax.dot_general(q, k, (((1,), (1,)), ((), ())),
                                preferred_element_type=jnp.float32) * dmask_ref[...]
        o = jnp.dot(inner.astype(jnp.bfloat16), v, preferred_element_type=jnp.float32)
        state = state_ref[...]
        cross = jnp.dot(q, state.astype(jnp.bfloat16), preferred_element_type=jnp.float32)
        o = o + cross * xi
        kz = (k.astype(jnp.float32) * zeta).astype(jnp.bfloat16)
        upd = lax.dot_general(kz, v, (((0,), (0,)), ((), ())), preferred_element_type=jnp.float32)
        state_ref[...] = g_c * state + upd
        o = o * _rms_scale(o)
        o_ref[rows, :] = (o * gate_ref[rows, :].astype(jnp.float32)).astype(o_ref.dtype)


def _retention_tables():
    c = RET_C
    log_g = jnp.log(1.0 - jnp.exp2(-5.0 - jnp.arange(RET_HEADS, dtype=jnp.float32)))
    j = jnp.arange(c, dtype=jnp.float32)
    diff = j[:, None] - j[None, :]
    dmask = jnp.where(diff >= 0, jnp.exp(jnp.maximum(diff, 0.0)[None] * log_g[:, None, None]), 0.0)
    xi = jnp.exp((j[None, :] + 1.0) * log_g[:, None])
    zeta = jnp.exp((c - 1.0 - j)[None, :] * log_g[:, None])
    g_c = jnp.exp(c * log_g)
    bc = lambda a: jnp.broadcast_to(a[..., None], a.shape + (LANES,))
    return dmask, bc(xi), bc(zeta), bc(g_c[:, None])


def _retention(q, k, v, gate, batch, seq):
    nc = seq // RET_BLK
    dmask, xi, zeta, g_c = _retention_tables()
    row = lambda n: pl.BlockSpec((RET_BLK, n), lambda b, h, c: (b * nc + c, h))
    head = lambda shape: pl.BlockSpec((None,) + shape, lambda b, h, c: (h, 0, 0))
    return pl.pallas_call(
        _retention_kernel,
        grid=(batch, RET_HEADS, nc),
        in_specs=[row(RET_QK), row(RET_QK), row(RET_V), row(RET_V),
                  head((RET_C, RET_C)), head((RET_C, LANES)), head((RET_C, LANES)), head((1, LANES))],
        out_specs=row(RET_V),
        out_shape=jax.ShapeDtypeStruct((batch * seq, RET_HEADS * RET_V), jnp.bfloat16),
        scratch_shapes=[pltpu.VMEM((RET_QK, RET_V), jnp.float32)],
        compiler_params=_cparams(("arbitrary",) * 3),
        name="retention",
    )(q, k, v, gate, dmask, xi, zeta, g_c)


def _final_kernel(x_ref, ys_ref, route_ref, g_ref, o_ref):
    x = _combine(x_ref[...], ys_ref, route_ref, x_ref.shape[0])
    o_ref[...] = x * _rms_scale(x) * g_ref[...]


def _final(x2d, ys, route, g):
    t = x2d.shape[0]
    tm = TM_OUT
    return pl.pallas_call(
        _final_kernel,
        grid=(t // tm,),
        in_specs=[
            pl.BlockSpec((tm, D_MODEL), lambda i: (i, 0)),
            pl.BlockSpec((tm * TOP_K * ROWS_PER_TOK, LANES), lambda i: (i, 0)),
            pl.BlockSpec((tm, LANES), lambda i: (i, 0)),
            pl.BlockSpec((1, D_MODEL), lambda i: (0, 0)),
        ],
        out_specs=pl.BlockSpec((tm, D_MODEL), lambda i: (i, 0)),
        out_shape=jax.ShapeDtypeStruct((t, D_MODEL), jnp.float32),
        compiler_params=_cparams(("arbitrary",)),
        name="final_norm",
    )(x2d, ys, route, g)


def kernel(x, ln_mix_e, w_in_e, ln_q_e, w_uq_e, ln_kv_e, w_ukv_e, b_f_e, w_out_e, ln_mix_o, w_in_o,
           w_out_o, ln_ffn, w_rg, b_rg, w_re, b_re, w_gate, w_up, w_down, ln_f):
    batch, seq, d = x.shape
    t = batch * seq
    assert d == D_MODEL and seq % TK == 0 and seq % TQ == 0 and seq % RET_BLK == 0
    assert seq % TM_PROJ == 0 and t % TM_OUT == 0 and t % TE == 0
    bf16 = jnp.bfloat16
    x2d = x.reshape(t, d)

    w_in_p, w_uq_p, w_ukv_p, b_f_p = _even_weights(w_in_e[0], w_uq_e[0], w_ukv_e[0], b_f_e[0])
    q, k, vt = _even_proj(x2d, seq, ln_mix_e[0][None, :], w_in_p, ln_q_e[0][None, :], w_uq_p,
                         ln_kv_e[0][None, :], w_ukv_p, b_f_p)
    o_t = (_attention(q, k, vt, batch, seq, 0, TQ, TK), _attention(q, k, vt, batch, seq, 1, TQ, TK))
    w_r, b_r = _router_weights(w_rg[0], b_rg[0], w_re[0], b_re[0])
    x2d, xg, route = _out_proj(x2d, o_t, w_out_e[0].astype(bf16), ln_ffn[0][None, :], w_r, b_r, seq)
    ys = _moe(xg, route[:, 0:TOP_K].astype(jnp.int32), w_gate, w_up, w_down, 0, t)

    x2d, rq, rk, rv, rg = _odd_proj(x2d, ys, route, seq, ln_mix_o[0][None, :], w_in_o[0].astype(bf16))
    og = _retention(rq, rk, rv, rg, batch, seq)
    w_r, b_r = _router_weights(w_rg[1], b_rg[1], w_re[1], b_re[1])
    x2d, xg, route = _out_proj(x2d, og, w_out_o[0].astype(bf16), ln_ffn[1][None, :], w_r, b_r, seq)
    ys = _moe(xg, route[:, 0:TOP_K].astype(jnp.int32), w_gate, w_up, w_down, 1, t)

    out = _final(x2d, ys, route, ln_f[None, :])
    return out.reshape(batch, seq, d)
```

```python
import functools
import math

import numpy as np
import jax
import jax.numpy as jnp
from jax import lax
from jax.experimental import pallas as pl
from jax.experimental.pallas import tpu as pltpu

D_MODEL = 1024
CHUNK = 64
ROPE_BASE = 10000.0
EPS = 1e-6
MLA_HEADS = 8
MLA_NOPE = 64
MLA_ROPE = 32
MLA_V = 64
MLA_Q_LORA = 256
MLA_KV_LORA = 128
FOX_HEADS = 8
FOX_DIM = 64
RET_HEADS = 4
RET_QK = 256
RET_V = 512
N_GROUPS = 4
EXPERTS_PER_GROUP = 8
N_EXPERTS = N_GROUPS * EXPERTS_PER_GROUP
TOP_K = 2
D_EXPERT = 512

LANES = 128
SUBLANES = 8
VMEM_LIMIT = 52 * 1024 * 1024

TM_PROJ = 256
TM_OUT = 512
OUT_SPLIT = 2
TQ = 1024
VT_ROWS = 80
TK = 1024
LOG2E = math.log2(math.e)
RET_C = 256
RET_BLK = 2048
TE = 256
ROWS_PER_TOK = D_MODEL // LANES

N_PAIRS = MLA_HEADS // 2
PAIR_W = 256
ROPE_H = MLA_ROPE // 2
NEG = -1e30

_X1_OFF = 0
_X2_OFF = 64
_AUG_W = 6

_C_Q = 0
_C_KV = _C_Q + MLA_Q_LORA
_C_KPE = _C_KV + MLA_KV_LORA
_C_QF = _C_KPE + LANES
_C_KF = _C_QF + FOX_HEADS * FOX_DIM
_C_VF = _C_KF + FOX_HEADS * FOX_DIM
_C_F = _C_VF + FOX_HEADS * FOX_DIM
EVEN_N = _C_F + LANES


def _cparams(sem):
    return pltpu.CompilerParams(dimension_semantics=sem, vmem_limit_bytes=VMEM_LIMIT)


def _rms_scale(x):
    return lax.rsqrt(jnp.mean(x * x, axis=-1, keepdims=True) + EPS)


def _split3(c):
    hi = c.astype(jnp.bfloat16)
    r1 = c - hi.astype(jnp.float32)
    mid = r1.astype(jnp.bfloat16)
    r2 = r1 - mid.astype(jnp.float32)
    lo = r2.astype(jnp.bfloat16)
    return hi, mid, lo


def _even_proj_kernel(tiles_per_seq, x_ref, g_ref, win_ref, lnq_ref, wuq_ref, lnkv_ref, wukv_ref,
                      bf_ref, cos_ref, sin_ref, tri_ref, pq_ref, pk_ref, oq_ref, ok_ref,
                      q_ref, k_ref, vt_ref, carry_ref):
    i = pl.program_id(0)
    x = x_ref[...]
    xn = (x * _rms_scale(x) * g_ref[...]).astype(jnp.bfloat16)
    z = jnp.dot(xn, win_ref[...], preferred_element_type=jnp.float32)
    cos = cos_ref[...]
    sin = sin_ref[...]

    def rope_slab(s):
        return s * cos + pltpu.roll(s, 64, axis=1) * sin

    c_q = z[:, _C_Q:_C_Q + MLA_Q_LORA]
    cqn = (c_q * _rms_scale(c_q) * lnq_ref[...]).astype(jnp.bfloat16)
    q = jnp.dot(cqn, wuq_ref[...], preferred_element_type=jnp.float32)
    scale_a = (MLA_NOPE + MLA_ROPE) ** -0.5 * LOG2E
    for p in range(N_PAIRS):
        lo = p * PAIR_W
        q_ref[0, :, lo:lo + LANES] = (q[:, lo:lo + LANES] * scale_a).astype(jnp.bfloat16)
        q_ref[0, :, lo + LANES:lo + PAIR_W] = (
            rope_slab(q[:, lo + LANES:lo + PAIR_W]) * scale_a).astype(jnp.bfloat16)

    c_kv = z[:, _C_KV:_C_KV + MLA_KV_LORA]
    ckn = (c_kv * _rms_scale(c_kv) * lnkv_ref[...]).astype(jnp.bfloat16)
    kv = jnp.dot(ckn, wukv_ref[...], preferred_element_type=jnp.float32)
    kpe = rope_slab(z[:, _C_KPE:_C_KPE + LANES]).astype(jnp.bfloat16)
    nk = MLA_HEADS * MLA_NOPE
    for p in range(N_PAIRS):
        lo = p * PAIR_W
        k_ref[0, :, lo:lo + LANES] = kv[:, p * LANES:(p + 1) * LANES].astype(jnp.bfloat16)
        k_ref[0, :, lo + LANES:lo + PAIR_W] = kpe
    _store_values_transposed(vt_ref, 0, kv[:, nk:nk + MLA_HEADS * MLA_V])

    lane = lax.broadcasted_iota(jnp.int32, (1, LANES), 1)
    fz = z[:, _C_F:_C_F + LANES] + bf_ref[...]
    log_f = -(jnp.maximum(-fz, 0.0) + jnp.log1p(jnp.exp(-jnp.abs(fz))))
    log_f = jnp.where(lane < FOX_HEADS, log_f, 0.0)

    @pl.when(i % tiles_per_seq == 0)
    def _():
        carry_ref[...] = jnp.zeros_like(carry_ref)

    hi, mid, lo3 = _split3(log_f)
    tri = tri_ref[...]
    cum = (jnp.dot(tri, hi, preferred_element_type=jnp.float32)
           + jnp.dot(tri, mid, preferred_element_type=jnp.float32)
           + jnp.dot(tri, lo3, preferred_element_type=jnp.float32)) + carry_ref[...]
    tm = cum.shape[0]
    carry_ref[...] = cum[tm - 1:tm, :]

    parts = jnp.concatenate(_split3(cum * LOG2E), axis=1)
    aug_q = jnp.dot(parts, pq_ref[...], preferred_element_type=jnp.float32) + oq_ref[...]
    aug_k = jnp.dot(parts, pk_ref[...], preferred_element_type=jnp.float32) + ok_ref[...]
    scale_b = FOX_DIM ** -0.5 * LOG2E
    for p in range(N_PAIRS):
        lo = p * PAIR_W
        q_ref[1, :, lo:lo + LANES] = (
            z[:, _C_QF + p * LANES:_C_QF + (p + 1) * LANES] * scale_b).astype(jnp.bfloat16)
        q_ref[1, :, lo + LANES:lo + PAIR_W] = aug_q[:, p * LANES:(p + 1) * LANES].astype(jnp.bfloat16)
        k_ref[1, :, lo:lo + LANES] = z[:, _C_KF + p * LANES:_C_KF + (p + 1) * LANES].astype(jnp.bfloat16)
        k_ref[1, :, lo + LANES:lo + PAIR_W] = aug_k[:, p * LANES:(p + 1) * LANES].astype(jnp.bfloat16)
    _store_values_transposed(vt_ref, 1, z[:, _C_VF:_C_VF + FOX_HEADS * FOX_DIM])


def _store_values_transposed(vt_ref, g, v):
    vt = v.T
    ones = jnp.ones((VT_ROWS - MLA_V, vt.shape[1]), vt_ref.dtype)
    for h in range(MLA_HEADS):
        vt_ref[g, h * VT_ROWS:h * VT_ROWS + MLA_V, :] = vt[h * MLA_V:(h + 1) * MLA_V].astype(vt_ref.dtype)
        vt_ref[g, h * VT_ROWS + MLA_V:(h + 1) * VT_ROWS, :] = ones


def _even_weights(w_in, w_uq, w_ukv, b_f):
    d = w_in.shape[0]
    zeros = lambda n: jnp.zeros((d, n), w_in.dtype)
    o_cq, o_ckv = 0, MLA_Q_LORA
    o_kpe = o_ckv + MLA_KV_LORA
    o_qf = o_kpe + MLA_ROPE
    o_kf = o_qf + FOX_HEADS * FOX_DIM
    o_vf = o_kf + FOX_HEADS * FOX_DIM
    o_f = o_vf + FOX_HEADS * FOX_DIM
    kpe1 = w_in[:, o_kpe:o_kpe + ROPE_H]
    kpe2 = w_in[:, o_kpe + ROPE_H:o_kpe + MLA_ROPE]
    kpe_slab = jnp.concatenate([kpe1, kpe1, zeros(32), kpe2, kpe2, zeros(32)], axis=1)
    f_slab = jnp.concatenate([w_in[:, o_f:o_f + FOX_HEADS], zeros(LANES - FOX_HEADS)], axis=1)
    w_in_p = jnp.concatenate([w_in[:, o_cq:o_kpe], kpe_slab, w_in[:, o_qf:o_f], f_slab], axis=1)

    dq = w_uq.shape[0]
    zq = lambda n: jnp.zeros((dq, n), w_uq.dtype)
    hd = MLA_NOPE + MLA_ROPE
    blocks = []
    for p in range(N_PAIRS):
        h0, h1 = 2 * p, 2 * p + 1
        nope = lambda h: w_uq[:, h * hd:h * hd + MLA_NOPE]
        r1 = lambda h: w_uq[:, h * hd + MLA_NOPE:h * hd + MLA_NOPE + ROPE_H]
        r2 = lambda h: w_uq[:, h * hd + MLA_NOPE + ROPE_H:(h + 1) * hd]
        blocks += [nope(h0), nope(h1), r1(h0), r1(h1), zq(32), r2(h0), r2(h1), zq(32)]
    w_uq_p = jnp.concatenate(blocks, axis=1)

    kvd = MLA_NOPE + MLA_V
    k_cols = [w_ukv[:, h * kvd:h * kvd + MLA_NOPE] for h in range(MLA_HEADS)]
    v_cols = [w_ukv[:, h * kvd + MLA_NOPE:(h + 1) * kvd] for h in range(MLA_HEADS)]
    w_ukv_p = jnp.concatenate(k_cols + v_cols, axis=1)
    b_f_p = jnp.concatenate([b_f, jnp.zeros((LANES - FOX_HEADS,), b_f.dtype)])[None, :]
    bf16 = jnp.bfloat16
    return w_in_p.astype(bf16), w_uq_p.astype(bf16), w_ukv_p.astype(bf16), b_f_p


def _aug_placement():
    pq = np.zeros((3 * LANES, N_PAIRS * LANES), np.float32)
    pk = np.zeros((3 * LANES, N_PAIRS * LANES), np.float32)
    oq = np.zeros((1, N_PAIRS * LANES), np.float32)
    ok = np.zeros((1, N_PAIRS * LANES), np.float32)
    for p in range(N_PAIRS):
        for j in range(2):
            h = 2 * p + j
            base = p * LANES + _X1_OFF + j * ROPE_H
            for t in range(3):
                pq[t * LANES + h, base + 3 + t] = 1.0
                pk[t * LANES + h, base + t] = -1.0
                oq[0, base + t] = 1.0
                ok[0, base + 3 + t] = 1.0
    return (jnp.asarray(pq, jnp.bfloat16), jnp.asarray(pk, jnp.bfloat16),
            jnp.asarray(oq), jnp.asarray(ok))


def _mla_rope_tables(s):
    inv = ROPE_BASE ** (-jnp.arange(ROPE_H, dtype=jnp.float32) / ROPE_H)
    ang = jnp.arange(s, dtype=jnp.float32)[:, None] * inv[None, :]
    c, sn = jnp.cos(ang), jnp.sin(ang)
    z = jnp.zeros((s, 32), jnp.float32)
    cos = jnp.concatenate([c, c, z, c, c, z], axis=1)
    sin = jnp.concatenate([-sn, -sn, z, sn, sn, z], axis=1)
    return cos, sin


def _even_proj(x2d, seq, g, w_in_p, ln_q, w_uq_p, ln_kv, w_ukv_p, b_f_p):
    t = x2d.shape[0]
    tm = TM_OUT
    tiles_per_seq = seq // tm
    cos, sin = _mla_rope_tables(seq)
    tri = jnp.asarray(np.tril(np.ones((tm, tm), np.float32)), jnp.bfloat16)
    pq, pk, oq, ok = _aug_placement()
    const = lambda shape: pl.BlockSpec(shape, lambda i: (0,) * len(shape))
    return pl.pallas_call(
        functools.partial(_even_proj_kernel, tiles_per_seq),
        grid=(t // tm,),
        in_specs=[
            pl.BlockSpec((tm, D_MODEL), lambda i: (i, 0)),
            const((1, D_MODEL)), const((D_MODEL, EVEN_N)),
            const((1, MLA_Q_LORA)), const((MLA_Q_LORA, N_PAIRS * PAIR_W)),
            const((1, MLA_KV_LORA)), const((MLA_KV_LORA, 2 * MLA_HEADS * MLA_NOPE)),
            const((1, LANES)),
            pl.BlockSpec((tm, LANES), lambda i: (i % tiles_per_seq, 0)),
            pl.BlockSpec((tm, LANES), lambda i: (i % tiles_per_seq, 0)),
            const((tm, tm)), const(pq.shape), const(pk.shape), const(oq.shape), const(ok.shape),
        ],
        out_specs=[
            pl.BlockSpec((2, tm, N_PAIRS * PAIR_W), lambda i: (0, i, 0)),
            pl.BlockSpec((2, tm, N_PAIRS * PAIR_W), lambda i: (0, i, 0)),
            pl.BlockSpec((2, None, MLA_HEADS * VT_ROWS, tm),
                         lambda i: (0, i // tiles_per_seq, 0, i % tiles_per_seq)),
        ],
        out_shape=[
            jax.ShapeDtypeStruct((2, t, N_PAIRS * PAIR_W), jnp.bfloat16),
            jax.ShapeDtypeStruct((2, t, N_PAIRS * PAIR_W), jnp.bfloat16),
            jax.ShapeDtypeStruct((2, t // seq, MLA_HEADS * VT_ROWS, seq), jnp.bfloat16),
        ],
        scratch_shapes=[pltpu.VMEM((1, LANES), jnp.float32)],
        compiler_params=_cparams(("arbitrary",)),
        name="even_proj",
    )(x2d, g, w_in_p, ln_q, w_uq_p, ln_kv, w_ukv_p, b_f_p, cos, sin, tri, pq, pk, oq, ok)


def _attn_kernel(group, tk, q_ref, k_ref, vt_ref, o_ref, qcat_ref, s0_ref, s1_ref, m0_ref, m1_ref,
                 a0_ref, a1_ref, acc_ref):
    i = pl.program_id(2)
    tq = q_ref.shape[0]
    s_bufs, m_bufs, a_bufs = (s0_ref, s1_ref), (m0_ref, m1_ref), (a0_ref, a1_ref)

    lane = lax.broadcasted_iota(jnp.int32, (1, PAIR_W), 1)

    def head_mask(j):
        a = (lane >= j * MLA_NOPE) & (lane < (j + 1) * MLA_NOPE)
        b = (lane >= LANES + _X1_OFF + j * ROPE_H) & (lane < LANES + _X1_OFF + (j + 1) * ROPE_H)
        c = (lane >= LANES + _X2_OFF + j * ROPE_H) & (lane < LANES + _X2_OFF + (j + 1) * ROPE_H)
        return a | b | c

    q = q_ref[...]
    zero = jnp.zeros_like(q)
    qcat_ref[0:tq, :] = jnp.where(head_mask(0), q, zero)
    qcat_ref[tq:2 * tq, :] = jnp.where(head_mask(1), q, zero)
    acc_ref[...] = jnp.zeros_like(acc_ref)

    q0 = i * tq
    n_chunks = q0 // tk + 1

    def scores(c, m_run, par, masked):
        kstart = pl.multiple_of(c * tk, tk)
        s = lax.dot_general(k_ref[pl.ds(kstart, tk), :], qcat_ref[...], (((1,), (1,)), ((), ())),
                            preferred_element_type=jnp.float32)
        if masked:
            col = lax.broadcasted_iota(jnp.int32, (1, 2 * tq), 1)
            qpos = q0 + jnp.where(col >= tq, col - tq, col)
            qlim = (qpos | (CHUNK - 1)) if group == 0 else qpos
            kpos = kstart + lax.broadcasted_iota(jnp.int32, (tk, 1), 0)
            s = jnp.where(kpos <= qlim, s, NEG)
        s_bufs[par][...] = s
        m_new = jnp.maximum(m_run, jnp.max(s, axis=0, keepdims=True))
        m_bufs[par][...] = m_new
        a_bufs[par][...] = jnp.exp2(m_run - m_new)

    def accumulate(c, par):
        kstart = pl.multiple_of(c * tk, tk)
        alpha = a_bufs[par][...]
        p = jnp.exp2(s_bufs[par][...] - m_bufs[par][...]).astype(jnp.bfloat16)
        for h in range(2):
            cols = slice(h * tq, (h + 1) * tq)
            acc_ref[h] = alpha[:, cols] * acc_ref[h] + jnp.dot(
                vt_ref[h * VT_ROWS:(h + 1) * VT_ROWS, pl.ds(kstart, tk)], p[:, cols],
                preferred_element_type=jnp.float32)

    def stage(c, par, masked):
        scores(c + 1, m_bufs[par][...], 1 - par, masked)
        accumulate(c, par)

    m_init = jnp.full((1, 2 * tq), NEG, jnp.float32)

    @pl.when(n_chunks == 1)
    def _():
        scores(0, m_init, 0, True)
        accumulate(0, 0)

    @pl.when(n_chunks > 1)
    def _():
        scores(0, m_init, 0, False)

    n_pairs = jnp.maximum(n_chunks - 2, 0) // 2

    def body(j, carry):
        for u in range(4):
            stage(4 * j + u, u % 2, False)
        return carry

    lax.fori_loop(0, n_pairs // 2, body, 0)

    @pl.when(n_pairs % 2 == 1)
    def _():
        stage(2 * n_pairs - 2, 0, False)
        stage(2 * n_pairs - 1, 1, False)
    c0 = 2 * n_pairs
    left = n_chunks - 1 - c0

    @pl.when((n_chunks > 1) & (left == 1))
    def _():
        stage(c0, 0, True)
        accumulate(c0 + 1, 1)

    @pl.when((n_chunks > 1) & (left == 2))
    def _():
        stage(c0, 0, False)
        stage(c0 + 1, 1, True)
        accumulate(c0 + 2, 0)

    for h in range(2):
        acc = acc_ref[h]
        o_ref[h * MLA_V:(h + 1) * MLA_V, :] = (acc[0:MLA_V] / acc[MLA_V:MLA_V + 1]).astype(o_ref.dtype)


def _attention(q, k, vt, batch, seq, group, tq, tk):
    nq = seq // tq
    f32 = jnp.float32
    return pl.pallas_call(
        functools.partial(_attn_kernel, group, tk),
        grid=(batch, N_PAIRS, nq),
        in_specs=[
            pl.BlockSpec((None, tq, PAIR_W), lambda b, p, i: (group, b * nq + i, p)),
            pl.BlockSpec((None, seq, PAIR_W), lambda b, p, i: (group, b, p)),
            pl.BlockSpec((None, None, 2 * VT_ROWS, seq), lambda b, p, i: (group, b, p, 0)),
        ],
        out_specs=pl.BlockSpec((None, LANES, tq), lambda b, p, i: (b, p, i)),
        out_shape=jax.ShapeDtypeStruct((batch, N_PAIRS * LANES, seq), jnp.bfloat16),
        scratch_shapes=[pltpu.VMEM((2 * tq, PAIR_W), jnp.bfloat16),
                        pltpu.VMEM((tk, 2 * tq), f32), pltpu.VMEM((tk, 2 * tq), f32),
                        pltpu.VMEM((1, 2 * tq), f32), pltpu.VMEM((1, 2 * tq), f32),
                        pltpu.VMEM((1, 2 * tq), f32), pltpu.VMEM((1, 2 * tq), f32),
                        pltpu.VMEM((2, VT_ROWS, tq), f32)],
        compiler_params=_cparams(("arbitrary",) * 3),
        name="attention_g%d" % group,
    )(q, k, vt)


def _router_t(lt):
    row = lax.broadcasted_iota(jnp.int32, lt.shape, 0)
    big = jnp.int32(LANES)
    ninf = -jnp.inf
    gl = jnp.where(row < N_GROUPS, lt, ninf)
    gmax = jnp.max(gl, axis=0, keepdims=True)
    gsel = jnp.min(jnp.where(gl == gmax, row, big), axis=0, keepdims=True)
    gsum = jnp.sum(jnp.where(row < N_GROUPS, jnp.exp(lt - gmax), 0.0), axis=0, keepdims=True)
    p_g = 1.0 / gsum
    lo = N_GROUPS + EXPERTS_PER_GROUP * gsel
    e = jnp.where((row >= lo) & (row < lo + EXPERTS_PER_GROUP), lt, ninf)
    m1 = jnp.max(e, axis=0, keepdims=True)
    i1 = jnp.min(jnp.where(e == m1, row, big), axis=0, keepdims=True)
    e2 = jnp.where(row == i1, ninf, e)
    m2 = jnp.max(e2, axis=0, keepdims=True)
    i2 = jnp.min(jnp.where(e2 == m2, row, big), axis=0, keepdims=True)
    a2 = jnp.exp(m2 - m1)
    den = 1.0 + a2
    g1 = p_g / den
    g2 = p_g * a2 / den
    f = lambda v: v.astype(jnp.float32)
    return jnp.where(row == 0, f(i1 - N_GROUPS),
                     jnp.where(row == 1, f(i2 - N_GROUPS),
                               jnp.where(row == 2, g1, jnp.where(row == 3, g2, 0.0))))


def _out_proj_kernel(n_o, transposed, x_ref, *refs):
    o_refs = refs[:n_o]
    w_ref, ln_ref, wr_ref, br_ref, xo_ref, xg_ref, route_ref = refs[n_o:]
    tm = x_ref.shape[0]
    hm = tm // OUT_SPLIT
    for part in range(OUT_SPLIT):
        rows = slice(part * hm, (part + 1) * hm)
        if transposed:
            y = x_ref[rows, :]
            nf = o_refs[0].shape[0]
            for g, o_ref in enumerate(o_refs):
                y = y + lax.dot_general(o_ref[:, rows], w_ref[g * nf:(g + 1) * nf, :],
                                        (((0,), (0,)), ((), ())), preferred_element_type=jnp.float32)
        else:
            y = x_ref[rows, :] + jnp.dot(o_refs[0][rows, :], w_ref[...], preferred_element_type=jnp.float32)
        xo_ref[rows, :] = y
        xn = y * _rms_scale(y) * ln_ref[...]
        hi = xn.astype(jnp.bfloat16)
        lo = (xn - hi.astype(jnp.float32)).astype(jnp.bfloat16)
        l2 = jnp.dot(jnp.concatenate([hi, lo], axis=1), wr_ref[...], preferred_element_type=jnp.float32)
        lt = (l2[:, :LANES] + l2[:, LANES:] + br_ref[...]).T
        route_ref[rows, :] = _router_t(lt).T
        for c in range(ROWS_PER_TOK):
            xg_ref[pl.ds(part * hm * ROWS_PER_TOK + c, hm, stride=ROWS_PER_TOK), :] = (
                xn[:, c * LANES:(c + 1) * LANES])


def _out_proj(x2d, o, w_out, ln, w_r, b_r, seq):
    t = x2d.shape[0]
    kdim = w_out.shape[0]
    tm = TM_OUT
    transposed = isinstance(o, tuple)
    tiles_per_seq = seq // tm
    if transposed:
        o_specs = [pl.BlockSpec((None, a.shape[1], tm), lambda i: (i // tiles_per_seq, 0, i % tiles_per_seq))
                   for a in o]
    else:
        o, o_specs = (o,), [pl.BlockSpec((tm, kdim), lambda i: (i, 0))]
    const = lambda shape: pl.BlockSpec(shape, lambda i: (0,) * len(shape))
    return pl.pallas_call(
        functools.partial(_out_proj_kernel, len(o), transposed),
        grid=(t // tm,),
        in_specs=[
            pl.BlockSpec((tm, D_MODEL), lambda i: (i, 0)),
            *o_specs,
            const((kdim, D_MODEL)), const((1, D_MODEL)), const((2 * D_MODEL, 2 * LANES)), const((1, LANES)),
        ],
        out_specs=[
            pl.BlockSpec((tm, D_MODEL), lambda i: (i, 0)),
            pl.BlockSpec((tm * ROWS_PER_TOK, LANES), lambda i: (i, 0)),
            pl.BlockSpec((tm, LANES), lambda i: (i, 0)),
        ],
        out_shape=[
            jax.ShapeDtypeStruct((t, D_MODEL), jnp.float32),
            jax.ShapeDtypeStruct((t * ROWS_PER_TOK, LANES), jnp.float32),
            jax.ShapeDtypeStruct((t, LANES), jnp.float32),
        ],
        compiler_params=_cparams(("arbitrary",)),
        name="out_proj_router",
    )(x2d, *o, w_out, ln, w_r, b_r)


def _router_weights(w_rg, b_rg, w_re, b_re):
    d = w_rg.shape[0]
    pad = LANES - N_GROUPS - N_EXPERTS
    w_r = jnp.concatenate([w_rg, w_re, jnp.zeros((d, pad), w_rg.dtype)], axis=1)
    b_r = jnp.concatenate([b_rg, b_re, jnp.zeros((pad,), b_rg.dtype)])[None, :]
    w_hi = w_r.astype(jnp.bfloat16)
    w_lo = (w_r - w_hi.astype(jnp.float32)).astype(jnp.bfloat16)
    half = jnp.concatenate([w_hi, w_lo], axis=1)
    return jnp.concatenate([half, half], axis=0), b_r


def _dispatch_plan(eid, t, te):
    a = t * TOP_K
    nb = a // te + N_EXPERTS
    e_flat = eid.reshape(a)
    experts = jnp.arange(N_EXPERTS, dtype=jnp.int32)
    _, order = lax.sort((e_flat, jnp.arange(a, dtype=jnp.int32)), num_keys=1, is_stable=True)
    counts = jnp.sum((e_flat[:, None] == experts[None, :]).astype(jnp.int32), axis=0)
    start = jnp.cumsum(counts) - counts
    padded = ((counts + te - 1) // te) * te
    pend = jnp.cumsum(padded)
    pstart = pend - padded
    blk0 = jnp.arange(nb, dtype=jnp.int32) * te
    blk_e = jnp.minimum(jnp.sum((blk0[:, None] >= pend[None, :]).astype(jnp.int32), axis=1), N_EXPERTS - 1)
    blk_used = (blk0 < pend[-1]).astype(jnp.int32)
    onehot = (blk_e[:, None] == experts[None, :]).astype(jnp.int32)
    per_blk = lambda v: jnp.sum(onehot * v[None, :], axis=1)[:, None]
    r = jnp.arange(te, dtype=jnp.int32)[None, :]
    off = blk0[:, None] + r - per_blk(pstart)
    valid = (off < per_blk(counts)) & (blk_used[:, None] > 0)
    src = jnp.where(valid, per_blk(start) + off, 0)
    assign = order[src]
    src_tok = jnp.where(valid, assign // TOP_K, 0).astype(jnp.int32)
    dst_row = jnp.where(valid, assign, a + r).astype(jnp.int32)
    nxt = jnp.minimum(jnp.arange(nb) + 1, nb - 1)
    return blk_e, blk_used, jnp.concatenate([src_tok, src_tok[nxt], dst_row], axis=1)[:, None, :]


def _moe_kernel(te, blk_e_ref, blk_used_ref, idx_ref, xg_ref, wg_ref, wu_ref, wd_ref,
                ys_ref, xbuf, ybuf, wg_bf, wu_bf, wd_bf, gsem, ssem):
    i = pl.program_id(0)
    nb = pl.num_programs(0)
    used = blk_used_ref[i] > 0
    slot = i % 2
    block_rows = te * ROWS_PER_TOK

    def start_gather(first, s):
        for r in range(te):
            tok = idx_ref[0, 0, first + r]
            pltpu.make_async_copy(
                xg_ref.at[pl.ds(pl.multiple_of(tok * ROWS_PER_TOK, ROWS_PER_TOK), ROWS_PER_TOK), :],
                xbuf.at[s, pl.ds(r * ROWS_PER_TOK, ROWS_PER_TOK), :],
                gsem.at[s]).start(priority=r % 2)

    def wait_gather(s):
        pltpu.make_async_copy(xg_ref.at[pl.ds(0, block_rows), :], xbuf.at[s], gsem.at[s]).wait()

    def start_scatter():
        for r in range(te):
            row = idx_ref[0, 0, 2 * te + r]
            pltpu.make_async_copy(
                ybuf.at[pl.ds(r * ROWS_PER_TOK, ROWS_PER_TOK), :],
                ys_ref.at[pl.ds(pl.multiple_of(row * ROWS_PER_TOK, ROWS_PER_TOK), ROWS_PER_TOK), :],
                ssem.at[0]).start(priority=r % 2)

    def wait_scatter():
        pltpu.make_async_copy(ybuf, ys_ref.at[pl.ds(0, block_rows), :], ssem.at[0]).wait()

    @pl.when(i == 0)
    def _():
        ybuf[...] = jnp.zeros_like(ybuf)
        pltpu.make_async_copy(
            ybuf, ys_ref.at[pl.ds(ys_ref.shape[0] - block_rows, block_rows), :], ssem.at[0]).start()
        start_gather(0, 0)

    @pl.when(used)
    def _():
        start_gather(te, 1 - slot)
        prev_e = blk_e_ref[jnp.maximum(i - 1, 0)]

        @pl.when((i == 0) | (blk_e_ref[i] != prev_e))
        def _():
            wg_bf[...] = wg_ref[...].astype(jnp.bfloat16)
            wu_bf[...] = wu_ref[...].astype(jnp.bfloat16)
            wd_bf[...] = wd_ref[...].astype(jnp.bfloat16)

        wait_gather(slot)
        xb = xbuf.at[slot]
        x = jnp.concatenate([xb[pl.ds(c, te, stride=ROWS_PER_TOK), :] for c in range(ROWS_PER_TOK)],
                            axis=1).astype(jnp.bfloat16)
        hg = jnp.dot(x, wg_bf[...], preferred_element_type=jnp.float32)
        hu = jnp.dot(x, wu_bf[...], preferred_element_type=jnp.float32)
        hid = (hg * jax.nn.sigmoid(hg) * hu).astype(jnp.bfloat16)
        y = jnp.dot(hid, wd_bf[...], preferred_element_type=jnp.float32)

        wait_scatter()
        for c in range(ROWS_PER_TOK):
            ybuf[pl.ds(c, te, stride=ROWS_PER_TOK), :] = y[:, c * LANES:(c + 1) * LANES]
        start_scatter()

        next_used = (i + 1 < nb) & (blk_used_ref[jnp.minimum(i + 1, nb - 1)] > 0)

        @pl.when(jnp.logical_not(next_used))
        def _():
            wait_gather(1 - slot)
            wait_scatter()


def _moe(xg, eid, w_gate, w_up, w_down, layer, t, te):
    a = t * TOP_K
    nb = a // te + N_EXPERTS
    blk_e, blk_used, idx = _dispatch_plan(eid, t, te)
    grid_spec = pltpu.PrefetchScalarGridSpec(
        num_scalar_prefetch=2,
        grid=(nb,),
        in_specs=[
            pl.BlockSpec((1, 1, 3 * te), lambda i, be, bu: (i, 0, 0), memory_space=pltpu.SMEM),
            pl.BlockSpec(memory_space=pl.ANY),
            pl.BlockSpec((None, None, D_MODEL, D_EXPERT), lambda i, be, bu: (layer, be[i], 0, 0)),
            pl.BlockSpec((None, None, D_MODEL, D_EXPERT), lambda i, be, bu: (layer, be[i], 0, 0)),
            pl.BlockSpec((None, None, D_EXPERT, D_MODEL), lambda i, be, bu: (layer, be[i], 0, 0)),
        ],
        out_specs=pl.BlockSpec(memory_space=pl.ANY),
        scratch_shapes=[
            pltpu.VMEM((2, te * ROWS_PER_TOK, LANES), jnp.float32),
            pltpu.VMEM((te * ROWS_PER_TOK, LANES), jnp.float32),
            pltpu.VMEM((D_MODEL, D_EXPERT), jnp.bfloat16),
            pltpu.VMEM((D_MODEL, D_EXPERT), jnp.bfloat16),
            pltpu.VMEM((D_EXPERT, D_MODEL), jnp.bfloat16),
            pltpu.SemaphoreType.DMA((2,)),
            pltpu.SemaphoreType.DMA((1,)),
        ],
    )
    return pl.pallas_call(
        functools.partial(_moe_kernel, te),
        grid_spec=grid_spec,
        out_shape=jax.ShapeDtypeStruct(((a + te) * ROWS_PER_TOK, LANES), jnp.float32),
        compiler_params=_cparams(("arbitrary",)),
        name="moe_experts",
    )(blk_e, blk_used, idx, xg, w_gate, w_up, w_down)


def _combine(x, ys_ref, route_ref, tm):
    stride = TOP_K * ROWS_PER_TOK
    y0 = jnp.concatenate([ys_ref[pl.ds(c, tm, stride=stride), :] for c in range(ROWS_PER_TOK)], axis=1)
    y1 = jnp.concatenate([ys_ref[pl.ds(ROWS_PER_TOK + c, tm, stride=stride), :]
                          for c in range(ROWS_PER_TOK)], axis=1)
    route = route_ref[...]
    return x + route[:, 2:3] * y0 + route[:, 3:4] * y1


def _odd_proj_kernel(x_ref, ys_ref, route_ref, g_ref, w_ref, cos_ref, sin_ref,
                     xo_ref, q_ref, k_ref, v_ref, gate_ref):
    tm = x_ref.shape[0]
    x = _combine(x_ref[...], ys_ref, route_ref, tm)
    xo_ref[...] = x
    xn = (x * _rms_scale(x) * g_ref[...]).astype(jnp.bfloat16)
    cos = cos_ref[...]
    sin = sin_ref[...]
    nqk = RET_HEADS * RET_QK
    half = RET_QK // 2

    def rope_store(z, dst, scale):
        for h in range(RET_HEADS):
            x1 = z[:, h * RET_QK:h * RET_QK + half]
            x2 = z[:, h * RET_QK + half:(h + 1) * RET_QK]
            dst[:, h * RET_QK:h * RET_QK + half] = ((x1 * cos - x2 * sin) * scale).astype(dst.dtype)
            dst[:, h * RET_QK + half:(h + 1) * RET_QK] = ((x1 * sin + x2 * cos) * scale).astype(dst.dtype)

    zq = jnp.dot(xn, w_ref[:, 0:nqk], preferred_element_type=jnp.float32)
    rope_store(zq, q_ref, 1.0)
    zk = jnp.dot(xn, w_ref[:, nqk:2 * nqk], preferred_element_type=jnp.float32)
    rope_store(zk, k_ref, RET_QK ** -0.5)
    nv = RET_HEADS * RET_V
    zv = jnp.dot(xn, w_ref[:, 2 * nqk:2 * nqk + nv], preferred_element_type=jnp.float32)
    v_ref[...] = zv.astype(v_ref.dtype)
    zg = jnp.dot(xn, w_ref[:, 2 * nqk + nv:2 * nqk + 2 * nv], preferred_element_type=jnp.float32)
    gate_ref[...] = (zg * jax.nn.sigmoid(zg)).astype(gate_ref.dtype)


def _ret_rope_tables(s):
    half = RET_QK // 2
    inv = ROPE_BASE ** (-jnp.arange(half, dtype=jnp.float32) / half)
    ang = jnp.arange(s, dtype=jnp.float32)[:, None] * inv[None, :]
    return jnp.cos(ang), jnp.sin(ang)


def _odd_proj(x2d, ys, route, seq, g, w_in):
    t = x2d.shape[0]
    tm = TM_PROJ
    tiles_per_seq = seq // tm
    cos, sin = _ret_rope_tables(seq)
    nqk = RET_HEADS * RET_QK
    nv = RET_HEADS * RET_V
    n_in = w_in.shape[1]
    const = lambda shape: pl.BlockSpec(shape, lambda i: (0,) * len(shape))
    tok = lambda n: pl.BlockSpec((tm, n), lambda i: (i, 0))
    return pl.pallas_call(
        _odd_proj_kernel,
        grid=(t // tm,),
        in_specs=[
            tok(D_MODEL),
            pl.BlockSpec((tm * TOP_K * ROWS_PER_TOK, LANES), lambda i: (i, 0)),
            tok(LANES),
            const((1, D_MODEL)), const((D_MODEL, n_in)),
            pl.BlockSpec((tm, LANES), lambda i: (i % tiles_per_seq, 0)),
            pl.BlockSpec((tm, LANES), lambda i: (i % tiles_per_seq, 0)),
        ],
        out_specs=[tok(D_MODEL), tok(nqk), tok(nqk), tok(nv), tok(nv)],
        out_shape=[
            jax.ShapeDtypeStruct((t, D_MODEL), jnp.float32),
            jax.ShapeDtypeStruct((t, nqk), jnp.bfloat16),
            jax.ShapeDtypeStruct((t, nqk), jnp.bfloat16),
            jax.ShapeDtypeStruct((t, nv), jnp.bfloat16),
            jax.ShapeDtypeStruct((t, nv), jnp.bfloat16),
        ],
        compiler_params=_cparams(("arbitrary",)),
        name="odd_proj",
    )(x2d, ys, route, g, w_in, cos, sin)


def _retention_kernel(q_ref, k_ref, v_ref, gate_ref, dmask_ref, xi_ref, zeta_ref, gc_ref, o_ref, state_ref):
    n = pl.program_id(2)

    @pl.when(n == 0)
    def _():
        state_ref[...] = jnp.zeros_like(state_ref)

    xi = xi_ref[...][:, 0:1]
    zeta = zeta_ref[...][:, 0:1]
    g_c = gc_ref[...][:, 0:1]
    for j in range(q_ref.shape[0] // RET_C):
        rows = slice(j * RET_C, (j + 1) * RET_C)
        q = q_ref[rows, :]
        k = k_ref[rows, :]
        v = v_ref[rows, :]
        inner = lax.dot_general(q, k, (((1,), (1,)), ((), ())),
                                preferred_element_type=jnp.float32) * dmask_ref[...]
        o = jnp.dot(inner.astype(jnp.bfloat16), v, preferred_element_type=jnp.float32)
        state = state_ref[...]
        cross = jnp.dot(q, state.astype(jnp.bfloat16), preferred_element_type=jnp.float32)
        o = o + cross * xi
        kz = (k.astype(jnp.float32) * zeta).astype(jnp.bfloat16)
        upd = lax.dot_general(kz, v, (((0,), (0,)), ((), ())), preferred_element_type=jnp.float32)
        state_ref[...] = g_c * state + upd
        o = o * _rms_scale(o)
        o_ref[rows, :] = (o * gate_ref[rows, :].astype(jnp.float32)).astype(o_ref.dtype)


def _retention_tables():
    c = RET_C
    log_g = jnp.log(1.0 - jnp.exp2(-5.0 - jnp.arange(RET_HEADS, dtype=jnp.float32)))
    j = jnp.arange(c, dtype=jnp.float32)
    diff = j[:, None] - j[None, :]
    dmask = jnp.where(diff >= 0, jnp.exp(jnp.maximum(diff, 0.0)[None] * log_g[:, None, None]), 0.0)
    xi = jnp.exp((j[None, :] + 1.0) * log_g[:, None])
    zeta = jnp.exp((c - 1.0 - j)[None, :] * log_g[:, None])
    g_c = jnp.exp(c * log_g)
    bc = lambda a: jnp.broadcast_to(a[..., None], a.shape + (LANES,))
    return dmask, bc(xi), bc(zeta), bc(g_c[:, None])


def _retention(q, k, v, gate, batch, seq):
    nc = seq // RET_BLK
    dmask, xi, zeta, g_c = _retention_tables()
    row = lambda n: pl.BlockSpec((RET_BLK, n), lambda b, h, c: (b * nc + c, h))
    head = lambda shape: pl.BlockSpec((None,) + shape, lambda b, h, c: (h, 0, 0))
    return pl.pallas_call(
        _retention_kernel,
        grid=(batch, RET_HEADS, nc),
        in_specs=[row(RET_QK), row(RET_QK), row(RET_V), row(RET_V),
                  head((RET_C, RET_C)), head((RET_C, LANES)), head((RET_C, LANES)), head((1, LANES))],
        out_specs=row(RET_V),
        out_shape=jax.ShapeDtypeStruct((batch * seq, RET_HEADS * RET_V), jnp.bfloat16),
        scratch_shapes=[pltpu.VMEM((RET_QK, RET_V), jnp.float32)],
        compiler_params=_cparams(("arbitrary",) * 3),
        name="retention",
    )(q, k, v, gate, dmask, xi, zeta, g_c)


def _final_kernel(x_ref, ys_ref, route_ref, g_ref, o_ref):
    x = _combine(x_ref[...], ys_ref, route_ref, x_ref.shape[0])
    o_ref[...] = x * _rms_scale(x) * g_ref[...]


def _final(x2d, ys, route, g):
    t = x2d.shape[0]
    tm = TM_OUT
    return pl.pallas_call(
        _final_kernel,
        grid=(t // tm,),
        in_specs=[
            pl.BlockSpec((tm, D_MODEL), lambda i: (i, 0)),
            pl.BlockSpec((tm * TOP_K * ROWS_PER_TOK, LANES), lambda i: (i, 0)),
            pl.BlockSpec((tm, LANES), lambda i: (i, 0)),
            pl.BlockSpec((1, D_MODEL), lambda i: (0, 0)),
        ],
        out_specs=pl.BlockSpec((tm, D_MODEL), lambda i: (i, 0)),
        out_shape=jax.ShapeDtypeStruct((t, D_MODEL), jnp.float32),
        compiler_params=_cparams(("arbitrary",)),
        name="final_norm",
    )(x2d, ys, route, g)


def kernel(x, ln_mix_e, w_in_e, ln_q_e, w_uq_e, ln_kv_e, w_ukv_e, b_f_e, w_out_e, ln_mix_o, w_in_o,
           w_out_o, ln_ffn, w_rg, b_rg, w_re, b_re, w_gate, w_up, w_down, ln_f):
    batch, seq, d = x.shape
    t = batch * seq
    assert d == D_MODEL and seq % TK == 0 and seq % TQ == 0 and seq % RET_BLK == 0
    assert seq % TM_PROJ == 0 and t % TM_OUT == 0 and t % TE == 0
    bf16 = jnp.bfloat16
    x2d = x.reshape(t, d)

    w_in_p, w_uq_p, w_ukv_p, b_f_p = _even_weights(w_in_e[0], w_uq_e[0], w_ukv_e[0], b_f_e[0])
    q, k, vt = _even_proj(x2d, seq, ln_mix_e[0][None, :], w_in_p, ln_q_e[0][None, :], w_uq_p,
                         ln_kv_e[0][None, :], w_ukv_p, b_f_p)
    o_t = (_attention(q, k, vt, batch, seq, 0, TQ, TK), _attention(q, k, vt, batch, seq, 1, TQ, TK))
    w_r, b_r = _router_weights(w_rg[0], b_rg[0], w_re[0], b_re[0])
    x2d, xg, route = _out_proj(x2d, o_t, w_out_e[0].astype(bf16), ln_ffn[0][None, :], w_r, b_r, seq)
    ys = _moe(xg, route[:, 0:TOP_K].astype(jnp.int32), w_gate, w_up, w_down, 0, t, TE // 2)

    x2d, rq, rk, rv, rg = _odd_proj(x2d, ys, route, seq, ln_mix_o[0][None, :], w_in_o[0].astype(bf16))
    og = _retention(rq, rk, rv, rg, batch, seq)
    w_r, b_r = _router_weights(w_rg[1], b_rg[1], w_re[1], b_re[1])
    x2d, xg, route = _out_proj(x2d, og, w_out_o[0].astype(bf16), ln_ffn[1][None, :], w_r, b_r, seq)
    ys = _moe(xg, route[:, 0:TOP_K].astype(jnp.int32), w_gate, w_up, w_down, 1, t, TE)

    out = _final(x2d, ys, route, ln_f[None, :])
    return out.reshape(batch, seq, d)
```

```python
import functools
import math

import numpy as np
import jax
import jax.numpy as jnp
from jax import lax
from jax.experimental import pallas as pl
from jax.experimental.pallas import tpu as pltpu

D_MODEL = 1024
CHUNK = 64
ROPE_BASE = 10000.0
EPS = 1e-6
MLA_HEADS = 8
MLA_NOPE = 64
MLA_ROPE = 32
MLA_V = 64
MLA_Q_LORA = 256
MLA_KV_LORA = 128
FOX_HEADS = 8
FOX_DIM = 64
RET_HEADS = 4
RET_QK = 256
RET_V = 512
N_GROUPS = 4
EXPERTS_PER_GROUP = 8
N_EXPERTS = N_GROUPS * EXPERTS_PER_GROUP
TOP_K = 2
D_EXPERT = 512

LANES = 128
SUBLANES = 8
VMEM_LIMIT = 52 * 1024 * 1024

TM_PROJ = 256
TM_OUT = 512
OUT_SPLIT = 2
TQ = 1024
VT_ROWS = 80
TK = 1024
LOG2E = math.log2(math.e)
RET_C = 256
RET_BLK = 2048
TE = 256
ROWS_PER_TOK = D_MODEL // LANES

N_PAIRS = MLA_HEADS // 2
PAIR_W = 256
ROPE_H = MLA_ROPE // 2
NEG = -1e30

_X1_OFF = 0
_X2_OFF = 64
_AUG_W = 6

_C_Q = 0
_C_KV = _C_Q + MLA_Q_LORA
_C_KPE = _C_KV + MLA_KV_LORA
_C_QF = _C_KPE + LANES
_C_KF = _C_QF + FOX_HEADS * FOX_DIM
_C_VF = _C_KF + FOX_HEADS * FOX_DIM
_C_F = _C_VF + FOX_HEADS * FOX_DIM
EVEN_N = _C_F + LANES


def _cparams(sem):
    return pltpu.CompilerParams(dimension_semantics=sem, vmem_limit_bytes=VMEM_LIMIT)


def _rms_scale(x):
    return lax.rsqrt(jnp.mean(x * x, axis=-1, keepdims=True) + EPS)


def _split3(c):
    hi = c.astype(jnp.bfloat16)
    r1 = c - hi.astype(jnp.float32)
    mid = r1.astype(jnp.bfloat16)
    r2 = r1 - mid.astype(jnp.float32)
    lo = r2.astype(jnp.bfloat16)
    return hi, mid, lo


def _even_proj_kernel(tiles_per_seq, x_ref, g_ref, win_ref, lnq_ref, wuq_ref, lnkv_ref, wukv_ref,
                      bf_ref, cos_ref, sin_ref, tri_ref, pq_ref, pk_ref, oq_ref, ok_ref,
                      q_ref, k_ref, vt_ref, carry_ref):
    i = pl.program_id(0)
    x = x_ref[...]
    xn = (x * _rms_scale(x) * g_ref[...]).astype(jnp.bfloat16)
    z = jnp.dot(xn, win_ref[...], preferred_element_type=jnp.float32)
    cos = cos_ref[...]
    sin = sin_ref[...]

    def rope_slab(s):
        return s * cos + pltpu.roll(s, 64, axis=1) * sin

    c_q = z[:, _C_Q:_C_Q + MLA_Q_LORA]
    cqn = (c_q * _rms_scale(c_q) * lnq_ref[...]).astype(jnp.bfloat16)
    q = jnp.dot(cqn, wuq_ref[...], preferred_element_type=jnp.float32)
    scale_a = (MLA_NOPE + MLA_ROPE) ** -0.5 * LOG2E
    for p in range(N_PAIRS):
        lo = p * PAIR_W
        q_ref[0, :, lo:lo + LANES] = (q[:, lo:lo + LANES] * scale_a).astype(jnp.bfloat16)
        q_ref[0, :, lo + LANES:lo + PAIR_W] = (
            rope_slab(q[:, lo + LANES:lo + PAIR_W]) * scale_a).astype(jnp.bfloat16)

    c_kv = z[:, _C_KV:_C_KV + MLA_KV_LORA]
    ckn = (c_kv * _rms_scale(c_kv) * lnkv_ref[...]).astype(jnp.bfloat16)
    kv = jnp.dot(ckn, wukv_ref[...], preferred_element_type=jnp.float32)
    kpe = rope_slab(z[:, _C_KPE:_C_KPE + LANES]).astype(jnp.bfloat16)
    nk = MLA_HEADS * MLA_NOPE
    for p in range(N_PAIRS):
        lo = p * PAIR_W
        k_ref[0, :, lo:lo + LANES] = kv[:, p * LANES:(p + 1) * LANES].astype(jnp.bfloat16)
        k_ref[0, :, lo + LANES:lo + PAIR_W] = kpe
    _store_values_transposed(vt_ref, 0, kv[:, nk:nk + MLA_HEADS * MLA_V])

    lane = lax.broadcasted_iota(jnp.int32, (1, LANES), 1)
    fz = z[:, _C_F:_C_F + LANES] + bf_ref[...]
    log_f = -(jnp.maximum(-fz, 0.0) + jnp.log1p(jnp.exp(-jnp.abs(fz))))
    log_f = jnp.where(lane < FOX_HEADS, log_f, 0.0)

    @pl.when(i % tiles_per_seq == 0)
    def _():
        carry_ref[...] = jnp.zeros_like(carry_ref)

    hi, mid, lo3 = _split3(log_f)
    tri = tri_ref[...]
    cum = (jnp.dot(tri, hi, preferred_element_type=jnp.float32)
           + jnp.dot(tri, mid, preferred_element_type=jnp.float32)
           + jnp.dot(tri, lo3, preferred_element_type=jnp.float32)) + carry_ref[...]
    tm = cum.shape[0]
    carry_ref[...] = cum[tm - 1:tm, :]

    parts = jnp.concatenate(_split3(cum * LOG2E), axis=1)
    aug_q = jnp.dot(parts, pq_ref[...], preferred_element_type=jnp.float32) + oq_ref[...]
    aug_k = jnp.dot(parts, pk_ref[...], preferred_element_type=jnp.float32) + ok_ref[...]
    scale_b = FOX_DIM ** -0.5 * LOG2E
    for p in range(N_PAIRS):
        lo = p * PAIR_W
        q_ref[1, :, lo:lo + LANES] = (
            z[:, _C_QF + p * LANES:_C_QF + (p + 1) * LANES] * scale_b).astype(jnp.bfloat16)
        q_ref[1, :, lo + LANES:lo + PAIR_W] = aug_q[:, p * LANES:(p + 1) * LANES].astype(jnp.bfloat16)
        k_ref[1, :, lo:lo + LANES] = z[:, _C_KF + p * LANES:_C_KF + (p + 1) * LANES].astype(jnp.bfloat16)
        k_ref[1, :, lo + LANES:lo + PAIR_W] = aug_k[:, p * LANES:(p + 1) * LANES].astype(jnp.bfloat16)
    _store_values_transposed(vt_ref, 1, z[:, _C_VF:_C_VF + FOX_HEADS * FOX_DIM])


def _store_values_transposed(vt_ref, g, v):
    vt = v.T
    ones = jnp.ones((VT_ROWS - MLA_V, vt.shape[1]), vt_ref.dtype)
    for h in range(MLA_HEADS):
        vt_ref[g, h * VT_ROWS:h * VT_ROWS + MLA_V, :] = vt[h * MLA_V:(h + 1) * MLA_V].astype(vt_ref.dtype)
        vt_ref[g, h * VT_ROWS + MLA_V:(h + 1) * VT_ROWS, :] = ones


def _even_weights(w_in, w_uq, w_ukv, b_f):
    d = w_in.shape[0]
    zeros = lambda n: jnp.zeros((d, n), w_in.dtype)
    o_cq, o_ckv = 0, MLA_Q_LORA
    o_kpe = o_ckv + MLA_KV_LORA
    o_qf = o_kpe + MLA_ROPE
    o_kf = o_qf + FOX_HEADS * FOX_DIM
    o_vf = o_kf + FOX_HEADS * FOX_DIM
    o_f = o_vf + FOX_HEADS * FOX_DIM
    kpe1 = w_in[:, o_kpe:o_kpe + ROPE_H]
    kpe2 = w_in[:, o_kpe + ROPE_H:o_kpe + MLA_ROPE]
    kpe_slab = jnp.concatenate([kpe1, kpe1, zeros(32), kpe2, kpe2, zeros(32)], axis=1)
    f_slab = jnp.concatenate([w_in[:, o_f:o_f + FOX_HEADS], zeros(LANES - FOX_HEADS)], axis=1)
    w_in_p = jnp.concatenate([w_in[:, o_cq:o_kpe], kpe_slab, w_in[:, o_qf:o_f], f_slab], axis=1)

    dq = w_uq.shape[0]
    zq = lambda n: jnp.zeros((dq, n), w_uq.dtype)
    hd = MLA_NOPE + MLA_ROPE
    blocks = []
    for p in range(N_PAIRS):
        h0, h1 = 2 * p, 2 * p + 1
        nope = lambda h: w_uq[:, h * hd:h * hd + MLA_NOPE]
        r1 = lambda h: w_uq[:, h * hd + MLA_NOPE:h * hd + MLA_NOPE + ROPE_H]
        r2 = lambda h: w_uq[:, h * hd + MLA_NOPE + ROPE_H:(h + 1) * hd]
        blocks += [nope(h0), nope(h1), r1(h0), r1(h1), zq(32), r2(h0), r2(h1), zq(32)]
    w_uq_p = jnp.concatenate(blocks, axis=1)

    kvd = MLA_NOPE + MLA_V
    k_cols = [w_ukv[:, h * kvd:h * kvd + MLA_NOPE] for h in range(MLA_HEADS)]
    v_cols = [w_ukv[:, h * kvd + MLA_NOPE:(h + 1) * kvd] for h in range(MLA_HEADS)]
    w_ukv_p = jnp.concatenate(k_cols + v_cols, axis=1)
    b_f_p = jnp.concatenate([b_f, jnp.zeros((LANES - FOX_HEADS,), b_f.dtype)])[None, :]
    bf16 = jnp.bfloat16
    return w_in_p.astype(bf16), w_uq_p.astype(bf16), w_ukv_p.astype(bf16), b_f_p


def _aug_placement():
    pq = np.zeros((3 * LANES, N_PAIRS * LANES), np.float32)
    pk = np.zeros((3 * LANES, N_PAIRS * LANES), np.float32)
    oq = np.zeros((1, N_PAIRS * LANES), np.float32)
    ok = np.zeros((1, N_PAIRS * LANES), np.float32)
    for p in range(N_PAIRS):
        for j in range(2):
            h = 2 * p + j
            base = p * LANES + _X1_OFF + j * ROPE_H
            for t in range(3):
                pq[t * LANES + h, base + 3 + t] = 1.0
                pk[t * LANES + h, base + t] = -1.0
                oq[0, base + t] = 1.0
                ok[0, base + 3 + t] = 1.0
    return (jnp.asarray(pq, jnp.bfloat16), jnp.asarray(pk, jnp.bfloat16),
            jnp.asarray(oq), jnp.asarray(ok))


def _mla_rope_tables(s):
    inv = ROPE_BASE ** (-jnp.arange(ROPE_H, dtype=jnp.float32) / ROPE_H)
    ang = jnp.arange(s, dtype=jnp.float32)[:, None] * inv[None, :]
    c, sn = jnp.cos(ang), jnp.sin(ang)
    z = jnp.zeros((s, 32), jnp.float32)
    cos = jnp.concatenate([c, c, z, c, c, z], axis=1)
    sin = jnp.concatenate([-sn, -sn, z, sn, sn, z], axis=1)
    return cos, sin


def _even_proj(x2d, seq, g, w_in_p, ln_q, w_uq_p, ln_kv, w_ukv_p, b_f_p):
    t = x2d.shape[0]
    tm = TM_OUT
    tiles_per_seq = seq // tm
    cos, sin = _mla_rope_tables(seq)
    tri = jnp.asarray(np.tril(np.ones((tm, tm), np.float32)), jnp.bfloat16)
    pq, pk, oq, ok = _aug_placement()
    const = lambda shape: pl.BlockSpec(shape, lambda i: (0,) * len(shape))
    return pl.pallas_call(
        functools.partial(_even_proj_kernel, tiles_per_seq),
        grid=(t // tm,),
        in_specs=[
            pl.BlockSpec((tm, D_MODEL), lambda i: (i, 0)),
            const((1, D_MODEL)), const((D_MODEL, EVEN_N)),
            const((1, MLA_Q_LORA)), const((MLA_Q_LORA, N_PAIRS * PAIR_W)),
            const((1, MLA_KV_LORA)), const((MLA_KV_LORA, 2 * MLA_HEADS * MLA_NOPE)),
            const((1, LANES)),
            pl.BlockSpec((tm, LANES), lambda i: (i % tiles_per_seq, 0)),
            pl.BlockSpec((tm, LANES), lambda i: (i % tiles_per_seq, 0)),
            const((tm, tm)), const(pq.shape), const(pk.shape), const(oq.shape), const(ok.shape),
        ],
        out_specs=[
            pl.BlockSpec((2, tm, N_PAIRS * PAIR_W), lambda i: (0, i, 0)),
            pl.BlockSpec((2, tm, N_PAIRS * PAIR_W), lambda i: (0, i, 0)),
            pl.BlockSpec((2, None, MLA_HEADS * VT_ROWS, tm),
                         lambda i: (0, i // tiles_per_seq, 0, i % tiles_per_seq)),
        ],
        out_shape=[
            jax.ShapeDtypeStruct((2, t, N_PAIRS * PAIR_W), jnp.bfloat16),
            jax.ShapeDtypeStruct((2, t, N_PAIRS * PAIR_W), jnp.bfloat16),
            jax.ShapeDtypeStruct((2, t // seq, MLA_HEADS * VT_ROWS, seq), jnp.bfloat16),
        ],
        scratch_shapes=[pltpu.VMEM((1, LANES), jnp.float32)],
        compiler_params=_cparams(("arbitrary",)),
        name="even_proj",
    )(x2d, g, w_in_p, ln_q, w_uq_p, ln_kv, w_ukv_p, b_f_p, cos, sin, tri, pq, pk, oq, ok)


def _attn_kernel(group, tk, q_ref, k_ref, vt_ref, o_ref, qcat_ref, s0_ref, s1_ref, m0_ref, m1_ref,
                 a0_ref, a1_ref, acc_ref):
    i = pl.program_id(2)
    tq = q_ref.shape[0]
    s_bufs, m_bufs, a_bufs = (s0_ref, s1_ref), (m0_ref, m1_ref), (a0_ref, a1_ref)

    lane = lax.broadcasted_iota(jnp.int32, (1, PAIR_W), 1)

    def head_mask(j):
        a = (lane >= j * MLA_NOPE) & (lane < (j + 1) * MLA_NOPE)
        b = (lane >= LANES + _X1_OFF + j * ROPE_H) & (lane < LANES + _X1_OFF + (j + 1) * ROPE_H)
        c = (lane >= LANES + _X2_OFF + j * ROPE_H) & (lane < LANES + _X2_OFF + (j + 1) * ROPE_H)
        return a | b | c

    q = q_ref[...]
    zero = jnp.zeros_like(q)
    qcat_ref[0:tq, :] = jnp.where(head_mask(0), q, zero)
    qcat_ref[tq:2 * tq, :] = jnp.where(head_mask(1), q, zero)
    acc_ref[...] = jnp.zeros_like(acc_ref)

    q0 = i * tq
    n_chunks = q0 // tk + 1

    def scores(c, m_run, par, masked):
        kstart = pl.multiple_of(c * tk, tk)
        s = lax.dot_general(k_ref[pl.ds(kstart, tk), :], qcat_ref[...], (((1,), (1,)), ((), ())),
                            preferred_element_type=jnp.float32)
        if masked:
            col = lax.broadcasted_iota(jnp.int32, (1, 2 * tq), 1)
            qpos = q0 + jnp.where(col >= tq, col - tq, col)
            qlim = (qpos | (CHUNK - 1)) if group == 0 else qpos
            kpos = kstart + lax.broadcasted_iota(jnp.int32, (tk, 1), 0)
            s = jnp.where(kpos <= qlim, s, NEG)
        s_bufs[par][...] = s
        m_new = jnp.maximum(m_run, jnp.max(s, axis=0, keepdims=True))
        m_bufs[par][...] = m_new
        a_bufs[par][...] = jnp.exp2(m_run - m_new)

    def accumulate(c, par):
        kstart = pl.multiple_of(c * tk, tk)
        alpha = a_bufs[par][...]
        p = jnp.exp2(s_bufs[par][...] - m_bufs[par][...]).astype(jnp.bfloat16)
        for h in range(2):
            cols = slice(h * tq, (h + 1) * tq)
            acc_ref[h] = alpha[:, cols] * acc_ref[h] + jnp.dot(
                vt_ref[h * VT_ROWS:(h + 1) * VT_ROWS, pl.ds(kstart, tk)], p[:, cols],
                preferred_element_type=jnp.float32)

    def stage(c, par, masked):
        scores(c + 1, m_bufs[par][...], 1 - par, masked)
        accumulate(c, par)

    m_init = jnp.full((1, 2 * tq), NEG, jnp.float32)

    @pl.when(n_chunks == 1)
    def _():
        scores(0, m_init, 0, True)
        accumulate(0, 0)

    @pl.when(n_chunks > 1)
    def _():
        scores(0, m_init, 0, False)

    n_pairs = jnp.maximum(n_chunks - 2, 0) // 2

    def body(j, carry):
        for u in range(4):
            stage(4 * j + u, u % 2, False)
        return carry

    lax.fori_loop(0, n_pairs // 2, body, 0)

    @pl.when(n_pairs % 2 == 1)
    def _():
        stage(2 * n_pairs - 2, 0, False)
        stage(2 * n_pairs - 1, 1, False)
    c0 = 2 * n_pairs
    left = n_chunks - 1 - c0

    @pl.when((n_chunks > 1) & (left == 1))
    def _():
        stage(c0, 0, True)
        accumulate(c0 + 1, 1)

    @pl.when((n_chunks > 1) & (left == 2))
    def _():
        stage(c0, 0, False)
        stage(c0 + 1, 1, True)
        accumulate(c0 + 2, 0)

    for h in range(2):
        acc = acc_ref[h]
        o_ref[h * MLA_V:(h + 1) * MLA_V, :] = (acc[0:MLA_V] / acc[MLA_V:MLA_V + 1]).astype(o_ref.dtype)


def _attention(q, k, vt, batch, seq, group, tq, tk):
    nq = seq // tq
    f32 = jnp.float32
    return pl.pallas_call(
        functools.partial(_attn_kernel, group, tk),
        grid=(batch, N_PAIRS, nq),
        in_specs=[
            pl.BlockSpec((None, tq, PAIR_W), lambda b, p, i: (group, b * nq + i, p)),
            pl.BlockSpec((None, seq, PAIR_W), lambda b, p, i: (group, b, p)),
            pl.BlockSpec((None, None, 2 * VT_ROWS, seq), lambda b, p, i: (group, b, p, 0)),
        ],
        out_specs=pl.BlockSpec((None, LANES, tq), lambda b, p, i: (b, p, i)),
        out_shape=jax.ShapeDtypeStruct((batch, N_PAIRS * LANES, seq), jnp.bfloat16),
        scratch_shapes=[pltpu.VMEM((2 * tq, PAIR_W), jnp.bfloat16),
                        pltpu.VMEM((tk, 2 * tq), f32), pltpu.VMEM((tk, 2 * tq), f32),
                        pltpu.VMEM((1, 2 * tq), f32), pltpu.VMEM((1, 2 * tq), f32),
                        pltpu.VMEM((1, 2 * tq), f32), pltpu.VMEM((1, 2 * tq), f32),
                        pltpu.VMEM((2, VT_ROWS, tq), f32)],
        compiler_params=_cparams(("arbitrary",) * 3),
        name="attention_g%d" % group,
    )(q, k, vt)


def _router_t(lt):
    row = lax.broadcasted_iota(jnp.int32, lt.shape, 0)
    big = jnp.int32(LANES)
    ninf = -jnp.inf
    gl = jnp.where(row < N_GROUPS, lt, ninf)
    gmax = jnp.max(gl, axis=0, keepdims=True)
    gsel = jnp.min(jnp.where(gl == gmax, row, big), axis=0, keepdims=True)
    gsum = jnp.sum(jnp.where(row < N_GROUPS, jnp.exp(lt - gmax), 0.0), axis=0, keepdims=True)
    p_g = 1.0 / gsum
    lo = N_GROUPS + EXPERTS_PER_GROUP * gsel
    e = jnp.where((row >= lo) & (row < lo + EXPERTS_PER_GROUP), lt, ninf)
    m1 = jnp.max(e, axis=0, keepdims=True)
    i1 = jnp.min(jnp.where(e == m1, row, big), axis=0, keepdims=True)
    e2 = jnp.where(row == i1, ninf, e)
    m2 = jnp.max(e2, axis=0, keepdims=True)
    i2 = jnp.min(jnp.where(e2 == m2, row, big), axis=0, keepdims=True)
    a2 = jnp.exp(m2 - m1)
    den = 1.0 + a2
    g1 = p_g / den
    g2 = p_g * a2 / den
    f = lambda v: v.astype(jnp.float32)
    return jnp.where(row == 0, f(i1 - N_GROUPS),
                     jnp.where(row == 1, f(i2 - N_GROUPS),
                               jnp.where(row == 2, g1, jnp.where(row == 3, g2, 0.0))))


def _out_proj_kernel(n_o, transposed, x_ref, *refs):
    o_refs = refs[:n_o]
    w_ref, ln_ref, wr_ref, br_ref, xo_ref, xg_ref, route_ref = refs[n_o:]
    tm = x_ref.shape[0]
    hm = tm // OUT_SPLIT
    for part in range(OUT_SPLIT):
        rows = slice(part * hm, (part + 1) * hm)
        if transposed:
            y = x_ref[rows, :]
            nf = o_refs[0].shape[0]
            for g, o_ref in enumerate(o_refs):
                y = y + lax.dot_general(o_ref[:, rows], w_ref[g * nf:(g + 1) * nf, :],
                                        (((0,), (0,)), ((), ())), preferred_element_type=jnp.float32)
        else:
            y = x_ref[rows, :] + jnp.dot(o_refs[0][rows, :], w_ref[...], preferred_element_type=jnp.float32)
        xo_ref[rows, :] = y
        xn = y * _rms_scale(y) * ln_ref[...]
        hi = xn.astype(jnp.bfloat16)
        lo = (xn - hi.astype(jnp.float32)).astype(jnp.bfloat16)
        l2 = jnp.dot(jnp.concatenate([hi, lo], axis=1), wr_ref[...], preferred_element_type=jnp.float32)
        lt = (l2[:, :LANES] + l2[:, LANES:] + br_ref[...]).T
        route_ref[rows, :] = _router_t(lt).T
        for c in range(ROWS_PER_TOK):
            xg_ref[pl.ds(part * hm * ROWS_PER_TOK + c, hm, stride=ROWS_PER_TOK), :] = (
                xn[:, c * LANES:(c + 1) * LANES])


def _out_proj(x2d, o, w_out, ln, w_r, b_r, seq):
    t = x2d.shape[0]
    kdim = w_out.shape[0]
    tm = TM_OUT
    transposed = isinstance(o, tuple)
    tiles_per_seq = seq // tm
    if transposed:
        o_specs = [pl.BlockSpec((None, a.shape[1], tm), lambda i: (i // tiles_per_seq, 0, i % tiles_per_seq))
                   for a in o]
    else:
        o, o_specs = (o,), [pl.BlockSpec((tm, kdim), lambda i: (i, 0))]
    const = lambda shape: pl.BlockSpec(shape, lambda i: (0,) * len(shape))
    return pl.pallas_call(
        functools.partial(_out_proj_kernel, len(o), transposed),
        grid=(t // tm,),
        in_specs=[
            pl.BlockSpec((tm, D_MODEL), lambda i: (i, 0)),
            *o_specs,
            const((kdim, D_MODEL)), const((1, D_MODEL)), const((2 * D_MODEL, 2 * LANES)), const((1, LANES)),
        ],
        out_specs=[
            pl.BlockSpec((tm, D_MODEL), lambda i: (i, 0)),
            pl.BlockSpec((tm * ROWS_PER_TOK, LANES), lambda i: (i, 0)),
            pl.BlockSpec((tm, LANES), lambda i: (i, 0)),
        ],
        out_shape=[
            jax.ShapeDtypeStruct((t, D_MODEL), jnp.float32),
            jax.ShapeDtypeStruct((t * ROWS_PER_TOK, LANES), jnp.float32),
            jax.ShapeDtypeStruct((t, LANES), jnp.float32),
        ],
        compiler_params=_cparams(("arbitrary",)),
        name="out_proj_router",
    )(x2d, *o, w_out, ln, w_r, b_r)


def _router_weights(w_rg, b_rg, w_re, b_re):
    d = w_rg.shape[0]
    pad = LANES - N_GROUPS - N_EXPERTS
    w_r = jnp.concatenate([w_rg, w_re, jnp.zeros((d, pad), w_rg.dtype)], axis=1)
    b_r = jnp.concatenate([b_rg, b_re, jnp.zeros((pad,), b_rg.dtype)])[None, :]
    w_hi = w_r.astype(jnp.bfloat16)
    w_lo = (w_r - w_hi.astype(jnp.float32)).astype(jnp.bfloat16)
    half = jnp.concatenate([w_hi, w_lo], axis=1)
    return jnp.concatenate([half, half], axis=0), b_r


def _dispatch_plan(eid, t, te):
    a = t * TOP_K
    nb = a // te + N_EXPERTS
    e_flat = eid.reshape(a)
    experts = jnp.arange(N_EXPERTS, dtype=jnp.int32)
    _, order = lax.sort((e_flat, jnp.arange(a, dtype=jnp.int32)), num_keys=1, is_stable=True)
    counts = jnp.sum((e_flat[:, None] == experts[None, :]).astype(jnp.int32), axis=0)
    start = jnp.cumsum(counts) - counts
    padded = ((counts + te - 1) // te) * te
    pend = jnp.cumsum(padded)
    pstart = pend - padded
    blk0 = jnp.arange(nb, dtype=jnp.int32) * te
    blk_e = jnp.minimum(jnp.sum((blk0[:, None] >= pend[None, :]).astype(jnp.int32), axis=1), N_EXPERTS - 1)
    blk_used = (blk0 < pend[-1]).astype(jnp.int32)
    onehot = (blk_e[:, None] == experts[None, :]).astype(jnp.int32)
    per_blk = lambda v: jnp.sum(onehot * v[None, :], axis=1)[:, None]
    r = jnp.arange(te, dtype=jnp.int32)[None, :]
    off = blk0[:, None] + r - per_blk(pstart)
    valid = (off < per_blk(counts)) & (blk_used[:, None] > 0)
    src = jnp.where(valid, per_blk(start) + off, 0)
    assign = order[src]
    src_tok = jnp.where(valid, assign // TOP_K, 0).astype(jnp.int32)
    dst_row = jnp.where(valid, assign, a + r).astype(jnp.int32)
    nxt = jnp.minimum(jnp.arange(nb) + 1, nb - 1)
    return blk_e, blk_used, jnp.concatenate([src_tok, src_tok[nxt], dst_row], axis=1)[:, None, :]


def _moe_kernel(te, gather_second, blk_e_ref, blk_used_ref, idx_ref, xg_ref, wg_ref, wu_ref, wd_ref,
                ys_ref, xbuf, ybuf, wg_bf, wu_bf, wd_bf, gsem, ssem):
    i = pl.program_id(0)
    nb = pl.num_programs(0)
    used = blk_used_ref[i] > 0
    slot = i % 2
    block_rows = te * ROWS_PER_TOK

    def start_gather(first, s):
        for r in range(te):
            tok = idx_ref[0, 0, first + r]
            pltpu.make_async_copy(
                xg_ref.at[pl.ds(pl.multiple_of(tok * ROWS_PER_TOK, ROWS_PER_TOK), ROWS_PER_TOK), :],
                xbuf.at[s, pl.ds(r * ROWS_PER_TOK, ROWS_PER_TOK), :],
                gsem.at[s]).start(priority=1 if gather_second else r % 2)

    def wait_gather(s):
        pltpu.make_async_copy(xg_ref.at[pl.ds(0, block_rows), :], xbuf.at[s], gsem.at[s]).wait()

    def start_scatter():
        for r in range(te):
            row = idx_ref[0, 0, 2 * te + r]
            pltpu.make_async_copy(
                ybuf.at[pl.ds(r * ROWS_PER_TOK, ROWS_PER_TOK), :],
                ys_ref.at[pl.ds(pl.multiple_of(row * ROWS_PER_TOK, ROWS_PER_TOK), ROWS_PER_TOK), :],
                ssem.at[0]).start(priority=r % 2)

    def wait_scatter():
        pltpu.make_async_copy(ybuf, ys_ref.at[pl.ds(0, block_rows), :], ssem.at[0]).wait()

    @pl.when(i == 0)
    def _():
        ybuf[...] = jnp.zeros_like(ybuf)
        pltpu.make_async_copy(
            ybuf, ys_ref.at[pl.ds(ys_ref.shape[0] - block_rows, block_rows), :], ssem.at[0]).start()
        start_gather(0, 0)

    @pl.when(used)
    def _():
        start_gather(te, 1 - slot)
        prev_e = blk_e_ref[jnp.maximum(i - 1, 0)]

        @pl.when((i == 0) | (blk_e_ref[i] != prev_e))
        def _():
            wg_bf[...] = wg_ref[...].astype(jnp.bfloat16)
            wu_bf[...] = wu_ref[...].astype(jnp.bfloat16)
            wd_bf[...] = wd_ref[...].astype(jnp.bfloat16)

        wait_gather(slot)
        xb = xbuf.at[slot]
        x = jnp.concatenate([xb[pl.ds(c, te, stride=ROWS_PER_TOK), :] for c in range(ROWS_PER_TOK)],
                            axis=1).astype(jnp.bfloat16)
        hg = jnp.dot(x, wg_bf[...], preferred_element_type=jnp.float32)
        hu = jnp.dot(x, wu_bf[...], preferred_element_type=jnp.float32)
        hid = (hg * jax.nn.sigmoid(hg) * hu).astype(jnp.bfloat16)
        y = jnp.dot(hid, wd_bf[...], preferred_element_type=jnp.float32)

        wait_scatter()
        for c in range(ROWS_PER_TOK):
            ybuf[pl.ds(c, te, stride=ROWS_PER_TOK), :] = y[:, c * LANES:(c + 1) * LANES]
        start_scatter()

        next_used = (i + 1 < nb) & (blk_used_ref[jnp.minimum(i + 1, nb - 1)] > 0)

        @pl.when(jnp.logical_not(next_used))
        def _():
            wait_gather(1 - slot)
            wait_scatter()


def _moe(xg, eid, w_gate, w_up, w_down, layer, t, te, gather_second):
    a = t * TOP_K
    nb = a // te + N_EXPERTS
    blk_e, blk_used, idx = _dispatch_plan(eid, t, te)
    grid_spec = pltpu.PrefetchScalarGridSpec(
        num_scalar_prefetch=2,
        grid=(nb,),
        in_specs=[
            pl.BlockSpec((1, 1, 3 * te), lambda i, be, bu: (i, 0, 0), memory_space=pltpu.SMEM),
            pl.BlockSpec(memory_space=pl.ANY),
            pl.BlockSpec((None, None, D_MODEL, D_EXPERT), lambda i, be, bu: (layer, be[i], 0, 0)),
            pl.BlockSpec((None, None, D_MODEL, D_EXPERT), lambda i, be, bu: (layer, be[i], 0, 0)),
            pl.BlockSpec((None, None, D_EXPERT, D_MODEL), lambda i, be, bu: (layer, be[i], 0, 0)),
        ],
        out_specs=pl.BlockSpec(memory_space=pl.ANY),
        scratch_shapes=[
            pltpu.VMEM((2, te * ROWS_PER_TOK, LANES), jnp.float32),
            pltpu.VMEM((te * ROWS_PER_TOK, LANES), jnp.float32),
            pltpu.VMEM((D_MODEL, D_EXPERT), jnp.bfloat16),
            pltpu.VMEM((D_MODEL, D_EXPERT), jnp.bfloat16),
            pltpu.VMEM((D_EXPERT, D_MODEL), jnp.bfloat16),
            pltpu.SemaphoreType.DMA((2,)),
            pltpu.SemaphoreType.DMA((1,)),
        ],
    )
    return pl.pallas_call(
        functools.partial(_moe_kernel, te, gather_second),
        grid_spec=grid_spec,
        out_shape=jax.ShapeDtypeStruct(((a + te) * ROWS_PER_TOK, LANES), jnp.float32),
        compiler_params=_cparams(("arbitrary",)),
        name="moe_experts",
    )(blk_e, blk_used, idx, xg, w_gate, w_up, w_down)


def _combine(x, ys_ref, route_ref, tm):
    stride = TOP_K * ROWS_PER_TOK
    y0 = jnp.concatenate([ys_ref[pl.ds(c, tm, stride=stride), :] for c in range(ROWS_PER_TOK)], axis=1)
    y1 = jnp.concatenate([ys_ref[pl.ds(ROWS_PER_TOK + c, tm, stride=stride), :]
                          for c in range(ROWS_PER_TOK)], axis=1)
    route = route_ref[...]
    return x + route[:, 2:3] * y0 + route[:, 3:4] * y1


def _odd_proj_kernel(x_ref, ys_ref, route_ref, g_ref, w_ref, cos_ref, sin_ref,
                     xo_ref, q_ref, k_ref, v_ref, gate_ref):
    tm = x_ref.shape[0]
    x = _combine(x_ref[...], ys_ref, route_ref, tm)
    xo_ref[...] = x
    xn = (x * _rms_scale(x) * g_ref[...]).astype(jnp.bfloat16)
    cos = cos_ref[...]
    sin = sin_ref[...]
    nqk = RET_HEADS * RET_QK
    half = RET_QK // 2

    def rope_store(z, dst, scale):
        for h in range(RET_HEADS):
            x1 = z[:, h * RET_QK:h * RET_QK + half]
            x2 = z[:, h * RET_QK + half:(h + 1) * RET_QK]
            dst[:, h * RET_QK:h * RET_QK + half] = ((x1 * cos - x2 * sin) * scale).astype(dst.dtype)
            dst[:, h * RET_QK + half:(h + 1) * RET_QK] = ((x1 * sin + x2 * cos) * scale).astype(dst.dtype)

    zq = jnp.dot(xn, w_ref[:, 0:nqk], preferred_element_type=jnp.float32)
    rope_store(zq, q_ref, 1.0)
    zk = jnp.dot(xn, w_ref[:, nqk:2 * nqk], preferred_element_type=jnp.float32)
    rope_store(zk, k_ref, RET_QK ** -0.5)
    nv = RET_HEADS * RET_V
    zv = jnp.dot(xn, w_ref[:, 2 * nqk:2 * nqk + nv], preferred_element_type=jnp.float32)
    v_ref[...] = zv.astype(v_ref.dtype)
    zg = jnp.dot(xn, w_ref[:, 2 * nqk + nv:2 * nqk + 2 * nv], preferred_element_type=jnp.float32)
    gate_ref[...] = (zg * jax.nn.sigmoid(zg)).astype(gate_ref.dtype)


def _ret_rope_tables(s):
    half = RET_QK // 2
    inv = ROPE_BASE ** (-jnp.arange(half, dtype=jnp.float32) / half)
    ang = jnp.arange(s, dtype=jnp.float32)[:, None] * inv[None, :]
    return jnp.cos(ang), jnp.sin(ang)


def _odd_proj(x2d, ys, route, seq, g, w_in):
    t = x2d.shape[0]
    tm = TM_PROJ
    tiles_per_seq = seq // tm
    cos, sin = _ret_rope_tables(seq)
    nqk = RET_HEADS * RET_QK
    nv = RET_HEADS * RET_V
    n_in = w_in.shape[1]
    const = lambda shape: pl.BlockSpec(shape, lambda i: (0,) * len(shape))
    tok = lambda n: pl.BlockSpec((tm, n), lambda i: (i, 0))
    return pl.pallas_call(
        _odd_proj_kernel,
        grid=(t // tm,),
        in_specs=[
            tok(D_MODEL),
            pl.BlockSpec((tm * TOP_K * ROWS_PER_TOK, LANES), lambda i: (i, 0)),
            tok(LANES),
            const((1, D_MODEL)), const((D_MODEL, n_in)),
            pl.BlockSpec((tm, LANES), lambda i: (i % tiles_per_seq, 0)),
            pl.BlockSpec((tm, LANES), lambda i: (i % tiles_per_seq, 0)),
        ],
        out_specs=[tok(D_MODEL), tok(nqk), tok(nqk), tok(nv), tok(nv)],
        out_shape=[
            jax.ShapeDtypeStruct((t, D_MODEL), jnp.float32),
            jax.ShapeDtypeStruct((t, nqk), jnp.bfloat16),
            jax.ShapeDtypeStruct((t, nqk), jnp.bfloat16),
            jax.ShapeDtypeStruct((t, nv), jnp.bfloat16),
            jax.ShapeDtypeStruct((t, nv), jnp.bfloat16),
        ],
        compiler_params=_cparams(("arbitrary",)),
        name="odd_proj",
    )(x2d, ys, route, g, w_in, cos, sin)


def _retention_kernel(q_ref, k_ref, v_ref, gate_ref, dmask_ref, xi_ref, zeta_ref, gc_ref, o_ref, state_ref):
    n = pl.program_id(2)

    @pl.when(n == 0)
    def _():
        state_ref[...] = jnp.zeros_like(state_ref)

    xi = xi_ref[...][:, 0:1]
    zeta = zeta_ref[...][:, 0:1]
    g_c = gc_ref[...][:, 0:1]
    for j in range(q_ref.shape[0] // RET_C):
        rows = slice(j * RET_C, (j + 1) * RET_C)
        q = q_ref[rows, :]
        k = k_ref[rows, :]
        v = v_ref[rows, :]
        inner = lax.dot_general(q, k, (((1,), (1,)), ((), ())),
                                preferred_element_type=jnp.float32) * dmask_ref[...]
        o = jnp.dot(inner.astype(jnp.bfloat16), v, preferred_element_type=jnp.float32)
        state = state_ref[...]
        cross = jnp.dot(q, state.astype(jnp.bfloat16), preferred_element_type=jnp.float32)
        o = o + cross * xi
        kz = (k.astype(jnp.float32) * zeta).astype(jnp.bfloat16)
        upd = lax.dot_general(kz, v, (((0,), (0,)), ((), ())), preferred_element_type=jnp.float32)
        state_ref[...] = g_c * state + upd
        o = o * _rms_scale(o)
        o_ref[rows, :] = (o * gate_ref[rows, :].astype(jnp.float32)).astype(o_ref.dtype)


def _retention_tables():
    c = RET_C
    log_g = jnp.log(1.0 - jnp.exp2(-5.0 - jnp.arange(RET_HEADS, dtype=jnp.float32)))
    j = jnp.arange(c, dtype=jnp.float32)
    diff = j[:, None] - j[None, :]
    dmask = jnp.where(diff >= 0, jnp.exp(jnp.maximum(diff, 0.0)[None] * log_g[:, None, None]), 0.0)
    xi = jnp.exp((j[None, :] + 1.0) * log_g[:, None])
    zeta = jnp.exp((c - 1.0 - j)[None, :] * log_g[:, None])
    g_c = jnp.exp(c * log_g)
    bc = lambda a: jnp.broadcast_to(a[..., None], a.shape + (LANES,))
    return dmask, bc(xi), bc(zeta), bc(g_c[:, None])


def _retention(q, k, v, gate, batch, seq):
    nc = seq // RET_BLK
    dmask, xi, zeta, g_c = _retention_tables()
    row = lambda n: pl.BlockSpec((RET_BLK, n), lambda b, h, c: (b * nc + c, h))
    head = lambda shape: pl.BlockSpec((None,) + shape, lambda b, h, c: (h, 0, 0))
    return pl.pallas_call(
        _retention_kernel,
        grid=(batch, RET_HEADS, nc),
        in_specs=[row(RET_QK), row(RET_QK), row(RET_V), row(RET_V),
                  head((RET_C, RET_C)), head((RET_C, LANES)), head((RET_C, LANES)), head((1, LANES))],
        out_specs=row(RET_V),
        out_shape=jax.ShapeDtypeStruct((batch * seq, RET_HEADS * RET_V), jnp.bfloat16),
        scratch_shapes=[pltpu.VMEM((RET_QK, RET_V), jnp.float32)],
        compiler_params=_cparams(("arbitrary",) * 3),
        name="retention",
    )(q, k, v, gate, dmask, xi, zeta, g_c)


def _final_kernel(x_ref, ys_ref, route_ref, g_ref, o_ref):
    x = _combine(x_ref[...], ys_ref, route_ref, x_ref.shape[0])
    o_ref[...] = x * _rms_scale(x) * g_ref[...]


def _final(x2d, ys, route, g):
    t = x2d.shape[0]
    tm = TM_OUT
    return pl.pallas_call(
        _final_kernel,
        grid=(t // tm,),
        in_specs=[
            pl.BlockSpec((tm, D_MODEL), lambda i: (i, 0)),
            pl.BlockSpec((tm * TOP_K * ROWS_PER_TOK, LANES), lambda i: (i, 0)),
            pl.BlockSpec((tm, LANES), lambda i: (i, 0)),
            pl.BlockSpec((1, D_MODEL), lambda i: (0, 0)),
        ],
        out_specs=pl.BlockSpec((tm, D_MODEL), lambda i: (i, 0)),
        out_shape=jax.ShapeDtypeStruct((t, D_MODEL), jnp.float32),
        compiler_params=_cparams(("arbitrary",)),
        name="final_norm",
    )(x2d, ys, route, g)


def kernel(x, ln_mix_e, w_in_e, ln_q_e, w_uq_e, ln_kv_e, w_ukv_e, b_f_e, w_out_e, ln_mix_o, w_in_o,
           w_out_o, ln_ffn, w_rg, b_rg, w_re, b_re, w_gate, w_up, w_down, ln_f):
    batch, seq, d = x.shape
    t = batch * seq
    assert d == D_MODEL and seq % TK == 0 and seq % TQ == 0 and seq % RET_BLK == 0
    assert seq % TM_PROJ == 0 and t % TM_OUT == 0 and t % TE == 0
    bf16 = jnp.bfloat16
    x2d = x.reshape(t, d)

    w_in_p, w_uq_p, w_ukv_p, b_f_p = _even_weights(w_in_e[0], w_uq_e[0], w_ukv_e[0], b_f_e[0])
    q, k, vt = _even_proj(x2d, seq, ln_mix_e[0][None, :], w_in_p, ln_q_e[0][None, :], w_uq_p,
                         ln_kv_e[0][None, :], w_ukv_p, b_f_p)
    o_t = (_attention(q, k, vt, batch, seq, 0, TQ, TK), _attention(q, k, vt, batch, seq, 1, TQ, TK))
    w_r, b_r = _router_weights(w_rg[0], b_rg[0], w_re[0], b_re[0])
    x2d, xg, route = _out_proj(x2d, o_t, w_out_e[0].astype(bf16), ln_ffn[0][None, :], w_r, b_r, seq)
    ys = _moe(xg, route[:, 0:TOP_K].astype(jnp.int32), w_gate, w_up, w_down, 0, t, TE, False)

    x2d, rq, rk, rv, rg = _odd_proj(x2d, ys, route, seq, ln_mix_o[0][None, :], w_in_o[0].astype(bf16))
    og = _retention(rq, rk, rv, rg, batch, seq)
    w_r, b_r = _router_weights(w_rg[1], b_rg[1], w_re[1], b_re[1])
    x2d, xg, route = _out_proj(x2d, og, w_out_o[0].astype(bf16), ln_ffn[1][None, :], w_r, b_r, seq)
    ys = _moe(xg, route[:, 0:TOP_K].astype(jnp.int32), w_gate, w_up, w_down, 1, t, TE, True)

    out = _final(x2d, ys, route, ln_f[None, :])
    return out.reshape(batch, seq, d)
```

```python
import functools
import math

import numpy as np
import jax
import jax.numpy as jnp
from jax import lax
from jax.experimental import pallas as pl
from jax.experimental.pallas import tpu as pltpu

D_MODEL = 1024
CHUNK = 64
ROPE_BASE = 10000.0
EPS = 1e-6
MLA_HEADS = 8
MLA_NOPE = 64
MLA_ROPE = 32
MLA_V = 64
MLA_Q_LORA = 256
MLA_KV_LORA = 128
FOX_HEADS = 8
FOX_DIM = 64
RET_HEADS = 4
RET_QK = 256
RET_V = 512
N_GROUPS = 4
EXPERTS_PER_GROUP = 8
N_EXPERTS = N_GROUPS * EXPERTS_PER_GROUP
TOP_K = 2
D_EXPERT = 512

LANES = 128
SUBLANES = 8
VMEM_LIMIT = 52 * 1024 * 1024

TM_PROJ = 256
TM_OUT = 512
OUT_SPLIT = 2
TQ = 1024
VT_ROWS = 80
TK = 1024
LOG2E = math.log2(math.e)
RET_C = 256
RET_BLK = 2048
TE = 256
ROWS_PER_TOK = D_MODEL // LANES

N_PAIRS = MLA_HEADS // 2
PAIR_W = 256
ROPE_H = MLA_ROPE // 2
NEG = -1e30

_X1_OFF = 0
_X2_OFF = 64
_AUG_W = 6

_C_Q = 0
_C_KV = _C_Q + MLA_Q_LORA
_C_KPE = _C_KV + MLA_KV_LORA
_C_QF = _C_KPE + LANES
_C_KF = _C_QF + FOX_HEADS * FOX_DIM
_C_VF = _C_KF + FOX_HEADS * FOX_DIM
_C_F = _C_VF + FOX_HEADS * FOX_DIM
EVEN_N = _C_F + LANES


def _cparams(sem):
    return pltpu.CompilerParams(dimension_semantics=sem, vmem_limit_bytes=VMEM_LIMIT)


def _rms_scale(x):
    return lax.rsqrt(jnp.mean(x * x, axis=-1, keepdims=True) + EPS)


def _split3(c):
    hi = c.astype(jnp.bfloat16)
    r1 = c - hi.astype(jnp.float32)
    mid = r1.astype(jnp.bfloat16)
    r2 = r1 - mid.astype(jnp.float32)
    lo = r2.astype(jnp.bfloat16)
    return hi, mid, lo


def _even_proj_kernel(tiles_per_seq, x_ref, g_ref, win_ref, lnq_ref, wuq_ref, lnkv_ref, wukv_ref,
                      bf_ref, cos_ref, sin_ref, tri_ref, pq_ref, pk_ref, oq_ref, ok_ref,
                      q_ref, k_ref, vt_ref, carry_ref):
    i = pl.program_id(0)
    x = x_ref[...]
    xn = (x * _rms_scale(x) * g_ref[...]).astype(jnp.bfloat16)
    z = jnp.dot(xn, win_ref[...], preferred_element_type=jnp.float32)
    cos = cos_ref[...]
    sin = sin_ref[...]

    def rope_slab(s):
        return s * cos + pltpu.roll(s, 64, axis=1) * sin

    c_q = z[:, _C_Q:_C_Q + MLA_Q_LORA]
    cqn = (c_q * _rms_scale(c_q) * lnq_ref[...]).astype(jnp.bfloat16)
    q = jnp.dot(cqn, wuq_ref[...], preferred_element_type=jnp.float32)
    scale_a = (MLA_NOPE + MLA_ROPE) ** -0.5 * LOG2E
    for p in range(N_PAIRS):
        lo = p * PAIR_W
        q_ref[0, :, lo:lo + LANES] = (q[:, lo:lo + LANES] * scale_a).astype(jnp.bfloat16)
        q_ref[0, :, lo + LANES:lo + PAIR_W] = (
            rope_slab(q[:, lo + LANES:lo + PAIR_W]) * scale_a).astype(jnp.bfloat16)

    c_kv = z[:, _C_KV:_C_KV + MLA_KV_LORA]
    ckn = (c_kv * _rms_scale(c_kv) * lnkv_ref[...]).astype(jnp.bfloat16)
    kv = jnp.dot(ckn, wukv_ref[...], preferred_element_type=jnp.float32)
    kpe = rope_slab(z[:, _C_KPE:_C_KPE + LANES]).astype(jnp.bfloat16)
    nk = MLA_HEADS * MLA_NOPE
    for p in range(N_PAIRS):
        lo = p * PAIR_W
        k_ref[0, :, lo:lo + LANES] = kv[:, p * LANES:(p + 1) * LANES].astype(jnp.bfloat16)
        k_ref[0, :, lo + LANES:lo + PAIR_W] = kpe
    _store_values_transposed(vt_ref, 0, kv[:, nk:nk + MLA_HEADS * MLA_V])

    lane = lax.broadcasted_iota(jnp.int32, (1, LANES), 1)
    fz = z[:, _C_F:_C_F + LANES] + bf_ref[...]
    log_f = -(jnp.maximum(-fz, 0.0) + jnp.log1p(jnp.exp(-jnp.abs(fz))))
    log_f = jnp.where(lane < FOX_HEADS, log_f, 0.0)

    @pl.when(i % tiles_per_seq == 0)
    def _():
        carry_ref[...] = jnp.zeros_like(carry_ref)

    hi, mid, lo3 = _split3(log_f)
    tri = tri_ref[...]
    cum = (jnp.dot(tri, hi, preferred_element_type=jnp.float32)
           + jnp.dot(tri, mid, preferred_element_type=jnp.float32)
           + jnp.dot(tri, lo3, preferred_element_type=jnp.float32)) + carry_ref[...]
    tm = cum.shape[0]
    carry_ref[...] = cum[tm - 1:tm, :]

    parts = jnp.concatenate(_split3(cum * LOG2E), axis=1)
    aug_q = jnp.dot(parts, pq_ref[...], preferred_element_type=jnp.float32) + oq_ref[...]
    aug_k = jnp.dot(parts, pk_ref[...], preferred_element_type=jnp.float32) + ok_ref[...]
    scale_b = FOX_DIM ** -0.5 * LOG2E
    for p in range(N_PAIRS):
        lo = p * PAIR_W
        q_ref[1, :, lo:lo + LANES] = (
            z[:, _C_QF + p * LANES:_C_QF + (p + 1) * LANES] * scale_b).astype(jnp.bfloat16)
        q_ref[1, :, lo + LANES:lo + PAIR_W] = aug_q[:, p * LANES:(p + 1) * LANES].astype(jnp.bfloat16)
        k_ref[1, :, lo:lo + LANES] = z[:, _C_KF + p * LANES:_C_KF + (p + 1) * LANES].astype(jnp.bfloat16)
        k_ref[1, :, lo + LANES:lo + PAIR_W] = aug_k[:, p * LANES:(p + 1) * LANES].astype(jnp.bfloat16)
    _store_values_transposed(vt_ref, 1, z[:, _C_VF:_C_VF + FOX_HEADS * FOX_DIM])


def _store_values_transposed(vt_ref, g, v):
    vt = v.T
    ones = jnp.ones((VT_ROWS - MLA_V, vt.shape[1]), vt_ref.dtype)
    for h in range(MLA_HEADS):
        vt_ref[g, h * VT_ROWS:h * VT_ROWS + MLA_V, :] = vt[h * MLA_V:(h + 1) * MLA_V].astype(vt_ref.dtype)
        vt_ref[g, h * VT_ROWS + MLA_V:(h + 1) * VT_ROWS, :] = ones


def _even_weights(w_in, w_uq, w_ukv, b_f):
    d = w_in.shape[0]
    zeros = lambda n: jnp.zeros((d, n), w_in.dtype)
    o_cq, o_ckv = 0, MLA_Q_LORA
    o_kpe = o_ckv + MLA_KV_LORA
    o_qf = o_kpe + MLA_ROPE
    o_kf = o_qf + FOX_HEADS * FOX_DIM
    o_vf = o_kf + FOX_HEADS * FOX_DIM
    o_f = o_vf + FOX_HEADS * FOX_DIM
    kpe1 = w_in[:, o_kpe:o_kpe + ROPE_H]
    kpe2 = w_in[:, o_kpe + ROPE_H:o_kpe + MLA_ROPE]
    kpe_slab = jnp.concatenate([kpe1, kpe1, zeros(32), kpe2, kpe2, zeros(32)], axis=1)
    f_slab = jnp.concatenate([w_in[:, o_f:o_f + FOX_HEADS], zeros(LANES - FOX_HEADS)], axis=1)
    w_in_p = jnp.concatenate([w_in[:, o_cq:o_kpe], kpe_slab, w_in[:, o_qf:o_f], f_slab], axis=1)

    dq = w_uq.shape[0]
    zq = lambda n: jnp.zeros((dq, n), w_uq.dtype)
    hd = MLA_NOPE + MLA_ROPE
    blocks = []
    for p in range(N_PAIRS):
        h0, h1 = 2 * p, 2 * p + 1
        nope = lambda h: w_uq[:, h * hd:h * hd + MLA_NOPE]
        r1 = lambda h: w_uq[:, h * hd + MLA_NOPE:h * hd + MLA_NOPE + ROPE_H]
        r2 = lambda h: w_uq[:, h * hd + MLA_NOPE + ROPE_H:(h + 1) * hd]
        blocks += [nope(h0), nope(h1), r1(h0), r1(h1), zq(32), r2(h0), r2(h1), zq(32)]
    w_uq_p = jnp.concatenate(blocks, axis=1)

    kvd = MLA_NOPE + MLA_V
    k_cols = [w_ukv[:, h * kvd:h * kvd + MLA_NOPE] for h in range(MLA_HEADS)]
    v_cols = [w_ukv[:, h * kvd + MLA_NOPE:(h + 1) * kvd] for h in range(MLA_HEADS)]
    w_ukv_p = jnp.concatenate(k_cols + v_cols, axis=1)
    b_f_p = jnp.concatenate([b_f, jnp.zeros((LANES - FOX_HEADS,), b_f.dtype)])[None, :]
    bf16 = jnp.bfloat16
    return w_in_p.astype(bf16), w_uq_p.astype(bf16), w_ukv_p.astype(bf16), b_f_p


def _aug_placement():
    pq = np.zeros((3 * LANES, N_PAIRS * LANES), np.float32)
    pk = np.zeros((3 * LANES, N_PAIRS * LANES), np.float32)
    oq = np.zeros((1, N_PAIRS * LANES), np.float32)
    ok = np.zeros((1, N_PAIRS * LANES), np.float32)
    for p in range(N_PAIRS):
        for j in range(2):
            h = 2 * p + j
            base = p * LANES + _X1_OFF + j * ROPE_H
            for t in range(3):
                pq[t * LANES + h, base + 3 + t] = 1.0
                pk[t * LANES + h, base + t] = -1.0
                oq[0, base + t] = 1.0
                ok[0, base + 3 + t] = 1.0
    return (jnp.asarray(pq, jnp.bfloat16), jnp.asarray(pk, jnp.bfloat16),
            jnp.asarray(oq), jnp.asarray(ok))


def _mla_rope_tables(s):
    inv = ROPE_BASE ** (-jnp.arange(ROPE_H, dtype=jnp.float32) / ROPE_H)
    ang = jnp.arange(s, dtype=jnp.float32)[:, None] * inv[None, :]
    c, sn = jnp.cos(ang), jnp.sin(ang)
    z = jnp.zeros((s, 32), jnp.float32)
    cos = jnp.concatenate([c, c, z, c, c, z], axis=1)
    sin = jnp.concatenate([-sn, -sn, z, sn, sn, z], axis=1)
    return cos, sin


def _even_proj(x2d, seq, g, w_in_p, ln_q, w_uq_p, ln_kv, w_ukv_p, b_f_p):
    t = x2d.shape[0]
    tm = TM_OUT
    tiles_per_seq = seq // tm
    cos, sin = _mla_rope_tables(seq)
    tri = jnp.asarray(np.tril(np.ones((tm, tm), np.float32)), jnp.bfloat16)
    pq, pk, oq, ok = _aug_placement()
    const = lambda shape: pl.BlockSpec(shape, lambda i: (0,) * len(shape))
    return pl.pallas_call(
        functools.partial(_even_proj_kernel, tiles_per_seq),
        grid=(t // tm,),
        in_specs=[
            pl.BlockSpec((tm, D_MODEL), lambda i: (i, 0)),
            const((1, D_MODEL)), const((D_MODEL, EVEN_N)),
            const((1, MLA_Q_LORA)), const((MLA_Q_LORA, N_PAIRS * PAIR_W)),
            const((1, MLA_KV_LORA)), const((MLA_KV_LORA, 2 * MLA_HEADS * MLA_NOPE)),
            const((1, LANES)),
            pl.BlockSpec((tm, LANES), lambda i: (i % tiles_per_seq, 0)),
            pl.BlockSpec((tm, LANES), lambda i: (i % tiles_per_seq, 0)),
            const((tm, tm)), const(pq.shape), const(pk.shape), const(oq.shape), const(ok.shape),
        ],
        out_specs=[
            pl.BlockSpec((2, tm, N_PAIRS * PAIR_W), lambda i: (0, i, 0)),
            pl.BlockSpec((2, tm, N_PAIRS * PAIR_W), lambda i: (0, i, 0)),
            pl.BlockSpec((2, None, MLA_HEADS * VT_ROWS, tm),
                         lambda i: (0, i // tiles_per_seq, 0, i % tiles_per_seq)),
        ],
        out_shape=[
            jax.ShapeDtypeStruct((2, t, N_PAIRS * PAIR_W), jnp.bfloat16),
            jax.ShapeDtypeStruct((2, t, N_PAIRS * PAIR_W), jnp.bfloat16),
            jax.ShapeDtypeStruct((2, t // seq, MLA_HEADS * VT_ROWS, seq), jnp.bfloat16),
        ],
        scratch_shapes=[pltpu.VMEM((1, LANES), jnp.float32)],
        compiler_params=_cparams(("arbitrary",)),
        name="even_proj",
    )(x2d, g, w_in_p, ln_q, w_uq_p, ln_kv, w_ukv_p, b_f_p, cos, sin, tri, pq, pk, oq, ok)


def _attn_kernel(group, tk, q_ref, k_ref, vt_ref, o_ref, qcat_ref, s0_ref, s1_ref, m0_ref, m1_ref,
                 a0_ref, a1_ref, acc_ref):
    i = pl.program_id(2)
    tq = q_ref.shape[0]
    s_bufs, m_bufs, a_bufs = (s0_ref, s1_ref), (m0_ref, m1_ref), (a0_ref, a1_ref)

    lane = lax.broadcasted_iota(jnp.int32, (1, PAIR_W), 1)

    def head_mask(j):
        a = (lane >= j * MLA_NOPE) & (lane < (j + 1) * MLA_NOPE)
        b = (lane >= LANES + _X1_OFF + j * ROPE_H) & (lane < LANES + _X1_OFF + (j + 1) * ROPE_H)
        c = (lane >= LANES + _X2_OFF + j * ROPE_H) & (lane < LANES + _X2_OFF + (j + 1) * ROPE_H)
        return a | b | c

    q = q_ref[...]
    zero = jnp.zeros_like(q)
    qcat_ref[0:tq, :] = jnp.where(head_mask(0), q, zero)
    qcat_ref[tq:2 * tq, :] = jnp.where(head_mask(1), q, zero)
    acc_ref[...] = jnp.zeros_like(acc_ref)

    q0 = i * tq
    n_chunks = q0 // tk + 1

    def scores(c, m_run, par, masked):
        kstart = pl.multiple_of(c * tk, tk)
        s = lax.dot_general(k_ref[pl.ds(kstart, tk), :], qcat_ref[...], (((1,), (1,)), ((), ())),
                            preferred_element_type=jnp.float32)
        if masked:
            col = lax.broadcasted_iota(jnp.int32, (1, 2 * tq), 1)
            qpos = q0 + jnp.where(col >= tq, col - tq, col)
            qlim = (qpos | (CHUNK - 1)) if group == 0 else qpos
            kpos = kstart + lax.broadcasted_iota(jnp.int32, (tk, 1), 0)
            s = jnp.where(kpos <= qlim, s, NEG)
        s_bufs[par][...] = s
        m_new = jnp.maximum(m_run, jnp.max(s, axis=0, keepdims=True))
        m_bufs[par][...] = m_new
        a_bufs[par][...] = jnp.exp2(m_run - m_new)

    def accumulate(c, par):
        kstart = pl.multiple_of(c * tk, tk)
        alpha = a_bufs[par][...]
        p = jnp.exp2(s_bufs[par][...] - m_bufs[par][...]).astype(jnp.bfloat16)
        for h in range(2):
            cols = slice(h * tq, (h + 1) * tq)
            acc_ref[h] = alpha[:, cols] * acc_ref[h] + jnp.dot(
                vt_ref[h * VT_ROWS:(h + 1) * VT_ROWS, pl.ds(kstart, tk)], p[:, cols],
                preferred_element_type=jnp.float32)

    def stage(c, par, masked):
        scores(c + 1, m_bufs[par][...], 1 - par, masked)
        accumulate(c, par)

    m_init = jnp.full((1, 2 * tq), NEG, jnp.float32)

    @pl.when(n_chunks == 1)
    def _():
        scores(0, m_init, 0, True)
        accumulate(0, 0)

    @pl.when(n_chunks > 1)
    def _():
        scores(0, m_init, 0, False)

    n_pairs = jnp.maximum(n_chunks - 2, 0) // 2

    def body(j, carry):
        for u in range(4):
            stage(4 * j + u, u % 2, False)
        return carry

    lax.fori_loop(0, n_pairs // 2, body, 0)

    @pl.when(n_pairs % 2 == 1)
    def _():
        stage(2 * n_pairs - 2, 0, False)
        stage(2 * n_pairs - 1, 1, False)
    c0 = 2 * n_pairs
    left = n_chunks - 1 - c0

    @pl.when((n_chunks > 1) & (left == 1))
    def _():
        stage(c0, 0, True)
        accumulate(c0 + 1, 1)

    @pl.when((n_chunks > 1) & (left == 2))
    def _():
        stage(c0, 0, False)
        stage(c0 + 1, 1, True)
        accumulate(c0 + 2, 0)

    for h in range(2):
        acc = acc_ref[h]
        o_ref[h * MLA_V:(h + 1) * MLA_V, :] = (acc[0:MLA_V] / acc[MLA_V:MLA_V + 1]).astype(o_ref.dtype)


def _attention(q, k, vt, batch, seq, group, tq, tk):
    nq = seq // tq
    f32 = jnp.float32
    return pl.pallas_call(
        functools.partial(_attn_kernel, group, tk),
        grid=(batch, N_PAIRS, nq),
        in_specs=[
            pl.BlockSpec((None, tq, PAIR_W), lambda b, p, i: (group, b * nq + i, p)),
            pl.BlockSpec((None, seq, PAIR_W), lambda b, p, i: (group, b, p)),
            pl.BlockSpec((None, None, 2 * VT_ROWS, seq), lambda b, p, i: (group, b, p, 0)),
        ],
        out_specs=pl.BlockSpec((None, LANES, tq), lambda b, p, i: (b, p, i)),
        out_shape=jax.ShapeDtypeStruct((batch, N_PAIRS * LANES, seq), jnp.bfloat16),
        scratch_shapes=[pltpu.VMEM((2 * tq, PAIR_W), jnp.bfloat16),
                        pltpu.VMEM((tk, 2 * tq), f32), pltpu.VMEM((tk, 2 * tq), f32),
                        pltpu.VMEM((1, 2 * tq), f32), pltpu.VMEM((1, 2 * tq), f32),
                        pltpu.VMEM((1, 2 * tq), f32), pltpu.VMEM((1, 2 * tq), f32),
                        pltpu.VMEM((2, VT_ROWS, tq), f32)],
        compiler_params=_cparams(("arbitrary",) * 3),
        name="attention_g%d" % group,
    )(q, k, vt)


def _router_t(lt):
    row = lax.broadcasted_iota(jnp.int32, lt.shape, 0)
    big = jnp.int32(LANES)
    ninf = -jnp.inf
    gl = jnp.where(row < N_GROUPS, lt, ninf)
    gmax = jnp.max(gl, axis=0, keepdims=True)
    gsel = jnp.min(jnp.where(gl == gmax, row, big), axis=0, keepdims=True)
    gsum = jnp.sum(jnp.where(row < N_GROUPS, jnp.exp(lt - gmax), 0.0), axis=0, keepdims=True)
    p_g = 1.0 / gsum
    lo = N_GROUPS + EXPERTS_PER_GROUP * gsel
    e = jnp.where((row >= lo) & (row < lo + EXPERTS_PER_GROUP), lt, ninf)
    m1 = jnp.max(e, axis=0, keepdims=True)
    i1 = jnp.min(jnp.where(e == m1, row, big), axis=0, keepdims=True)
    e2 = jnp.where(row == i1, ninf, e)
    m2 = jnp.max(e2, axis=0, keepdims=True)
    i2 = jnp.min(jnp.where(e2 == m2, row, big), axis=0, keepdims=True)
    a2 = jnp.exp(m2 - m1)
    den = 1.0 + a2
    g1 = p_g / den
    g2 = p_g * a2 / den
    f = lambda v: v.astype(jnp.float32)
    return jnp.where(row == 0, f(i1 - N_GROUPS),
                     jnp.where(row == 1, f(i2 - N_GROUPS),
                               jnp.where(row == 2, g1, jnp.where(row == 3, g2, 0.0))))


def _out_proj_kernel(n_o, transposed, x_ref, *refs):
    o_refs = refs[:n_o]
    w_ref, ln_ref, wr_ref, br_ref, xo_ref, xg_ref, route_ref = refs[n_o:]
    tm = x_ref.shape[0]
    hm = tm // OUT_SPLIT
    for part in range(OUT_SPLIT):
        rows = slice(part * hm, (part + 1) * hm)
        if transposed:
            y = x_ref[rows, :]
            nf = o_refs[0].shape[0]
            for g, o_ref in enumerate(o_refs):
                y = y + lax.dot_general(o_ref[:, rows], w_ref[g * nf:(g + 1) * nf, :],
                                        (((0,), (0,)), ((), ())), preferred_element_type=jnp.float32)
        else:
            y = x_ref[rows, :] + jnp.dot(o_refs[0][rows, :], w_ref[...], preferred_element_type=jnp.float32)
        xo_ref[rows, :] = y
        xn = y * _rms_scale(y) * ln_ref[...]
        hi = xn.astype(jnp.bfloat16)
        lo = (xn - hi.astype(jnp.float32)).astype(jnp.bfloat16)
        l2 = jnp.dot(jnp.concatenate([hi, lo], axis=1), wr_ref[...], preferred_element_type=jnp.float32)
        lt = (l2[:, :LANES] + l2[:, LANES:] + br_ref[...]).T
        route_ref[rows, :] = _router_t(lt).T
        for c in range(ROWS_PER_TOK):
            xg_ref[pl.ds(part * hm * ROWS_PER_TOK + c, hm, stride=ROWS_PER_TOK), :] = (
                xn[:, c * LANES:(c + 1) * LANES])


def _out_proj(x2d, o, w_out, ln, w_r, b_r, seq):
    t = x2d.shape[0]
    kdim = w_out.shape[0]
    tm = TM_OUT
    transposed = isinstance(o, tuple)
    tiles_per_seq = seq // tm
    if transposed:
        o_specs = [pl.BlockSpec((None, a.shape[1], tm), lambda i: (i // tiles_per_seq, 0, i % tiles_per_seq))
                   for a in o]
    else:
        o, o_specs = (o,), [pl.BlockSpec((tm, kdim), lambda i: (i, 0))]
    const = lambda shape: pl.BlockSpec(shape, lambda i: (0,) * len(shape))
    return pl.pallas_call(
        functools.partial(_out_proj_kernel, len(o), transposed),
        grid=(t // tm,),
        in_specs=[
            pl.BlockSpec((tm, D_MODEL), lambda i: (i, 0)),
            *o_specs,
            const((kdim, D_MODEL)), const((1, D_MODEL)), const((2 * D_MODEL, 2 * LANES)), const((1, LANES)),
        ],
        out_specs=[
            pl.BlockSpec((tm, D_MODEL), lambda i: (i, 0)),
            pl.BlockSpec((tm * ROWS_PER_TOK, LANES), lambda i: (i, 0)),
            pl.BlockSpec((tm, LANES), lambda i: (i, 0)),
        ],
        out_shape=[
            jax.ShapeDtypeStruct((t, D_MODEL), jnp.float32),
            jax.ShapeDtypeStruct((t * ROWS_PER_TOK, LANES), jnp.float32),
            jax.ShapeDtypeStruct((t, LANES), jnp.float32),
        ],
        compiler_params=_cparams(("arbitrary",)),
        name="out_proj_router",
    )(x2d, *o, w_out, ln, w_r, b_r)


def _router_weights(w_rg, b_rg, w_re, b_re):
    d = w_rg.shape[0]
    pad = LANES - N_GROUPS - N_EXPERTS
    w_r = jnp.concatenate([w_rg, w_re, jnp.zeros((d, pad), w_rg.dtype)], axis=1)
    b_r = jnp.concatenate([b_rg, b_re, jnp.zeros((pad,), b_rg.dtype)])[None, :]
    w_hi = w_r.astype(jnp.bfloat16)
    w_lo = (w_r - w_hi.astype(jnp.float32)).astype(jnp.bfloat16)
    half = jnp.concatenate([w_hi, w_lo], axis=1)
    return jnp.concatenate([half, half], axis=0), b_r


def _dispatch_plan(eid, t, te):
    a = t * TOP_K
    nb = a // te + N_EXPERTS
    e_flat = eid.reshape(a)
    experts = jnp.arange(N_EXPERTS, dtype=jnp.int32)
    _, order = lax.sort((e_flat, jnp.arange(a, dtype=jnp.int32)), num_keys=1, is_stable=True)
    counts = jnp.sum((e_flat[:, None] == experts[None, :]).astype(jnp.int32), axis=0)
    start = jnp.cumsum(counts) - counts
    padded = ((counts + te - 1) // te) * te
    pend = jnp.cumsum(padded)
    pstart = pend - padded
    blk0 = jnp.arange(nb, dtype=jnp.int32) * te
    blk_e = jnp.minimum(jnp.sum((blk0[:, None] >= pend[None, :]).astype(jnp.int32), axis=1), N_EXPERTS - 1)
    blk_used = (blk0 < pend[-1]).astype(jnp.int32)
    onehot = (blk_e[:, None] == experts[None, :]).astype(jnp.int32)
    per_blk = lambda v: jnp.sum(onehot * v[None, :], axis=1)[:, None]
    r = jnp.arange(te, dtype=jnp.int32)[None, :]
    off = blk0[:, None] + r - per_blk(pstart)
    valid = (off < per_blk(counts)) & (blk_used[:, None] > 0)
    src = jnp.where(valid, per_blk(start) + off, 0)
    assign = order[src]
    src_tok = jnp.where(valid, assign // TOP_K, 0).astype(jnp.int32)
    dst_row = jnp.where(valid, assign, a + r).astype(jnp.int32)
    nxt = jnp.minimum(jnp.arange(nb) + 1, nb - 1)
    return blk_e, blk_used, jnp.concatenate([src_tok, src_tok[nxt], dst_row], axis=1)[:, None, :]


def _moe_kernel(te, two_out, blk_e_ref, blk_used_ref, idx_ref, xg_ref, wg_ref, wu_ref, wd_ref,
                ys_ref, xbuf, ybuf, wg_bf, wu_bf, wd_bf, gsem, ssem):
    i = pl.program_id(0)
    nb = pl.num_programs(0)
    used = blk_used_ref[i] > 0
    slot = i % 2
    block_rows = te * ROWS_PER_TOK

    def start_gather(first, s):
        for r in range(te):
            tok = idx_ref[0, 0, first + r]
            pltpu.make_async_copy(
                xg_ref.at[pl.ds(pl.multiple_of(tok * ROWS_PER_TOK, ROWS_PER_TOK), ROWS_PER_TOK), :],
                xbuf.at[s, pl.ds(r * ROWS_PER_TOK, ROWS_PER_TOK), :],
                gsem.at[s]).start(priority=r % 2)

    def wait_gather(s):
        pltpu.make_async_copy(xg_ref.at[pl.ds(0, block_rows), :], xbuf.at[s], gsem.at[s]).wait()

    oslot = slot if two_out else 0

    def start_scatter():
        for r in range(te):
            row = idx_ref[0, 0, 2 * te + r]
            pltpu.make_async_copy(
                ybuf.at[oslot, pl.ds(r * ROWS_PER_TOK, ROWS_PER_TOK), :],
                ys_ref.at[pl.ds(pl.multiple_of(row * ROWS_PER_TOK, ROWS_PER_TOK), ROWS_PER_TOK), :],
                ssem.at[oslot]).start(priority=r % 2)

    def wait_scatter(s):
        pltpu.make_async_copy(ybuf.at[s], ys_ref.at[pl.ds(0, block_rows), :], ssem.at[s]).wait()

    @pl.when(i == 0)
    def _():
        ybuf[0] = jnp.zeros(ybuf.shape[1:], ybuf.dtype)
        pltpu.make_async_copy(
            ybuf.at[0], ys_ref.at[pl.ds(ys_ref.shape[0] - block_rows, block_rows), :], ssem.at[0]).start()
        start_gather(0, 0)

    @pl.when(used)
    def _():
        start_gather(te, 1 - slot)
        prev_e = blk_e_ref[jnp.maximum(i - 1, 0)]

        @pl.when((i == 0) | (blk_e_ref[i] != prev_e))
        def _():
            wg_bf[...] = wg_ref[...].astype(jnp.bfloat16)
            wu_bf[...] = wu_ref[...].astype(jnp.bfloat16)
            wd_bf[...] = wd_ref[...].astype(jnp.bfloat16)

        wait_gather(slot)
        xb = xbuf.at[slot]
        x = jnp.concatenate([xb[pl.ds(c, te, stride=ROWS_PER_TOK), :] for c in range(ROWS_PER_TOK)],
                            axis=1).astype(jnp.bfloat16)
        hg = jnp.dot(x, wg_bf[...], preferred_element_type=jnp.float32)
        hu = jnp.dot(x, wu_bf[...], preferred_element_type=jnp.float32)
        hid = (hg * jax.nn.sigmoid(hg) * hu).astype(jnp.bfloat16)
        y = jnp.dot(hid, wd_bf[...], preferred_element_type=jnp.float32)

        if two_out:
            @pl.when((i == 0) | (i >= 2))
            def _():
                wait_scatter(oslot)
        else:
            wait_scatter(0)
        yb = ybuf.at[oslot]
        for c in range(ROWS_PER_TOK):
            yb[pl.ds(c, te, stride=ROWS_PER_TOK), :] = y[:, c * LANES:(c + 1) * LANES]
        start_scatter()

        next_used = (i + 1 < nb) & (blk_used_ref[jnp.minimum(i + 1, nb - 1)] > 0)

        @pl.when(jnp.logical_not(next_used))
        def _():
            wait_gather(1 - slot)
            wait_scatter(oslot)
            if two_out:
                @pl.when(i >= 1)
                def _():
                    wait_scatter(1 - oslot)


def _moe(xg, eid, w_gate, w_up, w_down, layer, t, te, two_out):
    a = t * TOP_K
    nb = a // te + N_EXPERTS
    blk_e, blk_used, idx = _dispatch_plan(eid, t, te)
    grid_spec = pltpu.PrefetchScalarGridSpec(
        num_scalar_prefetch=2,
        grid=(nb,),
        in_specs=[
            pl.BlockSpec((1, 1, 3 * te), lambda i, be, bu: (i, 0, 0), memory_space=pltpu.SMEM),
            pl.BlockSpec(memory_space=pl.ANY),
            pl.BlockSpec((None, None, D_MODEL, D_EXPERT), lambda i, be, bu: (layer, be[i], 0, 0)),
            pl.BlockSpec((None, None, D_MODEL, D_EXPERT), lambda i, be, bu: (layer, be[i], 0, 0)),
            pl.BlockSpec((None, None, D_EXPERT, D_MODEL), lambda i, be, bu: (layer, be[i], 0, 0)),
        ],
        out_specs=pl.BlockSpec(memory_space=pl.ANY),
        scratch_shapes=[
            pltpu.VMEM((2, te * ROWS_PER_TOK, LANES), jnp.float32),
            pltpu.VMEM((2, te * ROWS_PER_TOK, LANES), jnp.float32),
            pltpu.VMEM((D_MODEL, D_EXPERT), jnp.bfloat16),
            pltpu.VMEM((D_MODEL, D_EXPERT), jnp.bfloat16),
            pltpu.VMEM((D_EXPERT, D_MODEL), jnp.bfloat16),
            pltpu.SemaphoreType.DMA((2,)),
            pltpu.SemaphoreType.DMA((2,)),
        ],
    )
    return pl.pallas_call(
        functools.partial(_moe_kernel, te, two_out),
        grid_spec=grid_spec,
        out_shape=jax.ShapeDtypeStruct(((a + te) * ROWS_PER_TOK, LANES), jnp.float32),
        compiler_params=_cparams(("arbitrary",)),
        name="moe_experts",
    )(blk_e, blk_used, idx, xg, w_gate, w_up, w_down)


def _combine(x, ys_ref, route_ref, tm):
    stride = TOP_K * ROWS_PER_TOK
    y0 = jnp.concatenate([ys_ref[pl.ds(c, tm, stride=stride), :] for c in range(ROWS_PER_TOK)], axis=1)
    y1 = jnp.concatenate([ys_ref[pl.ds(ROWS_PER_TOK + c, tm, stride=stride), :]
                          for c in range(ROWS_PER_TOK)], axis=1)
    route = route_ref[...]
    return x + route[:, 2:3] * y0 + route[:, 3:4] * y1


def _odd_proj_kernel(x_ref, ys_ref, route_ref, g_ref, w_ref, cos_ref, sin_ref,
                     xo_ref, q_ref, k_ref, v_ref, gate_ref):
    tm = x_ref.shape[0]
    x = _combine(x_ref[...], ys_ref, route_ref, tm)
    xo_ref[...] = x
    xn = (x * _rms_scale(x) * g_ref[...]).astype(jnp.bfloat16)
    cos = cos_ref[...]
    sin = sin_ref[...]
    nqk = RET_HEADS * RET_QK
    half = RET_QK // 2

    def rope_store(z, dst, scale):
        for h in range(RET_HEADS):
            x1 = z[:, h * RET_QK:h * RET_QK + half]
            x2 = z[:, h * RET_QK + half:(h + 1) * RET_QK]
            dst[:, h * RET_QK:h * RET_QK + half] = ((x1 * cos - x2 * sin) * scale).astype(dst.dtype)
            dst[:, h * RET_QK + half:(h + 1) * RET_QK] = ((x1 * sin + x2 * cos) * scale).astype(dst.dtype)

    zq = jnp.dot(xn, w_ref[:, 0:nqk], preferred_element_type=jnp.float32)
    rope_store(zq, q_ref, 1.0)
    zk = jnp.dot(xn, w_ref[:, nqk:2 * nqk], preferred_element_type=jnp.float32)
    rope_store(zk, k_ref, RET_QK ** -0.5)
    nv = RET_HEADS * RET_V
    zv = jnp.dot(xn, w_ref[:, 2 * nqk:2 * nqk + nv], preferred_element_type=jnp.float32)
    v_ref[...] = zv.astype(v_ref.dtype)
    zg = jnp.dot(xn, w_ref[:, 2 * nqk + nv:2 * nqk + 2 * nv], preferred_element_type=jnp.float32)
    gate_ref[...] = (zg * jax.nn.sigmoid(zg)).astype(gate_ref.dtype)


def _ret_rope_tables(s):
    half = RET_QK // 2
    inv = ROPE_BASE ** (-jnp.arange(half, dtype=jnp.float32) / half)
    ang = jnp.arange(s, dtype=jnp.float32)[:, None] * inv[None, :]
    return jnp.cos(ang), jnp.sin(ang)


def _odd_proj(x2d, ys, route, seq, g, w_in):
    t = x2d.shape[0]
    tm = TM_PROJ
    tiles_per_seq = seq // tm
    cos, sin = _ret_rope_tables(seq)
    nqk = RET_HEADS * RET_QK
    nv = RET_HEADS * RET_V
    n_in = w_in.shape[1]
    const = lambda shape: pl.BlockSpec(shape, lambda i: (0,) * len(shape))
    tok = lambda n: pl.BlockSpec((tm, n), lambda i: (i, 0))
    return pl.pallas_call(
        _odd_proj_kernel,
        grid=(t // tm,),
        in_specs=[
            tok(D_MODEL),
            pl.BlockSpec((tm * TOP_K * ROWS_PER_TOK, LANES), lambda i: (i, 0)),
            tok(LANES),
            const((1, D_MODEL)), const((D_MODEL, n_in)),
            pl.BlockSpec((tm, LANES), lambda i: (i % tiles_per_seq, 0)),
            pl.BlockSpec((tm, LANES), lambda i: (i % tiles_per_seq, 0)),
        ],
        out_specs=[tok(D_MODEL), tok(nqk), tok(nqk), tok(nv), tok(nv)],
        out_shape=[
            jax.ShapeDtypeStruct((t, D_MODEL), jnp.float32),
            jax.ShapeDtypeStruct((t, nqk), jnp.bfloat16),
            jax.ShapeDtypeStruct((t, nqk), jnp.bfloat16),
            jax.ShapeDtypeStruct((t, nv), jnp.bfloat16),
            jax.ShapeDtypeStruct((t, nv), jnp.bfloat16),
        ],
        compiler_params=_cparams(("arbitrary",)),
        name="odd_proj",
    )(x2d, ys, route, g, w_in, cos, sin)


def _retention_kernel(q_ref, k_ref, v_ref, gate_ref, dmask_ref, xi_ref, zeta_ref, gc_ref, o_ref, state_ref):
    n = pl.program_id(2)

    @pl.when(n == 0)
    def _():
        state_ref[...] = jnp.zeros_like(state_ref)

    xi = xi_ref[...][:, 0:1]
    zeta = zeta_ref[...][:, 0:1]
    g_c = gc_ref[...][:, 0:1]
    for j in range(q_ref.shape[0] // RET_C):
        rows = slice(j * RET_C, (j + 1) * RET_C)
        q = q_ref[rows, :]
        k = k_ref[rows, :]
        v = v_ref[rows, :]
        inner = lax.dot_general(q, k, (((1,), (1,)), ((), ())),
                                preferred_element_type=jnp.float32) * dmask_ref[...]
        o = jnp.dot(inner.astype(jnp.bfloat16), v, preferred_element_type=jnp.float32)
        state = state_ref[...]
        cross = jnp.dot(q, state.astype(jnp.bfloat16), preferred_element_type=jnp.float32)
        o = o + cross * xi
        kz = (k.astype(jnp.float32) * zeta).astype(jnp.bfloat16)
        upd = lax.dot_general(kz, v, (((0,), (0,)), ((), ())), preferred_element_type=jnp.float32)
        state_ref[...] = g_c * state + upd
        o = o * _rms_scale(o)
        o_ref[rows, :] = (o * gate_ref[rows, :].astype(jnp.float32)).astype(o_ref.dtype)


def _retention_tables():
    c = RET_C
    log_g = jnp.log(1.0 - jnp.exp2(-5.0 - jnp.arange(RET_HEADS, dtype=jnp.float32)))
    j = jnp.arange(c, dtype=jnp.float32)
    diff = j[:, None] - j[None, :]
    dmask = jnp.where(diff >= 0, jnp.exp(jnp.maximum(diff, 0.0)[None] * log_g[:, None, None]), 0.0)
    xi = jnp.exp((j[None, :] + 1.0) * log_g[:, None])
    zeta = jnp.exp((c - 1.0 - j)[None, :] * log_g[:, None])
    g_c = jnp.exp(c * log_g)
    bc = lambda a: jnp.broadcast_to(a[..., None], a.shape + (LANES,))
    return dmask, bc(xi), bc(zeta), bc(g_c[:, None])


def _retention(q, k, v, gate, batch, seq):
    nc = seq // RET_BLK
    dmask, xi, zeta, g_c = _retention_tables()
    row = lambda n: pl.BlockSpec((RET_BLK, n), lambda b, h, c: (b * nc + c, h))
    head = lambda shape: pl.BlockSpec((None,) + shape, lambda b, h, c: (h, 0, 0))
    return pl.pallas_call(
        _retention_kernel,
        grid=(batch, RET_HEADS, nc),
        in_specs=[row(RET_QK), row(RET_QK), row(RET_V), row(RET_V),
                  head((RET_C, RET_C)), head((RET_C, LANES)), head((RET_C, LANES)), head((1, LANES))],
        out_specs=row(RET_V),
        out_shape=jax.ShapeDtypeStruct((batch * seq, RET_HEADS * RET_V), jnp.bfloat16),
        scratch_shapes=[pltpu.VMEM((RET_QK, RET_V), jnp.float32)],
        compiler_params=_cparams(("arbitrary",) * 3),
        name="retention",
    )(q, k, v, gate, dmask, xi, zeta, g_c)


def _final_kernel(x_ref, ys_ref, route_ref, g_ref, o_ref):
    x = _combine(x_ref[...], ys_ref, route_ref, x_ref.shape[0])
    o_ref[...] = x * _rms_scale(x) * g_ref[...]


def _final(x2d, ys, route, g):
    t = x2d.shape[0]
    tm = TM_OUT
    return pl.pallas_call(
        _final_kernel,
        grid=(t // tm,),
        in_specs=[
            pl.BlockSpec((tm, D_MODEL), lambda i: (i, 0)),
            pl.BlockSpec((tm * TOP_K * ROWS_PER_TOK, LANES), lambda i: (i, 0)),
            pl.BlockSpec((tm, LANES), lambda i: (i, 0)),
            pl.BlockSpec((1, D_MODEL), lambda i: (0, 0)),
        ],
        out_specs=pl.BlockSpec((tm, D_MODEL), lambda i: (i, 0)),
        out_shape=jax.ShapeDtypeStruct((t, D_MODEL), jnp.float32),
        compiler_params=_cparams(("arbitrary",)),
        name="final_norm",
    )(x2d, ys, route, g)


def kernel(x, ln_mix_e, w_in_e, ln_q_e, w_uq_e, ln_kv_e, w_ukv_e, b_f_e, w_out_e, ln_mix_o, w_in_o,
           w_out_o, ln_ffn, w_rg, b_rg, w_re, b_re, w_gate, w_up, w_down, ln_f):
    batch, seq, d = x.shape
    t = batch * seq
    assert d == D_MODEL and seq % TK == 0 and seq % TQ == 0 and seq % RET_BLK == 0
    assert seq % TM_PROJ == 0 and t % TM_OUT == 0 and t % TE == 0
    bf16 = jnp.bfloat16
    x2d = x.reshape(t, d)

    w_in_p, w_uq_p, w_ukv_p, b_f_p = _even_weights(w_in_e[0], w_uq_e[0], w_ukv_e[0], b_f_e[0])
    q, k, vt = _even_proj(x2d, seq, ln_mix_e[0][None, :], w_in_p, ln_q_e[0][None, :], w_uq_p,
                         ln_kv_e[0][None, :], w_ukv_p, b_f_p)
    o_t = (_attention(q, k, vt, batch, seq, 0, TQ, TK), _attention(q, k, vt, batch, seq, 1, TQ, TK))
    w_r, b_r = _router_weights(w_rg[0], b_rg[0], w_re[0], b_re[0])
    x2d, xg, route = _out_proj(x2d, o_t, w_out_e[0].astype(bf16), ln_ffn[0][None, :], w_r, b_r, seq)
    ys = _moe(xg, route[:, 0:TOP_K].astype(jnp.int32), w_gate, w_up, w_down, 0, t, TE, False)

    x2d, rq, rk, rv, rg = _odd_proj(x2d, ys, route, seq, ln_mix_o[0][None, :], w_in_o[0].astype(bf16))
    og = _retention(rq, rk, rv, rg, batch, seq)
    w_r, b_r = _router_weights(w_rg[1], b_rg[1], w_re[1], b_re[1])
    x2d, xg, route = _out_proj(x2d, og, w_out_o[0].astype(bf16), ln_ffn[1][None, :], w_r, b_r, seq)
    ys = _moe(xg, route[:, 0:TOP_K].astype(jnp.int32), w_gate, w_up, w_down, 1, t, TE, True)

    out = _final(x2d, ys, route, ln_f[None, :])
    return out.reshape(batch, seq, d)
```

```python
import functools
import math

import numpy as np
import jax
import jax.numpy as jnp
from jax import lax
from jax.experimental import pallas as pl
from jax.experimental.pallas import tpu as pltpu

D_MODEL = 1024
CHUNK = 64
ROPE_BASE = 10000.0
EPS = 1e-6
MLA_HEADS = 8
MLA_NOPE = 64
MLA_ROPE = 32
MLA_V = 64
MLA_Q_LORA = 256
MLA_KV_LORA = 128
FOX_HEADS = 8
FOX_DIM = 64
RET_HEADS = 4
RET_QK = 256
RET_V = 512
N_GROUPS = 4
EXPERTS_PER_GROUP = 8
N_EXPERTS = N_GROUPS * EXPERTS_PER_GROUP
TOP_K = 2
D_EXPERT = 512

LANES = 128
SUBLANES = 8
VMEM_LIMIT = 52 * 1024 * 1024

TM_PROJ = 256
TM_OUT = 512
OUT_SPLIT = 2
TQ = 1024
VT_ROWS = 80
TK = 1024
LOG2E = math.log2(math.e)
RET_C = 256
RET_BLK = 2048
TE = 256
ROWS_PER_TOK = D_MODEL // LANES

N_PAIRS = MLA_HEADS // 2
PAIR_W = 256
ROPE_H = MLA_ROPE // 2
NEG = -1e30

_X1_OFF = 0
_X2_OFF = 64
_AUG_W = 6

_C_Q = 0
_C_KV = _C_Q + MLA_Q_LORA
_C_KPE = _C_KV + MLA_KV_LORA
_C_QF = _C_KPE + LANES
_C_KF = _C_QF + FOX_HEADS * FOX_DIM
_C_VF = _C_KF + FOX_HEADS * FOX_DIM
_C_F = _C_VF + FOX_HEADS * FOX_DIM
EVEN_N = _C_F + LANES


def _cparams(sem):
    return pltpu.CompilerParams(dimension_semantics=sem, vmem_limit_bytes=VMEM_LIMIT)


def _rms_scale(x):
    return lax.rsqrt(jnp.mean(x * x, axis=-1, keepdims=True) + EPS)


def _split3(c):
    hi = c.astype(jnp.bfloat16)
    r1 = c - hi.astype(jnp.float32)
    mid = r1.astype(jnp.bfloat16)
    r2 = r1 - mid.astype(jnp.float32)
    lo = r2.astype(jnp.bfloat16)
    return hi, mid, lo


def _even_proj_kernel(tiles_per_seq, x_ref, g_ref, win_ref, lnq_ref, wuq_ref, lnkv_ref, wukv_ref,
                      bf_ref, cos_ref, sin_ref, tri_ref, pq_ref, pk_ref, oq_ref, ok_ref,
                      q_ref, k_ref, vt_ref, carry_ref):
    i = pl.program_id(0)
    x = x_ref[...]
    xn = (x * _rms_scale(x) * g_ref[...]).astype(jnp.bfloat16)
    z = jnp.dot(xn, win_ref[...], preferred_element_type=jnp.float32)
    cos = cos_ref[...]
    sin = sin_ref[...]

    def rope_slab(s):
        return s * cos + pltpu.roll(s, 64, axis=1) * sin

    c_q = z[:, _C_Q:_C_Q + MLA_Q_LORA]
    cqn = (c_q * _rms_scale(c_q) * lnq_ref[...]).astype(jnp.bfloat16)
    q = jnp.dot(cqn, wuq_ref[...], preferred_element_type=jnp.float32)
    scale_a = (MLA_NOPE + MLA_ROPE) ** -0.5 * LOG2E
    for p in range(N_PAIRS):
        lo = p * PAIR_W
        q_ref[0, :, lo:lo + LANES] = (q[:, lo:lo + LANES] * scale_a).astype(jnp.bfloat16)
        q_ref[0, :, lo + LANES:lo + PAIR_W] = (
            rope_slab(q[:, lo + LANES:lo + PAIR_W]) * scale_a).astype(jnp.bfloat16)

    c_kv = z[:, _C_KV:_C_KV + MLA_KV_LORA]
    ckn = (c_kv * _rms_scale(c_kv) * lnkv_ref[...]).astype(jnp.bfloat16)
    kv = jnp.dot(ckn, wukv_ref[...], preferred_element_type=jnp.float32)
    kpe = rope_slab(z[:, _C_KPE:_C_KPE + LANES]).astype(jnp.bfloat16)
    nk = MLA_HEADS * MLA_NOPE
    for p in range(N_PAIRS):
        lo = p * PAIR_W
        k_ref[0, :, lo:lo + LANES] = kv[:, p * LANES:(p + 1) * LANES].astype(jnp.bfloat16)
        k_ref[0, :, lo + LANES:lo + PAIR_W] = kpe
    _store_values_transposed(vt_ref, 0, kv[:, nk:nk + MLA_HEADS * MLA_V])

    lane = lax.broadcasted_iota(jnp.int32, (1, LANES), 1)
    fz = z[:, _C_F:_C_F + LANES] + bf_ref[...]
    log_f = -(jnp.maximum(-fz, 0.0) + jnp.log1p(jnp.exp(-jnp.abs(fz))))
    log_f = jnp.where(lane < FOX_HEADS, log_f, 0.0)

    @pl.when(i % tiles_per_seq == 0)
    def _():
        carry_ref[...] = jnp.zeros_like(carry_ref)

    hi, mid, lo3 = _split3(log_f)
    tri = tri_ref[...]
    cum = (jnp.dot(tri, hi, preferred_element_type=jnp.float32)
           + jnp.dot(tri, mid, preferred_element_type=jnp.float32)
           + jnp.dot(tri, lo3, preferred_element_type=jnp.float32)) + carry_ref[...]
    tm = cum.shape[0]
    carry_ref[...] = cum[tm - 1:tm, :]

    parts = jnp.concatenate(_split3(cum * LOG2E), axis=1)
    aug_q = jnp.dot(parts, pq_ref[...], preferred_element_type=jnp.float32) + oq_ref[...]
    aug_k = jnp.dot(parts, pk_ref[...], preferred_element_type=jnp.float32) + ok_ref[...]
    scale_b = FOX_DIM ** -0.5 * LOG2E
    for p in range(N_PAIRS):
        lo = p * PAIR_W
        q_ref[1, :, lo:lo + LANES] = (
            z[:, _C_QF + p * LANES:_C_QF + (p + 1) * LANES] * scale_b).astype(jnp.bfloat16)
        q_ref[1, :, lo + LANES:lo + PAIR_W] = aug_q[:, p * LANES:(p + 1) * LANES].astype(jnp.bfloat16)
        k_ref[1, :, lo:lo + LANES] = z[:, _C_KF + p * LANES:_C_KF + (p + 1) * LANES].astype(jnp.bfloat16)
        k_ref[1, :, lo + LANES:lo + PAIR_W] = aug_k[:, p * LANES:(p + 1) * LANES].astype(jnp.bfloat16)
    _store_values_transposed(vt_ref, 1, z[:, _C_VF:_C_VF + FOX_HEADS * FOX_DIM])


def _store_values_transposed(vt_ref, g, v):
    vt = v.T
    ones = jnp.ones((VT_ROWS - MLA_V, vt.shape[1]), vt_ref.dtype)
    for h in range(MLA_HEADS):
        vt_ref[g, h * VT_ROWS:h * VT_ROWS + MLA_V, :] = vt[h * MLA_V:(h + 1) * MLA_V].astype(vt_ref.dtype)
        vt_ref[g, h * VT_ROWS + MLA_V:(h + 1) * VT_ROWS, :] = ones


def _even_weights(w_in, w_uq, w_ukv, b_f):
    d = w_in.shape[0]
    zeros = lambda n: jnp.zeros((d, n), w_in.dtype)
    o_cq, o_ckv = 0, MLA_Q_LORA
    o_kpe = o_ckv + MLA_KV_LORA
    o_qf = o_kpe + MLA_ROPE
    o_kf = o_qf + FOX_HEADS * FOX_DIM
    o_vf = o_kf + FOX_HEADS * FOX_DIM
    o_f = o_vf + FOX_HEADS * FOX_DIM
    kpe1 = w_in[:, o_kpe:o_kpe + ROPE_H]
    kpe2 = w_in[:, o_kpe + ROPE_H:o_kpe + MLA_ROPE]
    kpe_slab = jnp.concatenate([kpe1, kpe1, zeros(32), kpe2, kpe2, zeros(32)], axis=1)
    f_slab = jnp.concatenate([w_in[:, o_f:o_f + FOX_HEADS], zeros(LANES - FOX_HEADS)], axis=1)
    w_in_p = jnp.concatenate([w_in[:, o_cq:o_kpe], kpe_slab, w_in[:, o_qf:o_f], f_slab], axis=1)

    dq = w_uq.shape[0]
    zq = lambda n: jnp.zeros((dq, n), w_uq.dtype)
    hd = MLA_NOPE + MLA_ROPE
    blocks = []
    for p in range(N_PAIRS):
        h0, h1 = 2 * p, 2 * p + 1
        nope = lambda h: w_uq[:, h * hd:h * hd + MLA_NOPE]
        r1 = lambda h: w_uq[:, h * hd + MLA_NOPE:h * hd + MLA_NOPE + ROPE_H]
        r2 = lambda h: w_uq[:, h * hd + MLA_NOPE + ROPE_H:(h + 1) * hd]
        blocks += [nope(h0), nope(h1), r1(h0), r1(h1), zq(32), r2(h0), r2(h1), zq(32)]
    w_uq_p = jnp.concatenate(blocks, axis=1)

    kvd = MLA_NOPE + MLA_V
    k_cols = [w_ukv[:, h * kvd:h * kvd + MLA_NOPE] for h in range(MLA_HEADS)]
    v_cols = [w_ukv[:, h * kvd + MLA_NOPE:(h + 1) * kvd] for h in range(MLA_HEADS)]
    w_ukv_p = jnp.concatenate(k_cols + v_cols, axis=1)
    b_f_p = jnp.concatenate([b_f, jnp.zeros((LANES - FOX_HEADS,), b_f.dtype)])[None, :]
    bf16 = jnp.bfloat16
    return w_in_p.astype(bf16), w_uq_p.astype(bf16), w_ukv_p.astype(bf16), b_f_p


def _aug_placement():
    pq = np.zeros((3 * LANES, N_PAIRS * LANES), np.float32)
    pk = np.zeros((3 * LANES, N_PAIRS * LANES), np.float32)
    oq = np.zeros((1, N_PAIRS * LANES), np.float32)
    ok = np.zeros((1, N_PAIRS * LANES), np.float32)
    for p in range(N_PAIRS):
        for j in range(2):
            h = 2 * p + j
            base = p * LANES + _X1_OFF + j * ROPE_H
            for t in range(3):
                pq[t * LANES + h, base + 3 + t] = 1.0
                pk[t * LANES + h, base + t] = -1.0
                oq[0, base + t] = 1.0
                ok[0, base + 3 + t] = 1.0
    return (jnp.asarray(pq, jnp.bfloat16), jnp.asarray(pk, jnp.bfloat16),
            jnp.asarray(oq), jnp.asarray(ok))


def _mla_rope_tables(s):
    inv = ROPE_BASE ** (-jnp.arange(ROPE_H, dtype=jnp.float32) / ROPE_H)
    ang = jnp.arange(s, dtype=jnp.float32)[:, None] * inv[None, :]
    c, sn = jnp.cos(ang), jnp.sin(ang)
    z = jnp.zeros((s, 32), jnp.float32)
    cos = jnp.concatenate([c, c, z, c, c, z], axis=1)
    sin = jnp.concatenate([-sn, -sn, z, sn, sn, z], axis=1)
    return cos, sin


def _even_proj(x2d, seq, g, w_in_p, ln_q, w_uq_p, ln_kv, w_ukv_p, b_f_p):
    t = x2d.shape[0]
    tm = TM_OUT
    tiles_per_seq = seq // tm
    cos, sin = _mla_rope_tables(seq)
    tri = jnp.asarray(np.tril(np.ones((tm, tm), np.float32)), jnp.bfloat16)
    pq, pk, oq, ok = _aug_placement()
    const = lambda shape: pl.BlockSpec(shape, lambda i: (0,) * len(shape))
    return pl.pallas_call(
        functools.partial(_even_proj_kernel, tiles_per_seq),
        grid=(t // tm,),
        in_specs=[
            pl.BlockSpec((tm, D_MODEL), lambda i: (i, 0)),
            const((1, D_MODEL)), const((D_MODEL, EVEN_N)),
            const((1, MLA_Q_LORA)), const((MLA_Q_LORA, N_PAIRS * PAIR_W)),
            const((1, MLA_KV_LORA)), const((MLA_KV_LORA, 2 * MLA_HEADS * MLA_NOPE)),
            const((1, LANES)),
            pl.BlockSpec((tm, LANES), lambda i: (i % tiles_per_seq, 0)),
            pl.BlockSpec((tm, LANES), lambda i: (i % tiles_per_seq, 0)),
            const((tm, tm)), const(pq.shape), const(pk.shape), const(oq.shape), const(ok.shape),
        ],
        out_specs=[
            pl.BlockSpec((2, tm, N_PAIRS * PAIR_W), lambda i: (0, i, 0)),
            pl.BlockSpec((2, tm, N_PAIRS * PAIR_W), lambda i: (0, i, 0)),
            pl.BlockSpec((2, None, MLA_HEADS * VT_ROWS, tm),
                         lambda i: (0, i // tiles_per_seq, 0, i % tiles_per_seq)),
        ],
        out_shape=[
            jax.ShapeDtypeStruct((2, t, N_PAIRS * PAIR_W), jnp.bfloat16),
            jax.ShapeDtypeStruct((2, t, N_PAIRS * PAIR_W), jnp.bfloat16),
            jax.ShapeDtypeStruct((2, t // seq, MLA_HEADS * VT_ROWS, seq), jnp.bfloat16),
        ],
        scratch_shapes=[pltpu.VMEM((1, LANES), jnp.float32)],
        compiler_params=_cparams(("arbitrary",)),
        name="even_proj",
    )(x2d, g, w_in_p, ln_q, w_uq_p, ln_kv, w_ukv_p, b_f_p, cos, sin, tri, pq, pk, oq, ok)


def _attn_kernel(group, tk, q_ref, k_ref, vt_ref, o_ref, qcat_ref, s0_ref, s1_ref, m0_ref, m1_ref,
                 a0_ref, a1_ref, acc_ref):
    i = pl.program_id(2)
    tq = q_ref.shape[0]
    s_bufs, m_bufs, a_bufs = (s0_ref, s1_ref), (m0_ref, m1_ref), (a0_ref, a1_ref)

    lane = lax.broadcasted_iota(jnp.int32, (1, PAIR_W), 1)

    def head_mask(j):
        a = (lane >= j * MLA_NOPE) & (lane < (j + 1) * MLA_NOPE)
        b = (lane >= LANES + _X1_OFF + j * ROPE_H) & (lane < LANES + _X1_OFF + (j + 1) * ROPE_H)
        c = (lane >= LANES + _X2_OFF + j * ROPE_H) & (lane < LANES + _X2_OFF + (j + 1) * ROPE_H)
        return a | b | c

    q = q_ref[...]
    zero = jnp.zeros_like(q)
    qcat_ref[0:tq, :] = jnp.where(head_mask(0), q, zero)
    qcat_ref[tq:2 * tq, :] = jnp.where(head_mask(1), q, zero)
    acc_ref[...] = jnp.zeros_like(acc_ref)

    q0 = i * tq
    n_chunks = q0 // tk + 1

    def scores(c, m_run, par, masked):
        kstart = pl.multiple_of(c * tk, tk)
        s = lax.dot_general(k_ref[pl.ds(kstart, tk), :], qcat_ref[...], (((1,), (1,)), ((), ())),
                            preferred_element_type=jnp.float32)
        if masked:
            col = lax.broadcasted_iota(jnp.int32, (1, 2 * tq), 1)
            qpos = q0 + jnp.where(col >= tq, col - tq, col)
            qlim = (qpos | (CHUNK - 1)) if group == 0 else qpos
            kpos = kstart + lax.broadcasted_iota(jnp.int32, (tk, 1), 0)
            s = jnp.where(kpos <= qlim, s, NEG)
        s_bufs[par][...] = s
        m_new = jnp.maximum(m_run, jnp.max(s, axis=0, keepdims=True))
        m_bufs[par][...] = m_new
        a_bufs[par][...] = jnp.exp2(m_run - m_new)

    def accumulate(c, par):
        kstart = pl.multiple_of(c * tk, tk)
        alpha = a_bufs[par][...]
        p = jnp.exp2(s_bufs[par][...] - m_bufs[par][...]).astype(jnp.bfloat16)
        for h in range(2):
            cols = slice(h * tq, (h + 1) * tq)
            acc_ref[h] = alpha[:, cols] * acc_ref[h] + jnp.dot(
                vt_ref[h * VT_ROWS:(h + 1) * VT_ROWS, pl.ds(kstart, tk)], p[:, cols],
                preferred_element_type=jnp.float32)

    def stage(c, par, masked):
        scores(c + 1, m_bufs[par][...], 1 - par, masked)
        accumulate(c, par)

    m_init = jnp.full((1, 2 * tq), NEG, jnp.float32)

    @pl.when(n_chunks == 1)
    def _():
        scores(0, m_init, 0, True)
        accumulate(0, 0)

    @pl.when(n_chunks > 1)
    def _():
        scores(0, m_init, 0, False)

    n_pairs = jnp.maximum(n_chunks - 2, 0) // 2

    def body(j, carry):
        for u in range(4):
            stage(4 * j + u, u % 2, False)
        return carry

    lax.fori_loop(0, n_pairs // 2, body, 0)

    @pl.when(n_pairs % 2 == 1)
    def _():
        stage(2 * n_pairs - 2, 0, False)
        stage(2 * n_pairs - 1, 1, False)
    c0 = 2 * n_pairs
    left = n_chunks - 1 - c0

    @pl.when((n_chunks > 1) & (left == 1))
    def _():
        stage(c0, 0, True)
        accumulate(c0 + 1, 1)

    @pl.when((n_chunks > 1) & (left == 2))
    def _():
        stage(c0, 0, False)
        stage(c0 + 1, 1, True)
        accumulate(c0 + 2, 0)

    for h in range(2):
        acc = acc_ref[h]
        o_ref[h * MLA_V:(h + 1) * MLA_V, :] = (acc[0:MLA_V] / acc[MLA_V:MLA_V + 1]).astype(o_ref.dtype)


def _attention(q, k, vt, batch, seq, group, tq, tk):
    nq = seq // tq
    f32 = jnp.float32
    return pl.pallas_call(
        functools.partial(_attn_kernel, group, tk),
        grid=(batch, N_PAIRS, nq),
        in_specs=[
            pl.BlockSpec((None, tq, PAIR_W), lambda b, p, i: (group, b * nq + i, p)),
            pl.BlockSpec((None, seq, PAIR_W), lambda b, p, i: (group, b, p)),
            pl.BlockSpec((None, None, 2 * VT_ROWS, seq), lambda b, p, i: (group, b, p, 0)),
        ],
        out_specs=pl.BlockSpec((None, LANES, tq), lambda b, p, i: (b, p, i)),
        out_shape=jax.ShapeDtypeStruct((batch, N_PAIRS * LANES, seq), jnp.bfloat16),
        scratch_shapes=[pltpu.VMEM((2 * tq, PAIR_W), jnp.bfloat16),
                        pltpu.VMEM((tk, 2 * tq), f32), pltpu.VMEM((tk, 2 * tq), f32),
                        pltpu.VMEM((1, 2 * tq), f32), pltpu.VMEM((1, 2 * tq), f32),
                        pltpu.VMEM((1, 2 * tq), f32), pltpu.VMEM((1, 2 * tq), f32),
                        pltpu.VMEM((2, VT_ROWS, tq), f32)],
        compiler_params=_cparams(("arbitrary",) * 3),
        name="attention_g%d" % group,
    )(q, k, vt)


def _router_t(lt):
    row = lax.broadcasted_iota(jnp.int32, lt.shape, 0)
    big = jnp.int32(LANES)
    ninf = -jnp.inf
    gl = jnp.where(row < N_GROUPS, lt, ninf)
    gmax = jnp.max(gl, axis=0, keepdims=True)
    gsel = jnp.min(jnp.where(gl == gmax, row, big), axis=0, keepdims=True)
    gsum = jnp.sum(jnp.where(row < N_GROUPS, jnp.exp(lt - gmax), 0.0), axis=0, keepdims=True)
    p_g = 1.0 / gsum
    lo = N_GROUPS + EXPERTS_PER_GROUP * gsel
    e = jnp.where((row >= lo) & (row < lo + EXPERTS_PER_GROUP), lt, ninf)
    m1 = jnp.max(e, axis=0, keepdims=True)
    i1 = jnp.min(jnp.where(e == m1, row, big), axis=0, keepdims=True)
    e2 = jnp.where(row == i1, ninf, e)
    m2 = jnp.max(e2, axis=0, keepdims=True)
    i2 = jnp.min(jnp.where(e2 == m2, row, big), axis=0, keepdims=True)
    a2 = jnp.exp(m2 - m1)
    den = 1.0 + a2
    g1 = p_g / den
    g2 = p_g * a2 / den
    f = lambda v: v.astype(jnp.float32)
    return jnp.where(row == 0, f(i1 - N_GROUPS),
                     jnp.where(row == 1, f(i2 - N_GROUPS),
                               jnp.where(row == 2, g1, jnp.where(row == 3, g2, 0.0))))


def _out_proj_kernel(n_o, transposed, x_ref, *refs):
    o_refs = refs[:n_o]
    w_ref, ln_ref, wr_ref, br_ref, xo_ref, xg_ref, route_ref = refs[n_o:]
    tm = x_ref.shape[0]
    hm = tm // OUT_SPLIT
    for part in range(OUT_SPLIT):
        rows = slice(part * hm, (part + 1) * hm)
        if transposed:
            y = x_ref[rows, :]
            nf = o_refs[0].shape[0]
            for g, o_ref in enumerate(o_refs):
                y = y + lax.dot_general(o_ref[:, rows], w_ref[g * nf:(g + 1) * nf, :],
                                        (((0,), (0,)), ((), ())), preferred_element_type=jnp.float32)
        else:
            y = x_ref[rows, :] + jnp.dot(o_refs[0][rows, :], w_ref[...], preferred_element_type=jnp.float32)
        xo_ref[rows, :] = y
        xn = y * _rms_scale(y) * ln_ref[...]
        hi = xn.astype(jnp.bfloat16)
        lo = (xn - hi.astype(jnp.float32)).astype(jnp.bfloat16)
        l2 = jnp.dot(jnp.concatenate([hi, lo], axis=1), wr_ref[...], preferred_element_type=jnp.float32)
        lt = (l2[:, :LANES] + l2[:, LANES:] + br_ref[...]).T
        route_ref[rows, :] = _router_t(lt).T
        for c in range(ROWS_PER_TOK):
            xg_ref[pl.ds(part * hm * ROWS_PER_TOK + c, hm, stride=ROWS_PER_TOK), :] = (
                xn[:, c * LANES:(c + 1) * LANES])


def _out_proj(x2d, o, w_out, ln, w_r, b_r, seq):
    t = x2d.shape[0]
    kdim = w_out.shape[0]
    tm = TM_OUT
    transposed = isinstance(o, tuple)
    tiles_per_seq = seq // tm
    if transposed:
        o_specs = [pl.BlockSpec((None, a.shape[1], tm), lambda i: (i // tiles_per_seq, 0, i % tiles_per_seq))
                   for a in o]
    else:
        o, o_specs = (o,), [pl.BlockSpec((tm, kdim), lambda i: (i, 0))]
    const = lambda shape: pl.BlockSpec(shape, lambda i: (0,) * len(shape))
    return pl.pallas_call(
        functools.partial(_out_proj_kernel, len(o), transposed),
        grid=(t // tm,),
        in_specs=[
            pl.BlockSpec((tm, D_MODEL), lambda i: (i, 0)),
            *o_specs,
            const((kdim, D_MODEL)), const((1, D_MODEL)), const((2 * D_MODEL, 2 * LANES)), const((1, LANES)),
        ],
        out_specs=[
            pl.BlockSpec((tm, D_MODEL), lambda i: (i, 0)),
            pl.BlockSpec((tm * ROWS_PER_TOK, LANES), lambda i: (i, 0)),
            pl.BlockSpec((tm, LANES), lambda i: (i, 0)),
        ],
        out_shape=[
            jax.ShapeDtypeStruct((t, D_MODEL), jnp.float32),
            jax.ShapeDtypeStruct((t * ROWS_PER_TOK, LANES), jnp.float32),
            jax.ShapeDtypeStruct((t, LANES), jnp.float32),
        ],
        compiler_params=_cparams(("arbitrary",)),
        name="out_proj_router",
    )(x2d, *o, w_out, ln, w_r, b_r)


def _router_weights(w_rg, b_rg, w_re, b_re):
    d = w_rg.shape[0]
    pad = LANES - N_GROUPS - N_EXPERTS
    w_r = jnp.concatenate([w_rg, w_re, jnp.zeros((d, pad), w_rg.dtype)], axis=1)
    b_r = jnp.concatenate([b_rg, b_re, jnp.zeros((pad,), b_rg.dtype)])[None, :]
    w_hi = w_r.astype(jnp.bfloat16)
    w_lo = (w_r - w_hi.astype(jnp.float32)).astype(jnp.bfloat16)
    half = jnp.concatenate([w_hi, w_lo], axis=1)
    return jnp.concatenate([half, half], axis=0), b_r


def _dispatch_plan(eid, t, te):
    a = t * TOP_K
    nb = a // te + N_EXPERTS
    e_flat = eid.reshape(a)
    experts = jnp.arange(N_EXPERTS, dtype=jnp.int32)
    _, order = lax.sort((e_flat, jnp.arange(a, dtype=jnp.int32)), num_keys=1, is_stable=True)
    counts = jnp.sum((e_flat[:, None] == experts[None, :]).astype(jnp.int32), axis=0)
    start = jnp.cumsum(counts) - counts
    padded = ((counts + te - 1) // te) * te
    pend = jnp.cumsum(padded)
    pstart = pend - padded
    blk0 = jnp.arange(nb, dtype=jnp.int32) * te
    blk_e = jnp.minimum(jnp.sum((blk0[:, None] >= pend[None, :]).astype(jnp.int32), axis=1), N_EXPERTS - 1)
    blk_used = (blk0 < pend[-1]).astype(jnp.int32)
    onehot = (blk_e[:, None] == experts[None, :]).astype(jnp.int32)
    per_blk = lambda v: jnp.sum(onehot * v[None, :], axis=1)[:, None]
    r = jnp.arange(te, dtype=jnp.int32)[None, :]
    off = blk0[:, None] + r - per_blk(pstart)
    valid = (off < per_blk(counts)) & (blk_used[:, None] > 0)
    src = jnp.where(valid, per_blk(start) + off, 0)
    assign = order[src]
    src_tok = jnp.where(valid, assign // TOP_K, 0).astype(jnp.int32)
    dst_row = jnp.where(valid, assign, a + r).astype(jnp.int32)
    nxt = jnp.minimum(jnp.arange(nb) + 1, nb - 1)
    return blk_e, blk_used, jnp.concatenate([src_tok, src_tok[nxt], dst_row], axis=1)[:, None, :]


def _moe_kernel(te, blk_e_ref, blk_used_ref, idx_ref, xg_ref, wg_ref, wu_ref, wd_ref,
                ys_ref, xbuf, ybuf, wg_bf, wu_bf, wd_bf, gsem, ssem):
    i = pl.program_id(0)
    nb = pl.num_programs(0)
    used = blk_used_ref[i] > 0
    slot = i % 2
    block_rows = te * ROWS_PER_TOK

    def start_gather(first, s):
        for r in range(te):
            tok = idx_ref[0, 0, first + r]
            pltpu.make_async_copy(
                xg_ref.at[pl.ds(pl.multiple_of(tok * ROWS_PER_TOK, ROWS_PER_TOK), ROWS_PER_TOK), :],
                xbuf.at[s, pl.ds(r * ROWS_PER_TOK, ROWS_PER_TOK), :],
                gsem.at[s]).start(priority=r % 2)

    def wait_gather(s):
        pltpu.make_async_copy(xg_ref.at[pl.ds(0, block_rows), :], xbuf.at[s], gsem.at[s]).wait()

    def start_scatter():
        for r in range(te):
            row = idx_ref[0, 0, 2 * te + r]
            pltpu.make_async_copy(
                ybuf.at[pl.ds(r * ROWS_PER_TOK, ROWS_PER_TOK), :],
                ys_ref.at[pl.ds(pl.multiple_of(row * ROWS_PER_TOK, ROWS_PER_TOK), ROWS_PER_TOK), :],
                ssem.at[0]).start(priority=r % 2)

    def wait_scatter():
        pltpu.make_async_copy(ybuf, ys_ref.at[pl.ds(0, block_rows), :], ssem.at[0]).wait()

    @pl.when(i == 0)
    def _():
        ybuf[...] = jnp.zeros_like(ybuf)
        pltpu.make_async_copy(
            ybuf, ys_ref.at[pl.ds(ys_ref.shape[0] - block_rows, block_rows), :], ssem.at[0]).start()
        start_gather(0, 0)

    @pl.when(used)
    def _():
        start_gather(te, 1 - slot)
        prev_e = blk_e_ref[jnp.maximum(i - 1, 0)]

        @pl.when((i == 0) | (blk_e_ref[i] != prev_e))
        def _():
            wg_bf[...] = wg_ref[...].astype(jnp.bfloat16)
            wu_bf[...] = wu_ref[...].astype(jnp.bfloat16)
            wd_bf[...] = wd_ref[...].astype(jnp.bfloat16)

        wait_gather(slot)
        xb = xbuf.at[slot]
        x = jnp.concatenate([xb[pl.ds(c, te, stride=ROWS_PER_TOK), :] for c in range(ROWS_PER_TOK)],
                            axis=1).astype(jnp.bfloat16)
        hg = jnp.dot(x, wg_bf[...], preferred_element_type=jnp.float32)
        hu = jnp.dot(x, wu_bf[...], preferred_element_type=jnp.float32)
        hid = (hg * jax.nn.sigmoid(hg) * hu).astype(jnp.bfloat16)
        y = jnp.dot(hid, wd_bf[...], preferred_element_type=jnp.float32)

        wait_scatter()
        for c in range(ROWS_PER_TOK):
            ybuf[pl.ds(c, te, stride=ROWS_PER_TOK), :] = y[:, c * LANES:(c + 1) * LANES]
        start_scatter()

        next_used = (i + 1 < nb) & (blk_used_ref[jnp.minimum(i + 1, nb - 1)] > 0)

        @pl.when(jnp.logical_not(next_used))
        def _():
            wait_gather(1 - slot)
            wait_scatter()


def _moe(xg, eid, w_gate, w_up, w_down, layer, t):
    te = TE
    a = t * TOP_K
    nb = a // te + N_EXPERTS
    blk_e, blk_used, idx = _dispatch_plan(eid, t, te)
    grid_spec = pltpu.PrefetchScalarGridSpec(
        num_scalar_prefetch=2,
        grid=(nb,),
        in_specs=[
            pl.BlockSpec((1, 1, 3 * te), lambda i, be, bu: (i, 0, 0), memory_space=pltpu.SMEM),
            pl.BlockSpec(memory_space=pl.ANY),
            pl.BlockSpec((None, None, D_MODEL, D_EXPERT), lambda i, be, bu: (layer, be[i], 0, 0)),
            pl.BlockSpec((None, None, D_MODEL, D_EXPERT), lambda i, be, bu: (layer, be[i], 0, 0)),
            pl.BlockSpec((None, None, D_EXPERT, D_MODEL), lambda i, be, bu: (layer, be[i], 0, 0)),
        ],
        out_specs=pl.BlockSpec(memory_space=pl.ANY),
        scratch_shapes=[
            pltpu.VMEM((2, te * ROWS_PER_TOK, LANES), jnp.float32),
            pltpu.VMEM((te * ROWS_PER_TOK, LANES), jnp.float32),
            pltpu.VMEM((D_MODEL, D_EXPERT), jnp.bfloat16),
            pltpu.VMEM((D_MODEL, D_EXPERT), jnp.bfloat16),
            pltpu.VMEM((D_EXPERT, D_MODEL), jnp.bfloat16),
            pltpu.SemaphoreType.DMA((2,)),
            pltpu.SemaphoreType.DMA((1,)),
        ],
    )
    return pl.pallas_call(
        functools.partial(_moe_kernel, te),
        grid_spec=grid_spec,
        out_shape=jax.ShapeDtypeStruct(((a + te) * ROWS_PER_TOK, LANES), jnp.float32),
        compiler_params=_cparams(("arbitrary",)),
        name="moe_experts",
    )(blk_e, blk_used, idx, xg, w_gate, w_up, w_down)


def _combine(x, ys_ref, route_ref, tm):
    stride = TOP_K * ROWS_PER_TOK
    y0 = jnp.concatenate([ys_ref[pl.ds(c, tm, stride=stride), :] for c in range(ROWS_PER_TOK)], axis=1)
    y1 = jnp.concatenate([ys_ref[pl.ds(ROWS_PER_TOK + c, tm, stride=stride), :]
                          for c in range(ROWS_PER_TOK)], axis=1)
    route = route_ref[...]
    return x + route[:, 2:3] * y0 + route[:, 3:4] * y1


def _odd_proj_kernel(x_ref, ys_ref, route_ref, g_ref, w_ref, cos_ref, sin_ref,
                     xo_ref, q_ref, k_ref, v_ref, gate_ref):
    tm = x_ref.shape[0]
    x = _combine(x_ref[...], ys_ref, route_ref, tm)
    xo_ref[...] = x
    xn = (x * _rms_scale(x) * g_ref[...]).astype(jnp.bfloat16)
    cos = cos_ref[...]
    sin = sin_ref[...]
    nqk = RET_HEADS * RET_QK
    half = RET_QK // 2

    def rope_store(z, dst, scale):
        for h in range(RET_HEADS):
            x1 = z[:, h * RET_QK:h * RET_QK + half]
            x2 = z[:, h * RET_QK + half:(h + 1) * RET_QK]
            dst[:, h * RET_QK:h * RET_QK + half] = ((x1 * cos - x2 * sin) * scale).astype(dst.dtype)
            dst[:, h * RET_QK + half:(h + 1) * RET_QK] = ((x1 * sin + x2 * cos) * scale).astype(dst.dtype)

    zq = jnp.dot(xn, w_ref[:, 0:nqk], preferred_element_type=jnp.float32)
    rope_store(zq, q_ref, 1.0)
    zk = jnp.dot(xn, w_ref[:, nqk:2 * nqk], preferred_element_type=jnp.float32)
    rope_store(zk, k_ref, RET_QK ** -0.5)
    nv = RET_HEADS * RET_V
    zv = jnp.dot(xn, w_ref[:, 2 * nqk:2 * nqk + nv], preferred_element_type=jnp.float32)
    v_ref[...] = zv.astype(v_ref.dtype)
    zg = jnp.dot(xn, w_ref[:, 2 * nqk + nv:2 * nqk + 2 * nv], preferred_element_type=jnp.float32)
    gate_ref[...] = (zg * jax.nn.sigmoid(zg)).astype(gate_ref.dtype)


def _ret_rope_tables(s):
    half = RET_QK // 2
    inv = ROPE_BASE ** (-jnp.arange(half, dtype=jnp.float32) / half)
    ang = jnp.arange(s, dtype=jnp.float32)[:, None] * inv[None, :]
    return jnp.cos(ang), jnp.sin(ang)


def _odd_proj(x2d, ys, route, seq, g, w_in):
    t = x2d.shape[0]
    tm = TM_PROJ
    tiles_per_seq = seq // tm
    cos, sin = _ret_rope_tables(seq)
    nqk = RET_HEADS * RET_QK
    nv = RET_HEADS * RET_V
    n_in = w_in.shape[1]
    const = lambda shape: pl.BlockSpec(shape, lambda i: (0,) * len(shape))
    tok = lambda n: pl.BlockSpec((tm, n), lambda i: (i, 0))
    return pl.pallas_call(
        _odd_proj_kernel,
        grid=(t // tm,),
        in_specs=[
            tok(D_MODEL),
            pl.BlockSpec((tm * TOP_K * ROWS_PER_TOK, LANES), lambda i: (i, 0)),
            tok(LANES),
            const((1, D_MODEL)), const((D_MODEL, n_in)),
            pl.BlockSpec((tm, LANES), lambda i: (i % tiles_per_seq, 0)),
            pl.BlockSpec((tm, LANES), lambda i: (i % tiles_per_seq, 0)),
        ],
        out_specs=[tok(D_MODEL), tok(nqk), tok(nqk), tok(nv), tok(nv)],
        out_shape=[
            jax.ShapeDtypeStruct((t, D_MODEL), jnp.float32),
            jax.ShapeDtypeStruct((t, nqk), jnp.bfloat16),
            jax.ShapeDtypeStruct((t, nqk), jnp.bfloat16),
            jax.ShapeDtypeStruct((t, nv), jnp.bfloat16),
            jax.ShapeDtypeStruct((t, nv), jnp.bfloat16),
        ],
        compiler_params=_cparams(("arbitrary",)),
        name="odd_proj",
    )(x2d, ys, route, g, w_in, cos, sin)


def _retention_kernel(q_ref, k_ref, v_ref, gate_ref, dmask_ref, xi_ref, zeta_ref, gc_ref, o_ref, state_ref):
    n = pl.program_id(2)

    @pl.when(n == 0)
    def _():
        state_ref[...] = jnp.zeros_like(state_ref)

    xi = xi_ref[...][:, 0:1]
    zeta = zeta_ref[...][:, 0:1]
    g_c = gc_ref[...][:, 0:1]
    for j in range(q_ref.shape[0] // RET_C):
        rows = slice(j * RET_C, (j + 1) * RET_C)
        q = q_ref[rows, :]
        k = k_ref[rows, :]
        v = v_ref[rows, :]
        inner = lax.dot_general(q, k, (((1,), (1,)), ((), ())),
                                preferred_element_type=jnp.float32) * dmask_ref[...]
        o = jnp.dot(inner.astype(jnp.bfloat16), v, preferred_element_type=jnp.float32)
        state = state_ref[...]
        cross = jnp.dot(q, state.astype(jnp.bfloat16), preferred_element_type=jnp.float32)
        o = o + cross * xi
        kz = (k.astype(jnp.float32) * zeta).astype(jnp.bfloat16)
        upd = lax.dot_general(kz, v, (((0,), (0,)), ((), ())), preferred_element_type=jnp.float32)
        state_ref[...] = g_c * state + upd
        o = o * _rms_scale(o)
        o_ref[rows, :] = (o * gate_ref[rows, :].astype(jnp.float32)).astype(o_ref.dtype)


def _retention_tables():
    c = RET_C
    log_g = jnp.log(1.0 - jnp.exp2(-5.0 - jnp.arange(RET_HEADS, dtype=jnp.float32)))
    j = jnp.arange(c, dtype=jnp.float32)
    diff = j[:, None] - j[None, :]
    dmask = jnp.where(diff >= 0, jnp.exp(jnp.maximum(diff, 0.0)[None] * log_g[:, None, None]), 0.0)
    xi = jnp.exp((j[None, :] + 1.0) * log_g[:, None])
    zeta = jnp.exp((c - 1.0 - j)[None, :] * log_g[:, None])
    g_c = jnp.exp(c * log_g)
    bc = lambda a: jnp.broadcast_to(a[..., None], a.shape + (LANES,))
    return dmask, bc(xi), bc(zeta), bc(g_c[:, None])


def _retention(q, k, v, gate, batch, seq):
    nc = seq // RET_BLK
    dmask, xi, zeta, g_c = _retention_tables()
    row = lambda n: pl.BlockSpec((RET_BLK, n), lambda b, h, c: (b * nc + c, h))
    head = lambda shape: pl.BlockSpec((None,) + shape, lambda b, h, c: (h, 0, 0))
    return pl.pallas_call(
        _retention_kernel,
        grid=(batch, RET_HEADS, nc),
        in_specs=[row(RET_QK), row(RET_QK), row(RET_V), row(RET_V),
                  head((RET_C, RET_C)), head((RET_C, LANES)), head((RET_C, LANES)), head((1, LANES))],
        out_specs=row(RET_V),
        out_shape=jax.ShapeDtypeStruct((batch * seq, RET_HEADS * RET_V), jnp.bfloat16),
        scratch_shapes=[pltpu.VMEM((RET_QK, RET_V), jnp.float32)],
        compiler_params=_cparams(("arbitrary",) * 3),
        name="retention",
    )(q, k, v, gate, dmask, xi, zeta, g_c)


def _final_kernel(x_ref, ys_ref, route_ref, g_ref, o_ref):
    x = _combine(x_ref[...], ys_ref, route_ref, x_ref.shape[0])
    o_ref[...] = x * _rms_scale(x) * g_ref[...]


def _final(x2d, ys, route, g):
    t = x2d.shape[0]
    tm = TM_OUT
    return pl.pallas_call(
        _final_kernel,
        grid=(t // tm,),
        in_specs=[
            pl.BlockSpec((tm, D_MODEL), lambda i: (i, 0)),
            pl.BlockSpec((tm * TOP_K * ROWS_PER_TOK, LANES), lambda i: (i, 0)),
            pl.BlockSpec((tm, LANES), lambda i: (i, 0)),
            pl.BlockSpec((1, D_MODEL), lambda i: (0, 0)),
        ],
        out_specs=pl.BlockSpec((tm, D_MODEL), lambda i: (i, 0)),
        out_shape=jax.ShapeDtypeStruct((t, D_MODEL), jnp.float32),
        compiler_params=_cparams(("arbitrary",)),
        name="final_norm",
    )(x2d, ys, route, g)


def kernel(x, ln_mix_e, w_in_e, ln_q_e, w_uq_e, ln_kv_e, w_ukv_e, b_f_e, w_out_e, ln_mix_o, w_in_o,
           w_out_o, ln_ffn, w_rg, b_rg, w_re, b_re, w_gate, w_up, w_down, ln_f):
    batch, seq, d = x.shape
    t = batch * seq
    assert d == D_MODEL and seq % TK == 0 and seq % TQ == 0 and seq % RET_BLK == 0
    assert seq % TM_PROJ == 0 and t % TM_OUT == 0 and t % TE == 0
    bf16 = jnp.bfloat16
    x2d = x.reshape(t, d)

    w_in_p, w_uq_p, w_ukv_p, b_f_p = _even_weights(w_in_e[0], w_uq_e[0], w_ukv_e[0], b_f_e[0])
    q, k, vt = _even_proj(x2d, seq, ln_mix_e[0][None, :], w_in_p, ln_q_e[0][None, :], w_uq_p,
                         ln_kv_e[0][None, :], w_ukv_p, b_f_p)
    o_t = (_attention(q, k, vt, batch, seq, 0, TQ, TK), _attention(q, k, vt, batch, seq, 1, TQ, TK))
    w_r, b_r = _router_weights(w_rg[0], b_rg[0], w_re[0], b_re[0])
    x2d, xg, route = _out_proj(x2d, o_t, w_out_e[0].astype(bf16), ln_ffn[0][None, :], w_r, b_r, seq)
    ys = _moe(xg, route[:, 0:TOP_K].astype(jnp.int32), w_gate, w_up, w_down, 0, t)

    x2d, rq, rk, rv, rg = _odd_proj(x2d, ys, route, seq, ln_mix_o[0][None, :], w_in_o[0].astype(bf16))
    og = _retention(rq, rk, rv, rg, batch, seq)
    w_r, b_r = _router_weights(w_rg[1], b_rg[1], w_re[1], b_re[1])
    x2d, xg, route = _out_proj(x2d, og, w_out_o[0].astype(bf16), ln_ffn[1][None, :], w_r, b_r, seq)
    ys = _moe(xg, route[:, 0:TOP_K].astype(jnp.int32), w_gate, w_up, w_down, 1, t)

    out = _final(x2d, ys, route, ln_f[None, :])
    return out.reshape(batch, seq, d)
```

```python
import functools
import math

import numpy as np
import jax
import jax.numpy as jnp
from jax import lax
from jax.experimental import pallas as pl
from jax.experimental.pallas import tpu as pltpu

D_MODEL = 1024
CHUNK = 64
ROPE_BASE = 10000.0
EPS = 1e-6
MLA_HEADS = 8
MLA_NOPE = 64
MLA_ROPE = 32
MLA_V = 64
MLA_Q_LORA = 256
MLA_KV_LORA = 128
FOX_HEADS = 8
FOX_DIM = 64
RET_HEADS = 4
RET_QK = 256
RET_V = 512
N_GROUPS = 4
EXPERTS_PER_GROUP = 8
N_EXPERTS = N_GROUPS * EXPERTS_PER_GROUP
TOP_K = 2
D_EXPERT = 512

LANES = 128
SUBLANES = 8
VMEM_LIMIT = 52 * 1024 * 1024

TM_PROJ = 256
TM_OUT = 512
OUT_SPLIT = 2
TQ = 1024
VT_ROWS = 80
TK = 1024
LOG2E = math.log2(math.e)
RET_C = 256
RET_BLK = 2048
TE = 256
ROWS_PER_TOK = D_MODEL // LANES

N_PAIRS = MLA_HEADS // 2
PAIR_W = 256
ROPE_H = MLA_ROPE // 2
NEG = -1e30

_X1_OFF = 0
_X2_OFF = 64
_AUG_W = 6

_C_Q = 0
_C_KV = _C_Q + MLA_Q_LORA
_C_KPE = _C_KV + MLA_KV_LORA
_C_QF = _C_KPE + LANES
_C_KF = _C_QF + FOX_HEADS * FOX_DIM
_C_VF = _C_KF + FOX_HEADS * FOX_DIM
_C_F = _C_VF + FOX_HEADS * FOX_DIM
EVEN_N = _C_F + LANES


def _cparams(sem):
    return pltpu.CompilerParams(dimension_semantics=sem, vmem_limit_bytes=VMEM_LIMIT)


def _rms_scale(x):
    return lax.rsqrt(jnp.mean(x * x, axis=-1, keepdims=True) + EPS)


def _split3(c):
    hi = c.astype(jnp.bfloat16)
    r1 = c - hi.astype(jnp.float32)
    mid = r1.astype(jnp.bfloat16)
    r2 = r1 - mid.astype(jnp.float32)
    lo = r2.astype(jnp.bfloat16)
    return hi, mid, lo


def _even_proj_kernel(tiles_per_seq, x_ref, g_ref, win_ref, lnq_ref, wuq_ref, lnkv_ref, wukv_ref,
                      bf_ref, cos_ref, sin_ref, tri_ref, pq_ref, pk_ref, oq_ref, ok_ref,
                      q_ref, k_ref, vt_ref, carry_ref):
    i = pl.program_id(0)
    x = x_ref[...]
    xn = (x * _rms_scale(x) * g_ref[...]).astype(jnp.bfloat16)
    z = jnp.dot(xn, win_ref[...], preferred_element_type=jnp.float32)
    cos = cos_ref[...]
    sin = sin_ref[...]

    def rope_slab(s):
        return s * cos + pltpu.roll(s, 64, axis=1) * sin

    c_q = z[:, _C_Q:_C_Q + MLA_Q_LORA]
    cqn = (c_q * _rms_scale(c_q) * lnq_ref[...]).astype(jnp.bfloat16)
    q = jnp.dot(cqn, wuq_ref[...], preferred_element_type=jnp.float32)
    scale_a = (MLA_NOPE + MLA_ROPE) ** -0.5 * LOG2E
    for p in range(N_PAIRS):
        lo = p * PAIR_W
        q_ref[0, :, lo:lo + LANES] = (q[:, lo:lo + LANES] * scale_a).astype(jnp.bfloat16)
        q_ref[0, :, lo + LANES:lo + PAIR_W] = (
            rope_slab(q[:, lo + LANES:lo + PAIR_W]) * scale_a).astype(jnp.bfloat16)

    c_kv = z[:, _C_KV:_C_KV + MLA_KV_LORA]
    ckn = (c_kv * _rms_scale(c_kv) * lnkv_ref[...]).astype(jnp.bfloat16)
    kv = jnp.dot(ckn, wukv_ref[...], preferred_element_type=jnp.float32)
    kpe = rope_slab(z[:, _C_KPE:_C_KPE + LANES]).astype(jnp.bfloat16)
    nk = MLA_HEADS * MLA_NOPE
    for p in range(N_PAIRS):
        lo = p * PAIR_W
        k_ref[0, :, lo:lo + LANES] = kv[:, p * LANES:(p + 1) * LANES].astype(jnp.bfloat16)
        k_ref[0, :, lo + LANES:lo + PAIR_W] = kpe
    _store_values_transposed(vt_ref, 0, kv[:, nk:nk + MLA_HEADS * MLA_V])

    lane = lax.broadcasted_iota(jnp.int32, (1, LANES), 1)
    fz = z[:, _C_F:_C_F + LANES] + bf_ref[...]
    log_f = -(jnp.maximum(-fz, 0.0) + jnp.log1p(jnp.exp(-jnp.abs(fz))))
    log_f = jnp.where(lane < FOX_HEADS, log_f, 0.0)

    @pl.when(i % tiles_per_seq == 0)
    def _():
        carry_ref[...] = jnp.zeros_like(carry_ref)

    hi, mid, lo3 = _split3(log_f)
    tri = tri_ref[...]
    cum = (jnp.dot(tri, hi, preferred_element_type=jnp.float32)
           + jnp.dot(tri, mid, preferred_element_type=jnp.float32)
           + jnp.dot(tri, lo3, preferred_element_type=jnp.float32)) + carry_ref[...]
    tm = cum.shape[0]
    carry_ref[...] = cum[tm - 1:tm, :]

    parts = jnp.concatenate(_split3(cum * LOG2E), axis=1)
    aug_q = jnp.dot(parts, pq_ref[...], preferred_element_type=jnp.float32) + oq_ref[...]
    aug_k = jnp.dot(parts, pk_ref[...], preferred_element_type=jnp.float32) + ok_ref[...]
    scale_b = FOX_DIM ** -0.5 * LOG2E
    for p in range(N_PAIRS):
        lo = p * PAIR_W
        q_ref[1, :, lo:lo + LANES] = (
            z[:, _C_QF + p * LANES:_C_QF + (p + 1) * LANES] * scale_b).astype(jnp.bfloat16)
        q_ref[1, :, lo + LANES:lo + PAIR_W] = aug_q[:, p * LANES:(p + 1) * LANES].astype(jnp.bfloat16)
        k_ref[1, :, lo:lo + LANES] = z[:, _C_KF + p * LANES:_C_KF + (p + 1) * LANES].astype(jnp.bfloat16)
        k_ref[1, :, lo + LANES:lo + PAIR_W] = aug_k[:, p * LANES:(p + 1) * LANES].astype(jnp.bfloat16)
    _store_values_transposed(vt_ref, 1, z[:, _C_VF:_C_VF + FOX_HEADS * FOX_DIM])


def _store_values_transposed(vt_ref, g, v):
    vt = v.T
    ones = jnp.ones((VT_ROWS - MLA_V, vt.shape[1]), vt_ref.dtype)
    for h in range(MLA_HEADS):
        vt_ref[g, h * VT_ROWS:h * VT_ROWS + MLA_V, :] = vt[h * MLA_V:(h + 1) * MLA_V].astype(vt_ref.dtype)
        vt_ref[g, h * VT_ROWS + MLA_V:(h + 1) * VT_ROWS, :] = ones


def _even_weights(w_in, w_uq, w_ukv, b_f):
    bf16 = jnp.bfloat16
    w_in, w_uq, w_ukv = w_in.astype(bf16), w_uq.astype(bf16), w_ukv.astype(bf16)
    d = w_in.shape[0]
    zeros = lambda n: jnp.zeros((d, n), w_in.dtype)
    o_cq, o_ckv = 0, MLA_Q_LORA
    o_kpe = o_ckv + MLA_KV_LORA
    o_qf = o_kpe + MLA_ROPE
    o_kf = o_qf + FOX_HEADS * FOX_DIM
    o_vf = o_kf + FOX_HEADS * FOX_DIM
    o_f = o_vf + FOX_HEADS * FOX_DIM
    kpe1 = w_in[:, o_kpe:o_kpe + ROPE_H]
    kpe2 = w_in[:, o_kpe + ROPE_H:o_kpe + MLA_ROPE]
    kpe_slab = jnp.concatenate([kpe1, kpe1, zeros(32), kpe2, kpe2, zeros(32)], axis=1)
    f_slab = jnp.concatenate([w_in[:, o_f:o_f + FOX_HEADS], zeros(LANES - FOX_HEADS)], axis=1)
    w_in_p = jnp.concatenate([w_in[:, o_cq:o_kpe], kpe_slab, w_in[:, o_qf:o_f], f_slab], axis=1)

    dq = w_uq.shape[0]
    zq = lambda n: jnp.zeros((dq, n), w_uq.dtype)
    hd = MLA_NOPE + MLA_ROPE
    blocks = []
    for p in range(N_PAIRS):
        h0, h1 = 2 * p, 2 * p + 1
        nope = lambda h: w_uq[:, h * hd:h * hd + MLA_NOPE]
        r1 = lambda h: w_uq[:, h * hd + MLA_NOPE:h * hd + MLA_NOPE + ROPE_H]
        r2 = lambda h: w_uq[:, h * hd + MLA_NOPE + ROPE_H:(h + 1) * hd]
        blocks += [nope(h0), nope(h1), r1(h0), r1(h1), zq(32), r2(h0), r2(h1), zq(32)]
    w_uq_p = jnp.concatenate(blocks, axis=1)

    kvd = MLA_NOPE + MLA_V
    k_cols = [w_ukv[:, h * kvd:h * kvd + MLA_NOPE] for h in range(MLA_HEADS)]
    v_cols = [w_ukv[:, h * kvd + MLA_NOPE:(h + 1) * kvd] for h in range(MLA_HEADS)]
    w_ukv_p = jnp.concatenate(k_cols + v_cols, axis=1)
    b_f_p = jnp.concatenate([b_f, jnp.zeros((LANES - FOX_HEADS,), b_f.dtype)])[None, :]
    return w_in_p, w_uq_p, w_ukv_p, b_f_p


def _aug_placement():
    pq = np.zeros((3 * LANES, N_PAIRS * LANES), np.float32)
    pk = np.zeros((3 * LANES, N_PAIRS * LANES), np.float32)
    oq = np.zeros((1, N_PAIRS * LANES), np.float32)
    ok = np.zeros((1, N_PAIRS * LANES), np.float32)
    for p in range(N_PAIRS):
        for j in range(2):
            h = 2 * p + j
            base = p * LANES + _X1_OFF + j * ROPE_H
            for t in range(3):
                pq[t * LANES + h, base + 3 + t] = 1.0
                pk[t * LANES + h, base + t] = -1.0
                oq[0, base + t] = 1.0
                ok[0, base + 3 + t] = 1.0
    return (jnp.asarray(pq, jnp.bfloat16), jnp.asarray(pk, jnp.bfloat16),
            jnp.asarray(oq), jnp.asarray(ok))


def _mla_rope_tables(s):
    inv = ROPE_BASE ** (-jnp.arange(ROPE_H, dtype=jnp.float32) / ROPE_H)
    ang = jnp.arange(s, dtype=jnp.float32)[:, None] * inv[None, :]
    c, sn = jnp.cos(ang), jnp.sin(ang)
    z = jnp.zeros((s, 32), jnp.float32)
    cos = jnp.concatenate([c, c, z, c, c, z], axis=1)
    sin = jnp.concatenate([-sn, -sn, z, sn, sn, z], axis=1)
    return cos, sin


def _even_proj(x2d, seq, g, w_in_p, ln_q, w_uq_p, ln_kv, w_ukv_p, b_f_p):
    t = x2d.shape[0]
    tm = TM_OUT
    tiles_per_seq = seq // tm
    cos, sin = _mla_rope_tables(seq)
    tri = jnp.asarray(np.tril(np.ones((tm, tm), np.float32)), jnp.bfloat16)
    pq, pk, oq, ok = _aug_placement()
    const = lambda shape: pl.BlockSpec(shape, lambda i: (0,) * len(shape))
    return pl.pallas_call(
        functools.partial(_even_proj_kernel, tiles_per_seq),
        grid=(t // tm,),
        in_specs=[
            pl.BlockSpec((tm, D_MODEL), lambda i: (i, 0)),
            const((1, D_MODEL)), const((D_MODEL, EVEN_N)),
            const((1, MLA_Q_LORA)), const((MLA_Q_LORA, N_PAIRS * PAIR_W)),
            const((1, MLA_KV_LORA)), const((MLA_KV_LORA, 2 * MLA_HEADS * MLA_NOPE)),
            const((1, LANES)),
            pl.BlockSpec((tm, LANES), lambda i: (i % tiles_per_seq, 0)),
            pl.BlockSpec((tm, LANES), lambda i: (i % tiles_per_seq, 0)),
            const((tm, tm)), const(pq.shape), const(pk.shape), const(oq.shape), const(ok.shape),
        ],
        out_specs=[
            pl.BlockSpec((2, tm, N_PAIRS * PAIR_W), lambda i: (0, i, 0)),
            pl.BlockSpec((2, tm, N_PAIRS * PAIR_W), lambda i: (0, i, 0)),
            pl.BlockSpec((2, None, MLA_HEADS * VT_ROWS, tm),
                         lambda i: (0, i // tiles_per_seq, 0, i % tiles_per_seq)),
        ],
        out_shape=[
            jax.ShapeDtypeStruct((2, t, N_PAIRS * PAIR_W), jnp.bfloat16),
            jax.ShapeDtypeStruct((2, t, N_PAIRS * PAIR_W), jnp.bfloat16),
            jax.ShapeDtypeStruct((2, t // seq, MLA_HEADS * VT_ROWS, seq), jnp.bfloat16),
        ],
        scratch_shapes=[pltpu.VMEM((1, LANES), jnp.float32)],
        compiler_params=_cparams(("arbitrary",)),
        name="even_proj",
    )(x2d, g, w_in_p, ln_q, w_uq_p, ln_kv, w_ukv_p, b_f_p, cos, sin, tri, pq, pk, oq, ok)


def _attn_kernel(group, tk, q_ref, k_ref, vt_ref, o_ref, qcat_ref, s0_ref, s1_ref, m0_ref, m1_ref,
                 a0_ref, a1_ref, acc_ref):
    i = pl.program_id(2)
    tq = q_ref.shape[0]
    s_bufs, m_bufs, a_bufs = (s0_ref, s1_ref), (m0_ref, m1_ref), (a0_ref, a1_ref)

    lane = lax.broadcasted_iota(jnp.int32, (1, PAIR_W), 1)

    def head_mask(j):
        a = (lane >= j * MLA_NOPE) & (lane < (j + 1) * MLA_NOPE)
        b = (lane >= LANES + _X1_OFF + j * ROPE_H) & (lane < LANES + _X1_OFF + (j + 1) * ROPE_H)
        c = (lane >= LANES + _X2_OFF + j * ROPE_H) & (lane < LANES + _X2_OFF + (j + 1) * ROPE_H)
        return a | b | c

    q = q_ref[...]
    zero = jnp.zeros_like(q)
    qcat_ref[0:tq, :] = jnp.where(head_mask(0), q, zero)
    qcat_ref[tq:2 * tq, :] = jnp.where(head_mask(1), q, zero)
    acc_ref[...] = jnp.zeros_like(acc_ref)

    q0 = i * tq
    n_chunks = q0 // tk + 1

    def scores(c, m_run, par, masked):
        kstart = pl.multiple_of(c * tk, tk)
        s = lax.dot_general(k_ref[pl.ds(kstart, tk), :], qcat_ref[...], (((1,), (1,)), ((), ())),
                            preferred_element_type=jnp.float32)
        if masked:
            col = lax.broadcasted_iota(jnp.int32, (1, 2 * tq), 1)
            qpos = q0 + jnp.where(col >= tq, col - tq, col)
            qlim = (qpos | (CHUNK - 1)) if group == 0 else qpos
            kpos = kstart + lax.broadcasted_iota(jnp.int32, (tk, 1), 0)
            s = jnp.where(kpos <= qlim, s, NEG)
        s_bufs[par][...] = s
        m_new = jnp.maximum(m_run, jnp.max(s, axis=0, keepdims=True))
        m_bufs[par][...] = m_new
        a_bufs[par][...] = jnp.exp2(m_run - m_new)

    def accumulate(c, par):
        kstart = pl.multiple_of(c * tk, tk)
        alpha = a_bufs[par][...]
        p = jnp.exp2(s_bufs[par][...] - m_bufs[par][...]).astype(jnp.bfloat16)
        for h in range(2):
            cols = slice(h * tq, (h + 1) * tq)
            acc_ref[h] = alpha[:, cols] * acc_ref[h] + jnp.dot(
                vt_ref[h * VT_ROWS:(h + 1) * VT_ROWS, pl.ds(kstart, tk)], p[:, cols],
                preferred_element_type=jnp.float32)

    def stage(c, par, masked):
        scores(c + 1, m_bufs[par][...], 1 - par, masked)
        accumulate(c, par)

    m_init = jnp.full((1, 2 * tq), NEG, jnp.float32)

    @pl.when(n_chunks == 1)
    def _():
        scores(0, m_init, 0, True)
        accumulate(0, 0)

    @pl.when(n_chunks > 1)
    def _():
        scores(0, m_init, 0, False)

    n_pairs = jnp.maximum(n_chunks - 2, 0) // 2

    def body(j, carry):
        for u in range(4):
            stage(4 * j + u, u % 2, False)
        return carry

    lax.fori_loop(0, n_pairs // 2, body, 0)

    @pl.when(n_pairs % 2 == 1)
    def _():
        stage(2 * n_pairs - 2, 0, False)
        stage(2 * n_pairs - 1, 1, False)
    c0 = 2 * n_pairs
    left = n_chunks - 1 - c0

    @pl.when((n_chunks > 1) & (left == 1))
    def _():
        stage(c0, 0, True)
        accumulate(c0 + 1, 1)

    @pl.when((n_chunks > 1) & (left == 2))
    def _():
        stage(c0, 0, False)
        stage(c0 + 1, 1, True)
        accumulate(c0 + 2, 0)

    for h in range(2):
        acc = acc_ref[h]
        o_ref[h * MLA_V:(h + 1) * MLA_V, :] = (acc[0:MLA_V] / acc[MLA_V:MLA_V + 1]).astype(o_ref.dtype)


def _attention(q, k, vt, batch, seq, group, tq, tk):
    nq = seq // tq
    f32 = jnp.float32
    return pl.pallas_call(
        functools.partial(_attn_kernel, group, tk),
        grid=(batch, N_PAIRS, nq),
        in_specs=[
            pl.BlockSpec((None, tq, PAIR_W), lambda b, p, i: (group, b * nq + i, p)),
            pl.BlockSpec((None, seq, PAIR_W), lambda b, p, i: (group, b, p)),
            pl.BlockSpec((None, None, 2 * VT_ROWS, seq), lambda b, p, i: (group, b, p, 0)),
        ],
        out_specs=pl.BlockSpec((None, LANES, tq), lambda b, p, i: (b, p, i)),
        out_shape=jax.ShapeDtypeStruct((batch, N_PAIRS * LANES, seq), jnp.bfloat16),
        scratch_shapes=[pltpu.VMEM((2 * tq, PAIR_W), jnp.bfloat16),
                        pltpu.VMEM((tk, 2 * tq), f32), pltpu.VMEM((tk, 2 * tq), f32),
                        pltpu.VMEM((1, 2 * tq), f32), pltpu.VMEM((1, 2 * tq), f32),
                        pltpu.VMEM((1, 2 * tq), f32), pltpu.VMEM((1, 2 * tq), f32),
                        pltpu.VMEM((2, VT_ROWS, tq), f32)],
        compiler_params=_cparams(("arbitrary",) * 3),
        name="attention_g%d" % group,
    )(q, k, vt)


def _router_t(lt):
    row = lax.broadcasted_iota(jnp.int32, lt.shape, 0)
    big = jnp.int32(LANES)
    ninf = -jnp.inf
    gl = jnp.where(row < N_GROUPS, lt, ninf)
    gmax = jnp.max(gl, axis=0, keepdims=True)
    gsel = jnp.min(jnp.where(gl == gmax, row, big), axis=0, keepdims=True)
    gsum = jnp.sum(jnp.where(row < N_GROUPS, jnp.exp(lt - gmax), 0.0), axis=0, keepdims=True)
    p_g = 1.0 / gsum
    lo = N_GROUPS + EXPERTS_PER_GROUP * gsel
    e = jnp.where((row >= lo) & (row < lo + EXPERTS_PER_GROUP), lt, ninf)
    m1 = jnp.max(e, axis=0, keepdims=True)
    i1 = jnp.min(jnp.where(e == m1, row, big), axis=0, keepdims=True)
    e2 = jnp.where(row == i1, ninf, e)
    m2 = jnp.max(e2, axis=0, keepdims=True)
    i2 = jnp.min(jnp.where(e2 == m2, row, big), axis=0, keepdims=True)
    a2 = jnp.exp(m2 - m1)
    den = 1.0 + a2
    g1 = p_g / den
    g2 = p_g * a2 / den
    f = lambda v: v.astype(jnp.float32)
    return jnp.where(row == 0, f(i1 - N_GROUPS),
                     jnp.where(row == 1, f(i2 - N_GROUPS),
                               jnp.where(row == 2, g1, jnp.where(row == 3, g2, 0.0))))


def _out_proj_kernel(n_o, transposed, x_ref, *refs):
    o_refs = refs[:n_o]
    w_ref, ln_ref, wr_ref, br_ref, xo_ref, xg_ref, route_ref = refs[n_o:]
    tm = x_ref.shape[0]
    hm = tm // OUT_SPLIT
    for part in range(OUT_SPLIT):
        rows = slice(part * hm, (part + 1) * hm)
        if transposed:
            y = x_ref[rows, :]
            nf = o_refs[0].shape[0]
            for g, o_ref in enumerate(o_refs):
                y = y + lax.dot_general(o_ref[:, rows], w_ref[g * nf:(g + 1) * nf, :],
                                        (((0,), (0,)), ((), ())), preferred_element_type=jnp.float32)
        else:
            y = x_ref[rows, :] + jnp.dot(o_refs[0][rows, :], w_ref[...], preferred_element_type=jnp.float32)
        xo_ref[rows, :] = y
        xn = y * _rms_scale(y) * ln_ref[...]
        hi = xn.astype(jnp.bfloat16)
        lo = (xn - hi.astype(jnp.float32)).astype(jnp.bfloat16)
        l2 = jnp.dot(jnp.concatenate([hi, lo], axis=1), wr_ref[...], preferred_element_type=jnp.float32)
        lt = (l2[:, :LANES] + l2[:, LANES:] + br_ref[...]).T
        route_ref[rows, :] = _router_t(lt).T
        for c in range(ROWS_PER_TOK):
            xg_ref[pl.ds(part * hm * ROWS_PER_TOK + c, hm, stride=ROWS_PER_TOK), :] = (
                xn[:, c * LANES:(c + 1) * LANES])


def _out_proj(x2d, o, w_out, ln, w_r, b_r, seq):
    t = x2d.shape[0]
    kdim = w_out.shape[0]
    tm = TM_OUT
    transposed = isinstance(o, tuple)
    tiles_per_seq = seq // tm
    if transposed:
        o_specs = [pl.BlockSpec((None, a.shape[1], tm), lambda i: (i // tiles_per_seq, 0, i % tiles_per_seq))
                   for a in o]
    else:
        o, o_specs = (o,), [pl.BlockSpec((tm, kdim), lambda i: (i, 0))]
    const = lambda shape: pl.BlockSpec(shape, lambda i: (0,) * len(shape))
    return pl.pallas_call(
        functools.partial(_out_proj_kernel, len(o), transposed),
        grid=(t // tm,),
        in_specs=[
            pl.BlockSpec((tm, D_MODEL), lambda i: (i, 0)),
            *o_specs,
            const((kdim, D_MODEL)), const((1, D_MODEL)), const((2 * D_MODEL, 2 * LANES)), const((1, LANES)),
        ],
        out_specs=[
            pl.BlockSpec((tm, D_MODEL), lambda i: (i, 0)),
            pl.BlockSpec((tm * ROWS_PER_TOK, LANES), lambda i: (i, 0)),
            pl.BlockSpec((tm, LANES), lambda i: (i, 0)),
        ],
        out_shape=[
            jax.ShapeDtypeStruct((t, D_MODEL), jnp.float32),
            jax.ShapeDtypeStruct((t * ROWS_PER_TOK, LANES), jnp.float32),
            jax.ShapeDtypeStruct((t, LANES), jnp.float32),
        ],
        compiler_params=_cparams(("arbitrary",)),
        name="out_proj_router",
    )(x2d, *o, w_out, ln, w_r, b_r)


def _router_weights(w_rg, b_rg, w_re, b_re):
    d = w_rg.shape[0]
    pad = LANES - N_GROUPS - N_EXPERTS
    w_r = jnp.concatenate([w_rg, w_re, jnp.zeros((d, pad), w_rg.dtype)], axis=1)
    b_r = jnp.concatenate([b_rg, b_re, jnp.zeros((pad,), b_rg.dtype)])[None, :]
    w_hi = w_r.astype(jnp.bfloat16)
    w_lo = (w_r - w_hi.astype(jnp.float32)).astype(jnp.bfloat16)
    half = jnp.concatenate([w_hi, w_lo], axis=1)
    return jnp.concatenate([half, half], axis=0), b_r


def _dispatch_plan(eid, t, te):
    a = t * TOP_K
    nb = a // te + N_EXPERTS
    e_flat = eid.reshape(a)
    experts = jnp.arange(N_EXPERTS, dtype=jnp.int32)
    _, order = lax.sort((e_flat, jnp.arange(a, dtype=jnp.int32)), num_keys=1, is_stable=True)
    counts = jnp.sum((e_flat[:, None] == experts[None, :]).astype(jnp.int32), axis=0)
    start = jnp.cumsum(counts) - counts
    padded = ((counts + te - 1) // te) * te
    pend = jnp.cumsum(padded)
    pstart = pend - padded
    blk0 = jnp.arange(nb, dtype=jnp.int32) * te
    blk_e = jnp.minimum(jnp.sum((blk0[:, None] >= pend[None, :]).astype(jnp.int32), axis=1), N_EXPERTS - 1)
    blk_used = (blk0 < pend[-1]).astype(jnp.int32)
    onehot = (blk_e[:, None] == experts[None, :]).astype(jnp.int32)
    per_blk = lambda v: jnp.sum(onehot * v[None, :], axis=1)[:, None]
    r = jnp.arange(te, dtype=jnp.int32)[None, :]
    off = blk0[:, None] + r - per_blk(pstart)
    valid = (off < per_blk(counts)) & (blk_used[:, None] > 0)
    src = jnp.where(valid, per_blk(start) + off, 0)
    assign = order[src]
    src_tok = jnp.where(valid, assign // TOP_K, 0).astype(jnp.int32)
    dst_row = jnp.where(valid, assign, a + r).astype(jnp.int32)
    nxt = jnp.minimum(jnp.arange(nb) + 1, nb - 1)
    return blk_e, blk_used, jnp.concatenate([src_tok, src_tok[nxt], dst_row], axis=1)[:, None, :]


def _moe_kernel(te, blk_e_ref, blk_used_ref, idx_ref, xg_ref, wg_ref, wu_ref, wd_ref,
                ys_ref, xbuf, ybuf, wg_bf, wu_bf, wd_bf, gsem, ssem):
    i = pl.program_id(0)
    nb = pl.num_programs(0)
    used = blk_used_ref[i] > 0
    slot = i % 2
    block_rows = te * ROWS_PER_TOK

    def start_gather(first, s):
        for r in range(te):
            tok = idx_ref[0, 0, first + r]
            pltpu.make_async_copy(
                xg_ref.at[pl.ds(pl.multiple_of(tok * ROWS_PER_TOK, ROWS_PER_TOK), ROWS_PER_TOK), :],
                xbuf.at[s, pl.ds(r * ROWS_PER_TOK, ROWS_PER_TOK), :],
                gsem.at[s]).start(priority=r % 2)

    def wait_gather(s):
        pltpu.make_async_copy(xg_ref.at[pl.ds(0, block_rows), :], xbuf.at[s], gsem.at[s]).wait()

    def start_scatter():
        for r in range(te):
            row = idx_ref[0, 0, 2 * te + r]
            pltpu.make_async_copy(
                ybuf.at[pl.ds(r * ROWS_PER_TOK, ROWS_PER_TOK), :],
                ys_ref.at[pl.ds(pl.multiple_of(row * ROWS_PER_TOK, ROWS_PER_TOK), ROWS_PER_TOK), :],
                ssem.at[0]).start(priority=r % 2)

    def wait_scatter():
        pltpu.make_async_copy(ybuf, ys_ref.at[pl.ds(0, block_rows), :], ssem.at[0]).wait()

    @pl.when(i == 0)
    def _():
        ybuf[...] = jnp.zeros_like(ybuf)
        pltpu.make_async_copy(
            ybuf, ys_ref.at[pl.ds(ys_ref.shape[0] - block_rows, block_rows), :], ssem.at[0]).start()
        start_gather(0, 0)

    @pl.when(used)
    def _():
        start_gather(te, 1 - slot)
        prev_e = blk_e_ref[jnp.maximum(i - 1, 0)]

        @pl.when((i == 0) | (blk_e_ref[i] != prev_e))
        def _():
            wg_bf[...] = wg_ref[...].astype(jnp.bfloat16)
            wu_bf[...] = wu_ref[...].astype(jnp.bfloat16)
            wd_bf[...] = wd_ref[...].astype(jnp.bfloat16)

        wait_gather(slot)
        xb = xbuf.at[slot]
        x = jnp.concatenate([xb[pl.ds(c, te, stride=ROWS_PER_TOK), :] for c in range(ROWS_PER_TOK)],
                            axis=1).astype(jnp.bfloat16)
        hg = jnp.dot(x, wg_bf[...], preferred_element_type=jnp.float32)
        hu = jnp.dot(x, wu_bf[...], preferred_element_type=jnp.float32)
        hid = (hg * jax.nn.sigmoid(hg) * hu).astype(jnp.bfloat16)
        y = jnp.dot(hid, wd_bf[...], preferred_element_type=jnp.float32)

        wait_scatter()
        for c in range(ROWS_PER_TOK):
            ybuf[pl.ds(c, te, stride=ROWS_PER_TOK), :] = y[:, c * LANES:(c + 1) * LANES]
        start_scatter()

        next_used = (i + 1 < nb) & (blk_used_ref[jnp.minimum(i + 1, nb - 1)] > 0)

        @pl.when(jnp.logical_not(next_used))
        def _():
            wait_gather(1 - slot)
            wait_scatter()


def _moe(xg, eid, w_gate, w_up, w_down, layer, t):
    te = TE
    a = t * TOP_K
    nb = a // te + N_EXPERTS
    blk_e, blk_used, idx = _dispatch_plan(eid, t, te)
    grid_spec = pltpu.PrefetchScalarGridSpec(
        num_scalar_prefetch=2,
        grid=(nb,),
        in_specs=[
            pl.BlockSpec((1, 1, 3 * te), lambda i, be, bu: (i, 0, 0), memory_space=pltpu.SMEM),
            pl.BlockSpec(memory_space=pl.ANY),
            pl.BlockSpec((None, None, D_MODEL, D_EXPERT), lambda i, be, bu: (layer, be[i], 0, 0)),
            pl.BlockSpec((None, None, D_MODEL, D_EXPERT), lambda i, be, bu: (layer, be[i], 0, 0)),
            pl.BlockSpec((None, None, D_EXPERT, D_MODEL), lambda i, be, bu: (layer, be[i], 0, 0)),
        ],
        out_specs=pl.BlockSpec(memory_space=pl.ANY),
        scratch_shapes=[
            pltpu.VMEM((2, te * ROWS_PER_TOK, LANES), jnp.float32),
            pltpu.VMEM((te * ROWS_PER_TOK, LANES), jnp.float32),
            pltpu.VMEM((D_MODEL, D_EXPERT), jnp.bfloat16),
            pltpu.VMEM((D_MODEL, D_EXPERT), jnp.bfloat16),
            pltpu.VMEM((D_EXPERT, D_MODEL), jnp.bfloat16),
            pltpu.SemaphoreType.DMA((2,)),
            pltpu.SemaphoreType.DMA((1,)),
        ],
    )
    return pl.pallas_call(
        functools.partial(_moe_kernel, te),
        grid_spec=grid_spec,
        out_shape=jax.ShapeDtypeStruct(((a + te) * ROWS_PER_TOK, LANES), jnp.float32),
        compiler_params=_cparams(("arbitrary",)),
        name="moe_experts",
    )(blk_e, blk_used, idx, xg, w_gate, w_up, w_down)


def _combine(x, ys_ref, route_ref, tm):
    stride = TOP_K * ROWS_PER_TOK
    y0 = jnp.concatenate([ys_ref[pl.ds(c, tm, stride=stride), :] for c in range(ROWS_PER_TOK)], axis=1)
    y1 = jnp.concatenate([ys_ref[pl.ds(ROWS_PER_TOK + c, tm, stride=stride), :]
                          for c in range(ROWS_PER_TOK)], axis=1)
    route = route_ref[...]
    return x + route[:, 2:3] * y0 + route[:, 3:4] * y1


def _odd_proj_kernel(x_ref, ys_ref, route_ref, g_ref, w_ref, cos_ref, sin_ref,
                     xo_ref, q_ref, k_ref, v_ref, gate_ref):
    tm = x_ref.shape[0]
    x_all = _combine(x_ref[...], ys_ref, route_ref, tm)
    xo_ref[...] = x_all
    nqk = RET_HEADS * RET_QK
    nv = RET_HEADS * RET_V
    half = RET_QK // 2
    hm = tm // 2
    for part in range(2):
        rows = slice(part * hm, (part + 1) * hm)
        x = x_all[rows]
        xn = (x * _rms_scale(x) * g_ref[...]).astype(jnp.bfloat16)
        cos = cos_ref[rows, :]
        sin = sin_ref[rows, :]

        def rope_store(z, dst, scale):
            for h in range(RET_HEADS):
                x1 = z[:, h * RET_QK:h * RET_QK + half]
                x2 = z[:, h * RET_QK + half:(h + 1) * RET_QK]
                dst[rows, h * RET_QK:h * RET_QK + half] = ((x1 * cos - x2 * sin) * scale).astype(dst.dtype)
                dst[rows, h * RET_QK + half:(h + 1) * RET_QK] = ((x1 * sin + x2 * cos) * scale).astype(dst.dtype)

        zq = jnp.dot(xn, w_ref[:, 0:nqk], preferred_element_type=jnp.float32)
        rope_store(zq, q_ref, 1.0)
        zk = jnp.dot(xn, w_ref[:, nqk:2 * nqk], preferred_element_type=jnp.float32)
        rope_store(zk, k_ref, RET_QK ** -0.5)
        zv = jnp.dot(xn, w_ref[:, 2 * nqk:2 * nqk + nv], preferred_element_type=jnp.float32)
        v_ref[rows, :] = zv.astype(v_ref.dtype)
        zg = jnp.dot(xn, w_ref[:, 2 * nqk + nv:2 * nqk + 2 * nv], preferred_element_type=jnp.float32)
        gate_ref[rows, :] = (zg * jax.nn.sigmoid(zg)).astype(gate_ref.dtype)


def _ret_rope_tables(s):
    half = RET_QK // 2
    inv = ROPE_BASE ** (-jnp.arange(half, dtype=jnp.float32) / half)
    ang = jnp.arange(s, dtype=jnp.float32)[:, None] * inv[None, :]
    return jnp.cos(ang), jnp.sin(ang)


def _odd_proj(x2d, ys, route, seq, g, w_in):
    t = x2d.shape[0]
    tm = TM_OUT
    tiles_per_seq = seq // tm
    cos, sin = _ret_rope_tables(seq)
    nqk = RET_HEADS * RET_QK
    nv = RET_HEADS * RET_V
    n_in = w_in.shape[1]
    const = lambda shape: pl.BlockSpec(shape, lambda i: (0,) * len(shape))
    tok = lambda n: pl.BlockSpec((tm, n), lambda i: (i, 0))
    return pl.pallas_call(
        _odd_proj_kernel,
        grid=(t // tm,),
        in_specs=[
            tok(D_MODEL),
            pl.BlockSpec((tm * TOP_K * ROWS_PER_TOK, LANES), lambda i: (i, 0)),
            tok(LANES),
            const((1, D_MODEL)), pl.BlockSpec((D_MODEL, n_in), lambda i: (0, 0), pipeline_mode=pl.Buffered(1)),
            pl.BlockSpec((tm, LANES), lambda i: (i % tiles_per_seq, 0)),
            pl.BlockSpec((tm, LANES), lambda i: (i % tiles_per_seq, 0)),
        ],
        out_specs=[tok(D_MODEL), tok(nqk), tok(nqk), tok(nv), tok(nv)],
        out_shape=[
            jax.ShapeDtypeStruct((t, D_MODEL), jnp.float32),
            jax.ShapeDtypeStruct((t, nqk), jnp.bfloat16),
            jax.ShapeDtypeStruct((t, nqk), jnp.bfloat16),
            jax.ShapeDtypeStruct((t, nv), jnp.bfloat16),
            jax.ShapeDtypeStruct((t, nv), jnp.bfloat16),
        ],
        compiler_params=_cparams(("arbitrary",)),
        name="odd_proj",
    )(x2d, ys, route, g, w_in, cos, sin)


def _retention_kernel(q_ref, k_ref, v_ref, gate_ref, dmask_ref, xi_ref, zeta_ref, gc_ref, o_ref, state_ref):
    n = pl.program_id(2)

    @pl.when(n == 0)
    def _():
        state_ref[...] = jnp.zeros_like(state_ref)

    xi = xi_ref[...][:, 0:1]
    zeta = zeta_ref[...][:, 0:1]
    g_c = gc_ref[...][:, 0:1]
    for j in range(q_ref.shape[0] // RET_C):
        rows = slice(j * RET_C, (j + 1) * RET_C)
        q = q_ref[rows, :]
        k = k_ref[rows, :]
        v = v_ref[rows, :]
        inner = lax.dot_general(q, k, (((1,), (1,)), ((), ())),
                                preferred_element_type=jnp.float32) * dmask_ref[...]
        o = jnp.dot(inner.astype(jnp.bfloat16), v, preferred_element_type=jnp.float32)
        state = state_ref[...]
        cross = jnp.dot(q, state.astype(jnp.bfloat16), preferred_element_type=jnp.float32)
        o = o + cross * xi
        kz = (k.astype(jnp.float32) * zeta).astype(jnp.bfloat16)
        upd = lax.dot_general(kz, v, (((0,), (0,)), ((), ())), preferred_element_type=jnp.float32)
        state_ref[...] = g_c * state + upd
        o = o * _rms_scale(o)
        o_ref[rows, :] = (o * gate_ref[rows, :].astype(jnp.float32)).astype(o_ref.dtype)


def _retention_tables():
    c = RET_C
    log_g = jnp.log(1.0 - jnp.exp2(-5.0 - jnp.arange(RET_HEADS, dtype=jnp.float32)))
    j = jnp.arange(c, dtype=jnp.float32)
    diff = j[:, None] - j[None, :]
    dmask = jnp.where(diff >= 0, jnp.exp(jnp.maximum(diff, 0.0)[None] * log_g[:, None, None]), 0.0)
    xi = jnp.exp((j[None, :] + 1.0) * log_g[:, None])
    zeta = jnp.exp((c - 1.0 - j)[None, :] * log_g[:, None])
    g_c = jnp.exp(c * log_g)
    bc = lambda a: jnp.broadcast_to(a[..., None], a.shape + (LANES,))
    return dmask, bc(xi), bc(zeta), bc(g_c[:, None])


def _retention(q, k, v, gate, batch, seq):
    nc = seq // RET_BLK
    dmask, xi, zeta, g_c = _retention_tables()
    row = lambda n: pl.BlockSpec((RET_BLK, n), lambda b, h, c: (b * nc + c, h))
    head = lambda shape: pl.BlockSpec((None,) + shape, lambda b, h, c: (h, 0, 0))
    return pl.pallas_call(
        _retention_kernel,
        grid=(batch, RET_HEADS, nc),
        in_specs=[row(RET_QK), row(RET_QK), row(RET_V), row(RET_V),
                  head((RET_C, RET_C)), head((RET_C, LANES)), head((RET_C, LANES)), head((1, LANES))],
        out_specs=row(RET_V),
        out_shape=jax.ShapeDtypeStruct((batch * seq, RET_HEADS * RET_V), jnp.bfloat16),
        scratch_shapes=[pltpu.VMEM((RET_QK, RET_V), jnp.float32)],
        compiler_params=_cparams(("arbitrary",) * 3),
        name="retention",
    )(q, k, v, gate, dmask, xi, zeta, g_c)


def _final_kernel(x_ref, ys_ref, route_ref, g_ref, o_ref):
    x = _combine(x_ref[...], ys_ref, route_ref, x_ref.shape[0])
    o_ref[...] = x * _rms_scale(x) * g_ref[...]


def _final(x2d, ys, route, g):
    t = x2d.shape[0]
    tm = TM_OUT
    return pl.pallas_call(
        _final_kernel,
        grid=(t // tm,),
        in_specs=[
            pl.BlockSpec((tm, D_MODEL), lambda i: (i, 0)),
            pl.BlockSpec((tm * TOP_K * ROWS_PER_TOK, LANES), lambda i: (i, 0)),
            pl.BlockSpec((tm, LANES), lambda i: (i, 0)),
            pl.BlockSpec((1, D_MODEL), lambda i: (0, 0)),
        ],
        out_specs=pl.BlockSpec((tm, D_MODEL), lambda i: (i, 0)),
        out_shape=jax.ShapeDtypeStruct((t, D_MODEL), jnp.float32),
        compiler_params=_cparams(("arbitrary",)),
        name="final_norm",
    )(x2d, ys, route, g)


def kernel(x, ln_mix_e, w_in_e, ln_q_e, w_uq_e, ln_kv_e, w_ukv_e, b_f_e, w_out_e, ln_mix_o, w_in_o,
           w_out_o, ln_ffn, w_rg, b_rg, w_re, b_re, w_gate, w_up, w_down, ln_f):
    batch, seq, d = x.shape
    t = batch * seq
    assert d == D_MODEL and seq % TK == 0 and seq % TQ == 0 and seq % RET_BLK == 0
    assert seq % TM_PROJ == 0 and t % TM_OUT == 0 and t % TE == 0
    bf16 = jnp.bfloat16
    x2d = x.reshape(t, d)

    w_in_p, w_uq_p, w_ukv_p, b_f_p = _even_weights(w_in_e[0], w_uq_e[0], w_ukv_e[0], b_f_e[0])
    q, k, vt = _even_proj(x2d, seq, ln_mix_e[0][None, :], w_in_p, ln_q_e[0][None, :], w_uq_p,
                         ln_kv_e[0][None, :], w_ukv_p, b_f_p)
    o_t = (_attention(q, k, vt, batch, seq, 0, TQ, TK), _attention(q, k, vt, batch, seq, 1, TQ, TK))
    w_r, b_r = _router_weights(w_rg[0], b_rg[0], w_re[0], b_re[0])
    x2d, xg, route = _out_proj(x2d, o_t, w_out_e[0].astype(bf16), ln_ffn[0][None, :], w_r, b_r, seq)
    ys = _moe(xg, route[:, 0:TOP_K].astype(jnp.int32), w_gate, w_up, w_down, 0, t)

    x2d, rq, rk, rv, rg = _odd_proj(x2d, ys, route, seq, ln_mix_o[0][None, :], w_in_o[0].astype(bf16))
    og = _retention(rq, rk, rv, rg, batch, seq)
    w_r, b_r = _router_weights(w_rg[1], b_rg[1], w_re[1], b_re[1])
    x2d, xg, route = _out_proj(x2d, og, w_out_o[0].astype(bf16), ln_ffn[1][None, :], w_r, b_r, seq)
    ys = _moe(xg, route[:, 0:TOP_K].astype(jnp.int32), w_gate, w_up, w_down, 1, t)

    out = _final(x2d, ys, route, ln_f[None, :])
    return out.reshape(batch, seq, d)
```

```python
import functools
import math

import numpy as np
import jax
import jax.numpy as jnp
from jax import lax
from jax.experimental import pallas as pl
from jax.experimental.pallas import tpu as pltpu

D_MODEL = 1024
CHUNK = 64
ROPE_BASE = 10000.0
EPS = 1e-6
MLA_HEADS = 8
MLA_NOPE = 64
MLA_ROPE = 32
MLA_V = 64
MLA_Q_LORA = 256
MLA_KV_LORA = 128
FOX_HEADS = 8
FOX_DIM = 64
RET_HEADS = 4
RET_QK = 256
RET_V = 512
N_GROUPS = 4
EXPERTS_PER_GROUP = 8
N_EXPERTS = N_GROUPS * EXPERTS_PER_GROUP
TOP_K = 2
D_EXPERT = 512

LANES = 128
SUBLANES = 8
VMEM_LIMIT = 52 * 1024 * 1024

TM_PROJ = 256
TM_OUT = 512
OUT_SPLIT = 2
TQ = 1024
VT_ROWS = 80
TK = 1024
LOG2E = math.log2(math.e)
RET_C = 256
RET_BLK = 2048
TE = 256
ROWS_PER_TOK = D_MODEL // LANES

N_PAIRS = MLA_HEADS // 2
PAIR_W = 256
ROPE_H = MLA_ROPE // 2
NEG = -1e30

_X1_OFF = 0
_X2_OFF = 64
_AUG_W = 6

_C_Q = 0
_C_KV = _C_Q + MLA_Q_LORA
_C_KPE = _C_KV + MLA_KV_LORA
_C_QF = _C_KPE + LANES
_C_KF = _C_QF + FOX_HEADS * FOX_DIM
_C_VF = _C_KF + FOX_HEADS * FOX_DIM
_C_F = _C_VF + FOX_HEADS * FOX_DIM
EVEN_N = _C_F + LANES


def _cparams(sem):
    return pltpu.CompilerParams(dimension_semantics=sem, vmem_limit_bytes=VMEM_LIMIT)


def _rms_scale(x):
    return lax.rsqrt(jnp.mean(x * x, axis=-1, keepdims=True) + EPS)


def _split3(c):
    hi = c.astype(jnp.bfloat16)
    r1 = c - hi.astype(jnp.float32)
    mid = r1.astype(jnp.bfloat16)
    r2 = r1 - mid.astype(jnp.float32)
    lo = r2.astype(jnp.bfloat16)
    return hi, mid, lo


def _even_proj_kernel(tiles_per_seq, x_ref, g_ref, win_ref, lnq_ref, wuq_ref, lnkv_ref, wukv_ref,
                      bf_ref, cos_ref, sin_ref, tri_ref, pq_ref, pk_ref, oq_ref, ok_ref,
                      q_ref, k_ref, vt_ref, carry_ref):
    i = pl.program_id(0)
    x = x_ref[...]
    xn = (x * _rms_scale(x) * g_ref[...]).astype(jnp.bfloat16)
    z = jnp.dot(xn, win_ref[...], preferred_element_type=jnp.float32)
    cos = cos_ref[...]
    sin = sin_ref[...]

    def rope_slab(s):
        return s * cos + pltpu.roll(s, 64, axis=1) * sin

    c_q = z[:, _C_Q:_C_Q + MLA_Q_LORA]
    cqn = (c_q * _rms_scale(c_q) * lnq_ref[...]).astype(jnp.bfloat16)
    q = jnp.dot(cqn, wuq_ref[...], preferred_element_type=jnp.float32)
    scale_a = (MLA_NOPE + MLA_ROPE) ** -0.5 * LOG2E
    for p in range(N_PAIRS):
        lo = p * PAIR_W
        q_ref[0, :, lo:lo + LANES] = (q[:, lo:lo + LANES] * scale_a).astype(jnp.bfloat16)
        q_ref[0, :, lo + LANES:lo + PAIR_W] = (
            rope_slab(q[:, lo + LANES:lo + PAIR_W]) * scale_a).astype(jnp.bfloat16)

    c_kv = z[:, _C_KV:_C_KV + MLA_KV_LORA]
    ckn = (c_kv * _rms_scale(c_kv) * lnkv_ref[...]).astype(jnp.bfloat16)
    kv = jnp.dot(ckn, wukv_ref[...], preferred_element_type=jnp.float32)
    kpe = rope_slab(z[:, _C_KPE:_C_KPE + LANES]).astype(jnp.bfloat16)
    nk = MLA_HEADS * MLA_NOPE
    for p in range(N_PAIRS):
        lo = p * PAIR_W
        k_ref[0, :, lo:lo + LANES] = kv[:, p * LANES:(p + 1) * LANES].astype(jnp.bfloat16)
        k_ref[0, :, lo + LANES:lo + PAIR_W] = kpe
    _store_values_transposed(vt_ref, 0, kv[:, nk:nk + MLA_HEADS * MLA_V])

    lane = lax.broadcasted_iota(jnp.int32, (1, LANES), 1)
    fz = z[:, _C_F:_C_F + LANES] + bf_ref[...]
    log_f = -(jnp.maximum(-fz, 0.0) + jnp.log1p(jnp.exp(-jnp.abs(fz))))
    log_f = jnp.where(lane < FOX_HEADS, log_f, 0.0)

    @pl.when(i % tiles_per_seq == 0)
    def _():
        carry_ref[...] = jnp.zeros_like(carry_ref)

    hi, mid, lo3 = _split3(log_f)
    tri = tri_ref[...]
    cum = (jnp.dot(tri, hi, preferred_element_type=jnp.float32)
           + jnp.dot(tri, mid, preferred_element_type=jnp.float32)
           + jnp.dot(tri, lo3, preferred_element_type=jnp.float32)) + carry_ref[...]
    tm = cum.shape[0]
    carry_ref[...] = cum[tm - 1:tm, :]

    parts = jnp.concatenate(_split3(cum * LOG2E), axis=1)
    aug_q = jnp.dot(parts, pq_ref[...], preferred_element_type=jnp.float32) + oq_ref[...]
    aug_k = jnp.dot(parts, pk_ref[...], preferred_element_type=jnp.float32) + ok_ref[...]
    scale_b = FOX_DIM ** -0.5 * LOG2E
    for p in range(N_PAIRS):
        lo = p * PAIR_W
        q_ref[1, :, lo:lo + LANES] = (
            z[:, _C_QF + p * LANES:_C_QF + (p + 1) * LANES] * scale_b).astype(jnp.bfloat16)
        q_ref[1, :, lo + LANES:lo + PAIR_W] = aug_q[:, p * LANES:(p + 1) * LANES].astype(jnp.bfloat16)
        k_ref[1, :, lo:lo + LANES] = z[:, _C_KF + p * LANES:_C_KF + (p + 1) * LANES].astype(jnp.bfloat16)
        k_ref[1, :, lo + LANES:lo + PAIR_W] = aug_k[:, p * LANES:(p + 1) * LANES].astype(jnp.bfloat16)
    _store_values_transposed(vt_ref, 1, z[:, _C_VF:_C_VF + FOX_HEADS * FOX_DIM])


def _store_values_transposed(vt_ref, g, v):
    vt = v.T
    ones = jnp.ones((VT_ROWS - MLA_V, vt.shape[1]), vt_ref.dtype)
    for h in range(MLA_HEADS):
        vt_ref[g, h * VT_ROWS:h * VT_ROWS + MLA_V, :] = vt[h * MLA_V:(h + 1) * MLA_V].astype(vt_ref.dtype)
        vt_ref[g, h * VT_ROWS + MLA_V:(h + 1) * VT_ROWS, :] = ones


def _even_weights(w_in, w_uq, w_ukv, b_f):
    bf16 = jnp.bfloat16
    w_in, w_uq, w_ukv = w_in.astype(bf16), w_uq.astype(bf16), w_ukv.astype(bf16)
    d = w_in.shape[0]
    zeros = lambda n: jnp.zeros((d, n), w_in.dtype)
    o_cq, o_ckv = 0, MLA_Q_LORA
    o_kpe = o_ckv + MLA_KV_LORA
    o_qf = o_kpe + MLA_ROPE
    o_kf = o_qf + FOX_HEADS * FOX_DIM
    o_vf = o_kf + FOX_HEADS * FOX_DIM
    o_f = o_vf + FOX_HEADS * FOX_DIM
    kpe1 = w_in[:, o_kpe:o_kpe + ROPE_H]
    kpe2 = w_in[:, o_kpe + ROPE_H:o_kpe + MLA_ROPE]
    kpe_slab = jnp.concatenate([kpe1, kpe1, zeros(32), kpe2, kpe2, zeros(32)], axis=1)
    f_slab = jnp.concatenate([w_in[:, o_f:o_f + FOX_HEADS], zeros(LANES - FOX_HEADS)], axis=1)
    w_in_p = jnp.concatenate([w_in[:, o_cq:o_kpe], kpe_slab, w_in[:, o_qf:o_f], f_slab], axis=1)

    dq = w_uq.shape[0]
    zq = lambda n: jnp.zeros((dq, n), w_uq.dtype)
    hd = MLA_NOPE + MLA_ROPE
    blocks = []
    for p in range(N_PAIRS):
        h0, h1 = 2 * p, 2 * p + 1
        nope = lambda h: w_uq[:, h * hd:h * hd + MLA_NOPE]
        r1 = lambda h: w_uq[:, h * hd + MLA_NOPE:h * hd + MLA_NOPE + ROPE_H]
        r2 = lambda h: w_uq[:, h * hd + MLA_NOPE + ROPE_H:(h + 1) * hd]
        blocks += [nope(h0), nope(h1), r1(h0), r1(h1), zq(32), r2(h0), r2(h1), zq(32)]
    w_uq_p = jnp.concatenate(blocks, axis=1)

    kvd = MLA_NOPE + MLA_V
    k_cols = [w_ukv[:, h * kvd:h * kvd + MLA_NOPE] for h in range(MLA_HEADS)]
    v_cols = [w_ukv[:, h * kvd + MLA_NOPE:(h + 1) * kvd] for h in range(MLA_HEADS)]
    w_ukv_p = jnp.concatenate(k_cols + v_cols, axis=1)
    b_f_p = jnp.concatenate([b_f, jnp.zeros((LANES - FOX_HEADS,), b_f.dtype)])[None, :]
    return w_in_p, w_uq_p, w_ukv_p, b_f_p


def _aug_placement():
    pq = np.zeros((3 * LANES, N_PAIRS * LANES), np.float32)
    pk = np.zeros((3 * LANES, N_PAIRS * LANES), np.float32)
    oq = np.zeros((1, N_PAIRS * LANES), np.float32)
    ok = np.zeros((1, N_PAIRS * LANES), np.float32)
    for p in range(N_PAIRS):
        for j in range(2):
            h = 2 * p + j
            base = p * LANES + _X1_OFF + j * ROPE_H
            for t in range(3):
                pq[t * LANES + h, base + 3 + t] = 1.0
                pk[t * LANES + h, base + t] = -1.0
                oq[0, base + t] = 1.0
                ok[0, base + 3 + t] = 1.0
    return (jnp.asarray(pq, jnp.bfloat16), jnp.asarray(pk, jnp.bfloat16),
            jnp.asarray(oq), jnp.asarray(ok))


def _mla_rope_tables(s):
    inv = ROPE_BASE ** (-jnp.arange(ROPE_H, dtype=jnp.float32) / ROPE_H)
    ang = jnp.arange(s, dtype=jnp.float32)[:, None] * inv[None, :]
    c, sn = jnp.cos(ang), jnp.sin(ang)
    z = jnp.zeros((s, 32), jnp.float32)
    cos = jnp.concatenate([c, c, z, c, c, z], axis=1)
    sin = jnp.concatenate([-sn, -sn, z, sn, sn, z], axis=1)
    return cos, sin


def _even_proj(x2d, seq, g, w_in_p, ln_q, w_uq_p, ln_kv, w_ukv_p, b_f_p):
    t = x2d.shape[0]
    tm = TM_OUT
    tiles_per_seq = seq // tm
    cos, sin = _mla_rope_tables(seq)
    tri = jnp.asarray(np.tril(np.ones((tm, tm), np.float32)), jnp.bfloat16)
    pq, pk, oq, ok = _aug_placement()
    const = lambda shape: pl.BlockSpec(shape, lambda i: (0,) * len(shape))
    return pl.pallas_call(
        functools.partial(_even_proj_kernel, tiles_per_seq),
        grid=(t // tm,),
        in_specs=[
            pl.BlockSpec((tm, D_MODEL), lambda i: (i, 0)),
            const((1, D_MODEL)), const((D_MODEL, EVEN_N)),
            const((1, MLA_Q_LORA)), const((MLA_Q_LORA, N_PAIRS * PAIR_W)),
            const((1, MLA_KV_LORA)), const((MLA_KV_LORA, 2 * MLA_HEADS * MLA_NOPE)),
            const((1, LANES)),
            pl.BlockSpec((tm, LANES), lambda i: (i % tiles_per_seq, 0)),
            pl.BlockSpec((tm, LANES), lambda i: (i % tiles_per_seq, 0)),
            const((tm, tm)), const(pq.shape), const(pk.shape), const(oq.shape), const(ok.shape),
        ],
        out_specs=[
            pl.BlockSpec((2, tm, N_PAIRS * PAIR_W), lambda i: (0, i, 0)),
            pl.BlockSpec((2, tm, N_PAIRS * PAIR_W), lambda i: (0, i, 0)),
            pl.BlockSpec((2, None, MLA_HEADS * VT_ROWS, tm),
                         lambda i: (0, i // tiles_per_seq, 0, i % tiles_per_seq)),
        ],
        out_shape=[
            jax.ShapeDtypeStruct((2, t, N_PAIRS * PAIR_W), jnp.bfloat16),
            jax.ShapeDtypeStruct((2, t, N_PAIRS * PAIR_W), jnp.bfloat16),
            jax.ShapeDtypeStruct((2, t // seq, MLA_HEADS * VT_ROWS, seq), jnp.bfloat16),
        ],
        scratch_shapes=[pltpu.VMEM((1, LANES), jnp.float32)],
        compiler_params=_cparams(("arbitrary",)),
        name="even_proj",
    )(x2d, g, w_in_p, ln_q, w_uq_p, ln_kv, w_ukv_p, b_f_p, cos, sin, tri, pq, pk, oq, ok)


def _attn_kernel(group, tk, stack_keys, q_ref, k_ref, vt_ref, o_ref, qcat_ref, s0_ref, s1_ref, m0_ref, m1_ref,
                 a0_ref, a1_ref, acc_ref):
    i = pl.program_id(2)
    tq = q_ref.shape[0]
    s_bufs, m_bufs, a_bufs = (s0_ref, s1_ref), (m0_ref, m1_ref), (a0_ref, a1_ref)

    lane = lax.broadcasted_iota(jnp.int32, (1, PAIR_W), 1)

    def head_mask(j):
        a = (lane >= j * MLA_NOPE) & (lane < (j + 1) * MLA_NOPE)
        b = (lane >= LANES + _X1_OFF + j * ROPE_H) & (lane < LANES + _X1_OFF + (j + 1) * ROPE_H)
        c = (lane >= LANES + _X2_OFF + j * ROPE_H) & (lane < LANES + _X2_OFF + (j + 1) * ROPE_H)
        return a | b | c

    q = q_ref[...]
    zero = jnp.zeros_like(q)
    qcat_ref[0:tq, :] = jnp.where(head_mask(0), q, zero)
    qcat_ref[tq:2 * tq, :] = jnp.where(head_mask(1), q, zero)
    acc_ref[...] = jnp.zeros_like(acc_ref)

    q0 = i * tq
    n_chunks = q0 // tk + 1

    def scores(c, m_run, par, masked):
        kstart = pl.multiple_of(c * tk, tk)
        if stack_keys:
            kc = k_ref[pl.ds(kstart, tk), :]
            kz = jnp.zeros_like(kc)
            kcat = jnp.concatenate([jnp.where(head_mask(0), kc, kz), jnp.where(head_mask(1), kc, kz)], axis=0)
            s2 = lax.dot_general(kcat, q_ref[...], (((1,), (1,)), ((), ())),
                                 preferred_element_type=jnp.float32)
            s = jnp.concatenate([s2[0:tk], s2[tk:2 * tk]], axis=1)
        else:
            s = lax.dot_general(k_ref[pl.ds(kstart, tk), :], qcat_ref[...], (((1,), (1,)), ((), ())),
                                preferred_element_type=jnp.float32)
        if masked:
            col = lax.broadcasted_iota(jnp.int32, (1, 2 * tq), 1)
            qpos = q0 + jnp.where(col >= tq, col - tq, col)
            qlim = (qpos | (CHUNK - 1)) if group == 0 else qpos
            kpos = kstart + lax.broadcasted_iota(jnp.int32, (tk, 1), 0)
            s = jnp.where(kpos <= qlim, s, NEG)
        s_bufs[par][...] = s
        m_new = jnp.maximum(m_run, jnp.max(s, axis=0, keepdims=True))
        m_bufs[par][...] = m_new
        a_bufs[par][...] = jnp.exp2(m_run - m_new)

    def accumulate(c, par):
        kstart = pl.multiple_of(c * tk, tk)
        alpha = a_bufs[par][...]
        p = jnp.exp2(s_bufs[par][...] - m_bufs[par][...]).astype(jnp.bfloat16)
        for h in range(2):
            cols = slice(h * tq, (h + 1) * tq)
            acc_ref[h] = alpha[:, cols] * acc_ref[h] + jnp.dot(
                vt_ref[h * VT_ROWS:(h + 1) * VT_ROWS, pl.ds(kstart, tk)], p[:, cols],
                preferred_element_type=jnp.float32)

    def stage(c, par, masked):
        scores(c + 1, m_bufs[par][...], 1 - par, masked)
        accumulate(c, par)

    m_init = jnp.full((1, 2 * tq), NEG, jnp.float32)

    @pl.when(n_chunks == 1)
    def _():
        scores(0, m_init, 0, True)
        accumulate(0, 0)

    @pl.when(n_chunks > 1)
    def _():
        scores(0, m_init, 0, False)

    n_pairs = jnp.maximum(n_chunks - 2, 0) // 2

    def body(j, carry):
        for u in range(4):
            stage(4 * j + u, u % 2, False)
        return carry

    lax.fori_loop(0, n_pairs // 2, body, 0)

    @pl.when(n_pairs % 2 == 1)
    def _():
        stage(2 * n_pairs - 2, 0, False)
        stage(2 * n_pairs - 1, 1, False)
    c0 = 2 * n_pairs
    left = n_chunks - 1 - c0

    @pl.when((n_chunks > 1) & (left == 1))
    def _():
        stage(c0, 0, True)
        accumulate(c0 + 1, 1)

    @pl.when((n_chunks > 1) & (left == 2))
    def _():
        stage(c0, 0, False)
        stage(c0 + 1, 1, True)
        accumulate(c0 + 2, 0)

    for h in range(2):
        acc = acc_ref[h]
        o_ref[h * MLA_V:(h + 1) * MLA_V, :] = (acc[0:MLA_V] / acc[MLA_V:MLA_V + 1]).astype(o_ref.dtype)


def _attention(q, k, vt, batch, seq, group, tq, tk, stack_keys):
    nq = seq // tq
    f32 = jnp.float32
    return pl.pallas_call(
        functools.partial(_attn_kernel, group, tk, stack_keys),
        grid=(batch, N_PAIRS, nq),
        in_specs=[
            pl.BlockSpec((None, tq, PAIR_W), lambda b, p, i: (group, b * nq + i, p)),
            pl.BlockSpec((None, seq, PAIR_W), lambda b, p, i: (group, b, p)),
            pl.BlockSpec((None, None, 2 * VT_ROWS, seq), lambda b, p, i: (group, b, p, 0)),
        ],
        out_specs=pl.BlockSpec((None, LANES, tq), lambda b, p, i: (b, p, i)),
        out_shape=jax.ShapeDtypeStruct((batch, N_PAIRS * LANES, seq), jnp.bfloat16),
        scratch_shapes=[pltpu.VMEM((2 * tq, PAIR_W), jnp.bfloat16),
                        pltpu.VMEM((tk, 2 * tq), f32), pltpu.VMEM((tk, 2 * tq), f32),
                        pltpu.VMEM((1, 2 * tq), f32), pltpu.VMEM((1, 2 * tq), f32),
                        pltpu.VMEM((1, 2 * tq), f32), pltpu.VMEM((1, 2 * tq), f32),
                        pltpu.VMEM((2, VT_ROWS, tq), f32)],
        compiler_params=_cparams(("arbitrary",) * 3),
        name="attention_g%d" % group,
    )(q, k, vt)


def _router_t(lt):
    row = lax.broadcasted_iota(jnp.int32, lt.shape, 0)
    big = jnp.int32(LANES)
    ninf = -jnp.inf
    gl = jnp.where(row < N_GROUPS, lt, ninf)
    gmax = jnp.max(gl, axis=0, keepdims=True)
    gsel = jnp.min(jnp.where(gl == gmax, row, big), axis=0, keepdims=True)
    gsum = jnp.sum(jnp.where(row < N_GROUPS, jnp.exp(lt - gmax), 0.0), axis=0, keepdims=True)
    p_g = 1.0 / gsum
    lo = N_GROUPS + EXPERTS_PER_GROUP * gsel
    e = jnp.where((row >= lo) & (row < lo + EXPERTS_PER_GROUP), lt, ninf)
    m1 = jnp.max(e, axis=0, keepdims=True)
    i1 = jnp.min(jnp.where(e == m1, row, big), axis=0, keepdims=True)
    e2 = jnp.where(row == i1, ninf, e)
    m2 = jnp.max(e2, axis=0, keepdims=True)
    i2 = jnp.min(jnp.where(e2 == m2, row, big), axis=0, keepdims=True)
    a2 = jnp.exp(m2 - m1)
    den = 1.0 + a2
    g1 = p_g / den
    g2 = p_g * a2 / den
    f = lambda v: v.astype(jnp.float32)
    return jnp.where(row == 0, f(i1 - N_GROUPS),
                     jnp.where(row == 1, f(i2 - N_GROUPS),
                               jnp.where(row == 2, g1, jnp.where(row == 3, g2, 0.0))))


def _out_proj_kernel(n_o, transposed, x_ref, *refs):
    o_refs = refs[:n_o]
    w_ref, ln_ref, wr_ref, br_ref, xo_ref, xg_ref, route_ref = refs[n_o:]
    tm = x_ref.shape[0]
    hm = tm // OUT_SPLIT
    for part in range(OUT_SPLIT):
        rows = slice(part * hm, (part + 1) * hm)
        if transposed:
            y = x_ref[rows, :]
            nf = o_refs[0].shape[0]
            for g, o_ref in enumerate(o_refs):
                y = y + lax.dot_general(o_ref[:, rows], w_ref[g * nf:(g + 1) * nf, :],
                                        (((0,), (0,)), ((), ())), preferred_element_type=jnp.float32)
        else:
            y = x_ref[rows, :] + jnp.dot(o_refs[0][rows, :], w_ref[...], preferred_element_type=jnp.float32)
        xo_ref[rows, :] = y
        xn = y * _rms_scale(y) * ln_ref[...]
        hi = xn.astype(jnp.bfloat16)
        lo = (xn - hi.astype(jnp.float32)).astype(jnp.bfloat16)
        l2 = jnp.dot(jnp.concatenate([hi, lo], axis=1), wr_ref[...], preferred_element_type=jnp.float32)
        lt = (l2[:, :LANES] + l2[:, LANES:] + br_ref[...]).T
        route_ref[rows, :] = _router_t(lt).T
        for c in range(ROWS_PER_TOK):
            xg_ref[pl.ds(part * hm * ROWS_PER_TOK + c, hm, stride=ROWS_PER_TOK), :] = (
                xn[:, c * LANES:(c + 1) * LANES])


def _out_proj(x2d, o, w_out, ln, w_r, b_r, seq):
    t = x2d.shape[0]
    kdim = w_out.shape[0]
    tm = TM_OUT
    transposed = isinstance(o, tuple)
    tiles_per_seq = seq // tm
    if transposed:
        o_specs = [pl.BlockSpec((None, a.shape[1], tm), lambda i: (i // tiles_per_seq, 0, i % tiles_per_seq))
                   for a in o]
    else:
        o, o_specs = (o,), [pl.BlockSpec((tm, kdim), lambda i: (i, 0))]
    const = lambda shape: pl.BlockSpec(shape, lambda i: (0,) * len(shape))
    return pl.pallas_call(
        functools.partial(_out_proj_kernel, len(o), transposed),
        grid=(t // tm,),
        in_specs=[
            pl.BlockSpec((tm, D_MODEL), lambda i: (i, 0)),
            *o_specs,
            const((kdim, D_MODEL)), const((1, D_MODEL)), const((2 * D_MODEL, 2 * LANES)), const((1, LANES)),
        ],
        out_specs=[
            pl.BlockSpec((tm, D_MODEL), lambda i: (i, 0)),
            pl.BlockSpec((tm * ROWS_PER_TOK, LANES), lambda i: (i, 0)),
            pl.BlockSpec((tm, LANES), lambda i: (i, 0)),
        ],
        out_shape=[
            jax.ShapeDtypeStruct((t, D_MODEL), jnp.float32),
            jax.ShapeDtypeStruct((t * ROWS_PER_TOK, LANES), jnp.float32),
            jax.ShapeDtypeStruct((t, LANES), jnp.float32),
        ],
        compiler_params=_cparams(("arbitrary",)),
        name="out_proj_router",
    )(x2d, *o, w_out, ln, w_r, b_r)


def _router_weights(w_rg, b_rg, w_re, b_re):
    d = w_rg.shape[0]
    pad = LANES - N_GROUPS - N_EXPERTS
    w_r = jnp.concatenate([w_rg, w_re, jnp.zeros((d, pad), w_rg.dtype)], axis=1)
    b_r = jnp.concatenate([b_rg, b_re, jnp.zeros((pad,), b_rg.dtype)])[None, :]
    w_hi = w_r.astype(jnp.bfloat16)
    w_lo = (w_r - w_hi.astype(jnp.float32)).astype(jnp.bfloat16)
    half = jnp.concatenate([w_hi, w_lo], axis=1)
    return jnp.concatenate([half, half], axis=0), b_r


def _dispatch_plan(eid, t, te):
    a = t * TOP_K
    nb = a // te + N_EXPERTS
    e_flat = eid.reshape(a)
    experts = jnp.arange(N_EXPERTS, dtype=jnp.int32)
    _, order = lax.sort((e_flat, jnp.arange(a, dtype=jnp.int32)), num_keys=1, is_stable=True)
    counts = jnp.sum((e_flat[:, None] == experts[None, :]).astype(jnp.int32), axis=0)
    start = jnp.cumsum(counts) - counts
    padded = ((counts + te - 1) // te) * te
    pend = jnp.cumsum(padded)
    pstart = pend - padded
    blk0 = jnp.arange(nb, dtype=jnp.int32) * te
    blk_e = jnp.minimum(jnp.sum((blk0[:, None] >= pend[None, :]).astype(jnp.int32), axis=1), N_EXPERTS - 1)
    blk_used = (blk0 < pend[-1]).astype(jnp.int32)
    onehot = (blk_e[:, None] == experts[None, :]).astype(jnp.int32)
    per_blk = lambda v: jnp.sum(onehot * v[None, :], axis=1)[:, None]
    r = jnp.arange(te, dtype=jnp.int32)[None, :]
    off = blk0[:, None] + r - per_blk(pstart)
    valid = (off < per_blk(counts)) & (blk_used[:, None] > 0)
    src = jnp.where(valid, per_blk(start) + off, 0)
    assign = order[src]
    src_tok = jnp.where(valid, assign // TOP_K, 0).astype(jnp.int32)
    dst_row = jnp.where(valid, assign, a + r).astype(jnp.int32)
    nxt = jnp.minimum(jnp.arange(nb) + 1, nb - 1)
    return blk_e, blk_used, jnp.concatenate([src_tok, src_tok[nxt], dst_row], axis=1)[:, None, :]


def _moe_kernel(te, blk_e_ref, blk_used_ref, idx_ref, xg_ref, wg_ref, wu_ref, wd_ref,
                ys_ref, xbuf, ybuf, wg_bf, wu_bf, wd_bf, gsem, ssem):
    i = pl.program_id(0)
    nb = pl.num_programs(0)
    used = blk_used_ref[i] > 0
    slot = i % 2
    block_rows = te * ROWS_PER_TOK

    def start_gather(first, s):
        for r in range(te):
            tok = idx_ref[0, 0, first + r]
            pltpu.make_async_copy(
                xg_ref.at[pl.ds(pl.multiple_of(tok * ROWS_PER_TOK, ROWS_PER_TOK), ROWS_PER_TOK), :],
                xbuf.at[s, pl.ds(r * ROWS_PER_TOK, ROWS_PER_TOK), :],
                gsem.at[s]).start(priority=r % 2)

    def wait_gather(s):
        pltpu.make_async_copy(xg_ref.at[pl.ds(0, block_rows), :], xbuf.at[s], gsem.at[s]).wait()

    def start_scatter():
        for r in range(te):
            row = idx_ref[0, 0, 2 * te + r]
            pltpu.make_async_copy(
                ybuf.at[pl.ds(r * ROWS_PER_TOK, ROWS_PER_TOK), :],
                ys_ref.at[pl.ds(pl.multiple_of(row * ROWS_PER_TOK, ROWS_PER_TOK), ROWS_PER_TOK), :],
                ssem.at[0]).start(priority=r % 2)

    def wait_scatter():
        pltpu.make_async_copy(ybuf, ys_ref.at[pl.ds(0, block_rows), :], ssem.at[0]).wait()

    @pl.when(i == 0)
    def _():
        ybuf[...] = jnp.zeros_like(ybuf)
        pltpu.make_async_copy(
            ybuf, ys_ref.at[pl.ds(ys_ref.shape[0] - block_rows, block_rows), :], ssem.at[0]).start()
        start_gather(0, 0)

    @pl.when(used)
    def _():
        start_gather(te, 1 - slot)
        prev_e = blk_e_ref[jnp.maximum(i - 1, 0)]

        @pl.when((i == 0) | (blk_e_ref[i] != prev_e))
        def _():
            wg_bf[...] = wg_ref[...].astype(jnp.bfloat16)
            wu_bf[...] = wu_ref[...].astype(jnp.bfloat16)
            wd_bf[...] = wd_ref[...].astype(jnp.bfloat16)

        wait_gather(slot)
        xb = xbuf.at[slot]
        x = jnp.concatenate([xb[pl.ds(c, te, stride=ROWS_PER_TOK), :] for c in range(ROWS_PER_TOK)],
                            axis=1).astype(jnp.bfloat16)
        hg = jnp.dot(x, wg_bf[...], preferred_element_type=jnp.float32)
        hu = jnp.dot(x, wu_bf[...], preferred_element_type=jnp.float32)
        hid = (hg * jax.nn.sigmoid(hg) * hu).astype(jnp.bfloat16)
        y = jnp.dot(hid, wd_bf[...], preferred_element_type=jnp.float32)

        wait_scatter()
        for c in range(ROWS_PER_TOK):
            ybuf[pl.ds(c, te, stride=ROWS_PER_TOK), :] = y[:, c * LANES:(c + 1) * LANES]
        start_scatter()

        next_used = (i + 1 < nb) & (blk_used_ref[jnp.minimum(i + 1, nb - 1)] > 0)

        @pl.when(jnp.logical_not(next_used))
        def _():
            wait_gather(1 - slot)
            wait_scatter()


def _moe(xg, eid, w_gate, w_up, w_down, layer, t):
    te = TE
    a = t * TOP_K
    nb = a // te + N_EXPERTS
    blk_e, blk_used, idx = _dispatch_plan(eid, t, te)
    grid_spec = pltpu.PrefetchScalarGridSpec(
        num_scalar_prefetch=2,
        grid=(nb,),
        in_specs=[
            pl.BlockSpec((1, 1, 3 * te), lambda i, be, bu: (i, 0, 0), memory_space=pltpu.SMEM),
            pl.BlockSpec(memory_space=pl.ANY),
            pl.BlockSpec((None, None, D_MODEL, D_EXPERT), lambda i, be, bu: (layer, be[i], 0, 0)),
            pl.BlockSpec((None, None, D_MODEL, D_EXPERT), lambda i, be, bu: (layer, be[i], 0, 0)),
            pl.BlockSpec((None, None, D_EXPERT, D_MODEL), lambda i, be, bu: (layer, be[i], 0, 0)),
        ],
        out_specs=pl.BlockSpec(memory_space=pl.ANY),
        scratch_shapes=[
            pltpu.VMEM((2, te * ROWS_PER_TOK, LANES), jnp.float32),
            pltpu.VMEM((te * ROWS_PER_TOK, LANES), jnp.float32),
            pltpu.VMEM((D_MODEL, D_EXPERT), jnp.bfloat16),
            pltpu.VMEM((D_MODEL, D_EXPERT), jnp.bfloat16),
            pltpu.VMEM((D_EXPERT, D_MODEL), jnp.bfloat16),
            pltpu.SemaphoreType.DMA((2,)),
            pltpu.SemaphoreType.DMA((1,)),
        ],
    )
    return pl.pallas_call(
        functools.partial(_moe_kernel, te),
        grid_spec=grid_spec,
        out_shape=jax.ShapeDtypeStruct(((a + te) * ROWS_PER_TOK, LANES), jnp.float32),
        compiler_params=_cparams(("arbitrary",)),
        name="moe_experts",
    )(blk_e, blk_used, idx, xg, w_gate, w_up, w_down)


def _combine(x, ys_ref, route_ref, tm):
    stride = TOP_K * ROWS_PER_TOK
    y0 = jnp.concatenate([ys_ref[pl.ds(c, tm, stride=stride), :] for c in range(ROWS_PER_TOK)], axis=1)
    y1 = jnp.concatenate([ys_ref[pl.ds(ROWS_PER_TOK + c, tm, stride=stride), :]
                          for c in range(ROWS_PER_TOK)], axis=1)
    route = route_ref[...]
    return x + route[:, 2:3] * y0 + route[:, 3:4] * y1


def _odd_proj_kernel(x_ref, ys_ref, route_ref, g_ref, w_ref, cos_ref, sin_ref,
                     xo_ref, q_ref, k_ref, v_ref, gate_ref):
    tm = x_ref.shape[0]
    x_all = _combine(x_ref[...], ys_ref, route_ref, tm)
    xo_ref[...] = x_all
    nqk = RET_HEADS * RET_QK
    nv = RET_HEADS * RET_V
    half = RET_QK // 2
    hm = tm // 2
    for part in range(2):
        rows = slice(part * hm, (part + 1) * hm)
        x = x_all[rows]
        xn = (x * _rms_scale(x) * g_ref[...]).astype(jnp.bfloat16)
        cos = cos_ref[rows, :]
        sin = sin_ref[rows, :]

        def rope_store(z, dst, scale):
            for h in range(RET_HEADS):
                x1 = z[:, h * RET_QK:h * RET_QK + half]
                x2 = z[:, h * RET_QK + half:(h + 1) * RET_QK]
                dst[rows, h * RET_QK:h * RET_QK + half] = ((x1 * cos - x2 * sin) * scale).astype(dst.dtype)
                dst[rows, h * RET_QK + half:(h + 1) * RET_QK] = ((x1 * sin + x2 * cos) * scale).astype(dst.dtype)

        zq = jnp.dot(xn, w_ref[:, 0:nqk], preferred_element_type=jnp.float32)
        rope_store(zq, q_ref, 1.0)
        zk = jnp.dot(xn, w_ref[:, nqk:2 * nqk], preferred_element_type=jnp.float32)
        rope_store(zk, k_ref, RET_QK ** -0.5)
        zv = jnp.dot(xn, w_ref[:, 2 * nqk:2 * nqk + nv], preferred_element_type=jnp.float32)
        v_ref[rows, :] = zv.astype(v_ref.dtype)
        zg = jnp.dot(xn, w_ref[:, 2 * nqk + nv:2 * nqk + 2 * nv], preferred_element_type=jnp.float32)
        gate_ref[rows, :] = (zg * jax.nn.sigmoid(zg)).astype(gate_ref.dtype)


def _ret_rope_tables(s):
    half = RET_QK // 2
    inv = ROPE_BASE ** (-jnp.arange(half, dtype=jnp.float32) / half)
    ang = jnp.arange(s, dtype=jnp.float32)[:, None] * inv[None, :]
    return jnp.cos(ang), jnp.sin(ang)


def _odd_proj(x2d, ys, route, seq, g, w_in):
    t = x2d.shape[0]
    tm = TM_OUT
    tiles_per_seq = seq // tm
    cos, sin = _ret_rope_tables(seq)
    nqk = RET_HEADS * RET_QK
    nv = RET_HEADS * RET_V
    n_in = w_in.shape[1]
    const = lambda shape: pl.BlockSpec(shape, lambda i: (0,) * len(shape))
    tok = lambda n: pl.BlockSpec((tm, n), lambda i: (i, 0))
    return pl.pallas_call(
        _odd_proj_kernel,
        grid=(t // tm,),
        in_specs=[
            tok(D_MODEL),
            pl.BlockSpec((tm * TOP_K * ROWS_PER_TOK, LANES), lambda i: (i, 0)),
            tok(LANES),
            const((1, D_MODEL)), pl.BlockSpec((D_MODEL, n_in), lambda i: (0, 0), pipeline_mode=pl.Buffered(1)),
            pl.BlockSpec((tm, LANES), lambda i: (i % tiles_per_seq, 0)),
            pl.BlockSpec((tm, LANES), lambda i: (i % tiles_per_seq, 0)),
        ],
        out_specs=[tok(D_MODEL), tok(nqk), tok(nqk), tok(nv), tok(nv)],
        out_shape=[
            jax.ShapeDtypeStruct((t, D_MODEL), jnp.float32),
            jax.ShapeDtypeStruct((t, nqk), jnp.bfloat16),
            jax.ShapeDtypeStruct((t, nqk), jnp.bfloat16),
            jax.ShapeDtypeStruct((t, nv), jnp.bfloat16),
            jax.ShapeDtypeStruct((t, nv), jnp.bfloat16),
        ],
        compiler_params=_cparams(("arbitrary",)),
        name="odd_proj",
    )(x2d, ys, route, g, w_in, cos, sin)


def _retention_kernel(q_ref, k_ref, v_ref, gate_ref, dmask_ref, xi_ref, zeta_ref, gc_ref, o_ref, state_ref):
    n = pl.program_id(2)

    @pl.when(n == 0)
    def _():
        state_ref[...] = jnp.zeros_like(state_ref)

    xi = xi_ref[...][:, 0:1]
    zeta = zeta_ref[...][:, 0:1]
    g_c = gc_ref[...][:, 0:1]
    for j in range(q_ref.shape[0] // RET_C):
        rows = slice(j * RET_C, (j + 1) * RET_C)
        q = q_ref[rows, :]
        k = k_ref[rows, :]
        v = v_ref[rows, :]
        inner = lax.dot_general(q, k, (((1,), (1,)), ((), ())),
                                preferred_element_type=jnp.float32) * dmask_ref[...]
        o = jnp.dot(inner.astype(jnp.bfloat16), v, preferred_element_type=jnp.float32)
        state = state_ref[...]
        cross = jnp.dot(q, state.astype(jnp.bfloat16), preferred_element_type=jnp.float32)
        o = o + cross * xi
        kz = (k.astype(jnp.float32) * zeta).astype(jnp.bfloat16)
        upd = lax.dot_general(kz, v, (((0,), (0,)), ((), ())), preferred_element_type=jnp.float32)
        state_ref[...] = g_c * state + upd
        o = o * _rms_scale(o)
        o_ref[rows, :] = (o * gate_ref[rows, :].astype(jnp.float32)).astype(o_ref.dtype)


def _retention_tables():
    c = RET_C
    log_g = jnp.log(1.0 - jnp.exp2(-5.0 - jnp.arange(RET_HEADS, dtype=jnp.float32)))
    j = jnp.arange(c, dtype=jnp.float32)
    diff = j[:, None] - j[None, :]
    dmask = jnp.where(diff >= 0, jnp.exp(jnp.maximum(diff, 0.0)[None] * log_g[:, None, None]), 0.0)
    xi = jnp.exp((j[None, :] + 1.0) * log_g[:, None])
    zeta = jnp.exp((c - 1.0 - j)[None, :] * log_g[:, None])
    g_c = jnp.exp(c * log_g)
    bc = lambda a: jnp.broadcast_to(a[..., None], a.shape + (LANES,))
    return dmask, bc(xi), bc(zeta), bc(g_c[:, None])


def _retention(q, k, v, gate, batch, seq):
    nc = seq // RET_BLK
    dmask, xi, zeta, g_c = _retention_tables()
    row = lambda n: pl.BlockSpec((RET_BLK, n), lambda b, h, c: (b * nc + c, h))
    head = lambda shape: pl.BlockSpec((None,) + shape, lambda b, h, c: (h, 0, 0))
    return pl.pallas_call(
        _retention_kernel,
        grid=(batch, RET_HEADS, nc),
        in_specs=[row(RET_QK), row(RET_QK), row(RET_V), row(RET_V),
                  head((RET_C, RET_C)), head((RET_C, LANES)), head((RET_C, LANES)), head((1, LANES))],
        out_specs=row(RET_V),
        out_shape=jax.ShapeDtypeStruct((batch * seq, RET_HEADS * RET_V), jnp.bfloat16),
        scratch_shapes=[pltpu.VMEM((RET_QK, RET_V), jnp.float32)],
        compiler_params=_cparams(("arbitrary",) * 3),
        name="retention",
    )(q, k, v, gate, dmask, xi, zeta, g_c)


def _final_kernel(x_ref, ys_ref, route_ref, g_ref, o_ref):
    x = _combine(x_ref[...], ys_ref, route_ref, x_ref.shape[0])
    o_ref[...] = x * _rms_scale(x) * g_ref[...]


def _final(x2d, ys, route, g):
    t = x2d.shape[0]
    tm = TM_OUT
    return pl.pallas_call(
        _final_kernel,
        grid=(t // tm,),
        in_specs=[
            pl.BlockSpec((tm, D_MODEL), lambda i: (i, 0)),
            pl.BlockSpec((tm * TOP_K * ROWS_PER_TOK, LANES), lambda i: (i, 0)),
            pl.BlockSpec((tm, LANES), lambda i: (i, 0)),
            pl.BlockSpec((1, D_MODEL), lambda i: (0, 0)),
        ],
        out_specs=pl.BlockSpec((tm, D_MODEL), lambda i: (i, 0)),
        out_shape=jax.ShapeDtypeStruct((t, D_MODEL), jnp.float32),
        compiler_params=_cparams(("arbitrary",)),
        name="final_norm",
    )(x2d, ys, route, g)


def kernel(x, ln_mix_e, w_in_e, ln_q_e, w_uq_e, ln_kv_e, w_ukv_e, b_f_e, w_out_e, ln_mix_o, w_in_o,
           w_out_o, ln_ffn, w_rg, b_rg, w_re, b_re, w_gate, w_up, w_down, ln_f):
    batch, seq, d = x.shape
    t = batch * seq
    assert d == D_MODEL and seq % TK == 0 and seq % TQ == 0 and seq % RET_BLK == 0
    assert seq % TM_PROJ == 0 and t % TM_OUT == 0 and t % TE == 0
    bf16 = jnp.bfloat16
    x2d = x.reshape(t, d)

    w_in_p, w_uq_p, w_ukv_p, b_f_p = _even_weights(w_in_e[0], w_uq_e[0], w_ukv_e[0], b_f_e[0])
    q, k, vt = _even_proj(x2d, seq, ln_mix_e[0][None, :], w_in_p, ln_q_e[0][None, :], w_uq_p,
                         ln_kv_e[0][None, :], w_ukv_p, b_f_p)
    o_t = (_attention(q, k, vt, batch, seq, 0, TQ, TK, True), _attention(q, k, vt, batch, seq, 1, TQ, TK, False))
    w_r, b_r = _router_weights(w_rg[0], b_rg[0], w_re[0], b_re[0])
    x2d, xg, route = _out_proj(x2d, o_t, w_out_e[0].astype(bf16), ln_ffn[0][None, :], w_r, b_r, seq)
    ys = _moe(xg, route[:, 0:TOP_K].astype(jnp.int32), w_gate, w_up, w_down, 0, t)

    x2d, rq, rk, rv, rg = _odd_proj(x2d, ys, route, seq, ln_mix_o[0][None, :], w_in_o[0].astype(bf16))
    og = _retention(rq, rk, rv, rg, batch, seq)
    w_r, b_r = _router_weights(w_rg[1], b_rg[1], w_re[1], b_re[1])
    x2d, xg, route = _out_proj(x2d, og, w_out_o[0].astype(bf16), ln_ffn[1][None, :], w_r, b_r, seq)
    ys = _moe(xg, route[:, 0:TOP_K].astype(jnp.int32), w_gate, w_up, w_down, 1, t)

    out = _final(x2d, ys, route, ln_f[None, :])
    return out.reshape(batch, seq, d)
```

```python
import functools
import math

import numpy as np
import jax
import jax.numpy as jnp
from jax import lax
from jax.experimental import pallas as pl
from jax.experimental.pallas import tpu as pltpu

D_MODEL = 1024
CHUNK = 64
ROPE_BASE = 10000.0
EPS = 1e-6
MLA_HEADS = 8
MLA_NOPE = 64
MLA_ROPE = 32
MLA_V = 64
MLA_Q_LORA = 256
MLA_KV_LORA = 128
FOX_HEADS = 8
FOX_DIM = 64
RET_HEADS = 4
RET_QK = 256
RET_V = 512
N_GROUPS = 4
EXPERTS_PER_GROUP = 8
N_EXPERTS = N_GROUPS * EXPERTS_PER_GROUP
TOP_K = 2
D_EXPERT = 512

LANES = 128
SUBLANES = 8
VMEM_LIMIT = 52 * 1024 * 1024

TM_PROJ = 256
TM_OUT = 512
OUT_SPLIT = 2
TQ = 1024
VT_ROWS = 80
TK = 1024
LOG2E = math.log2(math.e)
RET_C = 256
RET_BLK = 2048
TE = 256
ROWS_PER_TOK = D_MODEL // LANES

N_PAIRS = MLA_HEADS // 2
PAIR_W = 256
ROPE_H = MLA_ROPE // 2
NEG = -1e30

_X1_OFF = 0
_X2_OFF = 64
_AUG_W = 6

_C_Q = 0
_C_KV = _C_Q + MLA_Q_LORA
_C_KPE = _C_KV + MLA_KV_LORA
_C_QF = _C_KPE + LANES
_C_KF = _C_QF + FOX_HEADS * FOX_DIM
_C_VF = _C_KF + FOX_HEADS * FOX_DIM
_C_F = _C_VF + FOX_HEADS * FOX_DIM
EVEN_N = _C_F + LANES


def _cparams(sem):
    return pltpu.CompilerParams(dimension_semantics=sem, vmem_limit_bytes=VMEM_LIMIT)


def _rms_scale(x):
    return lax.rsqrt(jnp.mean(x * x, axis=-1, keepdims=True) + EPS)


def _split3(c):
    hi = c.astype(jnp.bfloat16)
    r1 = c - hi.astype(jnp.float32)
    mid = r1.astype(jnp.bfloat16)
    r2 = r1 - mid.astype(jnp.float32)
    lo = r2.astype(jnp.bfloat16)
    return hi, mid, lo


def _even_proj_kernel(tiles_per_seq, x_ref, g_ref, win_ref, lnq_ref, wuq_ref, lnkv_ref, wukv_ref,
                      bf_ref, cos_ref, sin_ref, tri_ref, pq_ref, pk_ref, oq_ref, ok_ref,
                      q_ref, k_ref, vt_ref, carry_ref):
    i = pl.program_id(0)
    x = x_ref[...]
    xn = (x * _rms_scale(x) * g_ref[...]).astype(jnp.bfloat16)
    z = jnp.dot(xn, win_ref[...], preferred_element_type=jnp.float32)
    cos = cos_ref[...]
    sin = sin_ref[...]

    def rope_slab(s):
        return s * cos + pltpu.roll(s, 64, axis=1) * sin

    c_q = z[:, _C_Q:_C_Q + MLA_Q_LORA]
    cqn = (c_q * _rms_scale(c_q) * lnq_ref[...]).astype(jnp.bfloat16)
    q = jnp.dot(cqn, wuq_ref[...], preferred_element_type=jnp.float32)
    scale_a = (MLA_NOPE + MLA_ROPE) ** -0.5 * LOG2E
    for p in range(N_PAIRS):
        lo = p * PAIR_W
        q_ref[0, :, lo:lo + LANES] = (q[:, lo:lo + LANES] * scale_a).astype(jnp.bfloat16)
        q_ref[0, :, lo + LANES:lo + PAIR_W] = (
            rope_slab(q[:, lo + LANES:lo + PAIR_W]) * scale_a).astype(jnp.bfloat16)

    c_kv = z[:, _C_KV:_C_KV + MLA_KV_LORA]
    ckn = (c_kv * _rms_scale(c_kv) * lnkv_ref[...]).astype(jnp.bfloat16)
    kv = jnp.dot(ckn, wukv_ref[...], preferred_element_type=jnp.float32)
    kpe = rope_slab(z[:, _C_KPE:_C_KPE + LANES]).astype(jnp.bfloat16)
    nk = MLA_HEADS * MLA_NOPE
    for p in range(N_PAIRS):
        lo = p * PAIR_W
        k_ref[0, :, lo:lo + LANES] = kv[:, p * LANES:(p + 1) * LANES].astype(jnp.bfloat16)
        k_ref[0, :, lo + LANES:lo + PAIR_W] = kpe
    _store_values_transposed(vt_ref, 0, kv[:, nk:nk + MLA_HEADS * MLA_V])

    lane = lax.broadcasted_iota(jnp.int32, (1, LANES), 1)
    fz = z[:, _C_F:_C_F + LANES] + bf_ref[...]
    log_f = -(jnp.maximum(-fz, 0.0) + jnp.log1p(jnp.exp(-jnp.abs(fz))))
    log_f = jnp.where(lane < FOX_HEADS, log_f, 0.0)

    @pl.when(i % tiles_per_seq == 0)
    def _():
        carry_ref[...] = jnp.zeros_like(carry_ref)

    hi, mid, lo3 = _split3(log_f)
    tri = tri_ref[...]
    cum = (jnp.dot(tri, hi, preferred_element_type=jnp.float32)
           + jnp.dot(tri, mid, preferred_element_type=jnp.float32)
           + jnp.dot(tri, lo3, preferred_element_type=jnp.float32)) + carry_ref[...]
    tm = cum.shape[0]
    carry_ref[...] = cum[tm - 1:tm, :]

    parts = jnp.concatenate(_split3(cum * LOG2E), axis=1)
    aug_q = jnp.dot(parts, pq_ref[...], preferred_element_type=jnp.float32) + oq_ref[...]
    aug_k = jnp.dot(parts, pk_ref[...], preferred_element_type=jnp.float32) + ok_ref[...]
    scale_b = FOX_DIM ** -0.5 * LOG2E
    for p in range(N_PAIRS):
        lo = p * PAIR_W
        q_ref[1, :, lo:lo + LANES] = (
            z[:, _C_QF + p * LANES:_C_QF + (p + 1) * LANES] * scale_b).astype(jnp.bfloat16)
        q_ref[1, :, lo + LANES:lo + PAIR_W] = aug_q[:, p * LANES:(p + 1) * LANES].astype(jnp.bfloat16)
        k_ref[1, :, lo:lo + LANES] = z[:, _C_KF + p * LANES:_C_KF + (p + 1) * LANES].astype(jnp.bfloat16)
        k_ref[1, :, lo + LANES:lo + PAIR_W] = aug_k[:, p * LANES:(p + 1) * LANES].astype(jnp.bfloat16)
    _store_values_transposed(vt_ref, 1, z[:, _C_VF:_C_VF + FOX_HEADS * FOX_DIM])


def _store_values_transposed(vt_ref, g, v):
    vt = v.T
    ones = jnp.ones((VT_ROWS - MLA_V, vt.shape[1]), vt_ref.dtype)
    for h in range(MLA_HEADS):
        vt_ref[g, h * VT_ROWS:h * VT_ROWS + MLA_V, :] = vt[h * MLA_V:(h + 1) * MLA_V].astype(vt_ref.dtype)
        vt_ref[g, h * VT_ROWS + MLA_V:(h + 1) * VT_ROWS, :] = ones


def _even_weights(w_in, w_uq, w_ukv, b_f):
    bf16 = jnp.bfloat16
    w_in, w_uq, w_ukv = w_in.astype(bf16), w_uq.astype(bf16), w_ukv.astype(bf16)
    d = w_in.shape[0]
    zeros = lambda n: jnp.zeros((d, n), w_in.dtype)
    o_cq, o_ckv = 0, MLA_Q_LORA
    o_kpe = o_ckv + MLA_KV_LORA
    o_qf = o_kpe + MLA_ROPE
    o_kf = o_qf + FOX_HEADS * FOX_DIM
    o_vf = o_kf + FOX_HEADS * FOX_DIM
    o_f = o_vf + FOX_HEADS * FOX_DIM
    kpe1 = w_in[:, o_kpe:o_kpe + ROPE_H]
    kpe2 = w_in[:, o_kpe + ROPE_H:o_kpe + MLA_ROPE]
    kpe_slab = jnp.concatenate([kpe1, kpe1, zeros(32), kpe2, kpe2, zeros(32)], axis=1)
    f_slab = jnp.concatenate([w_in[:, o_f:o_f + FOX_HEADS], zeros(LANES - FOX_HEADS)], axis=1)
    w_in_p = jnp.concatenate([w_in[:, o_cq:o_kpe], kpe_slab, w_in[:, o_qf:o_f], f_slab], axis=1)

    dq = w_uq.shape[0]
    zq = lambda n: jnp.zeros((dq, n), w_uq.dtype)
    hd = MLA_NOPE + MLA_ROPE
    blocks = []
    for p in range(N_PAIRS):
        h0, h1 = 2 * p, 2 * p + 1
        nope = lambda h: w_uq[:, h * hd:h * hd + MLA_NOPE]
        r1 = lambda h: w_uq[:, h * hd + MLA_NOPE:h * hd + MLA_NOPE + ROPE_H]
        r2 = lambda h: w_uq[:, h * hd + MLA_NOPE + ROPE_H:(h + 1) * hd]
        blocks += [nope(h0), nope(h1), r1(h0), r1(h1), zq(32), r2(h0), r2(h1), zq(32)]
    w_uq_p = jnp.concatenate(blocks, axis=1)

    kvd = MLA_NOPE + MLA_V
    k_cols = [w_ukv[:, h * kvd:h * kvd + MLA_NOPE] for h in range(MLA_HEADS)]
    v_cols = [w_ukv[:, h * kvd + MLA_NOPE:(h + 1) * kvd] for h in range(MLA_HEADS)]
    w_ukv_p = jnp.concatenate(k_cols + v_cols, axis=1)
    b_f_p = jnp.concatenate([b_f, jnp.zeros((LANES - FOX_HEADS,), b_f.dtype)])[None, :]
    return w_in_p, w_uq_p, w_ukv_p, b_f_p


def _aug_placement():
    pq = np.zeros((3 * LANES, N_PAIRS * LANES), np.float32)
    pk = np.zeros((3 * LANES, N_PAIRS * LANES), np.float32)
    oq = np.zeros((1, N_PAIRS * LANES), np.float32)
    ok = np.zeros((1, N_PAIRS * LANES), np.float32)
    for p in range(N_PAIRS):
        for j in range(2):
            h = 2 * p + j
            base = p * LANES + _X1_OFF + j * ROPE_H
            for t in range(3):
                pq[t * LANES + h, base + 3 + t] = 1.0
                pk[t * LANES + h, base + t] = -1.0
                oq[0, base + t] = 1.0
                ok[0, base + 3 + t] = 1.0
    return (jnp.asarray(pq, jnp.bfloat16), jnp.asarray(pk, jnp.bfloat16),
            jnp.asarray(oq), jnp.asarray(ok))


def _mla_rope_tables(s):
    inv = ROPE_BASE ** (-jnp.arange(ROPE_H, dtype=jnp.float32) / ROPE_H)
    ang = jnp.arange(s, dtype=jnp.float32)[:, None] * inv[None, :]
    c, sn = jnp.cos(ang), jnp.sin(ang)
    z = jnp.zeros((s, 32), jnp.float32)
    cos = jnp.concatenate([c, c, z, c, c, z], axis=1)
    sin = jnp.concatenate([-sn, -sn, z, sn, sn, z], axis=1)
    return cos, sin


def _even_proj(x2d, seq, g, w_in_p, ln_q, w_uq_p, ln_kv, w_ukv_p, b_f_p):
    t = x2d.shape[0]
    tm = TM_OUT
    tiles_per_seq = seq // tm
    cos, sin = _mla_rope_tables(seq)
    tri = jnp.asarray(np.tril(np.ones((tm, tm), np.float32)), jnp.bfloat16)
    pq, pk, oq, ok = _aug_placement()
    const = lambda shape: pl.BlockSpec(shape, lambda i: (0,) * len(shape))
    return pl.pallas_call(
        functools.partial(_even_proj_kernel, tiles_per_seq),
        grid=(t // tm,),
        in_specs=[
            pl.BlockSpec((tm, D_MODEL), lambda i: (i, 0)),
            const((1, D_MODEL)), const((D_MODEL, EVEN_N)),
            const((1, MLA_Q_LORA)), const((MLA_Q_LORA, N_PAIRS * PAIR_W)),
            const((1, MLA_KV_LORA)), const((MLA_KV_LORA, 2 * MLA_HEADS * MLA_NOPE)),
            const((1, LANES)),
            pl.BlockSpec((tm, LANES), lambda i: (i % tiles_per_seq, 0)),
            pl.BlockSpec((tm, LANES), lambda i: (i % tiles_per_seq, 0)),
            const((tm, tm)), const(pq.shape), const(pk.shape), const(oq.shape), const(ok.shape),
        ],
        out_specs=[
            pl.BlockSpec((2, tm, N_PAIRS * PAIR_W), lambda i: (0, i, 0)),
            pl.BlockSpec((2, tm, N_PAIRS * PAIR_W), lambda i: (0, i, 0)),
            pl.BlockSpec((2, None, MLA_HEADS * VT_ROWS, tm),
                         lambda i: (0, i // tiles_per_seq, 0, i % tiles_per_seq)),
        ],
        out_shape=[
            jax.ShapeDtypeStruct((2, t, N_PAIRS * PAIR_W), jnp.bfloat16),
            jax.ShapeDtypeStruct((2, t, N_PAIRS * PAIR_W), jnp.bfloat16),
            jax.ShapeDtypeStruct((2, t // seq, MLA_HEADS * VT_ROWS, seq), jnp.bfloat16),
        ],
        scratch_shapes=[pltpu.VMEM((1, LANES), jnp.float32)],
        compiler_params=_cparams(("arbitrary",)),
        name="even_proj",
    )(x2d, g, w_in_p, ln_q, w_uq_p, ln_kv, w_ukv_p, b_f_p, cos, sin, tri, pq, pk, oq, ok)


def _attn_kernel(group, tk, q_ref, k_ref, vt_ref, o_ref, qcat_ref, s0_ref, s1_ref, m0_ref, m1_ref,
                 a0_ref, a1_ref, acc_ref):
    i = pl.program_id(2)
    tq = q_ref.shape[0]
    s_bufs, m_bufs, a_bufs = (s0_ref, s1_ref), (m0_ref, m1_ref), (a0_ref, a1_ref)

    lane = lax.broadcasted_iota(jnp.int32, (1, PAIR_W), 1)

    def head_mask(j):
        a = (lane >= j * MLA_NOPE) & (lane < (j + 1) * MLA_NOPE)
        b = (lane >= LANES + _X1_OFF + j * ROPE_H) & (lane < LANES + _X1_OFF + (j + 1) * ROPE_H)
        c = (lane >= LANES + _X2_OFF + j * ROPE_H) & (lane < LANES + _X2_OFF + (j + 1) * ROPE_H)
        return a | b | c

    q = q_ref[...]
    zero = jnp.zeros_like(q)
    qcat_ref[0:tq, :] = jnp.where(head_mask(0), q, zero)
    qcat_ref[tq:2 * tq, :] = jnp.where(head_mask(1), q, zero)
    acc_ref[...] = jnp.zeros_like(acc_ref)

    q0 = i * tq
    n_chunks = q0 // tk + 1

    def scores(c, m_run, par, masked):
        kstart = pl.multiple_of(c * tk, tk)
        s = lax.dot_general(k_ref[pl.ds(kstart, tk), :], qcat_ref[...], (((1,), (1,)), ((), ())),
                            preferred_element_type=jnp.float32)
        if masked:
            col = lax.broadcasted_iota(jnp.int32, (1, 2 * tq), 1)
            qpos = q0 + jnp.where(col >= tq, col - tq, col)
            qlim = (qpos | (CHUNK - 1)) if group == 0 else qpos
            kpos = kstart + lax.broadcasted_iota(jnp.int32, (tk, 1), 0)
            s = jnp.where(kpos <= qlim, s, NEG)
        s_bufs[par][...] = s
        m_new = jnp.maximum(m_run, jnp.max(s, axis=0, keepdims=True))
        m_bufs[par][...] = m_new
        a_bufs[par][...] = jnp.exp2(m_run - m_new)

    def accumulate(c, par):
        kstart = pl.multiple_of(c * tk, tk)
        alpha = a_bufs[par][...]
        p = jnp.exp2(s_bufs[par][...] - m_bufs[par][...]).astype(jnp.bfloat16)
        for h in range(2):
            cols = slice(h * tq, (h + 1) * tq)
            acc_ref[h] = alpha[:, cols] * acc_ref[h] + jnp.dot(
                vt_ref[h * VT_ROWS:(h + 1) * VT_ROWS, pl.ds(kstart, tk)], p[:, cols],
                preferred_element_type=jnp.float32)

    def stage(c, par, masked):
        scores(c + 1, m_bufs[par][...], 1 - par, masked)
        accumulate(c, par)

    m_init = jnp.full((1, 2 * tq), NEG, jnp.float32)

    @pl.when(n_chunks == 1)
    def _():
        scores(0, m_init, 0, True)
        accumulate(0, 0)

    @pl.when(n_chunks > 1)
    def _():
        scores(0, m_init, 0, False)

    n_pairs = jnp.maximum(n_chunks - 2, 0) // 2

    def body(j, carry):
        for u in range(4):
            stage(4 * j + u, u % 2, False)
        return carry

    lax.fori_loop(0, n_pairs // 2, body, 0)

    @pl.when(n_pairs % 2 == 1)
    def _():
        stage(2 * n_pairs - 2, 0, False)
        stage(2 * n_pairs - 1, 1, False)
    c0 = 2 * n_pairs
    left = n_chunks - 1 - c0

    @pl.when((n_chunks > 1) & (left == 1))
    def _():
        stage(c0, 0, True)
        accumulate(c0 + 1, 1)

    @pl.when((n_chunks > 1) & (left == 2))
    def _():
        stage(c0, 0, False)
        stage(c0 + 1, 1, True)
        accumulate(c0 + 2, 0)

    for h in range(2):
        acc = acc_ref[h]
        o_ref[h * MLA_V:(h + 1) * MLA_V, :] = (acc[0:MLA_V] / acc[MLA_V:MLA_V + 1]).astype(o_ref.dtype)


def _attention(q, k, vt, batch, seq, group, tq, tk):
    nq = seq // tq
    f32 = jnp.float32
    return pl.pallas_call(
        functools.partial(_attn_kernel, group, tk),
        grid=(batch, N_PAIRS, nq),
        in_specs=[
            pl.BlockSpec((None, tq, PAIR_W), lambda b, p, i: (group, b * nq + i, p)),
            pl.BlockSpec((None, seq, PAIR_W), lambda b, p, i: (group, b, p)),
            pl.BlockSpec((None, None, 2 * VT_ROWS, seq), lambda b, p, i: (group, b, p, 0)),
        ],
        out_specs=pl.BlockSpec((None, LANES, tq), lambda b, p, i: (b, p, i)),
        out_shape=jax.ShapeDtypeStruct((batch, N_PAIRS * LANES, seq), jnp.bfloat16),
        scratch_shapes=[pltpu.VMEM((2 * tq, PAIR_W), jnp.bfloat16),
                        pltpu.VMEM((tk, 2 * tq), f32), pltpu.VMEM((tk, 2 * tq), f32),
                        pltpu.VMEM((1, 2 * tq), f32), pltpu.VMEM((1, 2 * tq), f32),
                        pltpu.VMEM((1, 2 * tq), f32), pltpu.VMEM((1, 2 * tq), f32),
                        pltpu.VMEM((2, VT_ROWS, tq), f32)],
        compiler_params=_cparams(("arbitrary",) * 3),
        name="attention_g%d" % group,
    )(q, k, vt)


def _router_t(lt):
    row = lax.broadcasted_iota(jnp.int32, lt.shape, 0)
    big = jnp.int32(LANES)
    ninf = -jnp.inf
    gl = jnp.where(row < N_GROUPS, lt, ninf)
    gmax = jnp.max(gl, axis=0, keepdims=True)
    gsel = jnp.min(jnp.where(gl == gmax, row, big), axis=0, keepdims=True)
    gsum = jnp.sum(jnp.where(row < N_GROUPS, jnp.exp(lt - gmax), 0.0), axis=0, keepdims=True)
    p_g = 1.0 / gsum
    lo = N_GROUPS + EXPERTS_PER_GROUP * gsel
    e = jnp.where((row >= lo) & (row < lo + EXPERTS_PER_GROUP), lt, ninf)
    m1 = jnp.max(e, axis=0, keepdims=True)
    i1 = jnp.min(jnp.where(e == m1, row, big), axis=0, keepdims=True)
    e2 = jnp.where(row == i1, ninf, e)
    m2 = jnp.max(e2, axis=0, keepdims=True)
    i2 = jnp.min(jnp.where(e2 == m2, row, big), axis=0, keepdims=True)
    a2 = jnp.exp(m2 - m1)
    den = 1.0 + a2
    g1 = p_g / den
    g2 = p_g * a2 / den
    f = lambda v: v.astype(jnp.float32)
    return jnp.where(row == 0, f(i1 - N_GROUPS),
                     jnp.where(row == 1, f(i2 - N_GROUPS),
                               jnp.where(row == 2, g1, jnp.where(row == 3, g2, 0.0))))


def _out_proj_kernel(n_o, transposed, x_ref, *refs):
    o_refs = refs[:n_o]
    w_ref, ln_ref, wr_ref, br_ref, xo_ref, xg_ref, route_ref = refs[n_o:]
    tm = x_ref.shape[0]
    hm = tm // OUT_SPLIT
    for part in range(OUT_SPLIT):
        rows = slice(part * hm, (part + 1) * hm)
        if transposed:
            y = x_ref[rows, :]
            nf = o_refs[0].shape[0]
            for g, o_ref in enumerate(o_refs):
                y = y + lax.dot_general(o_ref[:, rows], w_ref[g * nf:(g + 1) * nf, :],
                                        (((0,), (0,)), ((), ())), preferred_element_type=jnp.float32)
        else:
            y = x_ref[rows, :] + jnp.dot(o_refs[0][rows, :], w_ref[...], preferred_element_type=jnp.float32)
        xo_ref[rows, :] = y
        xn = y * _rms_scale(y) * ln_ref[...]
        hi = xn.astype(jnp.bfloat16)
        lo = (xn - hi.astype(jnp.float32)).astype(jnp.bfloat16)
        l2 = jnp.dot(jnp.concatenate([hi, lo], axis=1), wr_ref[...], preferred_element_type=jnp.float32)
        lt = (l2[:, :LANES] + l2[:, LANES:] + br_ref[...]).T
        route_ref[rows, :] = _router_t(lt).T
        for c in range(ROWS_PER_TOK):
            xg_ref[pl.ds(part * hm * ROWS_PER_TOK + c, hm, stride=ROWS_PER_TOK), :] = (
                xn[:, c * LANES:(c + 1) * LANES])


def _out_proj(x2d, o, w_out, ln, w_r, b_r, seq):
    t = x2d.shape[0]
    kdim = w_out.shape[0]
    tm = TM_OUT
    transposed = isinstance(o, tuple)
    tiles_per_seq = seq // tm
    if transposed:
        o_specs = [pl.BlockSpec((None, a.shape[1], tm), lambda i: (i // tiles_per_seq, 0, i % tiles_per_seq))
                   for a in o]
    else:
        o, o_specs = (o,), [pl.BlockSpec((tm, kdim), lambda i: (i, 0))]
    const = lambda shape: pl.BlockSpec(shape, lambda i: (0,) * len(shape))
    return pl.pallas_call(
        functools.partial(_out_proj_kernel, len(o), transposed),
        grid=(t // tm,),
        in_specs=[
            pl.BlockSpec((tm, D_MODEL), lambda i: (i, 0)),
            *o_specs,
            const((kdim, D_MODEL)), const((1, D_MODEL)), const((2 * D_MODEL, 2 * LANES)), const((1, LANES)),
        ],
        out_specs=[
            pl.BlockSpec((tm, D_MODEL), lambda i: (i, 0)),
            pl.BlockSpec((tm * ROWS_PER_TOK, LANES), lambda i: (i, 0)),
            pl.BlockSpec((tm, LANES), lambda i: (i, 0)),
        ],
        out_shape=[
            jax.ShapeDtypeStruct((t, D_MODEL), jnp.float32),
            jax.ShapeDtypeStruct((t * ROWS_PER_TOK, LANES), jnp.float32),
            jax.ShapeDtypeStruct((t, LANES), jnp.float32),
        ],
        compiler_params=_cparams(("arbitrary",)),
        name="out_proj_router",
    )(x2d, *o, w_out, ln, w_r, b_r)


def _router_weights(w_rg, b_rg, w_re, b_re):
    d = w_rg.shape[0]
    pad = LANES - N_GROUPS - N_EXPERTS
    w_r = jnp.concatenate([w_rg, w_re, jnp.zeros((d, pad), w_rg.dtype)], axis=1)
    b_r = jnp.concatenate([b_rg, b_re, jnp.zeros((pad,), b_rg.dtype)])[None, :]
    w_hi = w_r.astype(jnp.bfloat16)
    w_lo = (w_r - w_hi.astype(jnp.float32)).astype(jnp.bfloat16)
    half = jnp.concatenate([w_hi, w_lo], axis=1)
    return jnp.concatenate([half, half], axis=0), b_r


def _dispatch_plan(eid, t, te):
    a = t * TOP_K
    nb = a // te + N_EXPERTS
    e_flat = eid.reshape(a)
    experts = jnp.arange(N_EXPERTS, dtype=jnp.int32)
    _, order = lax.sort((e_flat, jnp.arange(a, dtype=jnp.int32)), num_keys=1, is_stable=True)
    counts = jnp.sum((e_flat[:, None] == experts[None, :]).astype(jnp.int32), axis=0)
    start = jnp.cumsum(counts) - counts
    padded = ((counts + te - 1) // te) * te
    pend = jnp.cumsum(padded)
    pstart = pend - padded
    blk0 = jnp.arange(nb, dtype=jnp.int32) * te
    blk_e = jnp.minimum(jnp.sum((blk0[:, None] >= pend[None, :]).astype(jnp.int32), axis=1), N_EXPERTS - 1)
    blk_used = (blk0 < pend[-1]).astype(jnp.int32)
    onehot = (blk_e[:, None] == experts[None, :]).astype(jnp.int32)
    per_blk = lambda v: jnp.sum(onehot * v[None, :], axis=1)[:, None]
    r = jnp.arange(te, dtype=jnp.int32)[None, :]
    off = blk0[:, None] + r - per_blk(pstart)
    valid = (off < per_blk(counts)) & (blk_used[:, None] > 0)
    src = jnp.where(valid, per_blk(start) + off, 0)
    assign = order[src]
    src_tok = jnp.where(valid, assign // TOP_K, 0).astype(jnp.int32)
    dst_row = jnp.where(valid, assign, a + r).astype(jnp.int32)
    nxt = jnp.minimum(jnp.arange(nb) + 1, nb - 1)
    return blk_e, blk_used, jnp.concatenate([src_tok, src_tok[nxt], dst_row], axis=1)[:, None, :]


def _moe_kernel(te, late_prefetch, blk_e_ref, blk_used_ref, idx_ref, xg_ref, wg_ref, wu_ref, wd_ref,
                ys_ref, xbuf, ybuf, wg_bf, wu_bf, wd_bf, gsem, ssem):
    i = pl.program_id(0)
    nb = pl.num_programs(0)
    used = blk_used_ref[i] > 0
    slot = i % 2
    block_rows = te * ROWS_PER_TOK

    def start_gather(first, s):
        for r in range(te):
            tok = idx_ref[0, 0, first + r]
            pltpu.make_async_copy(
                xg_ref.at[pl.ds(pl.multiple_of(tok * ROWS_PER_TOK, ROWS_PER_TOK), ROWS_PER_TOK), :],
                xbuf.at[s, pl.ds(r * ROWS_PER_TOK, ROWS_PER_TOK), :],
                gsem.at[s]).start(priority=r % 2)

    def wait_gather(s):
        pltpu.make_async_copy(xg_ref.at[pl.ds(0, block_rows), :], xbuf.at[s], gsem.at[s]).wait()

    def start_scatter():
        for r in range(te):
            row = idx_ref[0, 0, 2 * te + r]
            pltpu.make_async_copy(
                ybuf.at[pl.ds(r * ROWS_PER_TOK, ROWS_PER_TOK), :],
                ys_ref.at[pl.ds(pl.multiple_of(row * ROWS_PER_TOK, ROWS_PER_TOK), ROWS_PER_TOK), :],
                ssem.at[0]).start(priority=r % 2)

    def wait_scatter():
        pltpu.make_async_copy(ybuf, ys_ref.at[pl.ds(0, block_rows), :], ssem.at[0]).wait()

    @pl.when(i == 0)
    def _():
        ybuf[...] = jnp.zeros_like(ybuf)
        pltpu.make_async_copy(
            ybuf, ys_ref.at[pl.ds(ys_ref.shape[0] - block_rows, block_rows), :], ssem.at[0]).start()
        start_gather(0, 0)

    @pl.when(used)
    def _():
        if not late_prefetch:
            start_gather(te, 1 - slot)
        prev_e = blk_e_ref[jnp.maximum(i - 1, 0)]

        @pl.when((i == 0) | (blk_e_ref[i] != prev_e))
        def _():
            wg_bf[...] = wg_ref[...].astype(jnp.bfloat16)
            wu_bf[...] = wu_ref[...].astype(jnp.bfloat16)
            wd_bf[...] = wd_ref[...].astype(jnp.bfloat16)

        wait_gather(slot)
        xb = xbuf.at[slot]
        x = jnp.concatenate([xb[pl.ds(c, te, stride=ROWS_PER_TOK), :] for c in range(ROWS_PER_TOK)],
                            axis=1).astype(jnp.bfloat16)
        hg = jnp.dot(x, wg_bf[...], preferred_element_type=jnp.float32)
        hu = jnp.dot(x, wu_bf[...], preferred_element_type=jnp.float32)
        if late_prefetch:
            start_gather(te, 1 - slot)
        hid = (hg * jax.nn.sigmoid(hg) * hu).astype(jnp.bfloat16)
        y = jnp.dot(hid, wd_bf[...], preferred_element_type=jnp.float32)

        wait_scatter()
        for c in range(ROWS_PER_TOK):
            ybuf[pl.ds(c, te, stride=ROWS_PER_TOK), :] = y[:, c * LANES:(c + 1) * LANES]
        start_scatter()

        next_used = (i + 1 < nb) & (blk_used_ref[jnp.minimum(i + 1, nb - 1)] > 0)

        @pl.when(jnp.logical_not(next_used))
        def _():
            wait_gather(1 - slot)
            wait_scatter()


def _moe(xg, eid, w_gate, w_up, w_down, layer, t, late_prefetch):
    te = TE
    a = t * TOP_K
    nb = a // te + N_EXPERTS
    blk_e, blk_used, idx = _dispatch_plan(eid, t, te)
    grid_spec = pltpu.PrefetchScalarGridSpec(
        num_scalar_prefetch=2,
        grid=(nb,),
        in_specs=[
            pl.BlockSpec((1, 1, 3 * te), lambda i, be, bu: (i, 0, 0), memory_space=pltpu.SMEM),
            pl.BlockSpec(memory_space=pl.ANY),
            pl.BlockSpec((None, None, D_MODEL, D_EXPERT), lambda i, be, bu: (layer, be[i], 0, 0)),
            pl.BlockSpec((None, None, D_MODEL, D_EXPERT), lambda i, be, bu: (layer, be[i], 0, 0)),
            pl.BlockSpec((None, None, D_EXPERT, D_MODEL), lambda i, be, bu: (layer, be[i], 0, 0)),
        ],
        out_specs=pl.BlockSpec(memory_space=pl.ANY),
        scratch_shapes=[
            pltpu.VMEM((2, te * ROWS_PER_TOK, LANES), jnp.float32),
            pltpu.VMEM((te * ROWS_PER_TOK, LANES), jnp.float32),
            pltpu.VMEM((D_MODEL, D_EXPERT), jnp.bfloat16),
            pltpu.VMEM((D_MODEL, D_EXPERT), jnp.bfloat16),
            pltpu.VMEM((D_EXPERT, D_MODEL), jnp.bfloat16),
            pltpu.SemaphoreType.DMA((2,)),
            pltpu.SemaphoreType.DMA((1,)),
        ],
    )
    return pl.pallas_call(
        functools.partial(_moe_kernel, te, late_prefetch),
        grid_spec=grid_spec,
        out_shape=jax.ShapeDtypeStruct(((a + te) * ROWS_PER_TOK, LANES), jnp.float32),
        compiler_params=_cparams(("arbitrary",)),
        name="moe_experts",
    )(blk_e, blk_used, idx, xg, w_gate, w_up, w_down)


def _combine(x, ys_ref, route_ref, tm):
    stride = TOP_K * ROWS_PER_TOK
    y0 = jnp.concatenate([ys_ref[pl.ds(c, tm, stride=stride), :] for c in range(ROWS_PER_TOK)], axis=1)
    y1 = jnp.concatenate([ys_ref[pl.ds(ROWS_PER_TOK + c, tm, stride=stride), :]
                          for c in range(ROWS_PER_TOK)], axis=1)
    route = route_ref[...]
    return x + route[:, 2:3] * y0 + route[:, 3:4] * y1


def _odd_proj_kernel(x_ref, ys_ref, route_ref, g_ref, w_ref, cos_ref, sin_ref,
                     xo_ref, q_ref, k_ref, v_ref, gate_ref):
    tm = x_ref.shape[0]
    x_all = _combine(x_ref[...], ys_ref, route_ref, tm)
    xo_ref[...] = x_all
    nqk = RET_HEADS * RET_QK
    nv = RET_HEADS * RET_V
    half = RET_QK // 2
    hm = tm // 2
    for part in range(2):
        rows = slice(part * hm, (part + 1) * hm)
        x = x_all[rows]
        xn = (x * _rms_scale(x) * g_ref[...]).astype(jnp.bfloat16)
        cos = cos_ref[rows, :]
        sin = sin_ref[rows, :]

        def rope_store(z, dst, scale):
            for h in range(RET_HEADS):
                x1 = z[:, h * RET_QK:h * RET_QK + half]
                x2 = z[:, h * RET_QK + half:(h + 1) * RET_QK]
                dst[rows, h * RET_QK:h * RET_QK + half] = ((x1 * cos - x2 * sin) * scale).astype(dst.dtype)
                dst[rows, h * RET_QK + half:(h + 1) * RET_QK] = ((x1 * sin + x2 * cos) * scale).astype(dst.dtype)

        zq = jnp.dot(xn, w_ref[:, 0:nqk], preferred_element_type=jnp.float32)
        rope_store(zq, q_ref, 1.0)
        zk = jnp.dot(xn, w_ref[:, nqk:2 * nqk], preferred_element_type=jnp.float32)
        rope_store(zk, k_ref, RET_QK ** -0.5)
        zv = jnp.dot(xn, w_ref[:, 2 * nqk:2 * nqk + nv], preferred_element_type=jnp.float32)
        v_ref[rows, :] = zv.astype(v_ref.dtype)
        zg = jnp.dot(xn, w_ref[:, 2 * nqk + nv:2 * nqk + 2 * nv], preferred_element_type=jnp.float32)
        gate_ref[rows, :] = (zg * jax.nn.sigmoid(zg)).astype(gate_ref.dtype)


def _ret_rope_tables(s):
    half = RET_QK // 2
    inv = ROPE_BASE ** (-jnp.arange(half, dtype=jnp.float32) / half)
    ang = jnp.arange(s, dtype=jnp.float32)[:, None] * inv[None, :]
    return jnp.cos(ang), jnp.sin(ang)


def _odd_proj(x2d, ys, route, seq, g, w_in):
    t = x2d.shape[0]
    tm = TM_OUT
    tiles_per_seq = seq // tm
    cos, sin = _ret_rope_tables(seq)
    nqk = RET_HEADS * RET_QK
    nv = RET_HEADS * RET_V
    n_in = w_in.shape[1]
    const = lambda shape: pl.BlockSpec(shape, lambda i: (0,) * len(shape))
    tok = lambda n: pl.BlockSpec((tm, n), lambda i: (i, 0))
    return pl.pallas_call(
        _odd_proj_kernel,
        grid=(t // tm,),
        in_specs=[
            tok(D_MODEL),
            pl.BlockSpec((tm * TOP_K * ROWS_PER_TOK, LANES), lambda i: (i, 0)),
            tok(LANES),
            const((1, D_MODEL)), pl.BlockSpec((D_MODEL, n_in), lambda i: (0, 0), pipeline_mode=pl.Buffered(1)),
            pl.BlockSpec((tm, LANES), lambda i: (i % tiles_per_seq, 0)),
            pl.BlockSpec((tm, LANES), lambda i: (i % tiles_per_seq, 0)),
        ],
        out_specs=[tok(D_MODEL), tok(nqk), tok(nqk), tok(nv), tok(nv)],
        out_shape=[
            jax.ShapeDtypeStruct((t, D_MODEL), jnp.float32),
            jax.ShapeDtypeStruct((t, nqk), jnp.bfloat16),
            jax.ShapeDtypeStruct((t, nqk), jnp.bfloat16),
            jax.ShapeDtypeStruct((t, nv), jnp.bfloat16),
            jax.ShapeDtypeStruct((t, nv), jnp.bfloat16),
        ],
        compiler_params=_cparams(("arbitrary",)),
        name="odd_proj",
    )(x2d, ys, route, g, w_in, cos, sin)


def _retention_kernel(q_ref, k_ref, v_ref, gate_ref, dmask_ref, xi_ref, zeta_ref, gc_ref, o_ref, state_ref):
    n = pl.program_id(2)

    @pl.when(n == 0)
    def _():
        state_ref[...] = jnp.zeros_like(state_ref)

    xi = xi_ref[...][:, 0:1]
    zeta = zeta_ref[...][:, 0:1]
    g_c = gc_ref[...][:, 0:1]
    for j in range(q_ref.shape[0] // RET_C):
        rows = slice(j * RET_C, (j + 1) * RET_C)
        q = q_ref[rows, :]
        k = k_ref[rows, :]
        v = v_ref[rows, :]
        inner = lax.dot_general(q, k, (((1,), (1,)), ((), ())),
                                preferred_element_type=jnp.float32) * dmask_ref[...]
        o = jnp.dot(inner.astype(jnp.bfloat16), v, preferred_element_type=jnp.float32)
        state = state_ref[...]
        cross = jnp.dot(q, state.astype(jnp.bfloat16), preferred_element_type=jnp.float32)
        o = o + cross * xi
        kz = (k.astype(jnp.float32) * zeta).astype(jnp.bfloat16)
        upd = lax.dot_general(kz, v, (((0,), (0,)), ((), ())), preferred_element_type=jnp.float32)
        state_ref[...] = g_c * state + upd
        o = o * _rms_scale(o)
        o_ref[rows, :] = (o * gate_ref[rows, :].astype(jnp.float32)).astype(o_ref.dtype)


def _retention_tables():
    c = RET_C
    log_g = jnp.log(1.0 - jnp.exp2(-5.0 - jnp.arange(RET_HEADS, dtype=jnp.float32)))
    j = jnp.arange(c, dtype=jnp.float32)
    diff = j[:, None] - j[None, :]
    dmask = jnp.where(diff >= 0, jnp.exp(jnp.maximum(diff, 0.0)[None] * log_g[:, None, None]), 0.0)
    xi = jnp.exp((j[None, :] + 1.0) * log_g[:, None])
    zeta = jnp.exp((c - 1.0 - j)[None, :] * log_g[:, None])
    g_c = jnp.exp(c * log_g)
    bc = lambda a: jnp.broadcast_to(a[..., None], a.shape + (LANES,))
    return dmask, bc(xi), bc(zeta), bc(g_c[:, None])


def _retention(q, k, v, gate, batch, seq):
    nc = seq // RET_BLK
    dmask, xi, zeta, g_c = _retention_tables()
    row = lambda n: pl.BlockSpec((RET_BLK, n), lambda b, h, c: (b * nc + c, h))
    head = lambda shape: pl.BlockSpec((None,) + shape, lambda b, h, c: (h, 0, 0))
    return pl.pallas_call(
        _retention_kernel,
        grid=(batch, RET_HEADS, nc),
        in_specs=[row(RET_QK), row(RET_QK), row(RET_V), row(RET_V),
                  head((RET_C, RET_C)), head((RET_C, LANES)), head((RET_C, LANES)), head((1, LANES))],
        out_specs=row(RET_V),
        out_shape=jax.ShapeDtypeStruct((batch * seq, RET_HEADS * RET_V), jnp.bfloat16),
        scratch_shapes=[pltpu.VMEM((RET_QK, RET_V), jnp.float32)],
        compiler_params=_cparams(("arbitrary",) * 3),
        name="retention",
    )(q, k, v, gate, dmask, xi, zeta, g_c)


def _final_kernel(x_ref, ys_ref, route_ref, g_ref, o_ref):
    x = _combine(x_ref[...], ys_ref, route_ref, x_ref.shape[0])
    o_ref[...] = x * _rms_scale(x) * g_ref[...]


def _final(x2d, ys, route, g):
    t = x2d.shape[0]
    tm = TM_OUT
    return pl.pallas_call(
        _final_kernel,
        grid=(t // tm,),
        in_specs=[
            pl.BlockSpec((tm, D_MODEL), lambda i: (i, 0)),
            pl.BlockSpec((tm * TOP_K * ROWS_PER_TOK, LANES), lambda i: (i, 0)),
            pl.BlockSpec((tm, LANES), lambda i: (i, 0)),
            pl.BlockSpec((1, D_MODEL), lambda i: (0, 0)),
        ],
        out_specs=pl.BlockSpec((tm, D_MODEL), lambda i: (i, 0)),
        out_shape=jax.ShapeDtypeStruct((t, D_MODEL), jnp.float32),
        compiler_params=_cparams(("arbitrary",)),
        name="final_norm",
    )(x2d, ys, route, g)


def kernel(x, ln_mix_e, w_in_e, ln_q_e, w_uq_e, ln_kv_e, w_ukv_e, b_f_e, w_out_e, ln_mix_o, w_in_o,
           w_out_o, ln_ffn, w_rg, b_rg, w_re, b_re, w_gate, w_up, w_down, ln_f):
    batch, seq, d = x.shape
    t = batch * seq
    assert d == D_MODEL and seq % TK == 0 and seq % TQ == 0 and seq % RET_BLK == 0
    assert seq % TM_PROJ == 0 and t % TM_OUT == 0 and t % TE == 0
    bf16 = jnp.bfloat16
    x2d = x.reshape(t, d)

    w_in_p, w_uq_p, w_ukv_p, b_f_p = _even_weights(w_in_e[0], w_uq_e[0], w_ukv_e[0], b_f_e[0])
    q, k, vt = _even_proj(x2d, seq, ln_mix_e[0][None, :], w_in_p, ln_q_e[0][None, :], w_uq_p,
                         ln_kv_e[0][None, :], w_ukv_p, b_f_p)
    o_t = (_attention(q, k, vt, batch, seq, 0, TQ, TK), _attention(q, k, vt, batch, seq, 1, TQ, TK))
    w_r, b_r = _router_weights(w_rg[0], b_rg[0], w_re[0], b_re[0])
    x2d, xg, route = _out_proj(x2d, o_t, w_out_e[0].astype(bf16), ln_ffn[0][None, :], w_r, b_r, seq)
    ys = _moe(xg, route[:, 0:TOP_K].astype(jnp.int32), w_gate, w_up, w_down, 0, t, True)

    x2d, rq, rk, rv, rg = _odd_proj(x2d, ys, route, seq, ln_mix_o[0][None, :], w_in_o[0].astype(bf16))
    og = _retention(rq, rk, rv, rg, batch, seq)
    w_r, b_r = _router_weights(w_rg[1], b_rg[1], w_re[1], b_re[1])
    x2d, xg, route = _out_proj(x2d, og, w_out_o[0].astype(bf16), ln_ffn[1][None, :], w_r, b_r, seq)
    ys = _moe(xg, route[:, 0:TOP_K].astype(jnp.int32), w_gate, w_up, w_down, 1, t, False)

    out = _final(x2d, ys, route, ln_f[None, :])
    return out.reshape(batch, seq, d)
```
